```python
import math
import functools
import jax
import jax.numpy as jnp
from jax import lax
import numpy as np

D_MODEL = 2048
BATCH = 2
SEQ = 8192
DEPTH = 2

CTX_LEN = 256
GRID_W = 64
N_BRANCH = 4
BRANCH_W = D_MODEL // N_BRANCH
CONV_W = 4
LN_EPS = 1e-5

LRU_W = BRANCH_W
LRU_BLOCKS = 8
LRU_BW = LRU_W // LRU_BLOCKS
LRU_C = 8.0

NA_HEADS = 8
NA_HD = BRANCH_W // NA_HEADS
NA_KH = 8
NA_KW = 16

SSD_HEADS = 8
SSD_HD = BRANCH_W // SSD_HEADS
SSD_GROUPS = 2
SSD_STATE = 64
SSD_CHUNK = 128
SSD_GN = SSD_GROUPS * SSD_STATE

ML_HEADS = 4
ML_HD = BRANCH_W // ML_HEADS
ML_CHUNK = 128
ROPE_BASE = 10000.0

FFN_DENSE = 5632
N_EXPERTS = 8
TOP_K = 2
FFN_EXPERT = 2816
N_DENSE = (DEPTH + 1) // 2
N_MOE = DEPTH // 2

IN_SPLITS = (LRU_W, LRU_W,
             3 * BRANCH_W,
             BRANCH_W, BRANCH_W, SSD_GN, SSD_GN, 2 * SSD_HEADS,
             BRANCH_W, BRANCH_W, BRANCH_W, BRANCH_W, 4 * ML_HEADS,
             N_BRANCH * D_MODEL)
N_IN = sum(IN_SPLITS)

kernel_name = 'hybrid_lru_natten_ssd_mlstm_moe_dit'


def split_cols(p, sizes):
    return jnp.split(p, np.cumsum(sizes)[:-1].tolist(), axis=-1)


def flip_seq(t, direction):
    return t[:, ::-1] if direction == 1 else t


def layer_norm(x, g, b):
    xf = x.astype(jnp.float32)
    mu = xf.mean(-1, keepdims=True)
    var = jnp.square(xf - mu).mean(-1, keepdims=True)
    return ((xf - mu) * lax.rsqrt(var + LN_EPS)).astype(x.dtype) * g + b


def rms_norm(x, g):
    xf = x.astype(jnp.float32)
    return (xf * lax.rsqrt(jnp.square(xf).mean(-1, keepdims=True) + LN_EPS)).astype(x.dtype) * g


def dwconv_centred(x, w, b):
    ch = x.shape[-1]
    y = lax.conv_general_dilated(x, w[:, None, :].astype(x.dtype), window_strides=(1,),
                                 padding=[(CONV_W // 2, CONV_W - 1 - CONV_W // 2)],
                                 dimension_numbers=('NWC', 'WIO', 'NWC'), feature_group_count=ch)
    return y + b


def rope_2d_tables(seq, dtype):
    t = jnp.arange(seq, dtype=jnp.int32)
    pos = jnp.stack([t // GRID_W, t % GRID_W], axis=-1).astype(jnp.float32)
    nf = ML_HD // 4
    inv_freq = ROPE_BASE ** (-jnp.arange(nf, dtype=jnp.float32) / nf)
    ang = jnp.broadcast_to(pos[:, :, None, None] * inv_freq, (seq, 2, 2, nf)).reshape(seq, ML_HD)
    return jnp.cos(ang).astype(dtype), jnp.sin(ang).astype(dtype)


def apply_rope_2d(x, cos, sin):
    xs = x.reshape(x.shape[:-1] + (2, 2, ML_HD // 4))
    rot = jnp.stack([-xs[..., 1, :], xs[..., 0, :]], axis=-2).reshape(x.shape)
    return x * cos[:, None] + rot * sin[:, None]


def linear_scan(a, u, h0):
    def op(left, right):
        return (left[0] * right[0], right[0] * left[1] + right[1])
    a_cum, h = lax.associative_scan(op, (a, u), axis=1)
    return h + a_cum * h0[:, None]


def rglru_coeffs(x, wa, ba, wx, bx, lam):
    shp = x.shape
    xb = x.reshape(shp[:-1] + (LRU_BLOCKS, LRU_BW))
    r = jax.nn.sigmoid(jnp.einsum('blgi,gij->blgj', xb, wa).reshape(shp) + ba)
    i = jax.nn.sigmoid(jnp.einsum('blgi,gij->blgj', xb, wx).reshape(shp) + bx)
    log_a = (-LRU_C * r * jax.nn.softplus(-lam)).astype(jnp.float32)
    a = jnp.exp(log_a)
    u = jnp.sqrt(-jnp.expm1(2.0 * log_a)) * (i * x)
    return a.astype(x.dtype), u.astype(x.dtype)


def rglru_branch(xl, gl, xc, gc, conv_w, conv_b, wa, ba, wx, bx, lam, need_ctx_out):
    xl = dwconv_centred(xl, conv_w, conv_b)
    xc = dwconv_centred(xc, conv_w, conv_b)
    hl, hc = [], []
    for d in range(2):
        a, u = rglru_coeffs(flip_seq(xc, d), wa[d], ba[d], wx[d], bx[d], lam[d])
        h_c = linear_scan(a, u, jnp.zeros_like(u[:, 0]))
        a, u = rglru_coeffs(flip_seq(xl, d), wa[d], ba[d], wx[d], bx[d], lam[d])
        hl.append(flip_seq(linear_scan(a, u, h_c[:, -1]), d))
        if need_ctx_out:
            hc.append(flip_seq(h_c, d))
    y_l = (hl[0] + hl[1]) * jax.nn.gelu(gl)
    y_c = (hc[0] + hc[1]) * jax.nn.gelu(gc) if need_ctx_out else None
    return y_l, y_c


def na_branch(qkv_l, qkv_c, rpb, need_ctx_out):
    bsz, seq, _ = qkv_l.shape
    n_ctx = qkv_c.shape[1]
    rows = seq // GRID_W
    kh = min(NA_KH, rows)
    scale = NA_HD ** -0.5
    ql, kl, vl = [t.reshape(bsz, rows, GRID_W, NA_HEADS, NA_HD) for t in jnp.split(qkv_l, 3, axis=-1)]
    qc, kc, vc = [t.reshape(bsz, n_ctx, NA_HEADS, NA_HD) for t in jnp.split(qkv_c, 3, axis=-1)]
    r = jnp.arange(rows)
    row_start = jnp.clip(r - kh // 2, 0, rows - kh)
    key_rows = row_start[:, None] + jnp.arange(kh)[None, :]
    k_band = kl[:, key_rows]
    v_band = vl[:, key_rows]
    col = jnp.arange(GRID_W)
    col_start = jnp.clip(col - NA_KW // 2, 0, GRID_W - NA_KW)
    col_ok = (col[None, :] >= col_start[:, None]) & (col[None, :] < col_start[:, None] + NA_KW)
    dr = key_rows - r[:, None] + (NA_KH - 1)
    dc = jnp.clip(col[None, :] - col[:, None] + (NA_KW - 1), 0, 2 * NA_KW - 2)
    bias = rpb[:, dr[:, None, :, None], dc[None, :, None, :]]
    qs = ql * scale
    s_win = jnp.einsum('brwhd,brkxhd->bhrwkx', qs, k_band).astype(jnp.float32) + bias
    s_win = jnp.where(col_ok[:, None, :], s_win, -jnp.inf)
    s_ctx = jnp.einsum('brwhd,bchd->bhrwc', qs, kc).astype(jnp.float32)
    n_win = kh * GRID_W
    s = jnp.concatenate([s_win.reshape(bsz, NA_HEADS, rows, GRID_W, n_win), s_ctx], axis=-1)
    p = jax.nn.softmax(s, axis=-1).astype(vl.dtype)
    p_win = p[..., :n_win].reshape(bsz, NA_HEADS, rows, GRID_W, kh, GRID_W)
    out = (jnp.einsum('bhrwkx,brkxhd->brwhd', p_win, v_band)
           + jnp.einsum('bhrwc,bchd->brwhd', p[..., n_win:], vc))
    y_l = out.reshape(bsz, seq, BRANCH_W)
    y_c = None
    if need_ctx_out:
        p_c = jax.nn.softmax(jnp.einsum('bqhd,bkhd->bhqk', qc * scale, kc).astype(jnp.float32), axis=-1)
        y_c = jnp.einsum('bhqk,bkhd->bqhd', p_c.astype(vc.dtype), vc).reshape(bsz, n_ctx, BRANCH_W)
    return y_l, y_c


def ssd_scan(x, dt, a, bm, cm, h0, with_y):
    bsz, seq, nh, hp = x.shape
    ng, ns = bm.shape[-2:]
    hg = nh // ng
    q = SSD_CHUNK
    nc = seq // q
    xq = x.reshape(bsz, nc, q, ng, hg, hp)
    dtq = dt.reshape(bsz, nc, q, ng, hg).astype(jnp.float32)
    bq = bm.reshape(bsz, nc, q, ng, ns)
    cq = cm.reshape(bsz, nc, q, ng, ns)
    cum = jnp.cumsum(dtq * a.reshape(ng, hg), axis=2)
    last = cum[:, :, -1]
    w_end = jnp.exp(last[:, :, None] - cum) * dtq
    s_chunk = jnp.einsum('bcjgn,bcjgh,bcjghp->bcghpn', bq, w_end, xq)

    def step(h, inp):
        decay, s_c = inp
        return decay[..., None, None] * h + s_c, h

    h_fin, h_start = lax.scan(step, h0.reshape(bsz, ng, hg, hp, ns),
                              (jnp.moveaxis(jnp.exp(last), 1, 0), jnp.moveaxis(s_chunk, 1, 0)))
    h_fin = h_fin.reshape(bsz, nh, hp, ns)
    if not with_y:
        return None, h_fin
    h_start = jnp.moveaxis(h_start, 0, 1)
    y_inter = jnp.einsum('bcign,bcghpn,bcigh->bcighp', cq, h_start, jnp.exp(cum))
    cum_t = jnp.moveaxis(cum, 2, -1)
    lower = jnp.tril(jnp.ones((q, q), dtype=bool))
    decay = jnp.exp(jnp.where(lower, cum_t[..., :, None] - cum_t[..., None, :], -jnp.inf))
    cb = jnp.einsum('bcign,bcjgn->bcgij', cq, bq)
    m = cb[:, :, :, None] * decay * jnp.moveaxis(dtq, 2, -1)[..., None, :]
    y_intra = jnp.einsum('bcghij,bcjghp->bcighp', m, xq)
    return (y_intra + y_inter).reshape(bsz, seq, nh, hp), h_fin


def ssd_branch(parts_l, parts_c, conv_w, conv_b, dt_bias, a_log, d_skip, norm_g, need_ctx_out):
    def prep(parts):
        z, xs, bm, cm, dt_raw = parts
        bsz, seq = xs.shape[:2]
        xbc = jax.nn.silu(dwconv_centred(jnp.concatenate([xs, bm, cm], axis=-1), conv_w, conv_b))
        xs, bm, cm = split_cols(xbc, (BRANCH_W, SSD_GN, SSD_GN))
        return (z, xs.reshape(bsz, seq, SSD_HEADS, SSD_HD), bm.reshape(bsz, seq, SSD_GROUPS, SSD_STATE),
                cm.reshape(bsz, seq, SSD_GROUPS, SSD_STATE), dt_raw.reshape(bsz, seq, 2, SSD_HEADS))

    zl, xl, bl, cl, dtl = prep(parts_l)
    zc, xc, bc, cc, dtc = prep(parts_c)
    bsz = xl.shape[0]
    yl = xl * d_skip[:, None]
    yc = xc * d_skip[:, None] if need_ctx_out else None
    for d in range(2):
        a = -jnp.exp(a_log[d].astype(jnp.float32))
        dt_c = jax.nn.softplus(dtc[:, :, d] + dt_bias[d])
        dt_l = jax.nn.softplus(dtl[:, :, d] + dt_bias[d])
        h0 = jnp.zeros((bsz, SSD_HEADS, SSD_HD, SSD_STATE), jnp.float32)
        y_c, h_c = ssd_scan(flip_seq(xc, d), flip_seq(dt_c, d), a, flip_seq(bc, d), flip_seq(cc, d), h0, need_ctx_out)
        y_l, _ = ssd_scan(flip_seq(xl, d), flip_seq(dt_l, d), a, flip_seq(bl, d), flip_seq(cl, d), h_c, True)
        yl = yl + flip_seq(y_l, d)
        if need_ctx_out:
            yc = yc + flip_seq(y_c, d)

    def finish(y, z):
        return rms_norm(y.reshape(z.shape).astype(z.dtype) * jax.nn.silu(z), norm_g)

    return finish(yl, zl), (finish(yc, zc) if need_ctx_out else None)


def mlstm_scan(q, k, v, li, lf, state, with_y):
    bsz, seq = q.shape[:2]
    qn = ML_CHUNK
    nc = seq // qn
    lower = jnp.tril(jnp.ones((qn, qn), dtype=bool))

    def chunks(t):
        return jnp.moveaxis(t.reshape((bsz, nc, qn) + t.shape[2:]), 1, 0)

    def step(carry, inp):
        c_st, n_st, m_st = carry
        qc, kc, vc, ic, fc = inp
        b = jnp.cumsum(fc, axis=1)
        b_end = b[:, -1]
        end_log = b_end[:, None] - b + ic
        m_new = jnp.maximum(b_end + m_st, end_log.max(axis=1))
        w = jnp.exp(end_log - m_new[:, None])
        carry_scale = jnp.exp(b_end + m_st - m_new)
        c_new = carry_scale[..., None, None] * c_st + jnp.einsum('bjh,bjhv,bjhk->bhvk', w, vc, kc)
        n_new = carry_scale[..., None] * n_st + jnp.einsum('bjh,bjhk->bhk', w, kc)
        if not with_y:
            return (c_new, n_new, m_new), None
        dlog = jnp.where(lower[None, :, :, None], b[:, :, None, :] - b[:, None, :, :] + ic[:, None, :, :], -jnp.inf)
        m_inter = b + m_st[:, None]
        m_i = jnp.maximum(dlog.max(axis=2), m_inter)
        s = jnp.einsum('bihd,bjhd->bijh', qc, kc) * jnp.exp(dlog - m_i[:, :, None])
        w_in = jnp.exp(m_inter - m_i)
        num = jnp.einsum('bijh,bjhd->bihd', s, vc) + w_in[..., None] * jnp.einsum('bhvk,bihk->bihv', c_st, qc)
        den = s.sum(axis=2) + w_in * jnp.einsum('bhk,bihk->bih', n_st, qc)
        h = num / jnp.maximum(jnp.abs(den), jnp.exp(-m_i))[..., None]
        return (c_new, n_new, m_new), h

    state, hs = lax.scan(step, state, (chunks(q), chunks(k), chunks(v), chunks(li), chunks(lf)))
    if not with_y:
        return None, state
    return jnp.moveaxis(hs, 0, 1).reshape(q.shape), state


def mlstm_branch(parts_l, parts_c, cos, sin, conv_w, conv_b, i_bias, f_bias, need_ctx_out):
    def prep(parts, use_rope):
        qr, kr, v, o, g = parts
        bsz, seq = v.shape[:2]
        qk = jax.nn.silu(dwconv_centred(jnp.concatenate([qr, kr], axis=-1), conv_w, conv_b))
        q = qk[..., :BRANCH_W].reshape(bsz, seq, ML_HEADS, ML_HD)
        k = qk[..., BRANCH_W:].reshape(bsz, seq, ML_HEADS, ML_HD)
        if use_rope:
            q = apply_rope_2d(q, cos, sin)
            k = apply_rope_2d(k, cos, sin)
        return (q * ML_HD ** -0.5, k, v.reshape(bsz, seq, ML_HEADS, ML_HD), o,
                g.reshape(bsz, seq, 2, 2, ML_HEADS).astype(jnp.float32))

    ql, kl, vl, ol, gl = prep(parts_l, True)
    qc, kc, vc, oc, gc = prep(parts_c, False)
    bsz = ql.shape[0]
    hl, hc = [], []
    for d in range(2):
        state0 = (jnp.zeros((bsz, ML_HEADS, ML_HD, ML_HD), jnp.float32),
                  jnp.zeros((bsz, ML_HEADS, ML_HD), jnp.float32),
                  jnp.zeros((bsz, ML_HEADS), jnp.float32))
        li_c = gc[:, :, d, 0] + i_bias[d]
        lf_c = jax.nn.log_sigmoid(gc[:, :, d, 1] + f_bias[d])
        li_l = gl[:, :, d, 0] + i_bias[d]
        lf_l = jax.nn.log_sigmoid(gl[:, :, d, 1] + f_bias[d])
        h_c, st = mlstm_scan(flip_seq(qc, d), flip_seq(kc, d), flip_seq(vc, d), flip_seq(li_c, d),
                             flip_seq(lf_c, d), state0, need_ctx_out)
        h_l, _ = mlstm_scan(flip_seq(ql, d), flip_seq(kl, d), flip_seq(vl, d), flip_seq(li_l, d),
                            flip_seq(lf_l, d), st, True)
        hl.append(flip_seq(h_l, d))
        if need_ctx_out:
            hc.append(flip_seq(h_c, d))
    y_l = jax.nn.sigmoid(ol) * (hl[0] + hl[1]).reshape(ol.shape).astype(ol.dtype)
    y_c = jax.nn.sigmoid(oc) * (hc[0] + hc[1]).reshape(oc.shape).astype(oc.dtype) if need_ctx_out else None
    return y_l, y_c


def merge_branches(branches, gates_raw, w_branch, w_out):
    g = jax.nn.sigmoid(gates_raw)
    merged = None
    for n in range(N_BRANCH):
        term = g[..., n * D_MODEL:(n + 1) * D_MODEL] * (branches[n].astype(g.dtype) @ w_branch[n])
        merged = term if merged is None else merged + term
    return merged @ w_out


def hybrid_mixer(hl, hc, cos, sin, need_ctx_out, w_in, lru_conv_w, lru_conv_b, lru_wa, lru_ba, lru_wx,
                 lru_bx, lru_lambda, na_rpb, ssd_conv_w, ssd_conv_b, ssd_dt_bias, ssd_a_log, ssd_d,
                 ssd_norm_g, ml_conv_w, ml_conv_b, ml_i_bias, ml_f_bias, w_branch, w_out):
    pl = split_cols(hl @ w_in, IN_SPLITS)
    pc = split_cols(hc @ w_in, IN_SPLITS)
    ya_l, ya_c = rglru_branch(pl[0], pl[1], pc[0], pc[1], lru_conv_w, lru_conv_b, lru_wa, lru_ba,
                              lru_wx, lru_bx, lru_lambda, need_ctx_out)
    yb_l, yb_c = na_branch(pl[2], pc[2], na_rpb, need_ctx_out)
    yc_l, yc_c = ssd_branch(pl[3:8], pc[3:8], ssd_conv_w, ssd_conv_b, ssd_dt_bias, ssd_a_log, ssd_d,
                            ssd_norm_g, need_ctx_out)
    yd_l, yd_c = mlstm_branch(pl[8:13], pc[8:13], cos, sin, ml_conv_w, ml_conv_b, ml_i_bias, ml_f_bias,
                              need_ctx_out)
    y_l = merge_branches((ya_l, yb_l, yc_l, yd_l), pl[13], w_branch, w_out)
    y_c = merge_branches((ya_c, yb_c, yc_c, yd_c), pc[13], w_branch, w_out) if need_ctx_out else None
    return y_l, y_c


def swiglu(h, w_gate, w_up, w_down):
    return (jax.nn.silu(h @ w_gate) * (h @ w_up)) @ w_down


def moe_swiglu(h, w_router, b_router, w_gate, w_up, w_down):
    logits = (h @ w_router).astype(jnp.float32) + b_router
    top_val, top_idx = lax.top_k(logits, TOP_K)
    probs = jax.nn.softmax(top_val, axis=-1)
    combine = jnp.einsum('...k,...ke->...e', probs,
                         jax.nn.one_hot(top_idx, N_EXPERTS, dtype=probs.dtype)).astype(h.dtype)
    y = None
    for e in range(N_EXPERTS):
        term = combine[..., e:e + 1] * swiglu(h, w_gate[e], w_up[e], w_down[e])
        y = term if y is None else y + term
    return y


def setup_inputs(seed: int = 0) -> dict:
    key = jax.random.key(seed)
    keys = jax.random.split(key, 40)
    f32 = jnp.float32
    dm = D_MODEL
    beta = (8.0 * DEPTH) ** -0.25

    def nrm(i, shape, scale):
        return jax.random.normal(keys[i], shape, f32) * scale

    def uni(i, shape, lo, hi):
        return jax.random.uniform(keys[i], shape, f32, lo, hi)

    p_lru = uni(13, (DEPTH, 2, LRU_W), 0.9, 0.999) ** (1.0 / LRU_C)
    dt0 = jnp.exp(uni(17, (DEPTH, 2, SSD_HEADS), math.log(1e-3), math.log(1e-1)))
    return {
        'x': nrm(0, (BATCH, SEQ, dm), 1.0),
        'c': nrm(1, (BATCH, dm), 1.0),
        'ctx': nrm(2, (BATCH, CTX_LEN, dm), 1.0),
        'c_ctx': nrm(3, (dm,), 1.0),
        'w_ada': nrm(4, (DEPTH, dm, 6 * dm), 0.5 * dm ** -0.5),
        'b_ada': nrm(5, (DEPTH, 6 * dm), 0.02),
        'w_in': nrm(6, (DEPTH, dm, N_IN), dm ** -0.5),
        'lru_conv_w': nrm(7, (DEPTH, CONV_W, LRU_W), CONV_W ** -0.5),
        'lru_conv_b': nrm(8, (DEPTH, LRU_W), 0.02),
        'lru_wa': nrm(9, (DEPTH, 2, LRU_BLOCKS, LRU_BW, LRU_BW), LRU_BW ** -0.5),
        'lru_ba': nrm(10, (DEPTH, 2, LRU_W), 0.02),
        'lru_wx': nrm(11, (DEPTH, 2, LRU_BLOCKS, LRU_BW, LRU_BW), LRU_BW ** -0.5),
        'lru_bx': nrm(12, (DEPTH, 2, LRU_W), 0.02),
        'lru_lambda': jnp.log(p_lru) - jnp.log1p(-p_lru),
        'na_rpb': nrm(14, (DEPTH, NA_HEADS, 2 * NA_KH - 1, 2 * NA_KW - 1), 0.1),
        'ssd_conv_w': nrm(15, (DEPTH, CONV_W, BRANCH_W + 2 * SSD_GN), CONV_W ** -0.5),
        'ssd_conv_b': nrm(16, (DEPTH, BRANCH_W + 2 * SSD_GN), 0.02),
        'ssd_dt_bias': dt0 + jnp.log(-jnp.expm1(-dt0)),
        'ssd_a_log': jnp.log(uni(18, (DEPTH, 2, SSD_HEADS), 1.0, 16.0)),
        'ssd_d': 1.0 + nrm(19, (DEPTH, SSD_HEADS), 0.1),
        'ssd_norm_g': 1.0 + nrm(20, (DEPTH, BRANCH_W), 0.02),
        'ml_conv_w': nrm(21, (DEPTH, CONV_W, 2 * BRANCH_W), CONV_W ** -0.5),
        'ml_conv_b': nrm(22, (DEPTH, 2 * BRANCH_W), 0.02),
        'ml_i_bias': nrm(23, (DEPTH, 2, ML_HEADS), 0.1),
        'ml_f_bias': uni(24, (DEPTH, 2, ML_HEADS), 3.0, 6.0),
        'w_branch': nrm(25, (DEPTH, N_BRANCH, BRANCH_W, dm), BRANCH_W ** -0.5),
        'w_out': nrm(26, (DEPTH, dm, dm), beta * dm ** -0.5),
        'ln_g': 1.0 + nrm(27, (DEPTH, 2, dm), 0.02),
        'ln_b': nrm(28, (DEPTH, 2, dm), 0.02),
        'ffn_w_gate': nrm(29, (N_DENSE, dm, FFN_DENSE), dm ** -0.5),
        'ffn_w_up': nrm(30, (N_DENSE, dm, FFN_DENSE), dm ** -0.5),
        'ffn_w_down': nrm(31, (N_DENSE, FFN_DENSE, dm), beta * FFN_DENSE ** -0.5),
        'moe_w_router': nrm(32, (N_MOE, dm, N_EXPERTS), dm ** -0.5),
        'moe_b_router': nrm(33, (N_MOE, N_EXPERTS), 0.01),
        'moe_w_gate': nrm(34, (N_MOE, N_EXPERTS, dm, FFN_EXPERT), dm ** -0.5),
        'moe_w_up': nrm(35, (N_MOE, N_EXPERTS, dm, FFN_EXPERT), dm ** -0.5),
        'moe_w_down': nrm(36, (N_MOE, N_EXPERTS, FFN_EXPERT, dm), beta * FFN_EXPERT ** -0.5),
    }


def reference(x, c, ctx, c_ctx, w_ada, b_ada, w_in, lru_conv_w, lru_conv_b, lru_wa, lru_ba, lru_wx, lru_bx,
              lru_lambda, na_rpb, ssd_conv_w, ssd_conv_b, ssd_dt_bias, ssd_a_log, ssd_d, ssd_norm_g,
              ml_conv_w, ml_conv_b, ml_i_bias, ml_f_bias, w_branch, w_out, ln_g, ln_b,
              ffn_w_gate, ffn_w_up, ffn_w_down, moe_w_router, moe_b_router, moe_w_gate, moe_w_up, moe_w_down):
    alpha = (2.0 * DEPTH) ** 0.25
    cos, sin = rope_2d_tables(x.shape[1], x.dtype)
    c_act = jax.nn.silu(c)[:, None, :]
    cc_act = jax.nn.silu(c_ctx)[None, None, :]
    xl, xc = x, ctx
    for l in range(DEPTH):
        need_ctx_out = l < DEPTH - 1
        mod_l = jnp.split(c_act @ w_ada[l] + b_ada[l], 6, axis=-1)
        mod_c = jnp.split(cc_act @ w_ada[l] + b_ada[l], 6, axis=-1)
        yl, yc = hybrid_mixer(xl * (1.0 + mod_l[1]) + mod_l[0], xc * (1.0 + mod_c[1]) + mod_c[0], cos, sin,
                              need_ctx_out, w_in[l], lru_conv_w[l], lru_conv_b[l], lru_wa[l], lru_ba[l],
                              lru_wx[l], lru_bx[l], lru_lambda[l], na_rpb[l], ssd_conv_w[l], ssd_conv_b[l],
                              ssd_dt_bias[l], ssd_a_log[l], ssd_d[l], ssd_norm_g[l], ml_conv_w[l], ml_conv_b[l],
                              ml_i_bias[l], ml_f_bias[l], w_branch[l], w_out[l])
        xl = layer_norm(alpha * xl + mod_l[2] * yl, ln_g[l, 0], ln_b[l, 0])
        if need_ctx_out:
            xc = layer_norm(alpha * xc + mod_c[2] * yc, ln_g[l, 0], ln_b[l, 0])
        j = l // 2
        if l % 2 == 0:
            ffn = functools.partial(swiglu, w_gate=ffn_w_gate[j], w_up=ffn_w_up[j], w_down=ffn_w_down[j])
        else:
            ffn = functools.partial(moe_swiglu, w_router=moe_w_router[j], b_router=moe_b_router[j],
                                    w_gate=moe_w_gate[j], w_up=moe_w_up[j], w_down=moe_w_down[j])
        xl = layer_norm(alpha * xl + mod_l[5] * ffn(xl * (1.0 + mod_l[4]) + mod_l[3]), ln_g[l, 1], ln_b[l, 1])
        if need_ctx_out:
            xc = layer_norm(alpha * xc + mod_c[5] * ffn(xc * (1.0 + mod_c[4]) + mod_c[3]), ln_g[l, 1], ln_b[l, 1])
    return xl
```

```python
import functools
import math

import numpy as np
import jax
import jax.numpy as jnp
from jax import lax
from jax.experimental import pallas as pl
from jax.experimental.pallas import tpu as pltpu

F32 = jnp.float32
BF16 = jnp.bfloat16

D_MODEL = 2048
GRID_W = 64
N_BRANCH = 4
BRANCH_W = D_MODEL // N_BRANCH
CONV_W = 4
LN_EPS = 1e-5
LRU_BLOCKS = 8
LRU_BW = BRANCH_W // LRU_BLOCKS
LRU_C = 8.0
NA_HEADS = 8
NA_HD = BRANCH_W // NA_HEADS
NA_KH = 8
NA_KW = 16
SSD_HEADS = 8
SSD_HD = BRANCH_W // SSD_HEADS
SSD_GROUPS = 2
SSD_STATE = 64
SSD_CHUNK = 128
SSD_GN = SSD_GROUPS * SSD_STATE
ML_HEADS = 4
ML_HD = BRANCH_W // ML_HEADS
ML_CHUNK = 128
ROPE_BASE = 10000.0
N_EXPERTS = 8
TOP_K = 2

V7X_VMEM_LIMIT = 52 * 1024 * 1024

COL_GATES = 0
COL_LRU_X = 8192
COL_LRU_G = 8704
COL_NA = 9216
COL_SSD_Z = 10752
COL_SSD_X = 11264
COL_ML = 11776
COL_SSD_B = 13824
COL_SSD_C = 13952
COL_SSD_DT = 14080
COL_ML_G = 14096
N_IN_PAD = 14336


def _pack_w_in(w):
    parts = [w[:, 5920:14112], w[:, 0:1024], w[:, 1024:2560], w[:, 2560:3584], w[:, 3856:5904],
             w[:, 3584:3840], w[:, 3840:3856], w[:, 5904:5920],
             jnp.zeros((w.shape[0], N_IN_PAD - 14112), w.dtype)]
    return jnp.concatenate(parts, axis=1).astype(BF16)


def _sigmoid(x):
    return 1.0 / (1.0 + jnp.exp(-x))


def _layer_norm_rows(z, g, b):
    mu = jnp.mean(z, axis=-1, keepdims=True)
    zc = z - mu
    var = jnp.mean(zc * zc, axis=-1, keepdims=True)
    return zc * lax.rsqrt(var + LN_EPS) * g + b


def _mod_matmul_kernel(x_ref, sc_ref, sh_ref, w_ref, o_ref, hb_ref):
    @pl.when(pl.program_id(2) == 0)
    def _():
        hb_ref[...] = (x_ref[0] * (1.0 + sc_ref[0]) + sh_ref[0]).astype(BF16)

    o_ref[0] = jnp.dot(hb_ref[...], w_ref[...], preferred_element_type=F32)


def mod_matmul(x, sc, sh, w, tm=1024, tn=512):
    bsz, seq, d = x.shape
    n = w.shape[1]
    tm = min(tm, seq)
    return pl.pallas_call(
        _mod_matmul_kernel,
        grid=(bsz, seq // tm, n // tn),
        in_specs=[pl.BlockSpec((1, tm, d), lambda b, i, j: (b, i, 0)),
                  pl.BlockSpec((1, 1, d), lambda b, i, j: (b, 0, 0)),
                  pl.BlockSpec((1, 1, d), lambda b, i, j: (b, 0, 0)),
                  pl.BlockSpec((d, tn), lambda b, i, j: (0, j))],
        out_specs=pl.BlockSpec((1, tm, tn), lambda b, i, j: (b, i, j)),
        out_shape=jax.ShapeDtypeStruct((bsz, seq, n), F32),
        scratch_shapes=[pltpu.VMEM((tm, d), BF16)],
        compiler_params=pltpu.CompilerParams(
            dimension_semantics=("parallel", "parallel", "arbitrary"), vmem_limit_bytes=V7X_VMEM_LIMIT),
        name="in_proj",
    )(x, sc, sh, w)


def _merge_kernel(g0, g1, g2, g3, ya, yb, yc, yd, wb_ref, o_ref):
    acc = None
    for n, (g, y) in enumerate(((g0, ya), (g1, yb), (g2, yc), (g3, yd))):
        t = _sigmoid(g[0]) * jnp.dot(y[0].astype(BF16), wb_ref[n], preferred_element_type=F32)
        acc = t if acc is None else acc + t
    o_ref[0] = acc.astype(BF16)


def merge_branches(p, branches, wb, tm=512, tn=512):
    bsz, seq, _ = p.shape
    d = wb.shape[2]
    tm = min(tm, seq)
    nj = d // tn
    g_specs = [pl.BlockSpec((1, tm, tn), functools.partial(lambda b, i, j, n: (b, i, COL_GATES // tn + n * nj + j), n=n))
               for n in range(N_BRANCH)]
    y_specs = [pl.BlockSpec((1, tm, BRANCH_W), lambda b, i, j: (b, i, 0)) for _ in range(N_BRANCH)]
    return pl.pallas_call(
        _merge_kernel,
        grid=(bsz, seq // tm, nj),
        in_specs=g_specs + y_specs + [pl.BlockSpec((N_BRANCH, BRANCH_W, tn), lambda b, i, j: (0, 0, j))],
        out_specs=pl.BlockSpec((1, tm, tn), lambda b, i, j: (b, i, j)),
        out_shape=jax.ShapeDtypeStruct((bsz, seq, d), BF16),
        compiler_params=pltpu.CompilerParams(
            dimension_semantics=("parallel", "parallel", "arbitrary"), vmem_limit_bytes=V7X_VMEM_LIMIT),
        name="merge",
    )(p, p, p, p, *branches, wb)


def _proj_ln_kernel(m_ref, w_ref, x_ref, gt_ref, g_ref, b_ref, o_ref, *, alpha):
    y = jnp.dot(m_ref[0], w_ref[...], preferred_element_type=F32)
    o_ref[0] = _layer_norm_rows(alpha * x_ref[0] + gt_ref[0] * y, g_ref[...], b_ref[...])


def proj_residual_ln(m, w, x, gate, ln_g, ln_b, alpha, tm=256):
    bsz, seq, k = m.shape
    d = w.shape[1]
    tm = min(tm, seq)
    return pl.pallas_call(
        functools.partial(_proj_ln_kernel, alpha=alpha),
        grid=(bsz, seq // tm),
        in_specs=[pl.BlockSpec((1, tm, k), lambda b, i: (b, i, 0)),
                  pl.BlockSpec((k, d), lambda b, i: (0, 0)),
                  pl.BlockSpec((1, tm, d), lambda b, i: (b, i, 0)),
                  pl.BlockSpec((1, 1, d), lambda b, i: (b, 0, 0)),
                  pl.BlockSpec((1, d), lambda b, i: (0, 0)),
                  pl.BlockSpec((1, d), lambda b, i: (0, 0))],
        out_specs=pl.BlockSpec((1, tm, d), lambda b, i: (b, i, 0)),
        out_shape=jax.ShapeDtypeStruct((bsz, seq, d), F32),
        compiler_params=pltpu.CompilerParams(
            dimension_semantics=("parallel", "parallel"), vmem_limit_bytes=V7X_VMEM_LIMIT),
        name="out_proj_ln",
    )(m, w, x, gate, ln_g.reshape(1, d), ln_b.reshape(1, d))


def _ffn_kernel(x_ref, sc_ref, sh_ref, gt_ref, comb_ref, wg_ref, wu_ref, wd_ref, lg_ref, lb_ref, o_ref,
                hb_ref, acc_ref, *, alpha, use_comb):
    e = pl.program_id(2)
    j = pl.program_id(3)

    @pl.when((e == 0) & (j == 0))
    def _():
        hb_ref[...] = (x_ref[0] * (1.0 + sc_ref[0]) + sh_ref[0]).astype(BF16)
        acc_ref[...] = jnp.zeros_like(acc_ref)

    hb = hb_ref[...]
    g = jnp.dot(hb, wg_ref[0], preferred_element_type=F32)
    u = jnp.dot(hb, wu_ref[0], preferred_element_type=F32)
    a = g * _sigmoid(g) * u
    if use_comb:
        comb = comb_ref[0]
        lane = lax.broadcasted_iota(jnp.int32, comb.shape, 1)
        a = a * jnp.sum(jnp.where(lane == e, comb, 0.0), axis=-1, keepdims=True)
    acc_ref[...] += jnp.dot(a.astype(BF16), wd_ref[0], preferred_element_type=F32)

    @pl.when((e == pl.num_programs(2) - 1) & (j == pl.num_programs(3) - 1))
    def _():
        o_ref[0] = _layer_norm_rows(alpha * x_ref[0] + gt_ref[0] * acc_ref[...], lg_ref[...], lb_ref[...])


def ffn_residual_ln(x, sc, sh, gate, comb, wg, wu, wd, ln_g, ln_b, alpha, tm=512, tf=512):
    bsz, seq, d = x.shape
    n_e, _, f = wg.shape
    tm = min(tm, seq)
    use_comb = comb is not None
    if comb is None:
        comb = jnp.ones((bsz, seq, n_e), F32)
    return pl.pallas_call(
        functools.partial(_ffn_kernel, alpha=alpha, use_comb=use_comb),
        grid=(bsz, seq // tm, n_e, f // tf),
        in_specs=[pl.BlockSpec((1, tm, d), lambda b, i, e, j: (b, i, 0)),
                  pl.BlockSpec((1, 1, d), lambda b, i, e, j: (b, 0, 0)),
                  pl.BlockSpec((1, 1, d), lambda b, i, e, j: (b, 0, 0)),
                  pl.BlockSpec((1, 1, d), lambda b, i, e, j: (b, 0, 0)),
                  pl.BlockSpec((1, tm, n_e), lambda b, i, e, j: (b, i, 0)),
                  pl.BlockSpec((1, d, tf), lambda b, i, e, j: (e, 0, j)),
                  pl.BlockSpec((1, d, tf), lambda b, i, e, j: (e, 0, j)),
                  pl.BlockSpec((1, tf, d), lambda b, i, e, j: (e, j, 0)),
                  pl.BlockSpec((1, d), lambda b, i, e, j: (0, 0)),
                  pl.BlockSpec((1, d), lambda b, i, e, j: (0, 0))],
        out_specs=pl.BlockSpec((1, tm, d), lambda b, i, e, j: (b, i, 0)),
        out_shape=jax.ShapeDtypeStruct((bsz, seq, d), F32),
        scratch_shapes=[pltpu.VMEM((tm, d), BF16), pltpu.VMEM((tm, d), F32)],
        compiler_params=pltpu.CompilerParams(
            dimension_semantics=("parallel", "parallel", "arbitrary", "arbitrary"),
            vmem_limit_bytes=V7X_VMEM_LIMIT),
        name="ffn_ln",
    )(x, sc, sh, gate, comb, wg, wu, wd, ln_g.reshape(1, d), ln_b.reshape(1, d))


def _flip_seq(t, direction):
    return t[:, ::-1] if direction == 1 else t


def _dwconv_centred(x, w, b):
    ch = x.shape[-1]
    y = lax.conv_general_dilated(x, w[:, None, :].astype(x.dtype), window_strides=(1,),
                                 padding=[(CONV_W // 2, CONV_W - 1 - CONV_W // 2)],
                                 dimension_numbers=('NWC', 'WIO', 'NWC'), feature_group_count=ch)
    return y + b


def _rope_2d_tables(seq, dtype):
    t = jnp.arange(seq, dtype=jnp.int32)
    pos = jnp.stack([t // GRID_W, t % GRID_W], axis=-1).astype(jnp.float32)
    nf = ML_HD // 4
    inv_freq = ROPE_BASE ** (-jnp.arange(nf, dtype=jnp.float32) / nf)
    ang = jnp.broadcast_to(pos[:, :, None, None] * inv_freq, (seq, 2, 2, nf)).reshape(seq, ML_HD)
    return jnp.cos(ang).astype(dtype), jnp.sin(ang).astype(dtype)


def _apply_rope_2d(x, cos, sin):
    xs = x.reshape(x.shape[:-1] + (2, 2, ML_HD // 4))
    rot = jnp.stack([-xs[..., 1, :], xs[..., 0, :]], axis=-2).reshape(x.shape)
    return x * cos[:, None] + rot * sin[:, None]


def _linear_scan(a, u, h0):
    def op(left, right):
        return (left[0] * right[0], right[0] * left[1] + right[1])
    a_cum, h = lax.associative_scan(op, (a, u), axis=1)
    return h + a_cum * h0[:, None]


def _rglru_coeffs(x, wa, ba, wx, bx, lam):
    shp = x.shape
    xb = x.reshape(shp[:-1] + (LRU_BLOCKS, LRU_BW))
    r = jax.nn.sigmoid(jnp.einsum('blgi,gij->blgj', xb, wa).reshape(shp) + ba)
    i = jax.nn.sigmoid(jnp.einsum('blgi,gij->blgj', xb, wx).reshape(shp) + bx)
    log_a = (-LRU_C * r * jax.nn.softplus(-lam)).astype(jnp.float32)
    a = jnp.exp(log_a)
    u = jnp.sqrt(-jnp.expm1(2.0 * log_a)) * (i * x)
    return a, u


def _rglru_branch(xl, gl, xc, gc, conv_w, conv_b, wa, ba, wx, bx, lam, need_ctx_out):
    xl = _dwconv_centred(xl, conv_w, conv_b)
    xc = _dwconv_centred(xc, conv_w, conv_b)
    hl, hc = [], []
    for d in range(2):
        a, u = _rglru_coeffs(_flip_seq(xc, d), wa[d], ba[d], wx[d], bx[d], lam[d])
        h_c = _linear_scan(a, u, jnp.zeros_like(u[:, 0]))
        a, u = _rglru_coeffs(_flip_seq(xl, d), wa[d], ba[d], wx[d], bx[d], lam[d])
        hl.append(_flip_seq(_linear_scan(a, u, h_c[:, -1]), d))
        if need_ctx_out:
            hc.append(_flip_seq(h_c, d))
    y_l = (hl[0] + hl[1]) * jax.nn.gelu(gl)
    y_c = (hc[0] + hc[1]) * jax.nn.gelu(gc) if need_ctx_out else None
    return y_l, y_c


def _na_branch(qkv_l, qkv_c, rpb, need_ctx_out):
    bsz, seq, _ = qkv_l.shape
    n_ctx = qkv_c.shape[1]
    rows = seq // GRID_W
    kh = min(NA_KH, rows)
    scale = NA_HD ** -0.5
    ql, kl, vl = [t.reshape(bsz, rows, GRID_W, NA_HEADS, NA_HD) for t in jnp.split(qkv_l, 3, axis=-1)]
    qc, kc, vc = [t.reshape(bsz, n_ctx, NA_HEADS, NA_HD) for t in jnp.split(qkv_c, 3, axis=-1)]
    r = jnp.arange(rows)
    row_start = jnp.clip(r - kh // 2, 0, rows - kh)
    key_rows = row_start[:, None] + jnp.arange(kh)[None, :]
    k_band = kl[:, key_rows]
    v_band = vl[:, key_rows]
    col = jnp.arange(GRID_W)
    col_start = jnp.clip(col - NA_KW // 2, 0, GRID_W - NA_KW)
    col_ok = (col[None, :] >= col_start[:, None]) & (col[None, :] < col_start[:, None] + NA_KW)
    dr = key_rows - r[:, None] + (NA_KH - 1)
    dc = jnp.clip(col[None, :] - col[:, None] + (NA_KW - 1), 0, 2 * NA_KW - 2)
    bias = rpb[:, dr[:, None, :, None], dc[None, :, None, :]]
    qs = ql * scale
    s_win = jnp.einsum('brwhd,brkxhd->bhrwkx', qs, k_band).astype(jnp.float32) + bias
    s_win = jnp.where(col_ok[:, None, :], s_win, -jnp.inf)
    s_ctx = jnp.einsum('brwhd,bchd->bhrwc', qs, kc).astype(jnp.float32)
    n_win = kh * GRID_W
    s = jnp.concatenate([s_win.reshape(bsz, NA_HEADS, rows, GRID_W, n_win), s_ctx], axis=-1)
    p = jax.nn.softmax(s, axis=-1).astype(vl.dtype)
    p_win = p[..., :n_win].reshape(bsz, NA_HEADS, rows, GRID_W, kh, GRID_W)
    out = (jnp.einsum('bhrwkx,brkxhd->brwhd', p_win, v_band)
           + jnp.einsum('bhrwc,bchd->brwhd', p[..., n_win:], vc))
    y_l = out.reshape(bsz, seq, BRANCH_W)
    y_c = None
    if need_ctx_out:
        p_c = jax.nn.softmax(jnp.einsum('bqhd,bkhd->bhqk', qc * scale, kc).astype(jnp.float32), axis=-1)
        y_c = jnp.einsum('bhqk,bkhd->bqhd', p_c.astype(vc.dtype), vc).reshape(bsz, n_ctx, BRANCH_W)
    return y_l, y_c


def _ssd_scan(x, dt, a, bm, cm, h0, with_y):
    bsz, seq, nh, hp = x.shape
    ng, ns = bm.shape[-2:]
    hg = nh // ng
    q = SSD_CHUNK
    nc = seq // q
    xq = x.reshape(bsz, nc, q, ng, hg, hp)
    dtq = dt.reshape(bsz, nc, q, ng, hg).astype(jnp.float32)
    bq = bm.reshape(bsz, nc, q, ng, ns)
    cq = cm.reshape(bsz, nc, q, ng, ns)
    cum = jnp.cumsum(dtq * a.reshape(ng, hg), axis=2)
    last = cum[:, :, -1]
    w_end = jnp.exp(last[:, :, None] - cum) * dtq
    s_chunk = jnp.einsum('bcjgn,bcjgh,bcjghp->bcghpn', bq, w_end, xq)

    def step(h, inp):
        decay, s_c = inp
        return decay[..., None, None] * h + s_c, h

    h_fin, h_start = lax.scan(step, h0.reshape(bsz, ng, hg, hp, ns),
                              (jnp.moveaxis(jnp.exp(last), 1, 0), jnp.moveaxis(s_chunk, 1, 0)))
    h_fin = h_fin.reshape(bsz, nh, hp, ns)
    if not with_y:
        return None, h_fin
    h_start = jnp.moveaxis(h_start, 0, 1)
    y_inter = jnp.einsum('bcign,bcghpn,bcigh->bcighp', cq, h_start, jnp.exp(cum))
    cum_t = jnp.moveaxis(cum, 2, -1)
    lower = jnp.tril(jnp.ones((q, q), dtype=bool))
    decay = jnp.exp(jnp.where(lower, cum_t[..., :, None] - cum_t[..., None, :], -jnp.inf))
    cb = jnp.einsum('bcign,bcjgn->bcgij', cq, bq)
    m = cb[:, :, :, None] * decay * jnp.moveaxis(dtq, 2, -1)[..., None, :]
    y_intra = jnp.einsum('bcghij,bcjghp->bcighp', m, xq)
    return (y_intra + y_inter).reshape(bsz, seq, nh, hp), h_fin


def _rms_norm(x, g):
    return x * lax.rsqrt(jnp.square(x).mean(-1, keepdims=True) + LN_EPS) * g


def _ssd_branch(parts_l, parts_c, conv_w, conv_b, dt_bias, a_log, d_skip, norm_g, need_ctx_out):
    def prep(parts):
        z, xs, bm, cm, dt_raw = parts
        bsz, seq = xs.shape[:2]
        xbc = jax.nn.silu(_dwconv_centred(jnp.concatenate([xs, bm, cm], axis=-1), conv_w, conv_b))
        xs, bm, cm = xbc[..., :BRANCH_W], xbc[..., BRANCH_W:BRANCH_W + SSD_GN], xbc[..., BRANCH_W + SSD_GN:]
        return (z, xs.reshape(bsz, seq, SSD_HEADS, SSD_HD), bm.reshape(bsz, seq, SSD_GROUPS, SSD_STATE),
                cm.reshape(bsz, seq, SSD_GROUPS, SSD_STATE), dt_raw.reshape(bsz, seq, 2, SSD_HEADS))

    zl, xl, bl, cl, dtl = prep(parts_l)
    zc, xc, bc, cc, dtc = prep(parts_c)
    bsz = xl.shape[0]
    yl = xl * d_skip[:, None]
    yc = xc * d_skip[:, None] if need_ctx_out else None
    for d in range(2):
        a = -jnp.exp(a_log[d].astype(jnp.float32))
        dt_c = jax.nn.softplus(dtc[:, :, d] + dt_bias[d])
        dt_l = jax.nn.softplus(dtl[:, :, d] + dt_bias[d])
        h0 = jnp.zeros((bsz, SSD_HEADS, SSD_HD, SSD_STATE), jnp.float32)
        y_c, h_c = _ssd_scan(_flip_seq(xc, d), _flip_seq(dt_c, d), a, _flip_seq(bc, d), _flip_seq(cc, d), h0,
                             need_ctx_out)
        y_l, _ = _ssd_scan(_flip_seq(xl, d), _flip_seq(dt_l, d), a, _flip_seq(bl, d), _flip_seq(cl, d), h_c, True)
        yl = yl + _flip_seq(y_l, d)
        if need_ctx_out:
            yc = yc + _flip_seq(y_c, d)

    def finish(y, z):
        return _rms_norm(y.reshape(z.shape) * jax.nn.silu(z), norm_g)

    return finish(yl, zl), (finish(yc, zc) if need_ctx_out else None)


def _mlstm_scan(q, k, v, li, lf, state, with_y):
    bsz, seq = q.shape[:2]
    qn = ML_CHUNK
    nc = seq // qn
    lower = jnp.tril(jnp.ones((qn, qn), dtype=bool))

    def chunks(t):
        return jnp.moveaxis(t.reshape((bsz, nc, qn) + t.shape[2:]), 1, 0)

    def step(carry, inp):
        c_st, n_st, m_st = carry
        qc, kc, vc, ic, fc = inp
        b = jnp.cumsum(fc, axis=1)
        b_end = b[:, -1]
        end_log = b_end[:, None] - b + ic
        m_new = jnp.maximum(b_end + m_st, end_log.max(axis=1))
        w = jnp.exp(end_log - m_new[:, None])
        carry_scale = jnp.exp(b_end + m_st - m_new)
        c_new = carry_scale[..., None, None] * c_st + jnp.einsum('bjh,bjhv,bjhk->bhvk', w, vc, kc)
        n_new = carry_scale[..., None] * n_st + jnp.einsum('bjh,bjhk->bhk', w, kc)
        if not with_y:
            return (c_new, n_new, m_new), None
        dlog = jnp.where(lower[None, :, :, None], b[:, :, None, :] - b[:, None, :, :] + ic[:, None, :, :], -jnp.inf)
        m_inter = b + m_st[:, None]
        m_i = jnp.maximum(dlog.max(axis=2), m_inter)
        s = jnp.einsum('bihd,bjhd->bijh', qc, kc) * jnp.exp(dlog - m_i[:, :, None])
        w_in = jnp.exp(m_inter - m_i)
        num = jnp.einsum('bijh,bjhd->bihd', s, vc) + w_in[..., None] * jnp.einsum('bhvk,bihk->bihv', c_st, qc)
        den = s.sum(axis=2) + w_in * jnp.einsum('bhk,bihk->bih', n_st, qc)
        h = num / jnp.maximum(jnp.abs(den), jnp.exp(-m_i))[..., None]
        return (c_new, n_new, m_new), h

    state, hs = lax.scan(step, state, (chunks(q), chunks(k), chunks(v), chunks(li), chunks(lf)))
    if not with_y:
        return None, state
    return jnp.moveaxis(hs, 0, 1).reshape(q.shape), state


def _mlstm_branch(parts_l, parts_c, cos, sin, conv_w, conv_b, i_bias, f_bias, need_ctx_out):
    def prep(parts, use_rope):
        qr, kr, v, o, g = parts
        bsz, seq = v.shape[:2]
        qk = jax.nn.silu(_dwconv_centred(jnp.concatenate([qr, kr], axis=-1), conv_w, conv_b))
        q = qk[..., :BRANCH_W].reshape(bsz, seq, ML_HEADS, ML_HD)
        k = qk[..., BRANCH_W:].reshape(bsz, seq, ML_HEADS, ML_HD)
        if use_rope:
            q = _apply_rope_2d(q, cos, sin)
            k = _apply_rope_2d(k, cos, sin)
        return (q * ML_HD ** -0.5, k, v.reshape(bsz, seq, ML_HEADS, ML_HD), o,
                g.reshape(bsz, seq, 2, 2, ML_HEADS).astype(jnp.float32))

    ql, kl, vl, ol, gl = prep(parts_l, True)
    qc, kc, vc, oc, gc = prep(parts_c, False)
    bsz = ql.shape[0]
    hl, hc = [], []
    for d in range(2):
        state0 = (jnp.zeros((bsz, ML_HEADS, ML_HD, ML_HD), jnp.float32),
                  jnp.zeros((bsz, ML_HEADS, ML_HD), jnp.float32),
                  jnp.zeros((bsz, ML_HEADS), jnp.float32))
        li_c = gc[:, :, d, 0] + i_bias[d]
        lf_c = jax.nn.log_sigmoid(gc[:, :, d, 1] + f_bias[d])
        li_l = gl[:, :, d, 0] + i_bias[d]
        lf_l = jax.nn.log_sigmoid(gl[:, :, d, 1] + f_bias[d])
        h_c, st = _mlstm_scan(_flip_seq(qc, d), _flip_seq(kc, d), _flip_seq(vc, d), _flip_seq(li_c, d),
                              _flip_seq(lf_c, d), state0, need_ctx_out)
        h_l, _ = _mlstm_scan(_flip_seq(ql, d), _flip_seq(kl, d), _flip_seq(vl, d), _flip_seq(li_l, d),
                             _flip_seq(lf_l, d), st, True)
        hl.append(_flip_seq(h_l, d))
        if need_ctx_out:
            hc.append(_flip_seq(h_c, d))
    y_l = jax.nn.sigmoid(ol) * (hl[0] + hl[1]).reshape(ol.shape)
    y_c = jax.nn.sigmoid(oc) * (hc[0] + hc[1]).reshape(oc.shape) if need_ctx_out else None
    return y_l, y_c


def _cols(p, start, width):
    return p[..., start:start + width]


def _mixers(pl_, pc_, cos, sin, need_ctx_out, lru_conv_w, lru_conv_b, lru_wa, lru_ba, lru_wx, lru_bx, lru_lambda,
            na_rpb, ssd_conv_w, ssd_conv_b, ssd_dt_bias, ssd_a_log, ssd_d, ssd_norm_g, ml_conv_w, ml_conv_b,
            ml_i_bias, ml_f_bias):
    w = BRANCH_W
    ya = _rglru_branch(_cols(pl_, COL_LRU_X, w), _cols(pl_, COL_LRU_G, w), _cols(pc_, COL_LRU_X, w),
                       _cols(pc_, COL_LRU_G, w), lru_conv_w, lru_conv_b, lru_wa, lru_ba, lru_wx, lru_bx,
                       lru_lambda, need_ctx_out)
    yb = _na_branch(_cols(pl_, COL_NA, 3 * w), _cols(pc_, COL_NA, 3 * w), na_rpb, need_ctx_out)

    def ssd_parts(p):
        return (_cols(p, COL_SSD_Z, w), _cols(p, COL_SSD_X, w), _cols(p, COL_SSD_B, SSD_GN),
                _cols(p, COL_SSD_C, SSD_GN), _cols(p, COL_SSD_DT, 2 * SSD_HEADS))

    yc = _ssd_branch(ssd_parts(pl_), ssd_parts(pc_), ssd_conv_w, ssd_conv_b, ssd_dt_bias, ssd_a_log, ssd_d,
                     ssd_norm_g, need_ctx_out)

    def ml_parts(p):
        return (_cols(p, COL_ML, w), _cols(p, COL_ML + w, w), _cols(p, COL_ML + 2 * w, w),
                _cols(p, COL_ML + 3 * w, w), _cols(p, COL_ML_G, 4 * ML_HEADS))

    yd = _mlstm_branch(ml_parts(pl_), ml_parts(pc_), cos, sin, ml_conv_w, ml_conv_b, ml_i_bias, ml_f_bias,
                       need_ctx_out)
    return (ya[0], yb[0], yc[0], yd[0]), (ya[1], yb[1], yc[1], yd[1])


def kernel(x, c, ctx, c_ctx, w_ada, b_ada, w_in, lru_conv_w, lru_conv_b, lru_wa, lru_ba, lru_wx, lru_bx, lru_lambda, na_rpb, ssd_conv_w, ssd_conv_b, ssd_dt_bias, ssd_a_log, ssd_d, ssd_norm_g, ml_conv_w, ml_conv_b, ml_i_bias, ml_f_bias, w_branch, w_out, ln_g, ln_b, ffn_w_gate, ffn_w_up, ffn_w_down, moe_w_router, moe_b_router, moe_w_gate, moe_w_up, moe_w_down):
    depth = w_in.shape[0]
    bsz, seq, d = x.shape
    alpha = (2.0 * depth) ** 0.25
    cos, sin = _rope_2d_tables(seq, x.dtype)
    c_act = jax.nn.silu(c)
    cc_act = jax.nn.silu(c_ctx)[None, :]
    xl, xc = x, ctx
    for l in range(depth):
        need_ctx_out = l < depth - 1
        mod_l = jnp.split((c_act @ w_ada[l] + b_ada[l])[:, None, :], 6, axis=-1)
        mod_c = jnp.split(jnp.broadcast_to((cc_act @ w_ada[l] + b_ada[l])[:, None, :], (bsz, 1, 6 * d)), 6, axis=-1)
        w_in_p = _pack_w_in(w_in[l])
        p_l = mod_matmul(xl, mod_l[1], mod_l[0], w_in_p)
        p_c = mod_matmul(xc, mod_c[1], mod_c[0], w_in_p)
        br_l, br_c = _mixers(p_l, p_c, cos, sin, need_ctx_out, lru_conv_w[l], lru_conv_b[l], lru_wa[l], lru_ba[l],
                             lru_wx[l], lru_bx[l], lru_lambda[l], na_rpb[l], ssd_conv_w[l], ssd_conv_b[l],
                             ssd_dt_bias[l], ssd_a_log[l], ssd_d[l], ssd_norm_g[l], ml_conv_w[l], ml_conv_b[l],
                             ml_i_bias[l], ml_f_bias[l])
        wb = w_branch[l].astype(BF16)
        wo = w_out[l].astype(BF16)
        xl = proj_residual_ln(merge_branches(p_l, br_l, wb), wo, xl, mod_l[2], ln_g[l, 0], ln_b[l, 0], alpha)
        if need_ctx_out:
            xc = proj_residual_ln(merge_branches(p_c, br_c, wb), wo, xc, mod_c[2], ln_g[l, 0], ln_b[l, 0], alpha)
        j = l // 2
        if l % 2 == 0:
            wg = ffn_w_gate[j].astype(BF16)[None]
            wu = ffn_w_up[j].astype(BF16)[None]
            wd = ffn_w_down[j].astype(BF16)[None]

            def ffn(h, m, wg=wg, wu=wu, wd=wd):
                return ffn_residual_ln(h, m[4], m[3], m[5], None, wg, wu, wd, ln_g[l, 1], ln_b[l, 1], alpha, tf=512)
        else:
            wg = moe_w_gate[j].astype(BF16)
            wu = moe_w_up[j].astype(BF16)
            wd = moe_w_down[j].astype(BF16)

            def ffn(h, m, wg=wg, wu=wu, wd=wd, j=j):
                hm = h * (1.0 + m[4]) + m[3]
                logits = jnp.dot(hm, moe_w_router[j], precision=lax.Precision.HIGHEST) + moe_b_router[j]
                top_val, top_idx = lax.top_k(logits, TOP_K)
                probs = jax.nn.softmax(top_val, axis=-1)
                comb = jnp.einsum('...k,...ke->...e', probs, jax.nn.one_hot(top_idx, N_EXPERTS, dtype=probs.dtype))
                return ffn_residual_ln(h, m[4], m[3], m[5], comb, wg, wu, wd, ln_g[l, 1], ln_b[l, 1], alpha, tf=256)

        xl = ffn(xl, mod_l)
        if need_ctx_out:
            xc = ffn(xc, mod_c)
    return xl
```

```python
import functools
import math

import numpy as np
import jax
import jax.numpy as jnp
from jax import lax
from jax.experimental import pallas as pl
from jax.experimental.pallas import tpu as pltpu

F32 = jnp.float32
BF16 = jnp.bfloat16

D_MODEL = 2048
GRID_W = 64
N_BRANCH = 4
BRANCH_W = D_MODEL // N_BRANCH
CONV_W = 4
LN_EPS = 1e-5
LRU_BLOCKS = 8
LRU_BW = BRANCH_W // LRU_BLOCKS
LRU_C = 8.0
NA_HEADS = 8
NA_HD = BRANCH_W // NA_HEADS
NA_KH = 8
NA_KW = 16
SSD_HEADS = 8
SSD_HD = BRANCH_W // SSD_HEADS
SSD_GROUPS = 2
SSD_STATE = 64
SSD_CHUNK = 128
SSD_GN = SSD_GROUPS * SSD_STATE
ML_HEADS = 4
ML_HD = BRANCH_W // ML_HEADS
ML_CHUNK = 128
ROPE_BASE = 10000.0
N_EXPERTS = 8
TOP_K = 2

V7X_VMEM_LIMIT = 52 * 1024 * 1024

COL_GATES = 0
COL_LRU_X = 8192
COL_LRU_G = 8704
COL_NA = 9216
COL_SSD_Z = 10752
COL_SSD_X = 11264
COL_ML = 11776
COL_SSD_B = 13824
COL_SSD_C = 13952
COL_SSD_DT = 14080
COL_ML_G = 14096
N_IN_PAD = 14336


def _pack_w_in(w):
    parts = [w[:, 5920:14112], w[:, 0:1024], w[:, 1024:2560], w[:, 2560:3584], w[:, 3856:5904],
             w[:, 3584:3840], w[:, 3840:3856], w[:, 5904:5920],
             jnp.zeros((w.shape[0], N_IN_PAD - 14112), w.dtype)]
    return jnp.concatenate(parts, axis=1).astype(BF16)


def _sigmoid(x):
    return 1.0 / (1.0 + jnp.exp(-x))


def _layer_norm_rows(z, g, b):
    mu = jnp.mean(z, axis=-1, keepdims=True)
    zc = z - mu
    var = jnp.mean(zc * zc, axis=-1, keepdims=True)
    return zc * lax.rsqrt(var + LN_EPS) * g + b


def _mod_matmul_kernel(x_ref, sc_ref, sh_ref, w_ref, o_ref, hb_ref):
    @pl.when(pl.program_id(2) == 0)
    def _():
        hb_ref[...] = (x_ref[0] * (1.0 + sc_ref[0]) + sh_ref[0]).astype(BF16)

    o_ref[0] = jnp.dot(hb_ref[...], w_ref[...], preferred_element_type=F32)


def mod_matmul(x, sc, sh, w, tm=1024, tn=512):
    bsz, seq, d = x.shape
    n = w.shape[1]
    tm = min(tm, seq)
    return pl.pallas_call(
        _mod_matmul_kernel,
        grid=(bsz, seq // tm, n // tn),
        in_specs=[pl.BlockSpec((1, tm, d), lambda b, i, j: (b, i, 0)),
                  pl.BlockSpec((1, 1, d), lambda b, i, j: (b, 0, 0)),
                  pl.BlockSpec((1, 1, d), lambda b, i, j: (b, 0, 0)),
                  pl.BlockSpec((d, tn), lambda b, i, j: (0, j))],
        out_specs=pl.BlockSpec((1, tm, tn), lambda b, i, j: (b, i, j)),
        out_shape=jax.ShapeDtypeStruct((bsz, seq, n), F32),
        scratch_shapes=[pltpu.VMEM((tm, d), BF16)],
        compiler_params=pltpu.CompilerParams(
            dimension_semantics=("parallel", "parallel", "arbitrary"), vmem_limit_bytes=V7X_VMEM_LIMIT),
        name="in_proj",
    )(x, sc, sh, w)


def _merge_kernel(g0, g1, g2, g3, ya, yb, yc, yd, wb_ref, o_ref):
    acc = None
    for n, (g, y) in enumerate(((g0, ya), (g1, yb), (g2, yc), (g3, yd))):
        t = _sigmoid(g[0]) * jnp.dot(y[0].astype(BF16), wb_ref[n], preferred_element_type=F32)
        acc = t if acc is None else acc + t
    o_ref[0] = acc.astype(BF16)


def merge_branches(p, branches, wb, tm=512, tn=512):
    bsz, seq, _ = p.shape
    d = wb.shape[2]
    tm = min(tm, seq)
    nj = d // tn
    g_specs = [pl.BlockSpec((1, tm, tn), functools.partial(lambda b, i, j, n: (b, i, COL_GATES // tn + n * nj + j), n=n))
               for n in range(N_BRANCH)]
    y_specs = [pl.BlockSpec((1, tm, BRANCH_W), lambda b, i, j: (b, i, 0)) for _ in range(N_BRANCH)]
    return pl.pallas_call(
        _merge_kernel,
        grid=(bsz, seq // tm, nj),
        in_specs=g_specs + y_specs + [pl.BlockSpec((N_BRANCH, BRANCH_W, tn), lambda b, i, j: (0, 0, j))],
        out_specs=pl.BlockSpec((1, tm, tn), lambda b, i, j: (b, i, j)),
        out_shape=jax.ShapeDtypeStruct((bsz, seq, d), BF16),
        compiler_params=pltpu.CompilerParams(
            dimension_semantics=("parallel", "parallel", "arbitrary"), vmem_limit_bytes=V7X_VMEM_LIMIT),
        name="merge",
    )(p, p, p, p, *branches, wb)


def _proj_ln_kernel(m_ref, w_ref, x_ref, gt_ref, g_ref, b_ref, o_ref, *, alpha):
    y = jnp.dot(m_ref[0], w_ref[...], preferred_element_type=F32)
    o_ref[0] = _layer_norm_rows(alpha * x_ref[0] + gt_ref[0] * y, g_ref[...], b_ref[...])


def proj_residual_ln(m, w, x, gate, ln_g, ln_b, alpha, tm=256):
    bsz, seq, k = m.shape
    d = w.shape[1]
    tm = min(tm, seq)
    return pl.pallas_call(
        functools.partial(_proj_ln_kernel, alpha=alpha),
        grid=(bsz, seq // tm),
        in_specs=[pl.BlockSpec((1, tm, k), lambda b, i: (b, i, 0)),
                  pl.BlockSpec((k, d), lambda b, i: (0, 0)),
                  pl.BlockSpec((1, tm, d), lambda b, i: (b, i, 0)),
                  pl.BlockSpec((1, 1, d), lambda b, i: (b, 0, 0)),
                  pl.BlockSpec((1, d), lambda b, i: (0, 0)),
                  pl.BlockSpec((1, d), lambda b, i: (0, 0))],
        out_specs=pl.BlockSpec((1, tm, d), lambda b, i: (b, i, 0)),
        out_shape=jax.ShapeDtypeStruct((bsz, seq, d), F32),
        compiler_params=pltpu.CompilerParams(
            dimension_semantics=("parallel", "parallel"), vmem_limit_bytes=V7X_VMEM_LIMIT),
        name="out_proj_ln",
    )(m, w, x, gate, ln_g.reshape(1, d), ln_b.reshape(1, d))


def _ffn_kernel(x_ref, sc_ref, sh_ref, gt_ref, comb_ref, wg_ref, wu_ref, wd_ref, lg_ref, lb_ref, o_ref,
                hb_ref, acc_ref, *, alpha, use_comb):
    e = pl.program_id(2)
    j = pl.program_id(3)

    @pl.when((e == 0) & (j == 0))
    def _():
        hb_ref[...] = (x_ref[0] * (1.0 + sc_ref[0]) + sh_ref[0]).astype(BF16)
        acc_ref[...] = jnp.zeros_like(acc_ref)

    hb = hb_ref[...]
    g = jnp.dot(hb, wg_ref[0], preferred_element_type=F32)
    u = jnp.dot(hb, wu_ref[0], preferred_element_type=F32)
    a = g * _sigmoid(g) * u
    if use_comb:
        comb = comb_ref[0]
        lane = lax.broadcasted_iota(jnp.int32, comb.shape, 1)
        a = a * jnp.sum(jnp.where(lane == e, comb, 0.0), axis=-1, keepdims=True)
    acc_ref[...] += jnp.dot(a.astype(BF16), wd_ref[0], preferred_element_type=F32)

    @pl.when((e == pl.num_programs(2) - 1) & (j == pl.num_programs(3) - 1))
    def _():
        o_ref[0] = _layer_norm_rows(alpha * x_ref[0] + gt_ref[0] * acc_ref[...], lg_ref[...], lb_ref[...])


def ffn_residual_ln(x, sc, sh, gate, comb, wg, wu, wd, ln_g, ln_b, alpha, tm=512, tf=512):
    bsz, seq, d = x.shape
    n_e, _, f = wg.shape
    tm = min(tm, seq)
    use_comb = comb is not None
    if comb is None:
        comb = jnp.ones((bsz, seq, n_e), F32)
    return pl.pallas_call(
        functools.partial(_ffn_kernel, alpha=alpha, use_comb=use_comb),
        grid=(bsz, seq // tm, n_e, f // tf),
        in_specs=[pl.BlockSpec((1, tm, d), lambda b, i, e, j: (b, i, 0)),
                  pl.BlockSpec((1, 1, d), lambda b, i, e, j: (b, 0, 0)),
                  pl.BlockSpec((1, 1, d), lambda b, i, e, j: (b, 0, 0)),
                  pl.BlockSpec((1, 1, d), lambda b, i, e, j: (b, 0, 0)),
                  pl.BlockSpec((1, tm, n_e), lambda b, i, e, j: (b, i, 0)),
                  pl.BlockSpec((1, d, tf), lambda b, i, e, j: (e, 0, j)),
                  pl.BlockSpec((1, d, tf), lambda b, i, e, j: (e, 0, j)),
                  pl.BlockSpec((1, tf, d), lambda b, i, e, j: (e, j, 0)),
                  pl.BlockSpec((1, d), lambda b, i, e, j: (0, 0)),
                  pl.BlockSpec((1, d), lambda b, i, e, j: (0, 0))],
        out_specs=pl.BlockSpec((1, tm, d), lambda b, i, e, j: (b, i, 0)),
        out_shape=jax.ShapeDtypeStruct((bsz, seq, d), F32),
        scratch_shapes=[pltpu.VMEM((tm, d), BF16), pltpu.VMEM((tm, d), F32)],
        compiler_params=pltpu.CompilerParams(
            dimension_semantics=("parallel", "parallel", "arbitrary", "arbitrary"),
            vmem_limit_bytes=V7X_VMEM_LIMIT),
        name="ffn_ln",
    )(x, sc, sh, gate, comb, wg, wu, wd, ln_g.reshape(1, d), ln_b.reshape(1, d))


def _flip_seq(t, direction):
    return t[:, ::-1] if direction == 1 else t


def _dwconv_centred(x, w, b):
    ch = x.shape[-1]
    y = lax.conv_general_dilated(x, w[:, None, :].astype(x.dtype), window_strides=(1,),
                                 padding=[(CONV_W // 2, CONV_W - 1 - CONV_W // 2)],
                                 dimension_numbers=('NWC', 'WIO', 'NWC'), feature_group_count=ch)
    return y + b


def _rope_2d_tables(seq, dtype):
    t = jnp.arange(seq, dtype=jnp.int32)
    pos = jnp.stack([t // GRID_W, t % GRID_W], axis=-1).astype(jnp.float32)
    nf = ML_HD // 4
    inv_freq = ROPE_BASE ** (-jnp.arange(nf, dtype=jnp.float32) / nf)
    ang = jnp.broadcast_to(pos[:, :, None, None] * inv_freq, (seq, 2, 2, nf)).reshape(seq, ML_HD)
    return jnp.cos(ang).astype(dtype), jnp.sin(ang).astype(dtype)


def _apply_rope_2d(x, cos, sin):
    xs = x.reshape(x.shape[:-1] + (2, 2, ML_HD // 4))
    rot = jnp.stack([-xs[..., 1, :], xs[..., 0, :]], axis=-2).reshape(x.shape)
    return x * cos[:, None] + rot * sin[:, None]


def _linear_scan(a, u, h0):
    def op(left, right):
        return (left[0] * right[0], right[0] * left[1] + right[1])
    a_cum, h = lax.associative_scan(op, (a, u), axis=1)
    return h + a_cum * h0[:, None]


def _rglru_coeffs(x, wa, ba, wx, bx, lam):
    shp = x.shape
    xb = x.reshape(shp[:-1] + (LRU_BLOCKS, LRU_BW))
    r = jax.nn.sigmoid(jnp.einsum('blgi,gij->blgj', xb, wa).reshape(shp) + ba)
    i = jax.nn.sigmoid(jnp.einsum('blgi,gij->blgj', xb, wx).reshape(shp) + bx)
    log_a = (-LRU_C * r * jax.nn.softplus(-lam)).astype(jnp.float32)
    a = jnp.exp(log_a)
    u = jnp.sqrt(-jnp.expm1(2.0 * log_a)) * (i * x)
    return a, u


def _rglru_branch(xl, gl, xc, gc, conv_w, conv_b, wa, ba, wx, bx, lam, need_ctx_out):
    xl = _dwconv_centred(xl, conv_w, conv_b)
    xc = _dwconv_centred(xc, conv_w, conv_b)
    hl, hc = [], []
    for d in range(2):
        a, u = _rglru_coeffs(_flip_seq(xc, d), wa[d], ba[d], wx[d], bx[d], lam[d])
        h_c = _linear_scan(a, u, jnp.zeros_like(u[:, 0]))
        a, u = _rglru_coeffs(_flip_seq(xl, d), wa[d], ba[d], wx[d], bx[d], lam[d])
        hl.append(_flip_seq(_linear_scan(a, u, h_c[:, -1]), d))
        if need_ctx_out:
            hc.append(_flip_seq(h_c, d))
    y_l = (hl[0] + hl[1]) * jax.nn.gelu(gl)
    y_c = (hc[0] + hc[1]) * jax.nn.gelu(gc) if need_ctx_out else None
    return y_l, y_c


NA_NEG = -1e30


def _na_bias_slabs(rpb):
    w = jnp.arange(GRID_W)
    cs = jnp.clip(w - NA_KW // 2, 0, GRID_W - NA_KW)
    ok = (w[None, :] >= cs[:, None]) & (w[None, :] < cs[:, None] + NA_KW)
    dc = jnp.clip(w[None, :] - w[:, None] + (NA_KW - 1), 0, 2 * NA_KW - 2)
    tab = jnp.where(ok, rpb[:, :, dc], NA_NEG)
    idx = jnp.arange(NA_KH)[:, None] + jnp.arange(NA_KH)[None, :]
    slab = tab[:, idx]
    return slab.transpose(1, 0, 3, 2, 4).reshape(NA_KH, NA_HEADS, GRID_W, NA_KH * GRID_W)


def _softmax2(s_a, s_b):
    m = jnp.maximum(jnp.max(s_a, axis=-1, keepdims=True), jnp.max(s_b, axis=-1, keepdims=True))
    e_a = jnp.exp(s_a - m)
    e_b = jnp.exp(s_b - m)
    inv = 1.0 / (jnp.sum(e_a, axis=-1, keepdims=True) + jnp.sum(e_b, axis=-1, keepdims=True))
    return e_a * inv, e_b * inv


_NT = (((1,), (1,)), ((), ()))


def _na_kernel(q_ref, kp_ref, kc_ref, kn_ref, vp_ref, vc_ref, vn_ref, ck_ref, cv_ref, bias_ref, o_ref,
               kw_ref, vw_ref, ckb_ref, cvb_ref, *, rows):
    i = pl.program_id(1)
    blk = NA_KH * GRID_W
    for n, (kr, vr) in enumerate(((kp_ref, vp_ref), (kc_ref, vc_ref), (kn_ref, vn_ref))):
        kw_ref[n * blk:(n + 1) * blk, :] = kr[0].astype(BF16)
        vw_ref[n * blk:(n + 1) * blk, :] = vr[0].astype(BF16)
    ckb_ref[...] = ck_ref[0].astype(BF16)
    cvb_ref[...] = cv_ref[0].astype(BF16)
    scale = NA_HD ** -0.5

    def body(rr, carry):
        r = i * NA_KH + rr
        rs = jnp.clip(r - NA_KH // 2, 0, rows - NA_KH)
        off = pl.multiple_of((rs - (i - 1) * NA_KH) * GRID_W, GRID_W)
        v = rs - r + (NA_KH - 1)
        q_all = (q_ref[0, pl.ds(pl.multiple_of(rr * GRID_W, GRID_W), GRID_W), :] * scale).astype(BF16)
        kwin = kw_ref[pl.ds(off, blk), :]
        vwin = vw_ref[pl.ds(off, blk), :]
        outs = []
        for h in range(NA_HEADS):
            hs = slice(h * NA_HD, (h + 1) * NA_HD)
            q = q_all[:, hs]
            s_w = lax.dot_general(q, kwin[:, hs], _NT, preferred_element_type=F32) + bias_ref[v, h]
            s_c = lax.dot_general(q, ckb_ref[:, hs], _NT, preferred_element_type=F32)
            p_w, p_c = _softmax2(s_w, s_c)
            outs.append(jnp.dot(p_w.astype(BF16), vwin[:, hs], preferred_element_type=F32)
                        + jnp.dot(p_c.astype(BF16), cvb_ref[:, hs], preferred_element_type=F32))
        o_ref[0, pl.ds(pl.multiple_of(rr * GRID_W, GRID_W), GRID_W), :] = jnp.concatenate(outs, axis=-1)
        return carry

    lax.fori_loop(0, NA_KH, body, 0)


def _na_ctx_kernel(q_ref, k_ref, v_ref, o_ref):
    scale = NA_HD ** -0.5
    q_all = (q_ref[0] * scale).astype(BF16)
    k_all = k_ref[0].astype(BF16)
    v_all = v_ref[0].astype(BF16)
    outs = []
    for h in range(NA_HEADS):
        hs = slice(h * NA_HD, (h + 1) * NA_HD)
        s = lax.dot_general(q_all[:, hs], k_all[:, hs], _NT, preferred_element_type=F32)
        e = jnp.exp(s - jnp.max(s, axis=-1, keepdims=True))
        p = e * (1.0 / jnp.sum(e, axis=-1, keepdims=True))
        outs.append(jnp.dot(p.astype(BF16), v_all[:, hs], preferred_element_type=F32))
    o_ref[0] = jnp.concatenate(outs, axis=-1)


def na_branch(p_l, p_c, rpb, need_ctx_out):
    bsz, seq, _ = p_l.shape
    n_ctx = p_c.shape[1]
    rows = seq // GRID_W
    assert rows % NA_KH == 0 and rows >= 2 * NA_KH
    w = BRANCH_W
    blk = NA_KH * GRID_W
    nblk = rows // NA_KH
    cq = COL_NA // w

    def shifted(col, delta):
        return pl.BlockSpec((1, blk, w), lambda b, i: (b, jnp.clip(i + delta, 0, nblk - 1), col))

    y_l = pl.pallas_call(
        functools.partial(_na_kernel, rows=rows),
        grid=(bsz, nblk),
        in_specs=[shifted(cq, 0), shifted(cq + 1, -1), shifted(cq + 1, 0), shifted(cq + 1, 1),
                  shifted(cq + 2, -1), shifted(cq + 2, 0), shifted(cq + 2, 1),
                  pl.BlockSpec((1, n_ctx, w), lambda b, i: (b, 0, cq + 1)),
                  pl.BlockSpec((1, n_ctx, w), lambda b, i: (b, 0, cq + 2)),
                  pl.BlockSpec((NA_KH, NA_HEADS, GRID_W, blk), lambda b, i: (0, 0, 0, 0))],
        out_specs=pl.BlockSpec((1, blk, w), lambda b, i: (b, i, 0)),
        out_shape=jax.ShapeDtypeStruct((bsz, seq, w), F32),
        scratch_shapes=[pltpu.VMEM((3 * blk, w), BF16), pltpu.VMEM((3 * blk, w), BF16),
                        pltpu.VMEM((n_ctx, w), BF16), pltpu.VMEM((n_ctx, w), BF16)],
        compiler_params=pltpu.CompilerParams(
            dimension_semantics=("parallel", "arbitrary"), vmem_limit_bytes=V7X_VMEM_LIMIT),
        name="na_attn",
    )(p_l, p_l, p_l, p_l, p_l, p_l, p_l, p_c, p_c, _na_bias_slabs(rpb))
    y_c = None
    if need_ctx_out:
        y_c = pl.pallas_call(
            _na_ctx_kernel,
            grid=(bsz,),
            in_specs=[pl.BlockSpec((1, n_ctx, w), functools.partial(lambda b, c: (b, 0, c), c=cq + n)) for n in range(3)],
            out_specs=pl.BlockSpec((1, n_ctx, w), lambda b: (b, 0, 0)),
            out_shape=jax.ShapeDtypeStruct((bsz, n_ctx, w), F32),
            compiler_params=pltpu.CompilerParams(dimension_semantics=("parallel",)),
            name="na_ctx_attn",
        )(p_c, p_c, p_c)
    return y_l, y_c


def _ssd_scan(x, dt, a, bm, cm, h0, with_y):
    bsz, seq, nh, hp = x.shape
    ng, ns = bm.shape[-2:]
    hg = nh // ng
    q = SSD_CHUNK
    nc = seq // q
    xq = x.reshape(bsz, nc, q, ng, hg, hp)
    dtq = dt.reshape(bsz, nc, q, ng, hg).astype(jnp.float32)
    bq = bm.reshape(bsz, nc, q, ng, ns)
    cq = cm.reshape(bsz, nc, q, ng, ns)
    cum = jnp.cumsum(dtq * a.reshape(ng, hg), axis=2)
    last = cum[:, :, -1]
    w_end = jnp.exp(last[:, :, None] - cum) * dtq
    s_chunk = jnp.einsum('bcjgn,bcjgh,bcjghp->bcghpn', bq, w_end, xq)

    def step(h, inp):
        decay, s_c = inp
        return decay[..., None, None] * h + s_c, h

    h_fin, h_start = lax.scan(step, h0.reshape(bsz, ng, hg, hp, ns),
                              (jnp.moveaxis(jnp.exp(last), 1, 0), jnp.moveaxis(s_chunk, 1, 0)))
    h_fin = h_fin.reshape(bsz, nh, hp, ns)
    if not with_y:
        return None, h_fin
    h_start = jnp.moveaxis(h_start, 0, 1)
    y_inter = jnp.einsum('bcign,bcghpn,bcigh->bcighp', cq, h_start, jnp.exp(cum))
    cum_t = jnp.moveaxis(cum, 2, -1)
    lower = jnp.tril(jnp.ones((q, q), dtype=bool))
    decay = jnp.exp(jnp.where(lower, cum_t[..., :, None] - cum_t[..., None, :], -jnp.inf))
    cb = jnp.einsum('bcign,bcjgn->bcgij', cq, bq)
    m = cb[:, :, :, None] * decay * jnp.moveaxis(dtq, 2, -1)[..., None, :]
    y_intra = jnp.einsum('bcghij,bcjghp->bcighp', m, xq)
    return (y_intra + y_inter).reshape(bsz, seq, nh, hp), h_fin


def _rms_norm(x, g):
    return x * lax.rsqrt(jnp.square(x).mean(-1, keepdims=True) + LN_EPS) * g


def _ssd_branch(parts_l, parts_c, conv_w, conv_b, dt_bias, a_log, d_skip, norm_g, need_ctx_out):
    def prep(parts):
        z, xs, bm, cm, dt_raw = parts
        bsz, seq = xs.shape[:2]
        xbc = jax.nn.silu(_dwconv_centred(jnp.concatenate([xs, bm, cm], axis=-1), conv_w, conv_b))
        xs, bm, cm = xbc[..., :BRANCH_W], xbc[..., BRANCH_W:BRANCH_W + SSD_GN], xbc[..., BRANCH_W + SSD_GN:]
        return (z, xs.reshape(bsz, seq, SSD_HEADS, SSD_HD), bm.reshape(bsz, seq, SSD_GROUPS, SSD_STATE),
                cm.reshape(bsz, seq, SSD_GROUPS, SSD_STATE), dt_raw.reshape(bsz, seq, 2, SSD_HEADS))

    zl, xl, bl, cl, dtl = prep(parts_l)
    zc, xc, bc, cc, dtc = prep(parts_c)
    bsz = xl.shape[0]
    yl = xl * d_skip[:, None]
    yc = xc * d_skip[:, None] if need_ctx_out else None
    for d in range(2):
        a = -jnp.exp(a_log[d].astype(jnp.float32))
        dt_c = jax.nn.softplus(dtc[:, :, d] + dt_bias[d])
        dt_l = jax.nn.softplus(dtl[:, :, d] + dt_bias[d])
        h0 = jnp.zeros((bsz, SSD_HEADS, SSD_HD, SSD_STATE), jnp.float32)
        y_c, h_c = _ssd_scan(_flip_seq(xc, d), _flip_seq(dt_c, d), a, _flip_seq(bc, d), _flip_seq(cc, d), h0,
                             need_ctx_out)
        y_l, _ = _ssd_scan(_flip_seq(xl, d), _flip_seq(dt_l, d), a, _flip_seq(bl, d), _flip_seq(cl, d), h_c, True)
        yl = yl + _flip_seq(y_l, d)
        if need_ctx_out:
            yc = yc + _flip_seq(y_c, d)

    def finish(y, z):
        return _rms_norm(y.reshape(z.shape) * jax.nn.silu(z), norm_g)

    return finish(yl, zl), (finish(yc, zc) if need_ctx_out else None)


def _mlstm_scan(q, k, v, li, lf, state, with_y):
    bsz, seq = q.shape[:2]
    qn = ML_CHUNK
    nc = seq // qn
    lower = jnp.tril(jnp.ones((qn, qn), dtype=bool))

    def chunks(t):
        return jnp.moveaxis(t.reshape((bsz, nc, qn) + t.shape[2:]), 1, 0)

    def step(carry, inp):
        c_st, n_st, m_st = carry
        qc, kc, vc, ic, fc = inp
        b = jnp.cumsum(fc, axis=1)
        b_end = b[:, -1]
        end_log = b_end[:, None] - b + ic
        m_new = jnp.maximum(b_end + m_st, end_log.max(axis=1))
        w = jnp.exp(end_log - m_new[:, None])
        carry_scale = jnp.exp(b_end + m_st - m_new)
        c_new = carry_scale[..., None, None] * c_st + jnp.einsum('bjh,bjhv,bjhk->bhvk', w, vc, kc)
        n_new = carry_scale[..., None] * n_st + jnp.einsum('bjh,bjhk->bhk', w, kc)
        if not with_y:
            return (c_new, n_new, m_new), None
        dlog = jnp.where(lower[None, :, :, None], b[:, :, None, :] - b[:, None, :, :] + ic[:, None, :, :], -jnp.inf)
        m_inter = b + m_st[:, None]
        m_i = jnp.maximum(dlog.max(axis=2), m_inter)
        s = jnp.einsum('bihd,bjhd->bijh', qc, kc) * jnp.exp(dlog - m_i[:, :, None])
        w_in = jnp.exp(m_inter - m_i)
        num = jnp.einsum('bijh,bjhd->bihd', s, vc) + w_in[..., None] * jnp.einsum('bhvk,bihk->bihv', c_st, qc)
        den = s.sum(axis=2) + w_in * jnp.einsum('bhk,bihk->bih', n_st, qc)
        h = num / jnp.maximum(jnp.abs(den), jnp.exp(-m_i))[..., None]
        return (c_new, n_new, m_new), h

    state, hs = lax.scan(step, state, (chunks(q), chunks(k), chunks(v), chunks(li), chunks(lf)))
    if not with_y:
        return None, state
    return jnp.moveaxis(hs, 0, 1).reshape(q.shape), state


def _mlstm_branch(parts_l, parts_c, cos, sin, conv_w, conv_b, i_bias, f_bias, need_ctx_out):
    def prep(parts, use_rope):
        qr, kr, v, o, g = parts
        bsz, seq = v.shape[:2]
        qk = jax.nn.silu(_dwconv_centred(jnp.concatenate([qr, kr], axis=-1), conv_w, conv_b))
        q = qk[..., :BRANCH_W].reshape(bsz, seq, ML_HEADS, ML_HD)
        k = qk[..., BRANCH_W:].reshape(bsz, seq, ML_HEADS, ML_HD)
        if use_rope:
            q = _apply_rope_2d(q, cos, sin)
            k = _apply_rope_2d(k, cos, sin)
        return (q * ML_HD ** -0.5, k, v.reshape(bsz, seq, ML_HEADS, ML_HD), o,
                g.reshape(bsz, seq, 2, 2, ML_HEADS).astype(jnp.float32))

    ql, kl, vl, ol, gl = prep(parts_l, True)
    qc, kc, vc, oc, gc = prep(parts_c, False)
    bsz = ql.shape[0]
    hl, hc = [], []
    for d in range(2):
        state0 = (jnp.zeros((bsz, ML_HEADS, ML_HD, ML_HD), jnp.float32),
                  jnp.zeros((bsz, ML_HEADS, ML_HD), jnp.float32),
                  jnp.zeros((bsz, ML_HEADS), jnp.float32))
        li_c = gc[:, :, d, 0] + i_bias[d]
        lf_c = jax.nn.log_sigmoid(gc[:, :, d, 1] + f_bias[d])
        li_l = gl[:, :, d, 0] + i_bias[d]
        lf_l = jax.nn.log_sigmoid(gl[:, :, d, 1] + f_bias[d])
        h_c, st = _mlstm_scan(_flip_seq(qc, d), _flip_seq(kc, d), _flip_seq(vc, d), _flip_seq(li_c, d),
                              _flip_seq(lf_c, d), state0, need_ctx_out)
        h_l, _ = _mlstm_scan(_flip_seq(ql, d), _flip_seq(kl, d), _flip_seq(vl, d), _flip_seq(li_l, d),
                             _flip_seq(lf_l, d), st, True)
        hl.append(_flip_seq(h_l, d))
        if need_ctx_out:
            hc.append(_flip_seq(h_c, d))
    y_l = jax.nn.sigmoid(ol) * (hl[0] + hl[1]).reshape(ol.shape)
    y_c = jax.nn.sigmoid(oc) * (hc[0] + hc[1]).reshape(oc.shape) if need_ctx_out else None
    return y_l, y_c


def _cols(p, start, width):
    return p[..., start:start + width]


def _mixers(pl_, pc_, cos, sin, need_ctx_out, lru_conv_w, lru_conv_b, lru_wa, lru_ba, lru_wx, lru_bx, lru_lambda,
            na_rpb, ssd_conv_w, ssd_conv_b, ssd_dt_bias, ssd_a_log, ssd_d, ssd_norm_g, ml_conv_w, ml_conv_b,
            ml_i_bias, ml_f_bias):
    w = BRANCH_W
    ya = _rglru_branch(_cols(pl_, COL_LRU_X, w), _cols(pl_, COL_LRU_G, w), _cols(pc_, COL_LRU_X, w),
                       _cols(pc_, COL_LRU_G, w), lru_conv_w, lru_conv_b, lru_wa, lru_ba, lru_wx, lru_bx,
                       lru_lambda, need_ctx_out)
    yb = na_branch(pl_, pc_, na_rpb, need_ctx_out)

    def ssd_parts(p):
        return (_cols(p, COL_SSD_Z, w), _cols(p, COL_SSD_X, w), _cols(p, COL_SSD_B, SSD_GN),
                _cols(p, COL_SSD_C, SSD_GN), _cols(p, COL_SSD_DT, 2 * SSD_HEADS))

    yc = _ssd_branch(ssd_parts(pl_), ssd_parts(pc_), ssd_conv_w, ssd_conv_b, ssd_dt_bias, ssd_a_log, ssd_d,
                     ssd_norm_g, need_ctx_out)

    def ml_parts(p):
        return (_cols(p, COL_ML, w), _cols(p, COL_ML + w, w), _cols(p, COL_ML + 2 * w, w),
                _cols(p, COL_ML + 3 * w, w), _cols(p, COL_ML_G, 4 * ML_HEADS))

    yd = _mlstm_branch(ml_parts(pl_), ml_parts(pc_), cos, sin, ml_conv_w, ml_conv_b, ml_i_bias, ml_f_bias,
                       need_ctx_out)
    return (ya[0], yb[0], yc[0], yd[0]), (ya[1], yb[1], yc[1], yd[1])


def kernel(x, c, ctx, c_ctx, w_ada, b_ada, w_in, lru_conv_w, lru_conv_b, lru_wa, lru_ba, lru_wx, lru_bx, lru_lambda, na_rpb, ssd_conv_w, ssd_conv_b, ssd_dt_bias, ssd_a_log, ssd_d, ssd_norm_g, ml_conv_w, ml_conv_b, ml_i_bias, ml_f_bias, w_branch, w_out, ln_g, ln_b, ffn_w_gate, ffn_w_up, ffn_w_down, moe_w_router, moe_b_router, moe_w_gate, moe_w_up, moe_w_down):
    depth = w_in.shape[0]
    bsz, seq, d = x.shape
    alpha = (2.0 * depth) ** 0.25
    cos, sin = _rope_2d_tables(seq, x.dtype)
    c_act = jax.nn.silu(c)
    cc_act = jax.nn.silu(c_ctx)[None, :]
    xl, xc = x, ctx
    for l in range(depth):
        need_ctx_out = l < depth - 1
        mod_l = jnp.split((c_act @ w_ada[l] + b_ada[l])[:, None, :], 6, axis=-1)
        mod_c = jnp.split(jnp.broadcast_to((cc_act @ w_ada[l] + b_ada[l])[:, None, :], (bsz, 1, 6 * d)), 6, axis=-1)
        w_in_p = _pack_w_in(w_in[l])
        p_l = mod_matmul(xl, mod_l[1], mod_l[0], w_in_p)
        p_c = mod_matmul(xc, mod_c[1], mod_c[0], w_in_p)
        br_l, br_c = _mixers(p_l, p_c, cos, sin, need_ctx_out, lru_conv_w[l], lru_conv_b[l], lru_wa[l], lru_ba[l],
                             lru_wx[l], lru_bx[l], lru_lambda[l], na_rpb[l], ssd_conv_w[l], ssd_conv_b[l],
                             ssd_dt_bias[l], ssd_a_log[l], ssd_d[l], ssd_norm_g[l], ml_conv_w[l], ml_conv_b[l],
                             ml_i_bias[l], ml_f_bias[l])
        wb = w_branch[l].astype(BF16)
        wo = w_out[l].astype(BF16)
        xl = proj_residual_ln(merge_branches(p_l, br_l, wb), wo, xl, mod_l[2], ln_g[l, 0], ln_b[l, 0], alpha)
        if need_ctx_out:
            xc = proj_residual_ln(merge_branches(p_c, br_c, wb), wo, xc, mod_c[2], ln_g[l, 0], ln_b[l, 0], alpha)
        j = l // 2
        if l % 2 == 0:
            wg = ffn_w_gate[j].astype(BF16)[None]
            wu = ffn_w_up[j].astype(BF16)[None]
            wd = ffn_w_down[j].astype(BF16)[None]

            def ffn(h, m, wg=wg, wu=wu, wd=wd):
                return ffn_residual_ln(h, m[4], m[3], m[5], None, wg, wu, wd, ln_g[l, 1], ln_b[l, 1], alpha, tf=512)
        else:
            wg = moe_w_gate[j].astype(BF16)
            wu = moe_w_up[j].astype(BF16)
            wd = moe_w_down[j].astype(BF16)

            def ffn(h, m, wg=wg, wu=wu, wd=wd, j=j):
                hm = h * (1.0 + m[4]) + m[3]
                logits = jnp.dot(hm, moe_w_router[j], precision=lax.Precision.HIGHEST) + moe_b_router[j]
                top_val, top_idx = lax.top_k(logits, TOP_K)
                probs = jax.nn.softmax(top_val, axis=-1)
                comb = jnp.einsum('...k,...ke->...e', probs, jax.nn.one_hot(top_idx, N_EXPERTS, dtype=probs.dtype))
                return ffn_residual_ln(h, m[4], m[3], m[5], comb, wg, wu, wd, ln_g[l, 1], ln_b[l, 1], alpha, tf=256)

        xl = ffn(xl, mod_l)
        if need_ctx_out:
            xc = ffn(xc, mod_c)
    return xl
```

```python
import functools
import math

import numpy as np
import jax
import jax.numpy as jnp
from jax import lax
from jax.experimental import pallas as pl
from jax.experimental.pallas import tpu as pltpu

F32 = jnp.float32
BF16 = jnp.bfloat16

D_MODEL = 2048
GRID_W = 64
N_BRANCH = 4
BRANCH_W = D_MODEL // N_BRANCH
CONV_W = 4
LN_EPS = 1e-5
LRU_BLOCKS = 8
LRU_BW = BRANCH_W // LRU_BLOCKS
LRU_C = 8.0
NA_HEADS = 8
NA_HD = BRANCH_W // NA_HEADS
NA_KH = 8
NA_KW = 16
SSD_HEADS = 8
SSD_HD = BRANCH_W // SSD_HEADS
SSD_GROUPS = 2
SSD_STATE = 64
SSD_CHUNK = 128
SSD_GN = SSD_GROUPS * SSD_STATE
ML_HEADS = 4
ML_HD = BRANCH_W // ML_HEADS
ML_CHUNK = 128
ROPE_BASE = 10000.0
N_EXPERTS = 8
TOP_K = 2

V7X_VMEM_LIMIT = 52 * 1024 * 1024

COL_GATES = 0
COL_LRU_X = 8192
COL_LRU_G = 8704
COL_NA = 9216
COL_SSD_Z = 10752
COL_SSD_X = 11264
COL_ML = 11776
COL_SSD_B = 13824
COL_SSD_C = 13952
COL_SSD_DT = 14080
COL_ML_G = 14096
N_IN_PAD = 14336


def _pack_w_in(w):
    parts = [w[:, 5920:14112], w[:, 0:1024], w[:, 1024:2560], w[:, 2560:3584], w[:, 3856:5904],
             w[:, 3584:3840], w[:, 3840:3856], w[:, 5904:5920],
             jnp.zeros((w.shape[0], N_IN_PAD - 14112), w.dtype)]
    return jnp.concatenate(parts, axis=1).astype(BF16)


def _sigmoid(x):
    return 1.0 / (1.0 + jnp.exp(-x))


def _layer_norm_rows(z, g, b):
    mu = jnp.mean(z, axis=-1, keepdims=True)
    zc = z - mu
    var = jnp.mean(zc * zc, axis=-1, keepdims=True)
    return zc * lax.rsqrt(var + LN_EPS) * g + b


def _mod_matmul_kernel(x_ref, sc_ref, sh_ref, w_ref, o_ref, hb_ref):
    @pl.when(pl.program_id(2) == 0)
    def _():
        hb_ref[...] = (x_ref[0] * (1.0 + sc_ref[0]) + sh_ref[0]).astype(BF16)

    o_ref[0] = jnp.dot(hb_ref[...], w_ref[...], preferred_element_type=F32)


def mod_matmul(x, sc, sh, w, tm=1024, tn=512):
    bsz, seq, d = x.shape
    n = w.shape[1]
    tm = min(tm, seq)
    return pl.pallas_call(
        _mod_matmul_kernel,
        grid=(bsz, seq // tm, n // tn),
        in_specs=[pl.BlockSpec((1, tm, d), lambda b, i, j: (b, i, 0)),
                  pl.BlockSpec((1, 1, d), lambda b, i, j: (b, 0, 0)),
                  pl.BlockSpec((1, 1, d), lambda b, i, j: (b, 0, 0)),
                  pl.BlockSpec((d, tn), lambda b, i, j: (0, j))],
        out_specs=pl.BlockSpec((1, tm, tn), lambda b, i, j: (b, i, j)),
        out_shape=jax.ShapeDtypeStruct((bsz, seq, n), F32),
        scratch_shapes=[pltpu.VMEM((tm, d), BF16)],
        compiler_params=pltpu.CompilerParams(
            dimension_semantics=("parallel", "parallel", "arbitrary"), vmem_limit_bytes=V7X_VMEM_LIMIT),
        name="in_proj",
    )(x, sc, sh, w)


def _merge_kernel(g0, g1, g2, g3, ya, yb, yc, yd, wb_ref, o_ref):
    acc = None
    for n, (g, y) in enumerate(((g0, ya), (g1, yb), (g2, yc), (g3, yd))):
        t = _sigmoid(g[0]) * jnp.dot(y[0].astype(BF16), wb_ref[n], preferred_element_type=F32)
        acc = t if acc is None else acc + t
    o_ref[0] = acc.astype(BF16)


def merge_branches(p, branches, wb, tm=512, tn=512):
    bsz, seq, _ = p.shape
    d = wb.shape[2]
    tm = min(tm, seq)
    nj = d // tn
    g_specs = [pl.BlockSpec((1, tm, tn), functools.partial(lambda b, i, j, n: (b, i, COL_GATES // tn + n * nj + j), n=n))
               for n in range(N_BRANCH)]
    y_specs = [pl.BlockSpec((1, tm, BRANCH_W), lambda b, i, j: (b, i, 0)) for _ in range(N_BRANCH)]
    return pl.pallas_call(
        _merge_kernel,
        grid=(bsz, seq // tm, nj),
        in_specs=g_specs + y_specs + [pl.BlockSpec((N_BRANCH, BRANCH_W, tn), lambda b, i, j: (0, 0, j))],
        out_specs=pl.BlockSpec((1, tm, tn), lambda b, i, j: (b, i, j)),
        out_shape=jax.ShapeDtypeStruct((bsz, seq, d), BF16),
        compiler_params=pltpu.CompilerParams(
            dimension_semantics=("parallel", "parallel", "arbitrary"), vmem_limit_bytes=V7X_VMEM_LIMIT),
        name="merge",
    )(p, p, p, p, *branches, wb)


def _proj_ln_kernel(m_ref, w_ref, x_ref, gt_ref, g_ref, b_ref, o_ref, *, alpha):
    y = jnp.dot(m_ref[0], w_ref[...], preferred_element_type=F32)
    o_ref[0] = _layer_norm_rows(alpha * x_ref[0] + gt_ref[0] * y, g_ref[...], b_ref[...])


def proj_residual_ln(m, w, x, gate, ln_g, ln_b, alpha, tm=256):
    bsz, seq, k = m.shape
    d = w.shape[1]
    tm = min(tm, seq)
    return pl.pallas_call(
        functools.partial(_proj_ln_kernel, alpha=alpha),
        grid=(bsz, seq // tm),
        in_specs=[pl.BlockSpec((1, tm, k), lambda b, i: (b, i, 0)),
                  pl.BlockSpec((k, d), lambda b, i: (0, 0)),
                  pl.BlockSpec((1, tm, d), lambda b, i: (b, i, 0)),
                  pl.BlockSpec((1, 1, d), lambda b, i: (b, 0, 0)),
                  pl.BlockSpec((1, d), lambda b, i: (0, 0)),
                  pl.BlockSpec((1, d), lambda b, i: (0, 0))],
        out_specs=pl.BlockSpec((1, tm, d), lambda b, i: (b, i, 0)),
        out_shape=jax.ShapeDtypeStruct((bsz, seq, d), F32),
        compiler_params=pltpu.CompilerParams(
            dimension_semantics=("parallel", "parallel"), vmem_limit_bytes=V7X_VMEM_LIMIT),
        name="out_proj_ln",
    )(m, w, x, gate, ln_g.reshape(1, d), ln_b.reshape(1, d))


def _ffn_kernel(x_ref, sc_ref, sh_ref, gt_ref, wg_ref, wu_ref, wd_ref, lg_ref, lb_ref, o_ref, hb_ref, acc_ref, *, alpha):
    j = pl.program_id(2)

    @pl.when(j == 0)
    def _():
        hb_ref[...] = (x_ref[0] * (1.0 + sc_ref[0]) + sh_ref[0]).astype(BF16)
        acc_ref[...] = jnp.zeros_like(acc_ref)

    hb = hb_ref[...]
    g = jnp.dot(hb, wg_ref[...], preferred_element_type=F32)
    u = jnp.dot(hb, wu_ref[...], preferred_element_type=F32)
    acc_ref[...] += jnp.dot((_silu(g) * u).astype(BF16), wd_ref[...], preferred_element_type=F32)

    @pl.when(j == pl.num_programs(2) - 1)
    def _():
        o_ref[0] = _layer_norm_rows(alpha * x_ref[0] + gt_ref[0] * acc_ref[...], lg_ref[...], lb_ref[...])


def ffn_residual_ln(x, sc, sh, gate, wg, wu, wd, ln_g, ln_b, alpha, tm=512, tf=512):
    bsz, seq, d = x.shape
    f = wg.shape[1]
    tm = min(tm, seq)
    vec = pl.BlockSpec((1, 1, d), lambda b, i, j: (b, 0, 0))
    par = pl.BlockSpec((1, d), lambda b, i, j: (0, 0))
    return pl.pallas_call(
        functools.partial(_ffn_kernel, alpha=alpha),
        grid=(bsz, seq // tm, f // tf),
        in_specs=[pl.BlockSpec((1, tm, d), lambda b, i, j: (b, i, 0)), vec, vec, vec,
                  pl.BlockSpec((d, tf), lambda b, i, j: (0, j)),
                  pl.BlockSpec((d, tf), lambda b, i, j: (0, j)),
                  pl.BlockSpec((tf, d), lambda b, i, j: (j, 0)), par, par],
        out_specs=pl.BlockSpec((1, tm, d), lambda b, i, j: (b, i, 0)),
        out_shape=jax.ShapeDtypeStruct((bsz, seq, d), F32),
        scratch_shapes=[pltpu.VMEM((tm, d), BF16), pltpu.VMEM((tm, d), F32)],
        compiler_params=pltpu.CompilerParams(
            dimension_semantics=("parallel", "parallel", "arbitrary"), vmem_limit_bytes=V7X_VMEM_LIMIT),
        name="ffn_ln",
    )(x, sc, sh, gate, wg, wu, wd, ln_g.reshape(1, d), ln_b.reshape(1, d))


def _ada_kernel(c_ref, w_ref, b_ref, o_ref):
    cv = c_ref[...]
    o_ref[...] = jnp.dot(cv * _sigmoid(cv), w_ref[...], precision=lax.Precision.HIGHEST,
                         preferred_element_type=F32) + b_ref[...]


def ada_modulation(cvecs, w, b, tn=1536):
    r, d = cvecs.shape
    n = w.shape[1]
    return pl.pallas_call(
        _ada_kernel,
        grid=(n // tn,),
        in_specs=[pl.BlockSpec((r, d), lambda j: (0, 0)), pl.BlockSpec((d, tn), lambda j: (0, j)),
                  pl.BlockSpec((1, tn), lambda j: (0, j))],
        out_specs=pl.BlockSpec((r, tn), lambda j: (0, j)),
        out_shape=jax.ShapeDtypeStruct((r, n), F32),
        compiler_params=pltpu.CompilerParams(dimension_semantics=("parallel",), vmem_limit_bytes=V7X_VMEM_LIMIT),
        name="ada_mod",
    )(cvecs, w, b.reshape(1, n))


NA_NEG = -1e30


def _na_bias_slabs(rpb):
    w = jnp.arange(GRID_W)
    cs = jnp.clip(w - NA_KW // 2, 0, GRID_W - NA_KW)
    ok = (w[None, :] >= cs[:, None]) & (w[None, :] < cs[:, None] + NA_KW)
    dc = jnp.clip(w[None, :] - w[:, None] + (NA_KW - 1), 0, 2 * NA_KW - 2)
    tab = jnp.where(ok, rpb[:, :, dc], NA_NEG)
    idx = jnp.arange(NA_KH)[:, None] + jnp.arange(NA_KH)[None, :]
    slab = tab[:, idx]
    return slab.transpose(1, 0, 3, 2, 4).reshape(NA_KH, NA_HEADS, GRID_W, NA_KH * GRID_W)


def _softmax2(s_a, s_b):
    m = jnp.maximum(jnp.max(s_a, axis=-1, keepdims=True), jnp.max(s_b, axis=-1, keepdims=True))
    e_a = jnp.exp(s_a - m)
    e_b = jnp.exp(s_b - m)
    inv = 1.0 / (jnp.sum(e_a, axis=-1, keepdims=True) + jnp.sum(e_b, axis=-1, keepdims=True))
    return e_a * inv, e_b * inv


_NT = (((1,), (1,)), ((), ()))


def _na_kernel(q_ref, kp_ref, kc_ref, kn_ref, vp_ref, vc_ref, vn_ref, ck_ref, cv_ref, bias_ref, o_ref,
               kw_ref, vw_ref, ckb_ref, cvb_ref, *, rows):
    i = pl.program_id(1)
    blk = NA_KH * GRID_W
    for n, (kr, vr) in enumerate(((kp_ref, vp_ref), (kc_ref, vc_ref), (kn_ref, vn_ref))):
        kw_ref[n * blk:(n + 1) * blk, :] = kr[0].astype(BF16)
        vw_ref[n * blk:(n + 1) * blk, :] = vr[0].astype(BF16)
    ckb_ref[...] = ck_ref[0].astype(BF16)
    cvb_ref[...] = cv_ref[0].astype(BF16)
    scale = NA_HD ** -0.5

    def body(rr, carry):
        r = i * NA_KH + rr
        rs = jnp.clip(r - NA_KH // 2, 0, rows - NA_KH)
        off = pl.multiple_of((rs - (i - 1) * NA_KH) * GRID_W, GRID_W)
        v = rs - r + (NA_KH - 1)
        q_all = (q_ref[0, pl.ds(pl.multiple_of(rr * GRID_W, GRID_W), GRID_W), :] * scale).astype(BF16)
        kwin = kw_ref[pl.ds(off, blk), :]
        vwin = vw_ref[pl.ds(off, blk), :]
        outs = []
        for h in range(NA_HEADS):
            hs = slice(h * NA_HD, (h + 1) * NA_HD)
            q = q_all[:, hs]
            s_w = lax.dot_general(q, kwin[:, hs], _NT, preferred_element_type=F32) + bias_ref[v, h]
            s_c = lax.dot_general(q, ckb_ref[:, hs], _NT, preferred_element_type=F32)
            p_w, p_c = _softmax2(s_w, s_c)
            outs.append(jnp.dot(p_w.astype(BF16), vwin[:, hs], preferred_element_type=F32)
                        + jnp.dot(p_c.astype(BF16), cvb_ref[:, hs], preferred_element_type=F32))
        o_ref[0, pl.ds(pl.multiple_of(rr * GRID_W, GRID_W), GRID_W), :] = jnp.concatenate(outs, axis=-1)
        return carry

    lax.fori_loop(0, NA_KH, body, 0)


def _na_ctx_kernel(q_ref, k_ref, v_ref, o_ref):
    scale = NA_HD ** -0.5
    q_all = (q_ref[0] * scale).astype(BF16)
    k_all = k_ref[0].astype(BF16)
    v_all = v_ref[0].astype(BF16)
    outs = []
    for h in range(NA_HEADS):
        hs = slice(h * NA_HD, (h + 1) * NA_HD)
        s = lax.dot_general(q_all[:, hs], k_all[:, hs], _NT, preferred_element_type=F32)
        e = jnp.exp(s - jnp.max(s, axis=-1, keepdims=True))
        p = e * (1.0 / jnp.sum(e, axis=-1, keepdims=True))
        outs.append(jnp.dot(p.astype(BF16), v_all[:, hs], preferred_element_type=F32))
    o_ref[0] = jnp.concatenate(outs, axis=-1)


def na_branch(p_l, p_c, rpb, need_ctx_out):
    bsz, seq, _ = p_l.shape
    n_ctx = p_c.shape[1]
    rows = seq // GRID_W
    assert rows % NA_KH == 0 and rows >= 2 * NA_KH
    w = BRANCH_W
    blk = NA_KH * GRID_W
    nblk = rows // NA_KH
    cq = COL_NA // w

    def shifted(col, delta):
        return pl.BlockSpec((1, blk, w), lambda b, i: (b, jnp.clip(i + delta, 0, nblk - 1), col))

    y_l = pl.pallas_call(
        functools.partial(_na_kernel, rows=rows),
        grid=(bsz, nblk),
        in_specs=[shifted(cq, 0), shifted(cq + 1, -1), shifted(cq + 1, 0), shifted(cq + 1, 1),
                  shifted(cq + 2, -1), shifted(cq + 2, 0), shifted(cq + 2, 1),
                  pl.BlockSpec((1, n_ctx, w), lambda b, i: (b, 0, cq + 1)),
                  pl.BlockSpec((1, n_ctx, w), lambda b, i: (b, 0, cq + 2)),
                  pl.BlockSpec((NA_KH, NA_HEADS, GRID_W, blk), lambda b, i: (0, 0, 0, 0))],
        out_specs=pl.BlockSpec((1, blk, w), lambda b, i: (b, i, 0)),
        out_shape=jax.ShapeDtypeStruct((bsz, seq, w), F32),
        scratch_shapes=[pltpu.VMEM((3 * blk, w), BF16), pltpu.VMEM((3 * blk, w), BF16),
                        pltpu.VMEM((n_ctx, w), BF16), pltpu.VMEM((n_ctx, w), BF16)],
        compiler_params=pltpu.CompilerParams(
            dimension_semantics=("parallel", "arbitrary"), vmem_limit_bytes=V7X_VMEM_LIMIT),
        name="na_attn",
    )(p_l, p_l, p_l, p_l, p_l, p_l, p_l, p_c, p_c, _na_bias_slabs(rpb))
    y_c = None
    if need_ctx_out:
        y_c = pl.pallas_call(
            _na_ctx_kernel,
            grid=(bsz,),
            in_specs=[pl.BlockSpec((1, n_ctx, w), functools.partial(lambda b, c: (b, 0, c), c=cq + n)) for n in range(3)],
            out_specs=pl.BlockSpec((1, n_ctx, w), lambda b: (b, 0, 0)),
            out_shape=jax.ShapeDtypeStruct((bsz, n_ctx, w), F32),
            compiler_params=pltpu.CompilerParams(dimension_semantics=("parallel",)),
            name="na_ctx_attn",
        )(p_c, p_c, p_c)
    return y_l, y_c


def _gelu_tanh(x):
    return 0.5 * x * (1.0 + jnp.tanh(math.sqrt(2.0 / math.pi) * (x + 0.044715 * (x * x * x))))


def _silu(x):
    return x * _sigmoid(x)


def _softplus(x):
    return jnp.maximum(x, 0.0) + jnp.log1p(jnp.exp(-jnp.abs(x)))


def _rope_2d_tables(seq):
    t = jnp.arange(seq, dtype=jnp.int32)
    pos = jnp.stack([t // GRID_W, t % GRID_W], axis=-1).astype(F32)
    nf = ML_HD // 4
    inv_freq = ROPE_BASE ** (-jnp.arange(nf, dtype=F32) / nf)
    ang = jnp.broadcast_to(pos[:, :, None, None] * inv_freq, (seq, 2, 2, nf)).reshape(seq, ML_HD)
    return jnp.cos(ang), jnp.sin(ang)


def _chunk_specs(tt, nchunk, n8, reverse):
    def pos(j):
        return nchunk - 1 - j if reverse else j

    def chunk(width, col):
        return pl.BlockSpec((1, tt, width), lambda b, j: (b, pos(j), col))

    def halo(width, col, delta):
        if delta < 0:
            return pl.BlockSpec((1, 8, width), lambda b, j: (b, jnp.maximum(pos(j) * (tt // 8) - 1, 0), col))
        return pl.BlockSpec((1, 8, width), lambda b, j: (b, jnp.minimum((pos(j) + 1) * (tt // 8), n8 - 1), col))

    return chunk, halo


def _const_spec(shape):
    return pl.BlockSpec(shape, lambda b, j: (0,) * len(shape))


def _batch_spec(shape):
    return pl.BlockSpec((1,) + shape, lambda b, j: (b,) + (0,) * len(shape))


def _conv4(x, prev8, next8, has_prev, has_next, cw, cb):
    tt = x.shape[0]
    row = lax.broadcasted_iota(jnp.int32, x.shape, 0)
    p6 = prev8[6:7, :] * has_prev
    p7 = prev8[7:8, :] * has_prev
    n0 = next8[0:1, :] * has_next
    xm1 = jnp.where(row == 0, p7, pltpu.roll(x, 1, 0))
    xm2 = jnp.where(row == 0, p6, jnp.where(row == 1, p7, pltpu.roll(x, 2, 0)))
    xp1 = jnp.where(row == tt - 1, n0, pltpu.roll(x, tt - 1, 0))
    return cw[0:1, :] * xm2 + cw[1:2, :] * xm1 + cw[2:3, :] * x + cw[3:4, :] * xp1 + cb


def _cumsum_rows(x, reverse):
    n = x.shape[0]
    row = lax.broadcasted_iota(jnp.int32, x.shape, 0)
    s = 1
    while s < n:
        if reverse:
            x = x + jnp.where(row < n - s, pltpu.roll(x, n - s, 0), 0.0)
        else:
            x = x + jnp.where(row >= s, pltpu.roll(x, s, 0), 0.0)
        s *= 2
    return x


def _causal_mask(n, reverse):
    ii = lax.broadcasted_iota(jnp.int32, (n, n), 0)
    jj = lax.broadcasted_iota(jnp.int32, (n, n), 1)
    return (jj >= ii) if reverse else (jj <= ii)


MASK_NEG = -1e30


LRU_TT = 256


def _lru_kernel(*refs, reverse, finalize, nchunk):
    if finalize:
        (xp_ref, x_ref, xn_ref, g_ref, ho_ref, h0_ref, cw_ref, cb_ref, w_ref, bias_ref, sp_ref,
         o_ref, hl_ref, carry_ref) = refs
    else:
        (xp_ref, x_ref, xn_ref, h0_ref, cw_ref, cb_ref, w_ref, bias_ref, sp_ref, o_ref, hl_ref, carry_ref) = refs
    j = pl.program_id(1)
    c = (nchunk - 1 - j) if reverse else j

    @pl.when(j == 0)
    def _():
        carry_ref[...] = h0_ref[0]

    tt = x_ref.shape[1]
    w = x_ref.shape[2]
    xc = _conv4(x_ref[0], xp_ref[0], xn_ref[0], (c > 0).astype(F32), (c < nchunk - 1).astype(F32),
                cw_ref[...], cb_ref[...])
    g = jnp.dot(xc.astype(BF16), w_ref[...], preferred_element_type=F32)
    r = _sigmoid(g[:, :w] + bias_ref[0:1, :])
    ig = _sigmoid(g[:, w:] + bias_ref[1:2, :])
    log_a = -LRU_C * r * sp_ref[...]
    a = jnp.exp(log_a)
    u = jnp.sqrt(1.0 - jnp.exp(2.0 * log_a)) * (ig * xc)
    row = lax.broadcasted_iota(jnp.int32, (tt, w), 0)
    s = 1
    while s < tt:
        if reverse:
            keep = row < tt - s
            a_s = jnp.where(keep, pltpu.roll(a, tt - s, 0), 1.0)
            u_s = jnp.where(keep, pltpu.roll(u, tt - s, 0), 0.0)
        else:
            keep = row >= s
            a_s = jnp.where(keep, pltpu.roll(a, s, 0), 1.0)
            u_s = jnp.where(keep, pltpu.roll(u, s, 0), 0.0)
        u = a * u_s + u
        a = a * a_s
        s *= 2
    h = u + a * carry_ref[...]
    new_carry = h[0:1, :] if reverse else h[tt - 1:tt, :]
    carry_ref[...] = new_carry
    hl_ref[0] = new_carry
    if finalize:
        o_ref[0] = (ho_ref[0] + h) * _gelu_tanh(g_ref[0])
    else:
        o_ref[0] = h


def _lru_pass(p, h0, other, cw, cb, wcat, bias, sp, reverse):
    bsz, seq, _ = p.shape
    w = BRANCH_W
    tt = min(LRU_TT, seq)
    nchunk = seq // tt
    chunk, halo = _chunk_specs(tt, nchunk, seq // 8, reverse)
    cx = COL_LRU_X // w
    finalize = other is not None
    in_specs = [halo(w, cx, -1), chunk(w, cx), halo(w, cx, 1)]
    args = [p, p, p]
    if finalize:
        in_specs += [chunk(w, COL_LRU_G // w), chunk(w, 0)]
        args += [p, other]
    in_specs += [_batch_spec((1, w)), _const_spec((CONV_W, w)), _const_spec((1, w)),
                 _const_spec((w, 2 * w)), _const_spec((2, w)), _const_spec((1, w))]
    args += [h0, cw, cb, wcat, bias, sp]
    return pl.pallas_call(
        functools.partial(_lru_kernel, reverse=reverse, finalize=finalize, nchunk=nchunk),
        grid=(bsz, nchunk),
        in_specs=in_specs,
        out_specs=[chunk(w, 0), _batch_spec((1, w))],
        out_shape=[jax.ShapeDtypeStruct((bsz, seq, w), F32), jax.ShapeDtypeStruct((bsz, 1, w), F32)],
        scratch_shapes=[pltpu.VMEM((1, w), F32)],
        compiler_params=pltpu.CompilerParams(dimension_semantics=("parallel", "arbitrary")),
        name="lru_scan",
    )(*args)


def _block_diag(wg):
    g, n, _ = wg.shape
    eye = jnp.eye(g, dtype=wg.dtype)
    return (wg[:, :, None, :] * eye[:, None, :, None]).reshape(g * n, g * n)


def lru_branch(p_l, p_c, conv_w, conv_b, wa, ba, wx, bx, lam, need_ctx_out):
    bsz = p_l.shape[0]
    w = BRANCH_W
    cb = conv_b.reshape(1, w)
    sp = jax.nn.softplus(-lam)
    zeros = jnp.zeros((bsz, 1, w), F32)
    h_c = h_l = None
    for d, reverse in ((0, False), (1, True)):
        wcat = jnp.concatenate([_block_diag(wa[d]), _block_diag(wx[d])], axis=1).astype(BF16)
        bias = jnp.stack([ba[d], bx[d]])
        other_c = h_c if (d == 1 and need_ctx_out) else None
        h_c, st = _lru_pass(p_c, zeros, other_c, conv_w, cb, wcat, bias, sp[d:d + 1], reverse)
        h_l, _ = _lru_pass(p_l, st, h_l if d == 1 else None, conv_w, cb, wcat, bias, sp[d:d + 1], reverse)
    return h_l, (h_c if need_ctx_out else None)


def _ssd_kernel(*refs, reverse, finalize, nchunk, d):
    (xp_ref, x_ref, xn_ref, bp_ref, bc_ref, bn_ref, dt_ref) = refs[:7]
    k = 7
    if finalize:
        z_ref, yp_ref = refs[k:k + 2]
        k += 2
    s0_ref, cwx_ref, cbx_ref, cwb_ref, cbb_ref, dtb_ref, a_ref = refs[k:k + 7]
    k += 7
    if finalize:
        dsk_ref, ng_ref = refs[k:k + 2]
        k += 2
    o_ref, so_ref, s_ref = refs[k:k + 3]
    j = pl.program_id(1)
    c = (nchunk - 1 - j) if reverse else j

    @pl.when(j == 0)
    def _():
        s_ref[...] = s0_ref[0]

    q = x_ref.shape[1]
    has_prev = (c > 0).astype(F32)
    has_next = (c < nchunk - 1).astype(F32)
    xs = _silu(_conv4(x_ref[0], xp_ref[0], xn_ref[0], has_prev, has_next, cwx_ref[...], cbx_ref[...]))
    bc = _silu(_conv4(bc_ref[0], bp_ref[0], bn_ref[0], has_prev, has_next, cwb_ref[...], cbb_ref[...]))
    dt = _softplus(dt_ref[0] + dtb_ref[...])
    cum = _cumsum_rows(dt * a_ref[...], reverse)
    tot = cum[0:1, :] if reverse else cum[q - 1:q, :]
    w_end = jnp.exp(tot - cum) * dt
    ecum = jnp.exp(cum)
    etot = jnp.exp(tot)
    dt_t = dt.T
    cum_t = cum.T
    b_t = bc[:, :SSD_GN].T
    mask = _causal_mask(q, reverse)
    xb = xs.astype(BF16)
    cgs = [bc[:, SSD_GN + g * SSD_STATE:SSD_GN + (g + 1) * SSD_STATE].astype(BF16) for g in range(SSD_GROUPS)]
    cbs = [lax.dot_general(cgs[g], bc[:, g * SSD_STATE:(g + 1) * SSD_STATE].astype(BF16), _NT,
                           preferred_element_type=F32) for g in range(SSD_GROUPS)]
    outs = []
    for h in range(SSD_HEADS):
        g = h // (SSD_HEADS // SSD_GROUPS)
        col = d * SSD_HEADS + h
        hs = slice(h * SSD_HD, (h + 1) * SSD_HD)
        diff = cum[:, col:col + 1] - cum_t[col:col + 1, :]
        m = cbs[g] * jnp.exp(jnp.where(mask, diff, MASK_NEG)) * dt_t[col:col + 1, :]
        s_old = s_ref[h]
        y = (jnp.dot(m.astype(BF16), xb[:, hs], preferred_element_type=F32)
             + ecum[:, col:col + 1] * jnp.dot(cgs[g], s_old.astype(BF16), preferred_element_type=F32))
        xw = (xs[:, hs] * w_end[:, col:col + 1]).astype(BF16)
        s_ref[h] = etot[:, col:col + 1] * s_old + jnp.dot(
            b_t[g * SSD_STATE:(g + 1) * SSD_STATE, :].astype(BF16), xw, preferred_element_type=F32)
        outs.append(y)
    y = jnp.concatenate(outs, axis=-1)
    so_ref[0] = s_ref[...]
    if finalize:
        yt = (xs * dsk_ref[...] + yp_ref[0] + y) * _silu(z_ref[0])
        o_ref[0] = yt * lax.rsqrt(jnp.mean(yt * yt, axis=-1, keepdims=True) + LN_EPS) * ng_ref[...]
    else:
        o_ref[0] = y


def _ssd_pass(p, s0, other, params, reverse, d):
    bsz, seq, _ = p.shape
    w = BRANCH_W
    q = SSD_CHUNK
    nchunk = seq // q
    chunk, halo = _chunk_specs(q, nchunk, seq // 8, reverse)
    cx, cb2, cdt = COL_SSD_X // w, COL_SSD_B // (2 * SSD_GN), COL_SSD_DT // 128
    finalize = other is not None
    cwx, cbx, cwb, cbb, dtb, arow, dsk, ng = params
    in_specs = [halo(w, cx, -1), chunk(w, cx), halo(w, cx, 1),
                halo(2 * SSD_GN, cb2, -1), chunk(2 * SSD_GN, cb2), halo(2 * SSD_GN, cb2, 1), chunk(128, cdt)]
    args = [p] * 7
    if finalize:
        in_specs += [chunk(w, COL_SSD_Z // w), chunk(w, 0)]
        args += [p, other]
    st_shape = (SSD_HEADS, SSD_STATE, SSD_HD)
    in_specs += [_batch_spec(st_shape), _const_spec((CONV_W, w)), _const_spec((1, w)),
                 _const_spec((CONV_W, 2 * SSD_GN)), _const_spec((1, 2 * SSD_GN)), _const_spec((1, 128)),
                 _const_spec((1, 128))]
    args += [s0, cwx, cbx, cwb, cbb, dtb, arow]
    if finalize:
        in_specs += [_const_spec((1, w)), _const_spec((1, w))]
        args += [dsk, ng]
    return pl.pallas_call(
        functools.partial(_ssd_kernel, reverse=reverse, finalize=finalize, nchunk=nchunk, d=d),
        grid=(bsz, nchunk),
        in_specs=in_specs,
        out_specs=[chunk(w, 0), _batch_spec(st_shape)],
        out_shape=[jax.ShapeDtypeStruct((bsz, seq, w), F32), jax.ShapeDtypeStruct((bsz,) + st_shape, F32)],
        scratch_shapes=[pltpu.VMEM(st_shape, F32)],
        compiler_params=pltpu.CompilerParams(dimension_semantics=("parallel", "arbitrary")),
        name="ssd_scan",
    )(*args)


def _lane_row(vals, start):
    return jnp.zeros((128,), F32).at[start:start + vals.shape[0]].set(vals.astype(F32)).reshape(1, 128)


def ssd_branch(p_l, p_c, conv_w, conv_b, dt_bias, a_log, d_skip, norm_g, need_ctx_out):
    bsz = p_l.shape[0]
    w = BRANCH_W
    params = (conv_w[:, :w], conv_b[:w].reshape(1, w), conv_w[:, w:], conv_b[w:].reshape(1, 2 * SSD_GN),
              _lane_row(dt_bias.reshape(-1), 0), _lane_row(-jnp.exp(a_log.astype(F32)).reshape(-1), 0),
              jnp.repeat(d_skip, SSD_HD).reshape(1, w), norm_g.reshape(1, w))
    zeros = jnp.zeros((bsz, SSD_HEADS, SSD_STATE, SSD_HD), F32)
    y_c = y_l = None
    for d, reverse in ((0, False), (1, True)):
        other_c = y_c if (d == 1 and need_ctx_out) else None
        y_c, st = _ssd_pass(p_c, zeros, other_c, params, reverse, d)
        y_l, _ = _ssd_pass(p_l, st, y_l if d == 1 else None, params, reverse, d)
    return y_l, (y_c if need_ctx_out else None)


def _rope_rotate(x):
    wl = x.shape[-1]
    half = ML_HD // 4
    lane = lax.broadcasted_iota(jnp.int32, x.shape, 1)
    return jnp.where(lane % (2 * half) < half, -pltpu.roll(x, wl - half, 1), pltpu.roll(x, half, 1))


def _log_sigmoid(x):
    return jnp.minimum(x, 0.0) - jnp.log1p(jnp.exp(-jnp.abs(x)))


def _mlstm_kernel(*refs, reverse, finalize, rope, nchunk, d):
    (qp_ref, q_ref, qn_ref, kp_ref, k_ref, kn_ref, v_ref, g_ref) = refs[:8]
    n = 8
    if rope:
        cos_ref, sin_ref = refs[n:n + 2]
        n += 2
    if finalize:
        og_ref, hp_ref = refs[n:n + 2]
        n += 2
    c0_ref, n0_ref, m0_ref, cwq_ref, cbq_ref, cwk_ref, cbk_ref, ib_ref, fb_ref = refs[n:n + 9]
    n += 9
    o_ref, co_ref, no_ref, mo_ref, c_ref, n_ref, m_ref = refs[n:n + 7]
    j = pl.program_id(1)
    c = (nchunk - 1 - j) if reverse else j

    @pl.when(j == 0)
    def _():
        c_ref[...] = c0_ref[0]
        n_ref[...] = n0_ref[0]
        m_ref[...] = m0_ref[0]

    qn = q_ref.shape[1]
    has_prev = (c > 0).astype(F32)
    has_next = (c < nchunk - 1).astype(F32)
    q = _silu(_conv4(q_ref[0], qp_ref[0], qn_ref[0], has_prev, has_next, cwq_ref[...], cbq_ref[...]))
    k = _silu(_conv4(k_ref[0], kp_ref[0], kn_ref[0], has_prev, has_next, cwk_ref[...], cbk_ref[...]))
    if rope:
        cos = cos_ref[...]
        sin = sin_ref[...]
        q = q * cos + _rope_rotate(q) * sin
        k = k * cos + _rope_rotate(k) * sin
    q = q * (ML_HD ** -0.5)
    v = v_ref[0]
    gb = g_ref[0]
    li = gb + ib_ref[...]
    b = _cumsum_rows(_log_sigmoid(gb + fb_ref[...]), reverse)
    tot = b[0:1, :] if reverse else b[qn - 1:qn, :]
    b_t = b.T
    li_t = li.T
    mask = _causal_mask(qn, reverse)
    outs = []
    for h in range(ML_HEADS):
        ci = 4 * ML_HEADS + d * 2 * ML_HEADS + h
        cf = ci + ML_HEADS
        hs = slice(h * ML_HD, (h + 1) * ML_HD)
        b_c = b[:, cf:cf + 1]
        b_end = tot[:, cf:cf + 1]
        m_st = m_ref[h:h + 1, 0:1]
        end_log = b_end - b_c + li[:, ci:ci + 1]
        m_new = jnp.maximum(b_end + m_st, jnp.max(end_log, axis=0, keepdims=True))
        w = jnp.exp(end_log - m_new)
        carry_scale = jnp.exp(b_end + m_st - m_new)
        dlog = jnp.where(mask, b_c - b_t[cf:cf + 1, :] + li_t[ci:ci + 1, :], MASK_NEG)
        m_inter = b_c + m_st
        m_i = jnp.maximum(jnp.max(dlog, axis=1, keepdims=True), m_inter)
        qh = q[:, hs].astype(BF16)
        kh = k[:, hs].astype(BF16)
        s = lax.dot_general(qh, kh, _NT, preferred_element_type=F32) * jnp.exp(dlog - m_i)
        w_in = jnp.exp(m_inter - m_i)
        c_old = c_ref[h]
        n_old = n_ref[h:h + 1, :]
        num = (jnp.dot(s.astype(BF16), v[:, hs].astype(BF16), preferred_element_type=F32)
               + w_in * lax.dot_general(qh, c_old.astype(BF16), _NT, preferred_element_type=F32))
        den = jnp.sum(s, axis=1, keepdims=True) + w_in * jnp.sum(q[:, hs] * n_old, axis=1, keepdims=True)
        outs.append(num / jnp.maximum(jnp.abs(den), jnp.exp(-m_i)))
        c_ref[h] = carry_scale * c_old + jnp.dot((v[:, hs] * w).T.astype(BF16), kh, preferred_element_type=F32)
        n_ref[h:h + 1, :] = carry_scale * n_old + jnp.sum(k[:, hs] * w, axis=0, keepdims=True)
        m_ref[h:h + 1, :] = jnp.broadcast_to(m_new, (1, ML_HD))
    hout = jnp.concatenate(outs, axis=-1)
    co_ref[0] = c_ref[...]
    no_ref[0] = n_ref[...]
    mo_ref[0] = m_ref[...]
    if finalize:
        o_ref[0] = _sigmoid(og_ref[0]) * (hp_ref[0] + hout)
    else:
        o_ref[0] = hout


def _mlstm_pass(p, state, other, tables, params, reverse, d):
    bsz, seq, _ = p.shape
    w = BRANCH_W
    qn = ML_CHUNK
    nchunk = seq // qn
    chunk, halo = _chunk_specs(qn, nchunk, seq // 8, reverse)
    cq = COL_ML // w
    finalize = other is not None
    rope = tables is not None
    in_specs = [halo(w, cq, -1), chunk(w, cq), halo(w, cq, 1), halo(w, cq + 1, -1), chunk(w, cq + 1),
                halo(w, cq + 1, 1), chunk(w, cq + 2), chunk(128, COL_ML_G // 128)]
    args = [p] * 8
    if rope:
        tab = pl.BlockSpec((qn, w), (lambda b, j: (nchunk - 1 - j, 0)) if reverse else (lambda b, j: (j, 0)))
        in_specs += [tab, tab]
        args += list(tables)
    if finalize:
        in_specs += [chunk(w, cq + 3), chunk(w, 0)]
        args += [p, other]
    st_shapes = [(ML_HEADS, ML_HD, ML_HD), (ML_HEADS, ML_HD), (ML_HEADS, ML_HD)]
    in_specs += [_batch_spec(s) for s in st_shapes]
    in_specs += [_const_spec((CONV_W, w)), _const_spec((1, w)), _const_spec((CONV_W, w)), _const_spec((1, w)),
                 _const_spec((1, 128)), _const_spec((1, 128))]
    args += list(state) + list(params)
    res = pl.pallas_call(
        functools.partial(_mlstm_kernel, reverse=reverse, finalize=finalize, rope=rope, nchunk=nchunk, d=d),
        grid=(bsz, nchunk),
        in_specs=in_specs,
        out_specs=[chunk(w, 0)] + [_batch_spec(s) for s in st_shapes],
        out_shape=[jax.ShapeDtypeStruct((bsz, seq, w), F32)]
                  + [jax.ShapeDtypeStruct((bsz,) + s, F32) for s in st_shapes],
        scratch_shapes=[pltpu.VMEM(s, F32) for s in st_shapes],
        compiler_params=pltpu.CompilerParams(dimension_semantics=("parallel", "arbitrary")),
        name="mlstm_scan",
    )(*args)
    return res[0], tuple(res[1:])


def mlstm_branch(p_l, p_c, conv_w, conv_b, i_bias, f_bias, need_ctx_out):
    bsz, seq, _ = p_l.shape
    w = BRANCH_W
    cos, sin = _rope_2d_tables(seq)
    tables = (jnp.tile(cos, (1, ML_HEADS)), jnp.tile(sin, (1, ML_HEADS)))
    zero_h = jnp.zeros_like(i_bias)
    ib = _lane_row(jnp.concatenate([i_bias, zero_h], axis=1).reshape(-1), 4 * ML_HEADS)
    fb = _lane_row(jnp.concatenate([zero_h, f_bias], axis=1).reshape(-1), 4 * ML_HEADS)
    params = (conv_w[:, :w], conv_b[:w].reshape(1, w), conv_w[:, w:], conv_b[w:].reshape(1, w), ib, fb)
    state0 = (jnp.zeros((bsz, ML_HEADS, ML_HD, ML_HD), F32), jnp.zeros((bsz, ML_HEADS, ML_HD), F32),
              jnp.zeros((bsz, ML_HEADS, ML_HD), F32))
    h_c = h_l = None
    for d, reverse in ((0, False), (1, True)):
        other_c = h_c if (d == 1 and need_ctx_out) else None
        h_c, st = _mlstm_pass(p_c, state0, other_c, None, params, reverse, d)
        h_l, _ = _mlstm_pass(p_l, st, h_l if d == 1 else None, tables, params, reverse, d)
    return h_l, (h_c if need_ctx_out else None)


MOE_TM = 512
ROUTE_LANES = 128


def _moe_router_kernel(x_ref, sc_ref, sh_ref, wr_ref, br_ref, hb_ref, rt_ref):
    h = x_ref[0] * (1.0 + sc_ref[0]) + sh_ref[0]
    hb_ref[0] = h.astype(BF16)
    logits = jnp.dot(h, wr_ref[...], precision=lax.Precision.HIGHEST, preferred_element_type=F32) + br_ref[...]
    lane = lax.broadcasted_iota(jnp.int32, logits.shape, 1)
    m1 = jnp.max(logits, axis=-1, keepdims=True)
    i1 = jnp.min(jnp.where(logits == m1, lane, ROUTE_LANES), axis=-1, keepdims=True)
    rest = jnp.where(lane == i1, MASK_NEG, logits)
    m2 = jnp.max(rest, axis=-1, keepdims=True)
    i2 = jnp.min(jnp.where(rest == m2, lane, ROUTE_LANES), axis=-1, keepdims=True)
    e2 = jnp.exp(m2 - m1)
    p1 = 1.0 / (1.0 + e2)
    p2 = e2 * p1
    rt_ref[0] = jnp.where(lane == 0, i1.astype(F32), jnp.where(lane == 1, i2.astype(F32),
                          jnp.where(lane == 2, p1, jnp.where(lane == 3, p2, 0.0))))


def moe_router(x, sc, sh, w_router, b_router, tm=512):
    bsz, seq, d = x.shape
    n_e = w_router.shape[1]
    tm = min(tm, seq)
    wr = jnp.zeros((d, ROUTE_LANES), F32).at[:, :n_e].set(w_router)
    br = jnp.full((1, ROUTE_LANES), MASK_NEG, F32).at[0, :n_e].set(b_router)
    return pl.pallas_call(
        _moe_router_kernel,
        grid=(bsz, seq // tm),
        in_specs=[pl.BlockSpec((1, tm, d), lambda b, i: (b, i, 0)),
                  pl.BlockSpec((1, 1, d), lambda b, i: (b, 0, 0)),
                  pl.BlockSpec((1, 1, d), lambda b, i: (b, 0, 0)),
                  pl.BlockSpec((d, ROUTE_LANES), lambda b, i: (0, 0)),
                  pl.BlockSpec((1, ROUTE_LANES), lambda b, i: (0, 0))],
        out_specs=[pl.BlockSpec((1, tm, d), lambda b, i: (b, i, 0)),
                   pl.BlockSpec((1, tm, ROUTE_LANES), lambda b, i: (b, i, 0))],
        out_shape=[jax.ShapeDtypeStruct((bsz, seq, d), BF16), jax.ShapeDtypeStruct((bsz, seq, ROUTE_LANES), F32)],
        compiler_params=pltpu.CompilerParams(dimension_semantics=("parallel", "parallel")),
        name="moe_router",
    )(x, sc, sh, wr, br)


def _route_tables(idx, n_e, tmg):
    n_tok = idx.shape[0]
    e_flat = idx.reshape(-1)
    onehot = (e_flat[:, None] == jnp.arange(n_e, dtype=jnp.int32)[None, :]).astype(jnp.int32)
    csum = jnp.cumsum(onehot, axis=0)
    rank = jnp.take_along_axis(csum - onehot, e_flat[:, None], axis=1)[:, 0]
    padded = ((csum[-1] + tmg - 1) // tmg) * tmg
    ends = jnp.cumsum(padded)
    pos = (ends - padded)[e_flat] + rank
    n_rows = TOP_K * n_tok + n_e * tmg
    ntiles = n_rows // tmg
    tile_start = jnp.arange(ntiles, dtype=jnp.int32) * tmg
    tile_expert = jnp.minimum(jnp.sum(tile_start[:, None] >= ends[None, :], axis=1), n_e - 1)
    src = jnp.zeros((n_rows,), jnp.int32).at[pos].set(jnp.arange(TOP_K * n_tok, dtype=jnp.int32) // TOP_K)
    meta = jnp.concatenate([tile_expert, ends[-1:] // tmg]).astype(jnp.int32)
    return pos.reshape(n_tok, TOP_K), src, meta


def _moe_up_kernel(meta_ref, x_ref, wg_ref, wu_ref, a_ref, *, ntiles):
    i = pl.program_id(1)

    @pl.when(i < meta_ref[ntiles])
    def _():
        x = x_ref[...]
        g = jnp.dot(x, wg_ref[0], preferred_element_type=F32)
        u = jnp.dot(x, wu_ref[0], preferred_element_type=F32)
        a_ref[...] = (_silu(g) * u).astype(BF16)

    @pl.when(i >= meta_ref[ntiles])
    def _():
        a_ref[...] = jnp.zeros_like(a_ref)


def _moe_down_kernel(meta_ref, a_ref, wd_ref, y_ref, *, ntiles):
    i = pl.program_id(0)

    @pl.when(i < meta_ref[ntiles])
    def _():
        y_ref[...] = jnp.dot(a_ref[...], wd_ref[0], preferred_element_type=F32)

    @pl.when(i >= meta_ref[ntiles])
    def _():
        y_ref[...] = jnp.zeros_like(y_ref)


def moe_experts(xs, meta, wg, wu, wd, tmg, tf):
    n_rows, d = xs.shape
    n_e, _, f = wg.shape
    ntiles = n_rows // tmg
    nf = f // tf
    a = pl.pallas_call(
        functools.partial(_moe_up_kernel, ntiles=ntiles),
        grid_spec=pltpu.PrefetchScalarGridSpec(
            num_scalar_prefetch=1,
            grid=(nf, ntiles),
            in_specs=[pl.BlockSpec((tmg, d), lambda j, i, m: (i, 0)),
                      pl.BlockSpec((1, d, tf), lambda j, i, m: (m[i], 0, j)),
                      pl.BlockSpec((1, d, tf), lambda j, i, m: (m[i], 0, j))],
            out_specs=pl.BlockSpec((tmg, tf), lambda j, i, m: (i, j))),
        out_shape=jax.ShapeDtypeStruct((n_rows, f), BF16),
        compiler_params=pltpu.CompilerParams(
            dimension_semantics=("arbitrary", "arbitrary"), vmem_limit_bytes=V7X_VMEM_LIMIT),
        name="moe_up",
    )(meta, xs, wg, wu)
    return pl.pallas_call(
        functools.partial(_moe_down_kernel, ntiles=ntiles),
        grid_spec=pltpu.PrefetchScalarGridSpec(
            num_scalar_prefetch=1,
            grid=(ntiles,),
            in_specs=[pl.BlockSpec((tmg, f), lambda i, m: (i, 0)),
                      pl.BlockSpec((1, f, d), lambda i, m: (m[i], 0, 0))],
            out_specs=pl.BlockSpec((tmg, d), lambda i, m: (i, 0))),
        out_shape=jax.ShapeDtypeStruct((n_rows, d), F32),
        compiler_params=pltpu.CompilerParams(
            dimension_semantics=("arbitrary",), vmem_limit_bytes=V7X_VMEM_LIMIT),
        name="moe_down",
    )(meta, a, wd)


def _combine_ln_kernel(x_ref, gt_ref, rt_ref, ya_ref, yb_ref, g_ref, b_ref, o_ref, *, alpha):
    rt = rt_ref[0]
    y = rt[:, 2:3] * ya_ref[0] + rt[:, 3:4] * yb_ref[0]
    o_ref[0] = _layer_norm_rows(alpha * x_ref[0] + gt_ref[0] * y, g_ref[...], b_ref[...])


def combine_residual_ln(x, gate, route, ya, yb, ln_g, ln_b, alpha, tm=512):
    bsz, seq, d = x.shape
    tm = min(tm, seq)
    row = pl.BlockSpec((1, tm, d), lambda b, i: (b, i, 0))
    return pl.pallas_call(
        functools.partial(_combine_ln_kernel, alpha=alpha),
        grid=(bsz, seq // tm),
        in_specs=[row, pl.BlockSpec((1, 1, d), lambda b, i: (b, 0, 0)),
                  pl.BlockSpec((1, tm, ROUTE_LANES), lambda b, i: (b, i, 0)), row, row,
                  pl.BlockSpec((1, d), lambda b, i: (0, 0)), pl.BlockSpec((1, d), lambda b, i: (0, 0))],
        out_specs=row,
        out_shape=jax.ShapeDtypeStruct((bsz, seq, d), F32),
        compiler_params=pltpu.CompilerParams(
            dimension_semantics=("parallel", "parallel"), vmem_limit_bytes=V7X_VMEM_LIMIT),
        name="moe_combine_ln",
    )(x, gate, route, ya, yb, ln_g.reshape(1, d), ln_b.reshape(1, d))


def moe_residual_ln(x, sc, sh, gate, w_router, b_router, wg, wu, wd, ln_g, ln_b, alpha):
    bsz, seq, d = x.shape
    n_e, _, f = wg.shape
    n_tok = bsz * seq
    tmg = min(MOE_TM, TOP_K * n_tok)
    hb, route = moe_router(x, sc, sh, w_router, b_router)
    idx = route[..., :TOP_K].astype(jnp.int32).reshape(n_tok, TOP_K)
    pos, src, meta = _route_tables(idx, n_e, tmg)
    xs = jnp.take(hb.reshape(n_tok, d), src, axis=0)
    ys = moe_experts(xs, meta, wg, wu, wd, tmg, f // 2)
    ya = jnp.take(ys, pos[:, 0], axis=0).reshape(bsz, seq, d)
    yb = jnp.take(ys, pos[:, 1], axis=0).reshape(bsz, seq, d)
    return combine_residual_ln(x, gate, route, ya, yb, ln_g, ln_b, alpha)


def _mixers(p_l, p_c, need_ctx_out, lru_conv_w, lru_conv_b, lru_wa, lru_ba, lru_wx, lru_bx, lru_lambda,
            na_rpb, ssd_conv_w, ssd_conv_b, ssd_dt_bias, ssd_a_log, ssd_d, ssd_norm_g, ml_conv_w, ml_conv_b,
            ml_i_bias, ml_f_bias):
    ya = lru_branch(p_l, p_c, lru_conv_w, lru_conv_b, lru_wa, lru_ba, lru_wx, lru_bx, lru_lambda, need_ctx_out)
    yb = na_branch(p_l, p_c, na_rpb, need_ctx_out)
    yc = ssd_branch(p_l, p_c, ssd_conv_w, ssd_conv_b, ssd_dt_bias, ssd_a_log, ssd_d, ssd_norm_g, need_ctx_out)
    yd = mlstm_branch(p_l, p_c, ml_conv_w, ml_conv_b, ml_i_bias, ml_f_bias, need_ctx_out)
    return (ya[0], yb[0], yc[0], yd[0]), (ya[1], yb[1], yc[1], yd[1])


def kernel(x, c, ctx, c_ctx, w_ada, b_ada, w_in, lru_conv_w, lru_conv_b, lru_wa, lru_ba, lru_wx, lru_bx, lru_lambda, na_rpb, ssd_conv_w, ssd_conv_b, ssd_dt_bias, ssd_a_log, ssd_d, ssd_norm_g, ml_conv_w, ml_conv_b, ml_i_bias, ml_f_bias, w_branch, w_out, ln_g, ln_b, ffn_w_gate, ffn_w_up, ffn_w_down, moe_w_router, moe_b_router, moe_w_gate, moe_w_up, moe_w_down):
    depth = w_in.shape[0]
    bsz, seq, d = x.shape
    alpha = (2.0 * depth) ** 0.25
    cvecs = jnp.zeros((8, d), F32).at[:bsz].set(c).at[bsz].set(c_ctx)
    xl, xc = x, ctx
    for l in range(depth):
        need_ctx_out = l < depth - 1
        mod = ada_modulation(cvecs, w_ada[l], b_ada[l])
        mod_l = jnp.split(mod[:bsz, None, :], 6, axis=-1)
        mod_c = jnp.split(jnp.broadcast_to(mod[bsz:bsz + 1, None, :], (bsz, 1, 6 * d)), 6, axis=-1)
        w_in_p = _pack_w_in(w_in[l])
        p_l = mod_matmul(xl, mod_l[1], mod_l[0], w_in_p)
        p_c = mod_matmul(xc, mod_c[1], mod_c[0], w_in_p)
        br_l, br_c = _mixers(p_l, p_c, need_ctx_out, lru_conv_w[l], lru_conv_b[l], lru_wa[l], lru_ba[l],
                             lru_wx[l], lru_bx[l], lru_lambda[l], na_rpb[l], ssd_conv_w[l], ssd_conv_b[l],
                             ssd_dt_bias[l], ssd_a_log[l], ssd_d[l], ssd_norm_g[l], ml_conv_w[l], ml_conv_b[l],
                             ml_i_bias[l], ml_f_bias[l])
        wb = w_branch[l].astype(BF16)
        wo = w_out[l].astype(BF16)
        xl = proj_residual_ln(merge_branches(p_l, br_l, wb), wo, xl, mod_l[2], ln_g[l, 0], ln_b[l, 0], alpha)
        if need_ctx_out:
            xc = proj_residual_ln(merge_branches(p_c, br_c, wb), wo, xc, mod_c[2], ln_g[l, 0], ln_b[l, 0], alpha)
        j = l // 2
        if l % 2 == 0:
            wg = ffn_w_gate[j].astype(BF16)
            wu = ffn_w_up[j].astype(BF16)
            wd = ffn_w_down[j].astype(BF16)

            def ffn(h, m, wg=wg, wu=wu, wd=wd):
                return ffn_residual_ln(h, m[4], m[3], m[5], wg, wu, wd, ln_g[l, 1], ln_b[l, 1], alpha)
        else:
            wg = moe_w_gate[j].astype(BF16)
            wu = moe_w_up[j].astype(BF16)
            wd = moe_w_down[j].astype(BF16)

            def ffn(h, m, wg=wg, wu=wu, wd=wd, j=j):
                return moe_residual_ln(h, m[4], m[3], m[5], moe_w_router[j], moe_b_router[j], wg, wu, wd,
                                       ln_g[l, 1], ln_b[l, 1], alpha)

        xl = ffn(xl, mod_l)
        if need_ctx_out:
            xc = ffn(xc, mod_c)
    return xl
```

```python
import functools
import math

import numpy as np
import jax
import jax.numpy as jnp
from jax import lax
from jax.experimental import pallas as pl
from jax.experimental.pallas import tpu as pltpu

F32 = jnp.float32
BF16 = jnp.bfloat16

D_MODEL = 2048
GRID_W = 64
N_BRANCH = 4
BRANCH_W = D_MODEL // N_BRANCH
CONV_W = 4
LN_EPS = 1e-5
LRU_BLOCKS = 8
LRU_BW = BRANCH_W // LRU_BLOCKS
LRU_C = 8.0
NA_HEADS = 8
NA_HD = BRANCH_W // NA_HEADS
NA_KH = 8
NA_KW = 16
SSD_HEADS = 8
SSD_HD = BRANCH_W // SSD_HEADS
SSD_GROUPS = 2
SSD_STATE = 64
SSD_CHUNK = 128
SSD_GN = SSD_GROUPS * SSD_STATE
ML_HEADS = 4
ML_HD = BRANCH_W // ML_HEADS
ML_CHUNK = 128
ROPE_BASE = 10000.0
N_EXPERTS = 8
TOP_K = 2

V7X_VMEM_LIMIT = 52 * 1024 * 1024

COL_LRU_X = 0
COL_LRU_G = 512
COL_NA = 1024
COL_SSD_Z = 2560
COL_SSD_X = 3072
COL_ML = 3584
COL_SSD_B = 5632
COL_SSD_C = 5760
COL_SSD_DT = 5888
COL_ML_G = 5904
N_MIX = 5920
N_MIX_PAD = 6144


def _pack_w_in(w):
    parts = [w[:, 0:1024], w[:, 1024:2560], w[:, 2560:3584], w[:, 3856:5904],
             w[:, 3584:3840], w[:, 3840:3856], w[:, 5904:5920],
             jnp.zeros((w.shape[0], N_MIX_PAD - N_MIX), w.dtype)]
    return jnp.concatenate(parts, axis=1).astype(BF16), w[:, N_MIX:].astype(BF16)


def _sigmoid(x):
    return 1.0 / (1.0 + jnp.exp(-x))


def _layer_norm_rows(z, g, b):
    mu = jnp.mean(z, axis=-1, keepdims=True)
    zc = z - mu
    var = jnp.mean(zc * zc, axis=-1, keepdims=True)
    return zc * lax.rsqrt(var + LN_EPS) * g + b


def _mod_matmul_kernel(x_ref, sc_ref, sh_ref, w_ref, o_ref, hb_ref):
    @pl.when(pl.program_id(2) == 0)
    def _():
        hb_ref[0] = (x_ref[0] * (1.0 + sc_ref[0]) + sh_ref[0]).astype(BF16)

    o_ref[0] = jnp.dot(hb_ref[0], w_ref[...], preferred_element_type=F32)


def mod_matmul(x, sc, sh, w, tm=1024, tn=512):
    bsz, seq, d = x.shape
    n = w.shape[1]
    tm = min(tm, seq)
    return pl.pallas_call(
        _mod_matmul_kernel,
        grid=(bsz, seq // tm, n // tn),
        in_specs=[pl.BlockSpec((1, tm, d), lambda b, i, j: (b, i, 0)),
                  pl.BlockSpec((1, 1, d), lambda b, i, j: (b, 0, 0)),
                  pl.BlockSpec((1, 1, d), lambda b, i, j: (b, 0, 0)),
                  pl.BlockSpec((d, tn), lambda b, i, j: (0, j))],
        out_specs=[pl.BlockSpec((1, tm, tn), lambda b, i, j: (b, i, j)),
                   pl.BlockSpec((1, tm, d), lambda b, i, j: (b, i, 0))],
        out_shape=[jax.ShapeDtypeStruct((bsz, seq, n), F32), jax.ShapeDtypeStruct((bsz, seq, d), BF16)],
        compiler_params=pltpu.CompilerParams(
            dimension_semantics=("parallel", "parallel", "arbitrary"), vmem_limit_bytes=V7X_VMEM_LIMIT),
        name="in_proj",
    )(x, sc, sh, w)


def _merge_kernel(hb_ref, g0, g1, g2, g3, ya, yb, yc, yd, wb_ref, o_ref):
    hb = hb_ref[0]
    acc = None
    for n, (wg, y) in enumerate(((g0, ya), (g1, yb), (g2, yc), (g3, yd))):
        gate = _sigmoid(jnp.dot(hb, wg[...], preferred_element_type=F32))
        t = gate * jnp.dot(y[0].astype(BF16), wb_ref[n], preferred_element_type=F32)
        acc = t if acc is None else acc + t
    o_ref[0] = acc.astype(BF16)


def merge_branches(hb, w_gates, branches, wb, tm=512, tn=512):
    bsz, seq, dk = hb.shape
    d = wb.shape[2]
    tm = min(tm, seq)
    nj = d // tn
    g_specs = [pl.BlockSpec((dk, tn), functools.partial(lambda b, i, j, n: (0, n * nj + j), n=n))
               for n in range(N_BRANCH)]
    y_specs = [pl.BlockSpec((1, tm, BRANCH_W), lambda b, i, j: (b, i, 0)) for _ in range(N_BRANCH)]
    return pl.pallas_call(
        _merge_kernel,
        grid=(bsz, seq // tm, nj),
        in_specs=[pl.BlockSpec((1, tm, dk), lambda b, i, j: (b, i, 0))] + g_specs + y_specs
                 + [pl.BlockSpec((N_BRANCH, BRANCH_W, tn), lambda b, i, j: (0, 0, j))],
        out_specs=pl.BlockSpec((1, tm, tn), lambda b, i, j: (b, i, j)),
        out_shape=jax.ShapeDtypeStruct((bsz, seq, d), BF16),
        compiler_params=pltpu.CompilerParams(
            dimension_semantics=("parallel", "parallel", "arbitrary"), vmem_limit_bytes=V7X_VMEM_LIMIT),
        name="merge",
    )(hb, w_gates, w_gates, w_gates, w_gates, *branches, wb)


def _proj_ln_kernel(m_ref, w_ref, x_ref, gt_ref, g_ref, b_ref, o_ref, *, alpha):
    y = jnp.dot(m_ref[0], w_ref[...], preferred_element_type=F32)
    o_ref[0] = _layer_norm_rows(alpha * x_ref[0] + gt_ref[0] * y, g_ref[...], b_ref[...])


def proj_residual_ln(m, w, x, gate, ln_g, ln_b, alpha, tm=256):
    bsz, seq, k = m.shape
    d = w.shape[1]
    tm = min(tm, seq)
    return pl.pallas_call(
        functools.partial(_proj_ln_kernel, alpha=alpha),
        grid=(bsz, seq // tm),
        in_specs=[pl.BlockSpec((1, tm, k), lambda b, i: (b, i, 0)),
                  pl.BlockSpec((k, d), lambda b, i: (0, 0)),
                  pl.BlockSpec((1, tm, d), lambda b, i: (b, i, 0)),
                  pl.BlockSpec((1, 1, d), lambda b, i: (b, 0, 0)),
                  pl.BlockSpec((1, d), lambda b, i: (0, 0)),
                  pl.BlockSpec((1, d), lambda b, i: (0, 0))],
        out_specs=pl.BlockSpec((1, tm, d), lambda b, i: (b, i, 0)),
        out_shape=jax.ShapeDtypeStruct((bsz, seq, d), F32),
        compiler_params=pltpu.CompilerParams(
            dimension_semantics=("parallel", "parallel"), vmem_limit_bytes=V7X_VMEM_LIMIT),
        name="out_proj_ln",
    )(m, w, x, gate, ln_g.reshape(1, d), ln_b.reshape(1, d))


def _ffn_kernel(x_ref, sc_ref, sh_ref, gt_ref, wg_ref, wu_ref, wd_ref, lg_ref, lb_ref, o_ref, hb_ref, acc_ref, *, alpha):
    j = pl.program_id(2)

    @pl.when(j == 0)
    def _():
        hb_ref[...] = (x_ref[0] * (1.0 + sc_ref[0]) + sh_ref[0]).astype(BF16)
        acc_ref[...] = jnp.zeros_like(acc_ref)

    hb = hb_ref[...]
    g = jnp.dot(hb, wg_ref[...], preferred_element_type=F32)
    u = jnp.dot(hb, wu_ref[...], preferred_element_type=F32)
    acc_ref[...] += jnp.dot((_silu(g) * u).astype(BF16), wd_ref[...], preferred_element_type=F32)

    @pl.when(j == pl.num_programs(2) - 1)
    def _():
        o_ref[0] = _layer_norm_rows(alpha * x_ref[0] + gt_ref[0] * acc_ref[...], lg_ref[...], lb_ref[...])


def ffn_residual_ln(x, sc, sh, gate, wg, wu, wd, ln_g, ln_b, alpha, tm=512, tf=512):
    bsz, seq, d = x.shape
    f = wg.shape[1]
    tm = min(tm, seq)
    vec = pl.BlockSpec((1, 1, d), lambda b, i, j: (b, 0, 0))
    par = pl.BlockSpec((1, d), lambda b, i, j: (0, 0))
    return pl.pallas_call(
        functools.partial(_ffn_kernel, alpha=alpha),
        grid=(bsz, seq // tm, f // tf),
        in_specs=[pl.BlockSpec((1, tm, d), lambda b, i, j: (b, i, 0)), vec, vec, vec,
                  pl.BlockSpec((d, tf), lambda b, i, j: (0, j)),
                  pl.BlockSpec((d, tf), lambda b, i, j: (0, j)),
                  pl.BlockSpec((tf, d), lambda b, i, j: (j, 0)), par, par],
        out_specs=pl.BlockSpec((1, tm, d), lambda b, i, j: (b, i, 0)),
        out_shape=jax.ShapeDtypeStruct((bsz, seq, d), F32),
        scratch_shapes=[pltpu.VMEM((tm, d), BF16), pltpu.VMEM((tm, d), F32)],
        compiler_params=pltpu.CompilerParams(
            dimension_semantics=("parallel", "parallel", "arbitrary"), vmem_limit_bytes=V7X_VMEM_LIMIT),
        name="ffn_ln",
    )(x, sc, sh, gate, wg, wu, wd, ln_g.reshape(1, d), ln_b.reshape(1, d))


def _ada_kernel(c_ref, w_ref, b_ref, o_ref):
    cv = c_ref[...]
    o_ref[...] = jnp.dot(cv * _sigmoid(cv), w_ref[...], precision=lax.Precision.HIGHEST,
                         preferred_element_type=F32) + b_ref[...]


def ada_modulation(cvecs, w, b, tn=1536):
    r, d = cvecs.shape
    n = w.shape[1]
    return pl.pallas_call(
        _ada_kernel,
        grid=(n // tn,),
        in_specs=[pl.BlockSpec((r, d), lambda j: (0, 0)), pl.BlockSpec((d, tn), lambda j: (0, j)),
                  pl.BlockSpec((1, tn), lambda j: (0, j))],
        out_specs=pl.BlockSpec((r, tn), lambda j: (0, j)),
        out_shape=jax.ShapeDtypeStruct((r, n), F32),
        compiler_params=pltpu.CompilerParams(dimension_semantics=("parallel",), vmem_limit_bytes=V7X_VMEM_LIMIT),
        name="ada_mod",
    )(cvecs, w, b.reshape(1, n))


NA_NEG = -1e30


def _na_bias_slabs(rpb):
    w = jnp.arange(GRID_W)
    cs = jnp.clip(w - NA_KW // 2, 0, GRID_W - NA_KW)
    ok = (w[None, :] >= cs[:, None]) & (w[None, :] < cs[:, None] + NA_KW)
    dc = jnp.clip(w[None, :] - w[:, None] + (NA_KW - 1), 0, 2 * NA_KW - 2)
    tab = jnp.where(ok, rpb[:, :, dc], NA_NEG)
    idx = jnp.arange(NA_KH)[:, None] + jnp.arange(NA_KH)[None, :]
    slab = tab[:, idx]
    return slab.transpose(1, 0, 3, 2, 4).reshape(NA_KH, NA_HEADS, GRID_W, NA_KH * GRID_W)


def _softmax2(s_a, s_b):
    m = jnp.maximum(jnp.max(s_a, axis=-1, keepdims=True), jnp.max(s_b, axis=-1, keepdims=True))
    e_a = jnp.exp(s_a - m)
    e_b = jnp.exp(s_b - m)
    inv = 1.0 / (jnp.sum(e_a, axis=-1, keepdims=True) + jnp.sum(e_b, axis=-1, keepdims=True))
    return e_a * inv, e_b * inv


_NT = (((1,), (1,)), ((), ()))


def _na_kernel(q_ref, kp_ref, kc_ref, kn_ref, vp_ref, vc_ref, vn_ref, ck_ref, cv_ref, bias_ref, o_ref,
               kw_ref, vw_ref, ckb_ref, cvb_ref, *, rows):
    i = pl.program_id(1)
    blk = NA_KH * GRID_W
    for n, (kr, vr) in enumerate(((kp_ref, vp_ref), (kc_ref, vc_ref), (kn_ref, vn_ref))):
        kw_ref[n * blk:(n + 1) * blk, :] = kr[0].astype(BF16)
        vw_ref[n * blk:(n + 1) * blk, :] = vr[0].astype(BF16)
    ckb_ref[...] = ck_ref[0].astype(BF16)
    cvb_ref[...] = cv_ref[0].astype(BF16)
    scale = NA_HD ** -0.5

    def body(rr, carry):
        r = i * NA_KH + rr
        rs = jnp.clip(r - NA_KH // 2, 0, rows - NA_KH)
        off = pl.multiple_of((rs - (i - 1) * NA_KH) * GRID_W, GRID_W)
        v = rs - r + (NA_KH - 1)
        q_all = (q_ref[0, pl.ds(pl.multiple_of(rr * GRID_W, GRID_W), GRID_W), :] * scale).astype(BF16)
        kwin = kw_ref[pl.ds(off, blk), :]
        vwin = vw_ref[pl.ds(off, blk), :]
        first = lax.broadcasted_iota(jnp.int32, (GRID_W, 2 * NA_HD), 1) < NA_HD
        pairs = [slice(n * 2 * NA_HD, (n + 1) * 2 * NA_HD) for n in range(NA_HEADS // 2)]
        scores = []
        for h in range(NA_HEADS):
            ps = pairs[h // 2]
            q = jnp.where(first if h % 2 == 0 else ~first, q_all[:, ps], jnp.zeros((), BF16))
            scores.append((lax.dot_general(q, kwin[:, ps], _NT, preferred_element_type=F32) + bias_ref[v, h],
                           lax.dot_general(q, ckb_ref[:, ps], _NT, preferred_element_type=F32)))
        probs = [_softmax2(s_w, s_c) for s_w, s_c in scores]
        both = [jnp.dot(p_w.astype(BF16), vwin[:, pairs[h // 2]], preferred_element_type=F32)
                + jnp.dot(p_c.astype(BF16), cvb_ref[:, pairs[h // 2]], preferred_element_type=F32)
                for h, (p_w, p_c) in enumerate(probs)]
        outs = [jnp.where(first, both[2 * n], both[2 * n + 1]) for n in range(NA_HEADS // 2)]
        o_ref[0, pl.ds(pl.multiple_of(rr * GRID_W, GRID_W), GRID_W), :] = jnp.concatenate(outs, axis=-1)
        return carry

    lax.fori_loop(0, NA_KH, body, 0)


def _na_ctx_kernel(q_ref, k_ref, v_ref, o_ref):
    scale = NA_HD ** -0.5
    q_all = (q_ref[0] * scale).astype(BF16)
    k_all = k_ref[0].astype(BF16)
    v_all = v_ref[0].astype(BF16)
    outs = []
    for h in range(NA_HEADS):
        hs = slice(h * NA_HD, (h + 1) * NA_HD)
        s = lax.dot_general(q_all[:, hs], k_all[:, hs], _NT, preferred_element_type=F32)
        e = jnp.exp(s - jnp.max(s, axis=-1, keepdims=True))
        p = e * (1.0 / jnp.sum(e, axis=-1, keepdims=True))
        outs.append(jnp.dot(p.astype(BF16), v_all[:, hs], preferred_element_type=F32))
    o_ref[0] = jnp.concatenate(outs, axis=-1)


def na_branch(p_l, p_c, rpb, need_ctx_out):
    bsz, seq, _ = p_l.shape
    n_ctx = p_c.shape[1]
    rows = seq // GRID_W
    assert rows % NA_KH == 0 and rows >= 2 * NA_KH
    w = BRANCH_W
    blk = NA_KH * GRID_W
    nblk = rows // NA_KH
    cq = COL_NA // w

    def shifted(col, delta):
        return pl.BlockSpec((1, blk, w), lambda b, i: (b, jnp.clip(i + delta, 0, nblk - 1), col))

    y_l = pl.pallas_call(
        functools.partial(_na_kernel, rows=rows),
        grid=(bsz, nblk),
        in_specs=[shifted(cq, 0), shifted(cq + 1, -1), shifted(cq + 1, 0), shifted(cq + 1, 1),
                  shifted(cq + 2, -1), shifted(cq + 2, 0), shifted(cq + 2, 1),
                  pl.BlockSpec((1, n_ctx, w), lambda b, i: (b, 0, cq + 1)),
                  pl.BlockSpec((1, n_ctx, w), lambda b, i: (b, 0, cq + 2)),
                  pl.BlockSpec((NA_KH, NA_HEADS, GRID_W, blk), lambda b, i: (0, 0, 0, 0))],
        out_specs=pl.BlockSpec((1, blk, w), lambda b, i: (b, i, 0)),
        out_shape=jax.ShapeDtypeStruct((bsz, seq, w), F32),
        scratch_shapes=[pltpu.VMEM((3 * blk, w), BF16), pltpu.VMEM((3 * blk, w), BF16),
                        pltpu.VMEM((n_ctx, w), BF16), pltpu.VMEM((n_ctx, w), BF16)],
        compiler_params=pltpu.CompilerParams(
            dimension_semantics=("parallel", "arbitrary"), vmem_limit_bytes=V7X_VMEM_LIMIT),
        name="na_attn",
    )(p_l, p_l, p_l, p_l, p_l, p_l, p_l, p_c, p_c, _na_bias_slabs(rpb))
    y_c = None
    if need_ctx_out:
        y_c = pl.pallas_call(
            _na_ctx_kernel,
            grid=(bsz,),
            in_specs=[pl.BlockSpec((1, n_ctx, w), functools.partial(lambda b, c: (b, 0, c), c=cq + n)) for n in range(3)],
            out_specs=pl.BlockSpec((1, n_ctx, w), lambda b: (b, 0, 0)),
            out_shape=jax.ShapeDtypeStruct((bsz, n_ctx, w), F32),
            compiler_params=pltpu.CompilerParams(dimension_semantics=("parallel",)),
            name="na_ctx_attn",
        )(p_c, p_c, p_c)
    return y_l, y_c


def _gelu_tanh(x):
    return 0.5 * x * (1.0 + jnp.tanh(math.sqrt(2.0 / math.pi) * (x + 0.044715 * (x * x * x))))


def _silu(x):
    return x * _sigmoid(x)


def _softplus(x):
    return jnp.maximum(x, 0.0) + jnp.log1p(jnp.exp(-jnp.abs(x)))


def _rope_2d_tables(seq):
    t = jnp.arange(seq, dtype=jnp.int32)
    pos = jnp.stack([t // GRID_W, t % GRID_W], axis=-1).astype(F32)
    nf = ML_HD // 4
    inv_freq = ROPE_BASE ** (-jnp.arange(nf, dtype=F32) / nf)
    ang = jnp.broadcast_to(pos[:, :, None, None] * inv_freq, (seq, 2, 2, nf)).reshape(seq, ML_HD)
    return jnp.cos(ang), jnp.sin(ang)


def _chunk_specs(tt, nchunk, n8, reverse):
    def pos(j):
        return nchunk - 1 - j if reverse else j

    def chunk(width, col):
        return pl.BlockSpec((1, tt, width), lambda b, j: (b, pos(j), col))

    def halo(width, col, delta):
        if delta < 0:
            return pl.BlockSpec((1, 8, width), lambda b, j: (b, jnp.maximum(pos(j) * (tt // 8) - 1, 0), col))
        return pl.BlockSpec((1, 8, width), lambda b, j: (b, jnp.minimum((pos(j) + 1) * (tt // 8), n8 - 1), col))

    return chunk, halo


def _const_spec(shape):
    return pl.BlockSpec(shape, lambda b, j: (0,) * len(shape))


def _batch_spec(shape):
    return pl.BlockSpec((1,) + shape, lambda b, j: (b,) + (0,) * len(shape))


def _conv4(x, prev8, next8, has_prev, has_next, cw, cb):
    tt = x.shape[0]
    row = lax.broadcasted_iota(jnp.int32, x.shape, 0)
    p6 = prev8[6:7, :] * has_prev
    p7 = prev8[7:8, :] * has_prev
    n0 = next8[0:1, :] * has_next
    xm1 = jnp.where(row == 0, p7, pltpu.roll(x, 1, 0))
    xm2 = jnp.where(row == 0, p6, jnp.where(row == 1, p7, pltpu.roll(x, 2, 0)))
    xp1 = jnp.where(row == tt - 1, n0, pltpu.roll(x, tt - 1, 0))
    return cw[0:1, :] * xm2 + cw[1:2, :] * xm1 + cw[2:3, :] * x + cw[3:4, :] * xp1 + cb


def _cumsum_rows(x, reverse):
    n = x.shape[0]
    row = lax.broadcasted_iota(jnp.int32, x.shape, 0)
    s = 1
    while s < n:
        if reverse:
            x = x + jnp.where(row < n - s, pltpu.roll(x, n - s, 0), 0.0)
        else:
            x = x + jnp.where(row >= s, pltpu.roll(x, s, 0), 0.0)
        s *= 2
    return x


def _causal_mask(n, reverse):
    ii = lax.broadcasted_iota(jnp.int32, (n, n), 0)
    jj = lax.broadcasted_iota(jnp.int32, (n, n), 1)
    return (jj >= ii) if reverse else (jj <= ii)


MASK_NEG = -1e30


LRU_TT = 256


def _lru_kernel(*refs, reverse, finalize, nchunk):
    if finalize:
        (xp_ref, x_ref, xn_ref, g_ref, ho_ref, h0_ref, cw_ref, cb_ref, w_ref, bias_ref, sp_ref,
         o_ref, hl_ref, carry_ref) = refs
    else:
        (xp_ref, x_ref, xn_ref, h0_ref, cw_ref, cb_ref, w_ref, bias_ref, sp_ref, o_ref, hl_ref, carry_ref) = refs
    j = pl.program_id(1)
    c = (nchunk - 1 - j) if reverse else j

    @pl.when(j == 0)
    def _():
        carry_ref[...] = h0_ref[0]

    tt = x_ref.shape[1]
    w = x_ref.shape[2]
    xc = _conv4(x_ref[0], xp_ref[0], xn_ref[0], (c > 0).astype(F32), (c < nchunk - 1).astype(F32),
                cw_ref[...], cb_ref[...])
    g = jnp.dot(xc.astype(BF16), w_ref[...], preferred_element_type=F32)
    r = _sigmoid(g[:, :w] + bias_ref[0:1, :])
    ig = _sigmoid(g[:, w:] + bias_ref[1:2, :])
    log_a = -LRU_C * r * sp_ref[...]
    a = jnp.exp(log_a)
    u = jnp.sqrt(1.0 - jnp.exp(2.0 * log_a)) * (ig * xc)
    row = lax.broadcasted_iota(jnp.int32, (tt, w), 0)
    s = 1
    while s < tt:
        if reverse:
            keep = row < tt - s
            a_s = jnp.where(keep, pltpu.roll(a, tt - s, 0), 1.0)
            u_s = jnp.where(keep, pltpu.roll(u, tt - s, 0), 0.0)
        else:
            keep = row >= s
            a_s = jnp.where(keep, pltpu.roll(a, s, 0), 1.0)
            u_s = jnp.where(keep, pltpu.roll(u, s, 0), 0.0)
        u = a * u_s + u
        a = a * a_s
        s *= 2
    h = u + a * carry_ref[...]
    new_carry = h[0:1, :] if reverse else h[tt - 1:tt, :]
    carry_ref[...] = new_carry
    hl_ref[0] = new_carry
    if finalize:
        o_ref[0] = (ho_ref[0] + h) * _gelu_tanh(g_ref[0])
    else:
        o_ref[0] = h


def _lru_pass(p, h0, other, cw, cb, wcat, bias, sp, reverse):
    bsz, seq, _ = p.shape
    w = BRANCH_W
    tt = min(LRU_TT, seq)
    nchunk = seq // tt
    chunk, halo = _chunk_specs(tt, nchunk, seq // 8, reverse)
    cx = COL_LRU_X // w
    finalize = other is not None
    in_specs = [halo(w, cx, -1), chunk(w, cx), halo(w, cx, 1)]
    args = [p, p, p]
    if finalize:
        in_specs += [chunk(w, COL_LRU_G // w), chunk(w, 0)]
        args += [p, other]
    in_specs += [_batch_spec((1, w)), _const_spec((CONV_W, w)), _const_spec((1, w)),
                 _const_spec((w, 2 * w)), _const_spec((2, w)), _const_spec((1, w))]
    args += [h0, cw, cb, wcat, bias, sp]
    return pl.pallas_call(
        functools.partial(_lru_kernel, reverse=reverse, finalize=finalize, nchunk=nchunk),
        grid=(bsz, nchunk),
        in_specs=in_specs,
        out_specs=[chunk(w, 0), _batch_spec((1, w))],
        out_shape=[jax.ShapeDtypeStruct((bsz, seq, w), F32), jax.ShapeDtypeStruct((bsz, 1, w), F32)],
        scratch_shapes=[pltpu.VMEM((1, w), F32)],
        compiler_params=pltpu.CompilerParams(dimension_semantics=("parallel", "arbitrary")),
        name="lru_scan",
    )(*args)


def _block_diag(wg):
    g, n, _ = wg.shape
    eye = jnp.eye(g, dtype=wg.dtype)
    return (wg[:, :, None, :] * eye[:, None, :, None]).reshape(g * n, g * n)


def lru_branch(p_l, p_c, conv_w, conv_b, wa, ba, wx, bx, lam, need_ctx_out):
    bsz = p_l.shape[0]
    w = BRANCH_W
    cb = conv_b.reshape(1, w)
    sp = jax.nn.softplus(-lam)
    zeros = jnp.zeros((bsz, 1, w), F32)
    h_c = h_l = None
    for d, reverse in ((0, False), (1, True)):
        wcat = jnp.concatenate([_block_diag(wa[d]), _block_diag(wx[d])], axis=1).astype(BF16)
        bias = jnp.stack([ba[d], bx[d]])
        other_c = h_c if (d == 1 and need_ctx_out) else None
        h_c, st = _lru_pass(p_c, zeros, other_c, conv_w, cb, wcat, bias, sp[d:d + 1], reverse)
        h_l, _ = _lru_pass(p_l, st, h_l if d == 1 else None, conv_w, cb, wcat, bias, sp[d:d + 1], reverse)
    return h_l, (h_c if need_ctx_out else None)


def _ssd_kernel(*refs, reverse, finalize, nchunk, d):
    (xp_ref, x_ref, xn_ref, bp_ref, bc_ref, bn_ref, dt_ref) = refs[:7]
    k = 7
    if finalize:
        z_ref, yp_ref = refs[k:k + 2]
        k += 2
    s0_ref, cwx_ref, cbx_ref, cwb_ref, cbb_ref, dtb_ref, a_ref = refs[k:k + 7]
    k += 7
    if finalize:
        dsk_ref, ng_ref = refs[k:k + 2]
        k += 2
    o_ref, so_ref, s_ref = refs[k:k + 3]
    j = pl.program_id(1)
    c = (nchunk - 1 - j) if reverse else j

    @pl.when(j == 0)
    def _():
        s_ref[...] = s0_ref[0]

    q = x_ref.shape[1]
    has_prev = (c > 0).astype(F32)
    has_next = (c < nchunk - 1).astype(F32)
    xs = _silu(_conv4(x_ref[0], xp_ref[0], xn_ref[0], has_prev, has_next, cwx_ref[...], cbx_ref[...]))
    bc = _silu(_conv4(bc_ref[0], bp_ref[0], bn_ref[0], has_prev, has_next, cwb_ref[...], cbb_ref[...]))
    dt = _softplus(dt_ref[0] + dtb_ref[...])
    cum = _cumsum_rows(dt * a_ref[...], reverse)
    tot = cum[0:1, :] if reverse else cum[q - 1:q, :]
    w_end = jnp.exp(tot - cum) * dt
    ecum = jnp.exp(cum)
    etot = jnp.exp(tot)
    dt_t = dt.T
    cum_t = cum.T
    b_t = bc[:, :SSD_GN].T
    mask = _causal_mask(q, reverse)
    xb = xs.astype(BF16)
    cgs = [bc[:, SSD_GN + g * SSD_STATE:SSD_GN + (g + 1) * SSD_STATE].astype(BF16) for g in range(SSD_GROUPS)]
    cbs = [lax.dot_general(cgs[g], bc[:, g * SSD_STATE:(g + 1) * SSD_STATE].astype(BF16), _NT,
                           preferred_element_type=F32) for g in range(SSD_GROUPS)]
    outs = []
    for h in range(SSD_HEADS):
        g = h // (SSD_HEADS // SSD_GROUPS)
        col = d * SSD_HEADS + h
        hs = slice(h * SSD_HD, (h + 1) * SSD_HD)
        diff = cum[:, col:col + 1] - cum_t[col:col + 1, :]
        m = cbs[g] * jnp.exp(jnp.where(mask, diff, MASK_NEG)) * dt_t[col:col + 1, :]
        s_old = s_ref[h]
        y = (jnp.dot(m.astype(BF16), xb[:, hs], preferred_element_type=F32)
             + ecum[:, col:col + 1] * jnp.dot(cgs[g], s_old.astype(BF16), preferred_element_type=F32))
        xw = (xs[:, hs] * w_end[:, col:col + 1]).astype(BF16)
        s_ref[h] = etot[:, col:col + 1] * s_old + jnp.dot(
            b_t[g * SSD_STATE:(g + 1) * SSD_STATE, :].astype(BF16), xw, preferred_element_type=F32)
        outs.append(y)
    y = jnp.concatenate(outs, axis=-1)
    so_ref[0] = s_ref[...]
    if finalize:
        yt = (xs * dsk_ref[...] + yp_ref[0] + y) * _silu(z_ref[0])
        o_ref[0] = yt * lax.rsqrt(jnp.mean(yt * yt, axis=-1, keepdims=True) + LN_EPS) * ng_ref[...]
    else:
        o_ref[0] = y


def _ssd_pass(p, s0, other, params, reverse, d):
    bsz, seq, _ = p.shape
    w = BRANCH_W
    q = SSD_CHUNK
    nchunk = seq // q
    chunk, halo = _chunk_specs(q, nchunk, seq // 8, reverse)
    cx, cb2, cdt = COL_SSD_X // w, COL_SSD_B // (2 * SSD_GN), COL_SSD_DT // 128
    finalize = other is not None
    cwx, cbx, cwb, cbb, dtb, arow, dsk, ng = params
    in_specs = [halo(w, cx, -1), chunk(w, cx), halo(w, cx, 1),
                halo(2 * SSD_GN, cb2, -1), chunk(2 * SSD_GN, cb2), halo(2 * SSD_GN, cb2, 1), chunk(128, cdt)]
    args = [p] * 7
    if finalize:
        in_specs += [chunk(w, COL_SSD_Z // w), chunk(w, 0)]
        args += [p, other]
    st_shape = (SSD_HEADS, SSD_STATE, SSD_HD)
    in_specs += [_batch_spec(st_shape), _const_spec((CONV_W, w)), _const_spec((1, w)),
                 _const_spec((CONV_W, 2 * SSD_GN)), _const_spec((1, 2 * SSD_GN)), _const_spec((1, 128)),
                 _const_spec((1, 128))]
    args += [s0, cwx, cbx, cwb, cbb, dtb, arow]
    if finalize:
        in_specs += [_const_spec((1, w)), _const_spec((1, w))]
        args += [dsk, ng]
    return pl.pallas_call(
        functools.partial(_ssd_kernel, reverse=reverse, finalize=finalize, nchunk=nchunk, d=d),
        grid=(bsz, nchunk),
        in_specs=in_specs,
        out_specs=[chunk(w, 0), _batch_spec(st_shape)],
        out_shape=[jax.ShapeDtypeStruct((bsz, seq, w), F32), jax.ShapeDtypeStruct((bsz,) + st_shape, F32)],
        scratch_shapes=[pltpu.VMEM(st_shape, F32)],
        compiler_params=pltpu.CompilerParams(dimension_semantics=("parallel", "arbitrary")),
        name="ssd_scan",
    )(*args)


def _lane_row(vals, start):
    return jnp.zeros((128,), F32).at[start:start + vals.shape[0]].set(vals.astype(F32)).reshape(1, 128)


def ssd_branch(p_l, p_c, conv_w, conv_b, dt_bias, a_log, d_skip, norm_g, need_ctx_out):
    bsz = p_l.shape[0]
    w = BRANCH_W
    params = (conv_w[:, :w], conv_b[:w].reshape(1, w), conv_w[:, w:], conv_b[w:].reshape(1, 2 * SSD_GN),
              _lane_row(dt_bias.reshape(-1), 0), _lane_row(-jnp.exp(a_log.astype(F32)).reshape(-1), 0),
              jnp.repeat(d_skip, SSD_HD).reshape(1, w), norm_g.reshape(1, w))
    zeros = jnp.zeros((bsz, SSD_HEADS, SSD_STATE, SSD_HD), F32)
    y_c = y_l = None
    for d, reverse in ((0, False), (1, True)):
        other_c = y_c if (d == 1 and need_ctx_out) else None
        y_c, st = _ssd_pass(p_c, zeros, other_c, params, reverse, d)
        y_l, _ = _ssd_pass(p_l, st, y_l if d == 1 else None, params, reverse, d)
    return y_l, (y_c if need_ctx_out else None)


def _rope_rotate(x):
    wl = x.shape[-1]
    half = ML_HD // 4
    lane = lax.broadcasted_iota(jnp.int32, x.shape, 1)
    return jnp.where(lane % (2 * half) < half, -pltpu.roll(x, wl - half, 1), pltpu.roll(x, half, 1))


def _log_sigmoid(x):
    return jnp.minimum(x, 0.0) - jnp.log1p(jnp.exp(-jnp.abs(x)))


def _mlstm_kernel(*refs, reverse, finalize, rope, nchunk, d):
    (qp_ref, q_ref, qn_ref, kp_ref, k_ref, kn_ref, v_ref, g_ref) = refs[:8]
    n = 8
    if rope:
        cos_ref, sin_ref = refs[n:n + 2]
        n += 2
    if finalize:
        og_ref, hp_ref = refs[n:n + 2]
        n += 2
    c0_ref, n0_ref, m0_ref, cwq_ref, cbq_ref, cwk_ref, cbk_ref, ib_ref, fb_ref = refs[n:n + 9]
    n += 9
    o_ref, co_ref, no_ref, mo_ref, c_ref, n_ref, m_ref = refs[n:n + 7]
    j = pl.program_id(1)
    c = (nchunk - 1 - j) if reverse else j

    @pl.when(j == 0)
    def _():
        c_ref[...] = c0_ref[0]
        n_ref[...] = n0_ref[0]
        m_ref[...] = m0_ref[0]

    qn = q_ref.shape[1]
    has_prev = (c > 0).astype(F32)
    has_next = (c < nchunk - 1).astype(F32)
    q = _silu(_conv4(q_ref[0], qp_ref[0], qn_ref[0], has_prev, has_next, cwq_ref[...], cbq_ref[...]))
    k = _silu(_conv4(k_ref[0], kp_ref[0], kn_ref[0], has_prev, has_next, cwk_ref[...], cbk_ref[...]))
    if rope:
        cos = cos_ref[...]
        sin = sin_ref[...]
        q = q * cos + _rope_rotate(q) * sin
        k = k * cos + _rope_rotate(k) * sin
    q = q * (ML_HD ** -0.5)
    v = v_ref[0]
    gb = g_ref[0]
    li = gb + ib_ref[...]
    b = _cumsum_rows(_log_sigmoid(gb + fb_ref[...]), reverse)
    tot = b[0:1, :] if reverse else b[qn - 1:qn, :]
    b_t = b.T
    li_t = li.T
    mask = _causal_mask(qn, reverse)
    heads = [slice(h * ML_HD, (h + 1) * ML_HD) for h in range(ML_HEADS)]
    qb = q.astype(BF16)
    kb = k.astype(BF16)
    vb = v.astype(BF16)
    qk = [lax.dot_general(qb[:, hs], kb[:, hs], _NT, preferred_element_type=F32) for hs in heads]
    c_old = [c_ref[h] for h in range(ML_HEADS)]
    n_old = [n_ref[h:h + 1, :] for h in range(ML_HEADS)]
    qc = [lax.dot_general(qb[:, hs], c_old[h].astype(BF16), _NT, preferred_element_type=F32)
          for h, hs in enumerate(heads)]
    gate = []
    for h in range(ML_HEADS):
        ci = 4 * ML_HEADS + d * 2 * ML_HEADS + h
        cf = ci + ML_HEADS
        b_c = b[:, cf:cf + 1]
        b_end = tot[:, cf:cf + 1]
        m_st = m_ref[h:h + 1, 0:1]
        end_log = b_end - b_c + li[:, ci:ci + 1]
        m_new = jnp.maximum(b_end + m_st, jnp.max(end_log, axis=0, keepdims=True))
        dlog = jnp.where(mask, b_c - b_t[cf:cf + 1, :] + li_t[ci:ci + 1, :], MASK_NEG)
        m_inter = b_c + m_st
        m_i = jnp.maximum(jnp.max(dlog, axis=1, keepdims=True), m_inter)
        gate.append((jnp.exp(end_log - m_new), jnp.exp(b_end + m_st - m_new), m_new,
                     jnp.exp(dlog - m_i), jnp.exp(m_inter - m_i), jnp.exp(-m_i)))
    s_all = [qk[h] * gate[h][3] for h in range(ML_HEADS)]
    sv = [jnp.dot(s_all[h].astype(BF16), vb[:, hs], preferred_element_type=F32) for h, hs in enumerate(heads)]
    upd = [jnp.dot((v[:, hs] * gate[h][0]).T.astype(BF16), kb[:, hs], preferred_element_type=F32)
           for h, hs in enumerate(heads)]
    outs = []
    for h, hs in enumerate(heads):
        w, carry_scale, m_new, _, w_in, floor = gate[h]
        num = sv[h] + w_in * qc[h]
        den = jnp.sum(s_all[h], axis=1, keepdims=True) + w_in * jnp.sum(q[:, hs] * n_old[h], axis=1, keepdims=True)
        outs.append(num / jnp.maximum(jnp.abs(den), floor))
        c_ref[h] = carry_scale * c_old[h] + upd[h]
        n_ref[h:h + 1, :] = carry_scale * n_old[h] + jnp.sum(k[:, hs] * w, axis=0, keepdims=True)
        m_ref[h:h + 1, :] = jnp.broadcast_to(m_new, (1, ML_HD))
    hout = jnp.concatenate(outs, axis=-1)
    co_ref[0] = c_ref[...]
    no_ref[0] = n_ref[...]
    mo_ref[0] = m_ref[...]
    if finalize:
        o_ref[0] = _sigmoid(og_ref[0]) * (hp_ref[0] + hout)
    else:
        o_ref[0] = hout


def _mlstm_pass(p, state, other, tables, params, reverse, d):
    bsz, seq, _ = p.shape
    w = BRANCH_W
    qn = ML_CHUNK
    nchunk = seq // qn
    chunk, halo = _chunk_specs(qn, nchunk, seq // 8, reverse)
    cq = COL_ML // w
    finalize = other is not None
    rope = tables is not None
    in_specs = [halo(w, cq, -1), chunk(w, cq), halo(w, cq, 1), halo(w, cq + 1, -1), chunk(w, cq + 1),
                halo(w, cq + 1, 1), chunk(w, cq + 2), chunk(128, COL_ML_G // 128)]
    args = [p] * 8
    if rope:
        tab = pl.BlockSpec((qn, w), (lambda b, j: (nchunk - 1 - j, 0)) if reverse else (lambda b, j: (j, 0)))
        in_specs += [tab, tab]
        args += list(tables)
    if finalize:
        in_specs += [chunk(w, cq + 3), chunk(w, 0)]
        args += [p, other]
    st_shapes = [(ML_HEADS, ML_HD, ML_HD), (ML_HEADS, ML_HD), (ML_HEADS, ML_HD)]
    in_specs += [_batch_spec(s) for s in st_shapes]
    in_specs += [_const_spec((CONV_W, w)), _const_spec((1, w)), _const_spec((CONV_W, w)), _const_spec((1, w)),
                 _const_spec((1, 128)), _const_spec((1, 128))]
    args += list(state) + list(params)
    res = pl.pallas_call(
        functools.partial(_mlstm_kernel, reverse=reverse, finalize=finalize, rope=rope, nchunk=nchunk, d=d),
        grid=(bsz, nchunk),
        in_specs=in_specs,
        out_specs=[chunk(w, 0)] + [_batch_spec(s) for s in st_shapes],
        out_shape=[jax.ShapeDtypeStruct((bsz, seq, w), F32)]
                  + [jax.ShapeDtypeStruct((bsz,) + s, F32) for s in st_shapes],
        scratch_shapes=[pltpu.VMEM(s, F32) for s in st_shapes],
        compiler_params=pltpu.CompilerParams(dimension_semantics=("parallel", "arbitrary")),
        name="mlstm_scan",
    )(*args)
    return res[0], tuple(res[1:])


def mlstm_branch(p_l, p_c, conv_w, conv_b, i_bias, f_bias, need_ctx_out):
    bsz, seq, _ = p_l.shape
    w = BRANCH_W
    cos, sin = _rope_2d_tables(seq)
    tables = (jnp.tile(cos, (1, ML_HEADS)), jnp.tile(sin, (1, ML_HEADS)))
    zero_h = jnp.zeros_like(i_bias)
    ib = _lane_row(jnp.concatenate([i_bias, zero_h], axis=1).reshape(-1), 4 * ML_HEADS)
    fb = _lane_row(jnp.concatenate([zero_h, f_bias], axis=1).reshape(-1), 4 * ML_HEADS)
    params = (conv_w[:, :w], conv_b[:w].reshape(1, w), conv_w[:, w:], conv_b[w:].reshape(1, w), ib, fb)
    state0 = (jnp.zeros((bsz, ML_HEADS, ML_HD, ML_HD), F32), jnp.zeros((bsz, ML_HEADS, ML_HD), F32),
              jnp.zeros((bsz, ML_HEADS, ML_HD), F32))
    h_c = h_l = None
    for d, reverse in ((0, False), (1, True)):
        other_c = h_c if (d == 1 and need_ctx_out) else None
        h_c, st = _mlstm_pass(p_c, state0, other_c, None, params, reverse, d)
        h_l, _ = _mlstm_pass(p_l, st, h_l if d == 1 else None, tables, params, reverse, d)
    return h_l, (h_c if need_ctx_out else None)


MOE_TM = 512
ROUTE_LANES = 128


def _moe_router_kernel(x_ref, sc_ref, sh_ref, wr_ref, br_ref, hb_ref, rt_ref):
    h = x_ref[0] * (1.0 + sc_ref[0]) + sh_ref[0]
    hb_ref[0] = h.astype(BF16)
    logits = jnp.dot(h, wr_ref[...], precision=lax.Precision.HIGHEST, preferred_element_type=F32) + br_ref[...]
    lane = lax.broadcasted_iota(jnp.int32, logits.shape, 1)
    m1 = jnp.max(logits, axis=-1, keepdims=True)
    i1 = jnp.min(jnp.where(logits == m1, lane, ROUTE_LANES), axis=-1, keepdims=True)
    rest = jnp.where(lane == i1, MASK_NEG, logits)
    m2 = jnp.max(rest, axis=-1, keepdims=True)
    i2 = jnp.min(jnp.where(rest == m2, lane, ROUTE_LANES), axis=-1, keepdims=True)
    e2 = jnp.exp(m2 - m1)
    p1 = 1.0 / (1.0 + e2)
    p2 = e2 * p1
    rt_ref[0] = jnp.where(lane == 0, i1.astype(F32), jnp.where(lane == 1, i2.astype(F32),
                          jnp.where(lane == 2, p1, jnp.where(lane == 3, p2, 0.0))))


def moe_router(x, sc, sh, w_router, b_router, tm=512):
    bsz, seq, d = x.shape
    n_e = w_router.shape[1]
    tm = min(tm, seq)
    wr = jnp.zeros((d, ROUTE_LANES), F32).at[:, :n_e].set(w_router)
    br = jnp.full((1, ROUTE_LANES), MASK_NEG, F32).at[0, :n_e].set(b_router)
    return pl.pallas_call(
        _moe_router_kernel,
        grid=(bsz, seq // tm),
        in_specs=[pl.BlockSpec((1, tm, d), lambda b, i: (b, i, 0)),
                  pl.BlockSpec((1, 1, d), lambda b, i: (b, 0, 0)),
                  pl.BlockSpec((1, 1, d), lambda b, i: (b, 0, 0)),
                  pl.BlockSpec((d, ROUTE_LANES), lambda b, i: (0, 0)),
                  pl.BlockSpec((1, ROUTE_LANES), lambda b, i: (0, 0))],
        out_specs=[pl.BlockSpec((1, tm, d), lambda b, i: (b, i, 0)),
                   pl.BlockSpec((1, tm, ROUTE_LANES), lambda b, i: (b, i, 0))],
        out_shape=[jax.ShapeDtypeStruct((bsz, seq, d), BF16), jax.ShapeDtypeStruct((bsz, seq, ROUTE_LANES), F32)],
        compiler_params=pltpu.CompilerParams(dimension_semantics=("parallel", "parallel")),
        name="moe_router",
    )(x, sc, sh, wr, br)


def _route_tables(idx, n_e, tmg):
    n_tok = idx.shape[0]
    e_flat = idx.reshape(-1)
    onehot = (e_flat[:, None] == jnp.arange(n_e, dtype=jnp.int32)[None, :]).astype(jnp.int32)
    csum = jnp.cumsum(onehot, axis=0)
    rank = jnp.take_along_axis(csum - onehot, e_flat[:, None], axis=1)[:, 0]
    padded = ((csum[-1] + tmg - 1) // tmg) * tmg
    ends = jnp.cumsum(padded)
    pos = (ends - padded)[e_flat] + rank
    n_rows = TOP_K * n_tok + n_e * tmg
    ntiles = n_rows // tmg
    tile_start = jnp.arange(ntiles, dtype=jnp.int32) * tmg
    tile_expert = jnp.minimum(jnp.sum(tile_start[:, None] >= ends[None, :], axis=1), n_e - 1)
    src = jnp.zeros((n_rows,), jnp.int32).at[pos].set(jnp.arange(TOP_K * n_tok, dtype=jnp.int32) // TOP_K)
    meta = jnp.concatenate([tile_expert, ends[-1:] // tmg]).astype(jnp.int32)
    return pos.reshape(n_tok, TOP_K), src, meta


def _expert_changed(meta_ref, i):
    return (i == 0) | (meta_ref[i] != meta_ref[jnp.maximum(i - 1, 0)])


def _moe_up_kernel(meta_ref, x_ref, wg_ref, wu_ref, a_ref, wgb_ref, wub_ref, *, ntiles):
    i = pl.program_id(1)

    @pl.when(_expert_changed(meta_ref, i))
    def _():
        wgb_ref[...] = wg_ref[0].astype(BF16)
        wub_ref[...] = wu_ref[0].astype(BF16)

    @pl.when(i < meta_ref[ntiles])
    def _():
        x = x_ref[...]
        g = jnp.dot(x, wgb_ref[...], preferred_element_type=F32)
        u = jnp.dot(x, wub_ref[...], preferred_element_type=F32)
        a_ref[...] = (_silu(g) * u).astype(BF16)

    @pl.when(i >= meta_ref[ntiles])
    def _():
        a_ref[...] = jnp.zeros_like(a_ref)


def _moe_down_kernel(meta_ref, a_ref, wd_ref, y_ref, wdb_ref, *, ntiles):
    i = pl.program_id(0)

    @pl.when(_expert_changed(meta_ref, i))
    def _():
        wdb_ref[...] = wd_ref[0].astype(BF16)

    @pl.when(i < meta_ref[ntiles])
    def _():
        y_ref[...] = jnp.dot(a_ref[...], wdb_ref[...], preferred_element_type=F32)

    @pl.when(i >= meta_ref[ntiles])
    def _():
        y_ref[...] = jnp.zeros_like(y_ref)


def moe_experts(xs, meta, wg, wu, wd, tmg, tf):
    n_rows, d = xs.shape
    n_e, _, f = wg.shape
    ntiles = n_rows // tmg
    nf = f // tf
    once = pl.Buffered(1)
    a = pl.pallas_call(
        functools.partial(_moe_up_kernel, ntiles=ntiles),
        grid_spec=pltpu.PrefetchScalarGridSpec(
            num_scalar_prefetch=1,
            grid=(nf, ntiles),
            in_specs=[pl.BlockSpec((tmg, d), lambda j, i, m: (i, 0)),
                      pl.BlockSpec((1, d, tf), lambda j, i, m: (m[i], 0, j), pipeline_mode=once),
                      pl.BlockSpec((1, d, tf), lambda j, i, m: (m[i], 0, j), pipeline_mode=once)],
            out_specs=pl.BlockSpec((tmg, tf), lambda j, i, m: (i, j)),
            scratch_shapes=[pltpu.VMEM((d, tf), BF16), pltpu.VMEM((d, tf), BF16)]),
        out_shape=jax.ShapeDtypeStruct((n_rows, f), BF16),
        compiler_params=pltpu.CompilerParams(
            dimension_semantics=("arbitrary", "arbitrary"), vmem_limit_bytes=V7X_VMEM_LIMIT),
        name="moe_up",
    )(meta, xs, wg, wu)
    return pl.pallas_call(
        functools.partial(_moe_down_kernel, ntiles=ntiles),
        grid_spec=pltpu.PrefetchScalarGridSpec(
            num_scalar_prefetch=1,
            grid=(ntiles,),
            in_specs=[pl.BlockSpec((tmg, f), lambda i, m: (i, 0)),
                      pl.BlockSpec((1, f, d), lambda i, m: (m[i], 0, 0), pipeline_mode=once)],
            out_specs=pl.BlockSpec((tmg, d), lambda i, m: (i, 0)),
            scratch_shapes=[pltpu.VMEM((f, d), BF16)]),
        out_shape=jax.ShapeDtypeStruct((n_rows, d), F32),
        compiler_params=pltpu.CompilerParams(
            dimension_semantics=("arbitrary",), vmem_limit_bytes=V7X_VMEM_LIMIT),
        name="moe_down",
    )(meta, a, wd)


def _combine_ln_kernel(x_ref, gt_ref, rt_ref, ya_ref, yb_ref, g_ref, b_ref, o_ref, *, alpha):
    rt = rt_ref[0]
    y = rt[:, 2:3] * ya_ref[0] + rt[:, 3:4] * yb_ref[0]
    o_ref[0] = _layer_norm_rows(alpha * x_ref[0] + gt_ref[0] * y, g_ref[...], b_ref[...])


def combine_residual_ln(x, gate, route, ya, yb, ln_g, ln_b, alpha, tm=512):
    bsz, seq, d = x.shape
    tm = min(tm, seq)
    row = pl.BlockSpec((1, tm, d), lambda b, i: (b, i, 0))
    return pl.pallas_call(
        functools.partial(_combine_ln_kernel, alpha=alpha),
        grid=(bsz, seq // tm),
        in_specs=[row, pl.BlockSpec((1, 1, d), lambda b, i: (b, 0, 0)),
                  pl.BlockSpec((1, tm, ROUTE_LANES), lambda b, i: (b, i, 0)), row, row,
                  pl.BlockSpec((1, d), lambda b, i: (0, 0)), pl.BlockSpec((1, d), lambda b, i: (0, 0))],
        out_specs=row,
        out_shape=jax.ShapeDtypeStruct((bsz, seq, d), F32),
        compiler_params=pltpu.CompilerParams(
            dimension_semantics=("parallel", "parallel"), vmem_limit_bytes=V7X_VMEM_LIMIT),
        name="moe_combine_ln",
    )(x, gate, route, ya, yb, ln_g.reshape(1, d), ln_b.reshape(1, d))


def moe_residual_ln(x, sc, sh, gate, w_router, b_router, wg, wu, wd, ln_g, ln_b, alpha):
    bsz, seq, d = x.shape
    n_e, _, f = wg.shape
    n_tok = bsz * seq
    tmg = min(MOE_TM, TOP_K * n_tok)
    hb, route = moe_router(x, sc, sh, w_router, b_router)
    idx = route[..., :TOP_K].astype(jnp.int32).reshape(n_tok, TOP_K)
    pos, src, meta = _route_tables(idx, n_e, tmg)
    xs = jnp.take(hb.reshape(n_tok, d), src, axis=0)
    ys = moe_experts(xs, meta, wg, wu, wd, tmg, f // 2)
    ya = jnp.take(ys, pos[:, 0], axis=0).reshape(bsz, seq, d)
    yb = jnp.take(ys, pos[:, 1], axis=0).reshape(bsz, seq, d)
    return combine_residual_ln(x, gate, route, ya, yb, ln_g, ln_b, alpha)


def _mixers(p_l, p_c, need_ctx_out, lru_conv_w, lru_conv_b, lru_wa, lru_ba, lru_wx, lru_bx, lru_lambda,
            na_rpb, ssd_conv_w, ssd_conv_b, ssd_dt_bias, ssd_a_log, ssd_d, ssd_norm_g, ml_conv_w, ml_conv_b,
            ml_i_bias, ml_f_bias):
    ya = lru_branch(p_l, p_c, lru_conv_w, lru_conv_b, lru_wa, lru_ba, lru_wx, lru_bx, lru_lambda, need_ctx_out)
    yb = na_branch(p_l, p_c, na_rpb, need_ctx_out)
    yc = ssd_branch(p_l, p_c, ssd_conv_w, ssd_conv_b, ssd_dt_bias, ssd_a_log, ssd_d, ssd_norm_g, need_ctx_out)
    yd = mlstm_branch(p_l, p_c, ml_conv_w, ml_conv_b, ml_i_bias, ml_f_bias, need_ctx_out)
    return (ya[0], yb[0], yc[0], yd[0]), (ya[1], yb[1], yc[1], yd[1])


def kernel(x, c, ctx, c_ctx, w_ada, b_ada, w_in, lru_conv_w, lru_conv_b, lru_wa, lru_ba, lru_wx, lru_bx, lru_lambda, na_rpb, ssd_conv_w, ssd_conv_b, ssd_dt_bias, ssd_a_log, ssd_d, ssd_norm_g, ml_conv_w, ml_conv_b, ml_i_bias, ml_f_bias, w_branch, w_out, ln_g, ln_b, ffn_w_gate, ffn_w_up, ffn_w_down, moe_w_router, moe_b_router, moe_w_gate, moe_w_up, moe_w_down):
    depth = w_in.shape[0]
    bsz, seq, d = x.shape
    alpha = (2.0 * depth) ** 0.25
    cvecs = jnp.zeros((8, d), F32).at[:bsz].set(c).at[bsz].set(c_ctx)
    xl, xc = x, ctx
    for l in range(depth):
        need_ctx_out = l < depth - 1
        mod = ada_modulation(cvecs, w_ada[l], b_ada[l])
        mod_l = jnp.split(mod[:bsz, None, :], 6, axis=-1)
        mod_c = jnp.split(jnp.broadcast_to(mod[bsz:bsz + 1, None, :], (bsz, 1, 6 * d)), 6, axis=-1)
        w_mix, w_gates = _pack_w_in(w_in[l])
        p_l, hb_l = mod_matmul(xl, mod_l[1], mod_l[0], w_mix)
        p_c, hb_c = mod_matmul(xc, mod_c[1], mod_c[0], w_mix)
        br_l, br_c = _mixers(p_l, p_c, need_ctx_out, lru_conv_w[l], lru_conv_b[l], lru_wa[l], lru_ba[l],
                             lru_wx[l], lru_bx[l], lru_lambda[l], na_rpb[l], ssd_conv_w[l], ssd_conv_b[l],
                             ssd_dt_bias[l], ssd_a_log[l], ssd_d[l], ssd_norm_g[l], ml_conv_w[l], ml_conv_b[l],
                             ml_i_bias[l], ml_f_bias[l])
        wb = w_branch[l].astype(BF16)
        wo = w_out[l].astype(BF16)
        xl = proj_residual_ln(merge_branches(hb_l, w_gates, br_l, wb), wo, xl, mod_l[2], ln_g[l, 0], ln_b[l, 0],
                              alpha)
        if need_ctx_out:
            xc = proj_residual_ln(merge_branches(hb_c, w_gates, br_c, wb), wo, xc, mod_c[2], ln_g[l, 0],
                                  ln_b[l, 0], alpha)
        j = l // 2
        if l % 2 == 0:
            wg = ffn_w_gate[j].astype(BF16)
            wu = ffn_w_up[j].astype(BF16)
            wd = ffn_w_down[j].astype(BF16)

            def ffn(h, m, wg=wg, wu=wu, wd=wd):
                return ffn_residual_ln(h, m[4], m[3], m[5], wg, wu, wd, ln_g[l, 1], ln_b[l, 1], alpha)
        else:
            def ffn(h, m, j=j):
                return moe_residual_ln(h, m[4], m[3], m[5], moe_w_router[j], moe_b_router[j], moe_w_gate[j],
                                       moe_w_up[j], moe_w_down[j], ln_g[l, 1], ln_b[l, 1], alpha)

        xl = ffn(xl, mod_l)
        if need_ctx_out:
            xc = ffn(xc, mod_c)
    return xl
```

```python
import functools
import math

import numpy as np
import jax
import jax.numpy as jnp
from jax import lax
from jax.experimental import pallas as pl
from jax.experimental.pallas import tpu as pltpu

F32 = jnp.float32
BF16 = jnp.bfloat16

D_MODEL = 2048
GRID_W = 64
N_BRANCH = 4
BRANCH_W = D_MODEL // N_BRANCH
CONV_W = 4
LN_EPS = 1e-5
LRU_BLOCKS = 8
LRU_BW = BRANCH_W // LRU_BLOCKS
LRU_C = 8.0
NA_HEADS = 8
NA_HD = BRANCH_W // NA_HEADS
NA_KH = 8
NA_KW = 16
SSD_HEADS = 8
SSD_HD = BRANCH_W // SSD_HEADS
SSD_GROUPS = 2
SSD_STATE = 64
SSD_CHUNK = 128
SSD_GN = SSD_GROUPS * SSD_STATE
ML_HEADS = 4
ML_HD = BRANCH_W // ML_HEADS
ML_CHUNK = 128
ROPE_BASE = 10000.0
N_EXPERTS = 8
TOP_K = 2

V7X_VMEM_LIMIT = 52 * 1024 * 1024

COL_LRU_X = 0
COL_LRU_G = 512
COL_NA = 1024
COL_SSD_Z = 2560
COL_SSD_X = 3072
COL_ML = 3584
COL_SSD_B = 5632
COL_SSD_C = 5760
COL_SSD_DT = 5888
COL_ML_G = 5904
N_MIX = 5920
N_MIX_PAD = 6144


def _pack_w_in(w):
    parts = [w[:, 0:1024], w[:, 1024:2560], w[:, 2560:3584], w[:, 3856:5904],
             w[:, 3584:3840], w[:, 3840:3856], w[:, 5904:5920],
             jnp.zeros((w.shape[0], N_MIX_PAD - N_MIX), w.dtype)]
    return jnp.concatenate(parts, axis=1).astype(BF16), w[:, N_MIX:].astype(BF16)


def _sigmoid(x):
    return 1.0 / (1.0 + jnp.exp(-x))


def _layer_norm_rows(z, g, b):
    mu = jnp.mean(z, axis=-1, keepdims=True)
    zc = z - mu
    var = jnp.mean(zc * zc, axis=-1, keepdims=True)
    return zc * lax.rsqrt(var + LN_EPS) * g + b


def _mod_matmul_kernel(x_ref, sc_ref, sh_ref, w_ref, o_ref, hb_ref):
    @pl.when(pl.program_id(2) == 0)
    def _():
        hb_ref[0] = (x_ref[0] * (1.0 + sc_ref[0]) + sh_ref[0]).astype(BF16)

    o_ref[0] = jnp.dot(hb_ref[0], w_ref[...], preferred_element_type=F32)


def mod_matmul(x, sc, sh, w, tm=1024, tn=512):
    bsz, seq, d = x.shape
    n = w.shape[1]
    tm = min(tm, seq)
    return pl.pallas_call(
        _mod_matmul_kernel,
        grid=(bsz, seq // tm, n // tn),
        in_specs=[pl.BlockSpec((1, tm, d), lambda b, i, j: (b, i, 0)),
                  pl.BlockSpec((1, 1, d), lambda b, i, j: (b, 0, 0)),
                  pl.BlockSpec((1, 1, d), lambda b, i, j: (b, 0, 0)),
                  pl.BlockSpec((d, tn), lambda b, i, j: (0, j))],
        out_specs=[pl.BlockSpec((1, tm, tn), lambda b, i, j: (b, i, j)),
                   pl.BlockSpec((1, tm, d), lambda b, i, j: (b, i, 0))],
        out_shape=[jax.ShapeDtypeStruct((bsz, seq, n), F32), jax.ShapeDtypeStruct((bsz, seq, d), BF16)],
        compiler_params=pltpu.CompilerParams(
            dimension_semantics=("parallel", "parallel", "arbitrary"), vmem_limit_bytes=V7X_VMEM_LIMIT),
        name="in_proj",
    )(x, sc, sh, w)


def _merge_kernel(hb_ref, g0, g1, g2, g3, ya, yb, yc, yd, wb_ref, o_ref):
    hb = hb_ref[0]
    acc = None
    for n, (wg, y) in enumerate(((g0, ya), (g1, yb), (g2, yc), (g3, yd))):
        gate = _sigmoid(jnp.dot(hb, wg[...], preferred_element_type=F32))
        t = gate * jnp.dot(y[0].astype(BF16), wb_ref[n], preferred_element_type=F32)
        acc = t if acc is None else acc + t
    o_ref[0] = acc.astype(BF16)


def merge_branches(hb, w_gates, branches, wb, tm=512, tn=512):
    bsz, seq, dk = hb.shape
    d = wb.shape[2]
    tm = min(tm, seq)
    nj = d // tn
    g_specs = [pl.BlockSpec((dk, tn), functools.partial(lambda b, i, j, n: (0, n * nj + j), n=n))
               for n in range(N_BRANCH)]
    y_specs = [pl.BlockSpec((1, tm, BRANCH_W), lambda b, i, j: (b, i, 0)) for _ in range(N_BRANCH)]
    return pl.pallas_call(
        _merge_kernel,
        grid=(bsz, seq // tm, nj),
        in_specs=[pl.BlockSpec((1, tm, dk), lambda b, i, j: (b, i, 0))] + g_specs + y_specs
                 + [pl.BlockSpec((N_BRANCH, BRANCH_W, tn), lambda b, i, j: (0, 0, j))],
        out_specs=pl.BlockSpec((1, tm, tn), lambda b, i, j: (b, i, j)),
        out_shape=jax.ShapeDtypeStruct((bsz, seq, d), BF16),
        compiler_params=pltpu.CompilerParams(
            dimension_semantics=("parallel", "parallel", "arbitrary"), vmem_limit_bytes=V7X_VMEM_LIMIT),
        name="merge",
    )(hb, w_gates, w_gates, w_gates, w_gates, *branches, wb)


def _proj_ln_kernel(m_ref, w_ref, x_ref, gt_ref, g_ref, b_ref, o_ref, *, alpha):
    y = jnp.dot(m_ref[0], w_ref[...], preferred_element_type=F32)
    o_ref[0] = _layer_norm_rows(alpha * x_ref[0] + gt_ref[0] * y, g_ref[...], b_ref[...])


def proj_residual_ln(m, w, x, gate, ln_g, ln_b, alpha, tm=256):
    bsz, seq, k = m.shape
    d = w.shape[1]
    tm = min(tm, seq)
    return pl.pallas_call(
        functools.partial(_proj_ln_kernel, alpha=alpha),
        grid=(bsz, seq // tm),
        in_specs=[pl.BlockSpec((1, tm, k), lambda b, i: (b, i, 0)),
                  pl.BlockSpec((k, d), lambda b, i: (0, 0)),
                  pl.BlockSpec((1, tm, d), lambda b, i: (b, i, 0)),
                  pl.BlockSpec((1, 1, d), lambda b, i: (b, 0, 0)),
                  pl.BlockSpec((1, d), lambda b, i: (0, 0)),
                  pl.BlockSpec((1, d), lambda b, i: (0, 0))],
        out_specs=pl.BlockSpec((1, tm, d), lambda b, i: (b, i, 0)),
        out_shape=jax.ShapeDtypeStruct((bsz, seq, d), F32),
        compiler_params=pltpu.CompilerParams(
            dimension_semantics=("parallel", "parallel"), vmem_limit_bytes=V7X_VMEM_LIMIT),
        name="out_proj_ln",
    )(m, w, x, gate, ln_g.reshape(1, d), ln_b.reshape(1, d))


def _ffn_kernel(x_ref, sc_ref, sh_ref, gt_ref, wg_ref, wu_ref, wd_ref, lg_ref, lb_ref, o_ref, hb_ref, acc_ref, *, alpha):
    j = pl.program_id(2)

    @pl.when(j == 0)
    def _():
        hb_ref[...] = (x_ref[0] * (1.0 + sc_ref[0]) + sh_ref[0]).astype(BF16)
        acc_ref[...] = jnp.zeros_like(acc_ref)

    hb = hb_ref[...]
    g = jnp.dot(hb, wg_ref[...], preferred_element_type=F32)
    u = jnp.dot(hb, wu_ref[...], preferred_element_type=F32)
    acc_ref[...] += jnp.dot((_silu(g) * u).astype(BF16), wd_ref[...], preferred_element_type=F32)

    @pl.when(j == pl.num_programs(2) - 1)
    def _():
        o_ref[0] = _layer_norm_rows(alpha * x_ref[0] + gt_ref[0] * acc_ref[...], lg_ref[...], lb_ref[...])


def ffn_residual_ln(x, sc, sh, gate, wg, wu, wd, ln_g, ln_b, alpha, tm=512, tf=512):
    bsz, seq, d = x.shape
    f = wg.shape[1]
    tm = min(tm, seq)
    vec = pl.BlockSpec((1, 1, d), lambda b, i, j: (b, 0, 0))
    par = pl.BlockSpec((1, d), lambda b, i, j: (0, 0))
    return pl.pallas_call(
        functools.partial(_ffn_kernel, alpha=alpha),
        grid=(bsz, seq // tm, f // tf),
        in_specs=[pl.BlockSpec((1, tm, d), lambda b, i, j: (b, i, 0)), vec, vec, vec,
                  pl.BlockSpec((d, tf), lambda b, i, j: (0, j)),
                  pl.BlockSpec((d, tf), lambda b, i, j: (0, j)),
                  pl.BlockSpec((tf, d), lambda b, i, j: (j, 0)), par, par],
        out_specs=pl.BlockSpec((1, tm, d), lambda b, i, j: (b, i, 0)),
        out_shape=jax.ShapeDtypeStruct((bsz, seq, d), F32),
        scratch_shapes=[pltpu.VMEM((tm, d), BF16), pltpu.VMEM((tm, d), F32)],
        compiler_params=pltpu.CompilerParams(
            dimension_semantics=("parallel", "parallel", "arbitrary"), vmem_limit_bytes=V7X_VMEM_LIMIT),
        name="ffn_ln",
    )(x, sc, sh, gate, wg, wu, wd, ln_g.reshape(1, d), ln_b.reshape(1, d))


def _ada_kernel(c_ref, w_ref, b_ref, o_ref):
    cv = c_ref[...]
    o_ref[0] = jnp.dot(cv * _sigmoid(cv), w_ref[0], precision=lax.Precision.HIGHEST,
                       preferred_element_type=F32) + b_ref[0]


def ada_modulation(cvecs, w, b, tn=1536):
    r, d = cvecs.shape
    depth, _, n = w.shape
    return pl.pallas_call(
        _ada_kernel,
        grid=(depth, n // tn),
        in_specs=[pl.BlockSpec((r, d), lambda l, j: (0, 0)), pl.BlockSpec((1, d, tn), lambda l, j: (l, 0, j)),
                  pl.BlockSpec((1, 1, tn), lambda l, j: (l, 0, j))],
        out_specs=pl.BlockSpec((1, r, tn), lambda l, j: (l, 0, j)),
        out_shape=jax.ShapeDtypeStruct((depth, r, n), F32),
        compiler_params=pltpu.CompilerParams(
            dimension_semantics=("parallel", "parallel"), vmem_limit_bytes=V7X_VMEM_LIMIT),
        name="ada_mod",
    )(cvecs, w, b.reshape(depth, 1, n))


NA_NEG = -1e30


def _na_bias_slabs(rpb):
    w = jnp.arange(GRID_W)
    cs = jnp.clip(w - NA_KW // 2, 0, GRID_W - NA_KW)
    ok = (w[None, :] >= cs[:, None]) & (w[None, :] < cs[:, None] + NA_KW)
    dc = jnp.clip(w[None, :] - w[:, None] + (NA_KW - 1), 0, 2 * NA_KW - 2)
    tab = jnp.where(ok, rpb[:, :, dc], NA_NEG)
    idx = jnp.arange(NA_KH)[:, None] + jnp.arange(NA_KH)[None, :]
    slab = tab[:, idx]
    return slab.transpose(1, 0, 3, 2, 4).reshape(NA_KH, NA_HEADS, GRID_W, NA_KH * GRID_W)


def _softmax2(s_a, s_b):
    m = jnp.maximum(jnp.max(s_a, axis=-1, keepdims=True), jnp.max(s_b, axis=-1, keepdims=True))
    e_a = jnp.exp(s_a - m)
    e_b = jnp.exp(s_b - m)
    inv = 1.0 / (jnp.sum(e_a, axis=-1, keepdims=True) + jnp.sum(e_b, axis=-1, keepdims=True))
    return e_a * inv, e_b * inv


_NT = (((1,), (1,)), ((), ()))


def _na_kernel(q_ref, kp_ref, kc_ref, kn_ref, vp_ref, vc_ref, vn_ref, ck_ref, cv_ref, bias_ref, o_ref,
               kw_ref, vw_ref, ckb_ref, cvb_ref, *, rows):
    i = pl.program_id(1)
    blk = NA_KH * GRID_W
    for n, (kr, vr) in enumerate(((kp_ref, vp_ref), (kc_ref, vc_ref), (kn_ref, vn_ref))):
        kw_ref[n * blk:(n + 1) * blk, :] = kr[0].astype(BF16)
        vw_ref[n * blk:(n + 1) * blk, :] = vr[0].astype(BF16)
    ckb_ref[...] = ck_ref[0].astype(BF16)
    cvb_ref[...] = cv_ref[0].astype(BF16)
    scale = NA_HD ** -0.5

    def body(rr, carry):
        r = i * NA_KH + rr
        rs = jnp.clip(r - NA_KH // 2, 0, rows - NA_KH)
        off = pl.multiple_of((rs - (i - 1) * NA_KH) * GRID_W, GRID_W)
        v = rs - r + (NA_KH - 1)
        q_all = (q_ref[0, pl.ds(pl.multiple_of(rr * GRID_W, GRID_W), GRID_W), :] * scale).astype(BF16)
        kwin = kw_ref[pl.ds(off, blk), :]
        vwin = vw_ref[pl.ds(off, blk), :]
        first = lax.broadcasted_iota(jnp.int32, (GRID_W, 2 * NA_HD), 1) < NA_HD
        pairs = [slice(n * 2 * NA_HD, (n + 1) * 2 * NA_HD) for n in range(NA_HEADS // 2)]
        scores = []
        for h in range(NA_HEADS):
            ps = pairs[h // 2]
            q = jnp.where(first if h % 2 == 0 else ~first, q_all[:, ps], jnp.zeros((), BF16))
            scores.append((lax.dot_general(q, kwin[:, ps], _NT, preferred_element_type=F32) + bias_ref[v, h],
                           lax.dot_general(q, ckb_ref[:, ps], _NT, preferred_element_type=F32)))
        probs = [_softmax2(s_w, s_c) for s_w, s_c in scores]
        both = [jnp.dot(p_w.astype(BF16), vwin[:, pairs[h // 2]], preferred_element_type=F32)
                + jnp.dot(p_c.astype(BF16), cvb_ref[:, pairs[h // 2]], preferred_element_type=F32)
                for h, (p_w, p_c) in enumerate(probs)]
        outs = [jnp.where(first, both[2 * n], both[2 * n + 1]) for n in range(NA_HEADS // 2)]
        o_ref[0, pl.ds(pl.multiple_of(rr * GRID_W, GRID_W), GRID_W), :] = jnp.concatenate(outs, axis=-1)
        return carry

    lax.fori_loop(0, NA_KH, body, 0)


def _na_ctx_kernel(q_ref, k_ref, v_ref, o_ref):
    scale = NA_HD ** -0.5
    q_all = (q_ref[0] * scale).astype(BF16)
    k_all = k_ref[0].astype(BF16)
    v_all = v_ref[0].astype(BF16)
    outs = []
    for h in range(NA_HEADS):
        hs = slice(h * NA_HD, (h + 1) * NA_HD)
        s = lax.dot_general(q_all[:, hs], k_all[:, hs], _NT, preferred_element_type=F32)
        e = jnp.exp(s - jnp.max(s, axis=-1, keepdims=True))
        p = e * (1.0 / jnp.sum(e, axis=-1, keepdims=True))
        outs.append(jnp.dot(p.astype(BF16), v_all[:, hs], preferred_element_type=F32))
    o_ref[0] = jnp.concatenate(outs, axis=-1)


def na_branch(p_l, p_c, rpb, need_ctx_out):
    bsz, seq, _ = p_l.shape
    n_ctx = p_c.shape[1]
    rows = seq // GRID_W
    assert rows % NA_KH == 0 and rows >= 2 * NA_KH
    w = BRANCH_W
    blk = NA_KH * GRID_W
    nblk = rows // NA_KH
    cq = COL_NA // w

    def shifted(col, delta):
        return pl.BlockSpec((1, blk, w), lambda b, i: (b, jnp.clip(i + delta, 0, nblk - 1), col))

    y_l = pl.pallas_call(
        functools.partial(_na_kernel, rows=rows),
        grid=(bsz, nblk),
        in_specs=[shifted(cq, 0), shifted(cq + 1, -1), shifted(cq + 1, 0), shifted(cq + 1, 1),
                  shifted(cq + 2, -1), shifted(cq + 2, 0), shifted(cq + 2, 1),
                  pl.BlockSpec((1, n_ctx, w), lambda b, i: (b, 0, cq + 1)),
                  pl.BlockSpec((1, n_ctx, w), lambda b, i: (b, 0, cq + 2)),
                  pl.BlockSpec((NA_KH, NA_HEADS, GRID_W, blk), lambda b, i: (0, 0, 0, 0))],
        out_specs=pl.BlockSpec((1, blk, w), lambda b, i: (b, i, 0)),
        out_shape=jax.ShapeDtypeStruct((bsz, seq, w), F32),
        scratch_shapes=[pltpu.VMEM((3 * blk, w), BF16), pltpu.VMEM((3 * blk, w), BF16),
                        pltpu.VMEM((n_ctx, w), BF16), pltpu.VMEM((n_ctx, w), BF16)],
        compiler_params=pltpu.CompilerParams(
            dimension_semantics=("parallel", "arbitrary"), vmem_limit_bytes=V7X_VMEM_LIMIT),
        name="na_attn",
    )(p_l, p_l, p_l, p_l, p_l, p_l, p_l, p_c, p_c, _na_bias_slabs(rpb))
    y_c = None
    if need_ctx_out:
        y_c = pl.pallas_call(
            _na_ctx_kernel,
            grid=(bsz,),
            in_specs=[pl.BlockSpec((1, n_ctx, w), functools.partial(lambda b, c: (b, 0, c), c=cq + n)) for n in range(3)],
            out_specs=pl.BlockSpec((1, n_ctx, w), lambda b: (b, 0, 0)),
            out_shape=jax.ShapeDtypeStruct((bsz, n_ctx, w), F32),
            compiler_params=pltpu.CompilerParams(dimension_semantics=("parallel",)),
            name="na_ctx_attn",
        )(p_c, p_c, p_c)
    return y_l, y_c


def _gelu_tanh(x):
    return 0.5 * x * (1.0 + jnp.tanh(math.sqrt(2.0 / math.pi) * (x + 0.044715 * (x * x * x))))


def _silu(x):
    return x * _sigmoid(x)


def _softplus(x):
    return jnp.maximum(x, 0.0) + jnp.log1p(jnp.exp(-jnp.abs(x)))


def _rope_2d_tables(seq):
    t = jnp.arange(seq, dtype=jnp.int32)
    pos = jnp.stack([t // GRID_W, t % GRID_W], axis=-1).astype(F32)
    nf = ML_HD // 4
    inv_freq = ROPE_BASE ** (-jnp.arange(nf, dtype=F32) / nf)
    ang = jnp.broadcast_to(pos[:, :, None, None] * inv_freq, (seq, 2, 2, nf)).reshape(seq, ML_HD)
    return jnp.cos(ang), jnp.sin(ang)


def _chunk_specs(tt, nchunk, n8, reverse):
    def pos(j):
        return nchunk - 1 - j if reverse else j

    def chunk(width, col):
        return pl.BlockSpec((1, tt, width), lambda b, j: (b, pos(j), col))

    def halo(width, col, delta):
        if delta < 0:
            return pl.BlockSpec((1, 8, width), lambda b, j: (b, jnp.maximum(pos(j) * (tt // 8) - 1, 0), col))
        return pl.BlockSpec((1, 8, width), lambda b, j: (b, jnp.minimum((pos(j) + 1) * (tt // 8), n8 - 1), col))

    return chunk, halo


def _const_spec(shape):
    return pl.BlockSpec(shape, lambda b, j: (0,) * len(shape))


def _batch_spec(shape):
    return pl.BlockSpec((1,) + shape, lambda b, j: (b,) + (0,) * len(shape))


def _conv4(x, prev8, next8, has_prev, has_next, cw, cb):
    tt = x.shape[0]
    row = lax.broadcasted_iota(jnp.int32, x.shape, 0)
    p6 = prev8[6:7, :] * has_prev
    p7 = prev8[7:8, :] * has_prev
    n0 = next8[0:1, :] * has_next
    xm1 = jnp.where(row == 0, p7, pltpu.roll(x, 1, 0))
    xm2 = jnp.where(row == 0, p6, jnp.where(row == 1, p7, pltpu.roll(x, 2, 0)))
    xp1 = jnp.where(row == tt - 1, n0, pltpu.roll(x, tt - 1, 0))
    return cw[0:1, :] * xm2 + cw[1:2, :] * xm1 + cw[2:3, :] * x + cw[3:4, :] * xp1 + cb


def _cumsum_rows(x, reverse):
    n = x.shape[0]
    row = lax.broadcasted_iota(jnp.int32, x.shape, 0)
    s = 1
    while s < n:
        if reverse:
            x = x + jnp.where(row < n - s, pltpu.roll(x, n - s, 0), 0.0)
        else:
            x = x + jnp.where(row >= s, pltpu.roll(x, s, 0), 0.0)
        s *= 2
    return x


def _causal_mask(n, reverse):
    ii = lax.broadcasted_iota(jnp.int32, (n, n), 0)
    jj = lax.broadcasted_iota(jnp.int32, (n, n), 1)
    return (jj >= ii) if reverse else (jj <= ii)


MASK_NEG = -1e30


LRU_TT = 256


def _lru_kernel(*refs, reverse, finalize, nchunk):
    if finalize:
        (xp_ref, x_ref, xn_ref, g_ref, ho_ref, h0_ref, cw_ref, cb_ref, w_ref, bias_ref, sp_ref,
         o_ref, hl_ref, carry_ref) = refs
    else:
        (xp_ref, x_ref, xn_ref, h0_ref, cw_ref, cb_ref, w_ref, bias_ref, sp_ref, o_ref, hl_ref, carry_ref) = refs
    j = pl.program_id(1)
    c = (nchunk - 1 - j) if reverse else j

    @pl.when(j == 0)
    def _():
        carry_ref[...] = h0_ref[0]

    tt = x_ref.shape[1]
    w = x_ref.shape[2]
    xc = _conv4(x_ref[0], xp_ref[0], xn_ref[0], (c > 0).astype(F32), (c < nchunk - 1).astype(F32),
                cw_ref[...], cb_ref[...])
    g = jnp.dot(xc.astype(BF16), w_ref[...], preferred_element_type=F32)
    r = _sigmoid(g[:, :w] + bias_ref[0:1, :])
    ig = _sigmoid(g[:, w:] + bias_ref[1:2, :])
    log_a = -LRU_C * r * sp_ref[...]
    a = jnp.exp(log_a)
    u = jnp.sqrt(1.0 - jnp.exp(2.0 * log_a)) * (ig * xc)
    sub = lax.broadcasted_iota(jnp.int32, (tt, w), 0) % 8
    s = 1
    while s < 8:
        if reverse:
            keep = sub < 8 - s
            a_s = jnp.where(keep, pltpu.roll(a, tt - s, 0), 1.0)
            u_s = jnp.where(keep, pltpu.roll(u, tt - s, 0), 0.0)
        else:
            keep = sub >= s
            a_s = jnp.where(keep, pltpu.roll(a, s, 0), 1.0)
            u_s = jnp.where(keep, pltpu.roll(u, s, 0), 0.0)
        u = a * u_s + u
        a = a * a_s
        s *= 2
    new_carry = carry_ref[...]
    nblk = tt // 8
    hs = [None] * nblk
    for blk in (reversed(range(nblk)) if reverse else range(nblk)):
        hb = u[blk * 8:(blk + 1) * 8, :] + a[blk * 8:(blk + 1) * 8, :] * new_carry
        new_carry = hb[0:1, :] if reverse else hb[7:8, :]
        hs[blk] = hb
    h = jnp.concatenate(hs, axis=0)
    carry_ref[...] = new_carry
    hl_ref[0] = new_carry
    if finalize:
        o_ref[0] = (ho_ref[0] + h) * _gelu_tanh(g_ref[0])
    else:
        o_ref[0] = h


def _lru_pass(p, h0, other, cw, cb, wcat, bias, sp, reverse):
    bsz, seq, _ = p.shape
    w = BRANCH_W
    tt = min(LRU_TT, seq)
    nchunk = seq // tt
    chunk, halo = _chunk_specs(tt, nchunk, seq // 8, reverse)
    cx = COL_LRU_X // w
    finalize = other is not None
    in_specs = [halo(w, cx, -1), chunk(w, cx), halo(w, cx, 1)]
    args = [p, p, p]
    if finalize:
        in_specs += [chunk(w, COL_LRU_G // w), chunk(w, 0)]
        args += [p, other]
    in_specs += [_batch_spec((1, w)), _const_spec((CONV_W, w)), _const_spec((1, w)),
                 _const_spec((w, 2 * w)), _const_spec((2, w)), _const_spec((1, w))]
    args += [h0, cw, cb, wcat, bias, sp]
    return pl.pallas_call(
        functools.partial(_lru_kernel, reverse=reverse, finalize=finalize, nchunk=nchunk),
        grid=(bsz, nchunk),
        in_specs=in_specs,
        out_specs=[chunk(w, 0), _batch_spec((1, w))],
        out_shape=[jax.ShapeDtypeStruct((bsz, seq, w), F32), jax.ShapeDtypeStruct((bsz, 1, w), F32)],
        scratch_shapes=[pltpu.VMEM((1, w), F32)],
        compiler_params=pltpu.CompilerParams(dimension_semantics=("parallel", "arbitrary")),
        name="lru_scan",
    )(*args)


def _block_diag(wg):
    g, n, _ = wg.shape
    eye = jnp.eye(g, dtype=wg.dtype)
    return (wg[:, :, None, :] * eye[:, None, :, None]).reshape(g * n, g * n)


def lru_branch(p_l, p_c, conv_w, conv_b, wa, ba, wx, bx, lam, need_ctx_out):
    bsz = p_l.shape[0]
    w = BRANCH_W
    cb = conv_b.reshape(1, w)
    sp = jax.nn.softplus(-lam)
    zeros = jnp.zeros((bsz, 1, w), F32)
    h_c = h_l = None
    for d, reverse in ((0, False), (1, True)):
        wcat = jnp.concatenate([_block_diag(wa[d]), _block_diag(wx[d])], axis=1).astype(BF16)
        bias = jnp.stack([ba[d], bx[d]])
        other_c = h_c if (d == 1 and need_ctx_out) else None
        h_c, st = _lru_pass(p_c, zeros, other_c, conv_w, cb, wcat, bias, sp[d:d + 1], reverse)
        h_l, _ = _lru_pass(p_l, st, h_l if d == 1 else None, conv_w, cb, wcat, bias, sp[d:d + 1], reverse)
    return h_l, (h_c if need_ctx_out else None)


def _ssd_kernel(*refs, reverse, finalize, nchunk, d):
    (xp_ref, x_ref, xn_ref, bp_ref, bc_ref, bn_ref, dt_ref) = refs[:7]
    k = 7
    if finalize:
        z_ref, yp_ref = refs[k:k + 2]
        k += 2
    s0_ref, cwx_ref, cbx_ref, cwb_ref, cbb_ref, dtb_ref, a_ref = refs[k:k + 7]
    k += 7
    if finalize:
        dsk_ref, ng_ref = refs[k:k + 2]
        k += 2
    o_ref, so_ref, s_ref = refs[k:k + 3]
    j = pl.program_id(1)
    c = (nchunk - 1 - j) if reverse else j

    @pl.when(j == 0)
    def _():
        s_ref[...] = s0_ref[0]

    q = x_ref.shape[1]
    has_prev = (c > 0).astype(F32)
    has_next = (c < nchunk - 1).astype(F32)
    xs = _silu(_conv4(x_ref[0], xp_ref[0], xn_ref[0], has_prev, has_next, cwx_ref[...], cbx_ref[...]))
    bc = _silu(_conv4(bc_ref[0], bp_ref[0], bn_ref[0], has_prev, has_next, cwb_ref[...], cbb_ref[...]))
    dt = _softplus(dt_ref[0] + dtb_ref[...])
    cum = _cumsum_rows(dt * a_ref[...], reverse)
    tot = cum[0:1, :] if reverse else cum[q - 1:q, :]
    w_end = jnp.exp(tot - cum) * dt
    ecum = jnp.exp(cum)
    etot = jnp.exp(tot)
    dt_t = dt.T
    cum_t = cum.T
    b_t = bc[:, :SSD_GN].T
    mask = _causal_mask(q, reverse)
    xb = xs.astype(BF16)
    cgs = [bc[:, SSD_GN + g * SSD_STATE:SSD_GN + (g + 1) * SSD_STATE].astype(BF16) for g in range(SSD_GROUPS)]
    cbs = [lax.dot_general(cgs[g], bc[:, g * SSD_STATE:(g + 1) * SSD_STATE].astype(BF16), _NT,
                           preferred_element_type=F32) for g in range(SSD_GROUPS)]
    first = lax.broadcasted_iota(jnp.int32, (q, 2 * SSD_HD), 1) < SSD_HD
    npair = SSD_HEADS // 2
    pair_group = [(2 * n) // (SSD_HEADS // SSD_GROUPS) for n in range(npair)]
    pairs = [slice(n * 2 * SSD_HD, (n + 1) * 2 * SSD_HD) for n in range(npair)]

    def per_lane(t, n):
        c0 = d * SSD_HEADS + 2 * n
        return jnp.where(first[:t.shape[0]], t[:, c0:c0 + 1], t[:, c0 + 1:c0 + 2])

    intra = []
    for h in range(SSD_HEADS):
        col = d * SSD_HEADS + h
        diff = cum[:, col:col + 1] - cum_t[col:col + 1, :]
        m = cbs[pair_group[h // 2]] * jnp.exp(jnp.where(mask, diff, MASK_NEG)) * dt_t[col:col + 1, :]
        intra.append(jnp.dot(m.astype(BF16), xb[:, pairs[h // 2]], preferred_element_type=F32))
    s_old = [s_ref[n] for n in range(npair)]
    inter = [jnp.dot(cgs[pair_group[n]], s_old[n].astype(BF16), preferred_element_type=F32) for n in range(npair)]
    outs = []
    for n in range(npair):
        g = pair_group[n]
        outs.append(jnp.where(first, intra[2 * n], intra[2 * n + 1]) + per_lane(ecum, n) * inter[n])
        xw = (xs[:, pairs[n]] * per_lane(w_end, n)).astype(BF16)
        s_ref[n] = per_lane(etot, n) * s_old[n] + jnp.dot(
            b_t[g * SSD_STATE:(g + 1) * SSD_STATE, :].astype(BF16), xw, preferred_element_type=F32)
    y = jnp.concatenate(outs, axis=-1)
    so_ref[0] = s_ref[...]
    if finalize:
        yt = (xs * dsk_ref[...] + yp_ref[0] + y) * _silu(z_ref[0])
        o_ref[0] = yt * lax.rsqrt(jnp.mean(yt * yt, axis=-1, keepdims=True) + LN_EPS) * ng_ref[...]
    else:
        o_ref[0] = y


def _ssd_pass(p, s0, other, params, reverse, d):
    bsz, seq, _ = p.shape
    w = BRANCH_W
    q = SSD_CHUNK
    nchunk = seq // q
    chunk, halo = _chunk_specs(q, nchunk, seq // 8, reverse)
    cx, cb2, cdt = COL_SSD_X // w, COL_SSD_B // (2 * SSD_GN), COL_SSD_DT // 128
    finalize = other is not None
    cwx, cbx, cwb, cbb, dtb, arow, dsk, ng = params
    in_specs = [halo(w, cx, -1), chunk(w, cx), halo(w, cx, 1),
                halo(2 * SSD_GN, cb2, -1), chunk(2 * SSD_GN, cb2), halo(2 * SSD_GN, cb2, 1), chunk(128, cdt)]
    args = [p] * 7
    if finalize:
        in_specs += [chunk(w, COL_SSD_Z // w), chunk(w, 0)]
        args += [p, other]
    st_shape = (SSD_HEADS // 2, SSD_STATE, 2 * SSD_HD)
    in_specs += [_batch_spec(st_shape), _const_spec((CONV_W, w)), _const_spec((1, w)),
                 _const_spec((CONV_W, 2 * SSD_GN)), _const_spec((1, 2 * SSD_GN)), _const_spec((1, 128)),
                 _const_spec((1, 128))]
    args += [s0, cwx, cbx, cwb, cbb, dtb, arow]
    if finalize:
        in_specs += [_const_spec((1, w)), _const_spec((1, w))]
        args += [dsk, ng]
    return pl.pallas_call(
        functools.partial(_ssd_kernel, reverse=reverse, finalize=finalize, nchunk=nchunk, d=d),
        grid=(bsz, nchunk),
        in_specs=in_specs,
        out_specs=[chunk(w, 0), _batch_spec(st_shape)],
        out_shape=[jax.ShapeDtypeStruct((bsz, seq, w), F32), jax.ShapeDtypeStruct((bsz,) + st_shape, F32)],
        scratch_shapes=[pltpu.VMEM(st_shape, F32)],
        compiler_params=pltpu.CompilerParams(dimension_semantics=("parallel", "arbitrary")),
        name="ssd_scan",
    )(*args)


def _lane_row(vals, start):
    return jnp.zeros((128,), F32).at[start:start + vals.shape[0]].set(vals.astype(F32)).reshape(1, 128)


def ssd_branch(p_l, p_c, conv_w, conv_b, dt_bias, a_log, d_skip, norm_g, need_ctx_out):
    bsz = p_l.shape[0]
    w = BRANCH_W
    params = (conv_w[:, :w], conv_b[:w].reshape(1, w), conv_w[:, w:], conv_b[w:].reshape(1, 2 * SSD_GN),
              _lane_row(dt_bias.reshape(-1), 0), _lane_row(-jnp.exp(a_log.astype(F32)).reshape(-1), 0),
              jnp.repeat(d_skip, SSD_HD).reshape(1, w), norm_g.reshape(1, w))
    zeros = jnp.zeros((bsz, SSD_HEADS // 2, SSD_STATE, 2 * SSD_HD), F32)
    y_c = y_l = None
    for d, reverse in ((0, False), (1, True)):
        other_c = y_c if (d == 1 and need_ctx_out) else None
        y_c, st = _ssd_pass(p_c, zeros, other_c, params, reverse, d)
        y_l, _ = _ssd_pass(p_l, st, y_l if d == 1 else None, params, reverse, d)
    return y_l, (y_c if need_ctx_out else None)


def _rope_rotate(x):
    wl = x.shape[-1]
    half = ML_HD // 4
    lane = lax.broadcasted_iota(jnp.int32, x.shape, 1)
    return jnp.where(lane % (2 * half) < half, -pltpu.roll(x, wl - half, 1), pltpu.roll(x, half, 1))


def _log_sigmoid(x):
    return jnp.minimum(x, 0.0) - jnp.log1p(jnp.exp(-jnp.abs(x)))


def _mlstm_kernel(*refs, reverse, finalize, rope, nchunk, d):
    (qp_ref, q_ref, qn_ref, kp_ref, k_ref, kn_ref, v_ref, g_ref) = refs[:8]
    n = 8
    if rope:
        cos_ref, sin_ref = refs[n:n + 2]
        n += 2
    if finalize:
        og_ref, hp_ref = refs[n:n + 2]
        n += 2
    c0_ref, n0_ref, m0_ref, cwq_ref, cbq_ref, cwk_ref, cbk_ref, ib_ref, fb_ref = refs[n:n + 9]
    n += 9
    o_ref, co_ref, no_ref, mo_ref, c_ref, n_ref, m_ref = refs[n:n + 7]
    j = pl.program_id(1)
    c = (nchunk - 1 - j) if reverse else j

    @pl.when(j == 0)
    def _():
        c_ref[...] = c0_ref[0]
        n_ref[...] = n0_ref[0]
        m_ref[...] = m0_ref[0]

    qn = q_ref.shape[1]
    has_prev = (c > 0).astype(F32)
    has_next = (c < nchunk - 1).astype(F32)
    q = _silu(_conv4(q_ref[0], qp_ref[0], qn_ref[0], has_prev, has_next, cwq_ref[...], cbq_ref[...]))
    k = _silu(_conv4(k_ref[0], kp_ref[0], kn_ref[0], has_prev, has_next, cwk_ref[...], cbk_ref[...]))
    if rope:
        cos = cos_ref[...]
        sin = sin_ref[...]
        q = q * cos + _rope_rotate(q) * sin
        k = k * cos + _rope_rotate(k) * sin
    q = q * (ML_HD ** -0.5)
    v = v_ref[0]
    gb = g_ref[0]
    li = gb + ib_ref[...]
    b = _cumsum_rows(_log_sigmoid(gb + fb_ref[...]), reverse)
    tot = b[0:1, :] if reverse else b[qn - 1:qn, :]
    b_t = b.T
    li_t = li.T
    mask = _causal_mask(qn, reverse)
    heads = [slice(h * ML_HD, (h + 1) * ML_HD) for h in range(ML_HEADS)]
    qb = q.astype(BF16)
    kb = k.astype(BF16)
    vb = v.astype(BF16)
    qk = [lax.dot_general(qb[:, hs], kb[:, hs], _NT, preferred_element_type=F32) for hs in heads]
    c_old = [c_ref[h] for h in range(ML_HEADS)]
    n_old = [n_ref[h:h + 1, :] for h in range(ML_HEADS)]
    qc = [lax.dot_general(qb[:, hs], c_old[h].astype(BF16), _NT, preferred_element_type=F32)
          for h, hs in enumerate(heads)]
    gate = []
    for h in range(ML_HEADS):
        ci = 4 * ML_HEADS + d * 2 * ML_HEADS + h
        cf = ci + ML_HEADS
        b_c = b[:, cf:cf + 1]
        b_end = tot[:, cf:cf + 1]
        m_st = m_ref[h:h + 1, 0:1]
        end_log = b_end - b_c + li[:, ci:ci + 1]
        m_new = jnp.maximum(b_end + m_st, jnp.max(end_log, axis=0, keepdims=True))
        dlog = jnp.where(mask, b_c - b_t[cf:cf + 1, :] + li_t[ci:ci + 1, :], MASK_NEG)
        m_inter = b_c + m_st
        m_i = jnp.maximum(jnp.max(dlog, axis=1, keepdims=True), m_inter)
        gate.append((jnp.exp(end_log - m_new), jnp.exp(b_end + m_st - m_new), m_new,
                     jnp.exp(dlog - m_i), jnp.exp(m_inter - m_i), jnp.exp(-m_i)))
    s_all = [qk[h] * gate[h][3] for h in range(ML_HEADS)]
    sv = [jnp.dot(s_all[h].astype(BF16), vb[:, hs], preferred_element_type=F32) for h, hs in enumerate(heads)]
    upd = [jnp.dot((v[:, hs] * gate[h][0]).T.astype(BF16), kb[:, hs], preferred_element_type=F32)
           for h, hs in enumerate(heads)]
    outs = []
    for h, hs in enumerate(heads):
        w, carry_scale, m_new, _, w_in, floor = gate[h]
        num = sv[h] + w_in * qc[h]
        den = jnp.sum(s_all[h], axis=1, keepdims=True) + w_in * jnp.sum(q[:, hs] * n_old[h], axis=1, keepdims=True)
        outs.append(num / jnp.maximum(jnp.abs(den), floor))
        c_ref[h] = carry_scale * c_old[h] + upd[h]
        n_ref[h:h + 1, :] = carry_scale * n_old[h] + jnp.sum(k[:, hs] * w, axis=0, keepdims=True)
        m_ref[h:h + 1, :] = jnp.broadcast_to(m_new, (1, ML_HD))
    hout = jnp.concatenate(outs, axis=-1)
    co_ref[0] = c_ref[...]
    no_ref[0] = n_ref[...]
    mo_ref[0] = m_ref[...]
    if finalize:
        o_ref[0] = _sigmoid(og_ref[0]) * (hp_ref[0] + hout)
    else:
        o_ref[0] = hout


def _mlstm_pass(p, state, other, tables, params, reverse, d):
    bsz, seq, _ = p.shape
    w = BRANCH_W
    qn = ML_CHUNK
    nchunk = seq // qn
    chunk, halo = _chunk_specs(qn, nchunk, seq // 8, reverse)
    cq = COL_ML // w
    finalize = other is not None
    rope = tables is not None
    in_specs = [halo(w, cq, -1), chunk(w, cq), halo(w, cq, 1), halo(w, cq + 1, -1), chunk(w, cq + 1),
                halo(w, cq + 1, 1), chunk(w, cq + 2), chunk(128, COL_ML_G // 128)]
    args = [p] * 8
    if rope:
        tab = pl.BlockSpec((qn, w), (lambda b, j: (nchunk - 1 - j, 0)) if reverse else (lambda b, j: (j, 0)))
        in_specs += [tab, tab]
        args += list(tables)
    if finalize:
        in_specs += [chunk(w, cq + 3), chunk(w, 0)]
        args += [p, other]
    st_shapes = [(ML_HEADS, ML_HD, ML_HD), (ML_HEADS, ML_HD), (ML_HEADS, ML_HD)]
    in_specs += [_batch_spec(s) for s in st_shapes]
    in_specs += [_const_spec((CONV_W, w)), _const_spec((1, w)), _const_spec((CONV_W, w)), _const_spec((1, w)),
                 _const_spec((1, 128)), _const_spec((1, 128))]
    args += list(state) + list(params)
    res = pl.pallas_call(
        functools.partial(_mlstm_kernel, reverse=reverse, finalize=finalize, rope=rope, nchunk=nchunk, d=d),
        grid=(bsz, nchunk),
        in_specs=in_specs,
        out_specs=[chunk(w, 0)] + [_batch_spec(s) for s in st_shapes],
        out_shape=[jax.ShapeDtypeStruct((bsz, seq, w), F32)]
                  + [jax.ShapeDtypeStruct((bsz,) + s, F32) for s in st_shapes],
        scratch_shapes=[pltpu.VMEM(s, F32) for s in st_shapes],
        compiler_params=pltpu.CompilerParams(dimension_semantics=("parallel", "arbitrary")),
        name="mlstm_scan",
    )(*args)
    return res[0], tuple(res[1:])


def mlstm_branch(p_l, p_c, conv_w, conv_b, i_bias, f_bias, need_ctx_out):
    bsz, seq, _ = p_l.shape
    w = BRANCH_W
    cos, sin = _rope_2d_tables(seq)
    tables = (jnp.tile(cos, (1, ML_HEADS)), jnp.tile(sin, (1, ML_HEADS)))
    zero_h = jnp.zeros_like(i_bias)
    ib = _lane_row(jnp.concatenate([i_bias, zero_h], axis=1).reshape(-1), 4 * ML_HEADS)
    fb = _lane_row(jnp.concatenate([zero_h, f_bias], axis=1).reshape(-1), 4 * ML_HEADS)
    params = (conv_w[:, :w], conv_b[:w].reshape(1, w), conv_w[:, w:], conv_b[w:].reshape(1, w), ib, fb)
    state0 = (jnp.zeros((bsz, ML_HEADS, ML_HD, ML_HD), F32), jnp.zeros((bsz, ML_HEADS, ML_HD), F32),
              jnp.zeros((bsz, ML_HEADS, ML_HD), F32))
    h_c = h_l = None
    for d, reverse in ((0, False), (1, True)):
        other_c = h_c if (d == 1 and need_ctx_out) else None
        h_c, st = _mlstm_pass(p_c, state0, other_c, None, params, reverse, d)
        h_l, _ = _mlstm_pass(p_l, st, h_l if d == 1 else None, tables, params, reverse, d)
    return h_l, (h_c if need_ctx_out else None)


MOE_TM = 512
ROUTE_LANES = 128


def _moe_router_kernel(x_ref, sc_ref, sh_ref, wr_ref, br_ref, hb_ref, rt_ref):
    h = x_ref[0] * (1.0 + sc_ref[0]) + sh_ref[0]
    hb_ref[0] = h.astype(BF16)
    logits = jnp.dot(h, wr_ref[...], precision=lax.Precision.HIGHEST, preferred_element_type=F32) + br_ref[...]
    lane = lax.broadcasted_iota(jnp.int32, logits.shape, 1)
    m1 = jnp.max(logits, axis=-1, keepdims=True)
    i1 = jnp.min(jnp.where(logits == m1, lane, ROUTE_LANES), axis=-1, keepdims=True)
    rest = jnp.where(lane == i1, MASK_NEG, logits)
    m2 = jnp.max(rest, axis=-1, keepdims=True)
    i2 = jnp.min(jnp.where(rest == m2, lane, ROUTE_LANES), axis=-1, keepdims=True)
    e2 = jnp.exp(m2 - m1)
    p1 = 1.0 / (1.0 + e2)
    p2 = e2 * p1
    rt_ref[0] = jnp.where(lane == 0, i1.astype(F32), jnp.where(lane == 1, i2.astype(F32),
                          jnp.where(lane == 2, p1, jnp.where(lane == 3, p2, 0.0))))


def moe_router(x, sc, sh, w_router, b_router, tm=512):
    bsz, seq, d = x.shape
    n_e = w_router.shape[1]
    tm = min(tm, seq)
    wr = jnp.zeros((d, ROUTE_LANES), F32).at[:, :n_e].set(w_router)
    br = jnp.full((1, ROUTE_LANES), MASK_NEG, F32).at[0, :n_e].set(b_router)
    return pl.pallas_call(
        _moe_router_kernel,
        grid=(bsz, seq // tm),
        in_specs=[pl.BlockSpec((1, tm, d), lambda b, i: (b, i, 0)),
                  pl.BlockSpec((1, 1, d), lambda b, i: (b, 0, 0)),
                  pl.BlockSpec((1, 1, d), lambda b, i: (b, 0, 0)),
                  pl.BlockSpec((d, ROUTE_LANES), lambda b, i: (0, 0)),
                  pl.BlockSpec((1, ROUTE_LANES), lambda b, i: (0, 0))],
        out_specs=[pl.BlockSpec((1, tm, d), lambda b, i: (b, i, 0)),
                   pl.BlockSpec((1, tm, ROUTE_LANES), lambda b, i: (b, i, 0))],
        out_shape=[jax.ShapeDtypeStruct((bsz, seq, d), BF16), jax.ShapeDtypeStruct((bsz, seq, ROUTE_LANES), F32)],
        compiler_params=pltpu.CompilerParams(dimension_semantics=("parallel", "parallel")),
        name="moe_router",
    )(x, sc, sh, wr, br)


def _route_tables(idx, n_e, tmg):
    n_tok = idx.shape[0]
    e_flat = idx.reshape(-1)
    onehot = (e_flat[:, None] == jnp.arange(n_e, dtype=jnp.int32)[None, :]).astype(jnp.int32)
    csum = jnp.cumsum(onehot, axis=0)
    rank = jnp.take_along_axis(csum - onehot, e_flat[:, None], axis=1)[:, 0]
    padded = ((csum[-1] + tmg - 1) // tmg) * tmg
    ends = jnp.cumsum(padded)
    pos = (ends - padded)[e_flat] + rank
    n_rows = TOP_K * n_tok + n_e * tmg
    ntiles = n_rows // tmg
    tile_start = jnp.arange(ntiles, dtype=jnp.int32) * tmg
    tile_expert = jnp.minimum(jnp.sum(tile_start[:, None] >= ends[None, :], axis=1), n_e - 1)
    src = jnp.zeros((n_rows,), jnp.int32).at[pos].set(jnp.arange(TOP_K * n_tok, dtype=jnp.int32) // TOP_K)
    meta = jnp.concatenate([tile_expert, ends[-1:] // tmg]).astype(jnp.int32)
    return pos.reshape(n_tok, TOP_K), src, meta


def _expert_changed(meta_ref, i):
    return (i == 0) | (meta_ref[i] != meta_ref[jnp.maximum(i - 1, 0)])


def _moe_up_kernel(meta_ref, x_ref, wg_ref, wu_ref, a_ref, wgb_ref, wub_ref, *, ntiles):
    i = pl.program_id(1)

    @pl.when(_expert_changed(meta_ref, i))
    def _():
        wgb_ref[...] = wg_ref[0].astype(BF16)
        wub_ref[...] = wu_ref[0].astype(BF16)

    @pl.when(i < meta_ref[ntiles])
    def _():
        x = x_ref[...]
        g = jnp.dot(x, wgb_ref[...], preferred_element_type=F32)
        u = jnp.dot(x, wub_ref[...], preferred_element_type=F32)
        a_ref[...] = (_silu(g) * u).astype(BF16)

    @pl.when(i >= meta_ref[ntiles])
    def _():
        a_ref[...] = jnp.zeros_like(a_ref)


def _moe_down_kernel(meta_ref, a_ref, wd_ref, y_ref, wdb_ref, *, ntiles):
    i = pl.program_id(0)

    @pl.when(_expert_changed(meta_ref, i))
    def _():
        wdb_ref[...] = wd_ref[0].astype(BF16)

    @pl.when(i < meta_ref[ntiles])
    def _():
        y_ref[...] = jnp.dot(a_ref[...], wdb_ref[...], preferred_element_type=F32).astype(y_ref.dtype)

    @pl.when(i >= meta_ref[ntiles])
    def _():
        y_ref[...] = jnp.zeros_like(y_ref)


def moe_experts(xs, meta, wg, wu, wd, tmg, tf):
    n_rows, d = xs.shape
    n_e, _, f = wg.shape
    ntiles = n_rows // tmg
    nf = f // tf
    once = pl.Buffered(1)
    a = pl.pallas_call(
        functools.partial(_moe_up_kernel, ntiles=ntiles),
        grid_spec=pltpu.PrefetchScalarGridSpec(
            num_scalar_prefetch=1,
            grid=(nf, ntiles),
            in_specs=[pl.BlockSpec((tmg, d), lambda j, i, m: (i, 0)),
                      pl.BlockSpec((1, d, tf), lambda j, i, m: (m[i], 0, j), pipeline_mode=once),
                      pl.BlockSpec((1, d, tf), lambda j, i, m: (m[i], 0, j), pipeline_mode=once)],
            out_specs=pl.BlockSpec((tmg, tf), lambda j, i, m: (i, j)),
            scratch_shapes=[pltpu.VMEM((d, tf), BF16), pltpu.VMEM((d, tf), BF16)]),
        out_shape=jax.ShapeDtypeStruct((n_rows, f), BF16),
        compiler_params=pltpu.CompilerParams(
            dimension_semantics=("arbitrary", "arbitrary"), vmem_limit_bytes=V7X_VMEM_LIMIT),
        name="moe_up",
    )(meta, xs, wg, wu)
    return pl.pallas_call(
        functools.partial(_moe_down_kernel, ntiles=ntiles),
        grid_spec=pltpu.PrefetchScalarGridSpec(
            num_scalar_prefetch=1,
            grid=(ntiles,),
            in_specs=[pl.BlockSpec((tmg, f), lambda i, m: (i, 0)),
                      pl.BlockSpec((1, f, d), lambda i, m: (m[i], 0, 0), pipeline_mode=once)],
            out_specs=pl.BlockSpec((tmg, d), lambda i, m: (i, 0)),
            scratch_shapes=[pltpu.VMEM((f, d), BF16)]),
        out_shape=jax.ShapeDtypeStruct((n_rows, d), BF16),
        compiler_params=pltpu.CompilerParams(
            dimension_semantics=("arbitrary",), vmem_limit_bytes=V7X_VMEM_LIMIT),
        name="moe_down",
    )(meta, a, wd)


def _combine_ln_kernel(x_ref, gt_ref, rt_ref, ya_ref, yb_ref, g_ref, b_ref, o_ref, *, alpha):
    rt = rt_ref[0]
    y = rt[:, 2:3] * ya_ref[0].astype(F32) + rt[:, 3:4] * yb_ref[0].astype(F32)
    o_ref[0] = _layer_norm_rows(alpha * x_ref[0] + gt_ref[0] * y, g_ref[...], b_ref[...])


def combine_residual_ln(x, gate, route, ya, yb, ln_g, ln_b, alpha, tm=512):
    bsz, seq, d = x.shape
    tm = min(tm, seq)
    row = pl.BlockSpec((1, tm, d), lambda b, i: (b, i, 0))
    return pl.pallas_call(
        functools.partial(_combine_ln_kernel, alpha=alpha),
        grid=(bsz, seq // tm),
        in_specs=[row, pl.BlockSpec((1, 1, d), lambda b, i: (b, 0, 0)),
                  pl.BlockSpec((1, tm, ROUTE_LANES), lambda b, i: (b, i, 0)), row, row,
                  pl.BlockSpec((1, d), lambda b, i: (0, 0)), pl.BlockSpec((1, d), lambda b, i: (0, 0))],
        out_specs=row,
        out_shape=jax.ShapeDtypeStruct((bsz, seq, d), F32),
        compiler_params=pltpu.CompilerParams(
            dimension_semantics=("parallel", "parallel"), vmem_limit_bytes=V7X_VMEM_LIMIT),
        name="moe_combine_ln",
    )(x, gate, route, ya, yb, ln_g.reshape(1, d), ln_b.reshape(1, d))


def moe_residual_ln(x, sc, sh, gate, w_router, b_router, wg, wu, wd, ln_g, ln_b, alpha):
    bsz, seq, d = x.shape
    n_e, _, f = wg.shape
    n_tok = bsz * seq
    tmg = min(MOE_TM, TOP_K * n_tok)
    hb, route = moe_router(x, sc, sh, w_router, b_router)
    idx = route[..., :TOP_K].astype(jnp.int32).reshape(n_tok, TOP_K)
    pos, src, meta = _route_tables(idx, n_e, tmg)
    xs = hb.reshape(n_tok, d).at[src].get(mode="promise_in_bounds")
    ys = moe_experts(xs, meta, wg, wu, wd, tmg, f // 2)
    ya = ys.at[pos[:, 0]].get(mode="promise_in_bounds").reshape(bsz, seq, d)
    yb = ys.at[pos[:, 1]].get(mode="promise_in_bounds").reshape(bsz, seq, d)
    return combine_residual_ln(x, gate, route, ya, yb, ln_g, ln_b, alpha)


def _mixers(p_l, p_c, need_ctx_out, lru_conv_w, lru_conv_b, lru_wa, lru_ba, lru_wx, lru_bx, lru_lambda,
            na_rpb, ssd_conv_w, ssd_conv_b, ssd_dt_bias, ssd_a_log, ssd_d, ssd_norm_g, ml_conv_w, ml_conv_b,
            ml_i_bias, ml_f_bias):
    ya = lru_branch(p_l, p_c, lru_conv_w, lru_conv_b, lru_wa, lru_ba, lru_wx, lru_bx, lru_lambda, need_ctx_out)
    yb = na_branch(p_l, p_c, na_rpb, need_ctx_out)
    yc = ssd_branch(p_l, p_c, ssd_conv_w, ssd_conv_b, ssd_dt_bias, ssd_a_log, ssd_d, ssd_norm_g, need_ctx_out)
    yd = mlstm_branch(p_l, p_c, ml_conv_w, ml_conv_b, ml_i_bias, ml_f_bias, need_ctx_out)
    return (ya[0], yb[0], yc[0], yd[0]), (ya[1], yb[1], yc[1], yd[1])


def kernel(x, c, ctx, c_ctx, w_ada, b_ada, w_in, lru_conv_w, lru_conv_b, lru_wa, lru_ba, lru_wx, lru_bx, lru_lambda, na_rpb, ssd_conv_w, ssd_conv_b, ssd_dt_bias, ssd_a_log, ssd_d, ssd_norm_g, ml_conv_w, ml_conv_b, ml_i_bias, ml_f_bias, w_branch, w_out, ln_g, ln_b, ffn_w_gate, ffn_w_up, ffn_w_down, moe_w_router, moe_b_router, moe_w_gate, moe_w_up, moe_w_down):
    depth = w_in.shape[0]
    bsz, seq, d = x.shape
    alpha = (2.0 * depth) ** 0.25
    cvecs = jnp.zeros((8, d), F32).at[:bsz].set(c).at[bsz].set(c_ctx)
    mods = ada_modulation(cvecs, w_ada, b_ada)
    xl, xc = x, ctx
    for l in range(depth):
        need_ctx_out = l < depth - 1
        mod = mods[l]
        mod_l = jnp.split(mod[:bsz, None, :], 6, axis=-1)
        mod_c = jnp.split(jnp.broadcast_to(mod[bsz:bsz + 1, None, :], (bsz, 1, 6 * d)), 6, axis=-1)
        w_mix, w_gates = _pack_w_in(w_in[l])
        p_l, hb_l = mod_matmul(xl, mod_l[1], mod_l[0], w_mix)
        p_c, hb_c = mod_matmul(xc, mod_c[1], mod_c[0], w_mix)
        br_l, br_c = _mixers(p_l, p_c, need_ctx_out, lru_conv_w[l], lru_conv_b[l], lru_wa[l], lru_ba[l],
                             lru_wx[l], lru_bx[l], lru_lambda[l], na_rpb[l], ssd_conv_w[l], ssd_conv_b[l],
                             ssd_dt_bias[l], ssd_a_log[l], ssd_d[l], ssd_norm_g[l], ml_conv_w[l], ml_conv_b[l],
                             ml_i_bias[l], ml_f_bias[l])
        wb = w_branch[l].astype(BF16)
        wo = w_out[l].astype(BF16)
        xl = proj_residual_ln(merge_branches(hb_l, w_gates, br_l, wb), wo, xl, mod_l[2], ln_g[l, 0], ln_b[l, 0],
                              alpha)
        if need_ctx_out:
            xc = proj_residual_ln(merge_branches(hb_c, w_gates, br_c, wb), wo, xc, mod_c[2], ln_g[l, 0],
                                  ln_b[l, 0], alpha)
        j = l // 2
        if l % 2 == 0:
            wg = ffn_w_gate[j].astype(BF16)
            wu = ffn_w_up[j].astype(BF16)
            wd = ffn_w_down[j].astype(BF16)

            def ffn(h, m, wg=wg, wu=wu, wd=wd):
                return ffn_residual_ln(h, m[4], m[3], m[5], wg, wu, wd, ln_g[l, 1], ln_b[l, 1], alpha)
        else:
            def ffn(h, m, j=j):
                return moe_residual_ln(h, m[4], m[3], m[5], moe_w_router[j], moe_b_router[j], moe_w_gate[j],
                                       moe_w_up[j], moe_w_down[j], ln_g[l, 1], ln_b[l, 1], alpha)

        xl = ffn(xl, mod_l)
        if need_ctx_out:
            xc = ffn(xc, mod_c)
    return xl
```

```python
import functools
import math

import numpy as np
import jax
import jax.numpy as jnp
from jax import lax
from jax.experimental import pallas as pl
from jax.experimental.pallas import tpu as pltpu

F32 = jnp.float32
BF16 = jnp.bfloat16

D_MODEL = 2048
GRID_W = 64
N_BRANCH = 4
BRANCH_W = D_MODEL // N_BRANCH
CONV_W = 4
LN_EPS = 1e-5
LRU_BLOCKS = 8
LRU_BW = BRANCH_W // LRU_BLOCKS
LRU_C = 8.0
NA_HEADS = 8
NA_HD = BRANCH_W // NA_HEADS
NA_KH = 8
NA_KW = 16
SSD_HEADS = 8
SSD_HD = BRANCH_W // SSD_HEADS
SSD_GROUPS = 2
SSD_STATE = 64
SSD_CHUNK = 128
SSD_GN = SSD_GROUPS * SSD_STATE
ML_HEADS = 4
ML_HD = BRANCH_W // ML_HEADS
ML_CHUNK = 128
ROPE_BASE = 10000.0
N_EXPERTS = 8
TOP_K = 2

V7X_VMEM_LIMIT = 52 * 1024 * 1024

COL_LRU_X = 0
COL_LRU_G = 512
COL_NA = 1024
COL_SSD_Z = 2560
COL_SSD_X = 3072
COL_ML = 3584
COL_SSD_B = 5632
COL_SSD_C = 5760
COL_SSD_DT = 5888
COL_ML_G = 5904
N_MIX = 5920
N_MIX_PAD = 6144


def _pack_w_in(w):
    parts = [w[:, 0:1024], w[:, 1024:2560], w[:, 2560:3584], w[:, 3856:5904],
             w[:, 3584:3840], w[:, 3840:3856], w[:, 5904:5920],
             jnp.zeros((w.shape[0], N_MIX_PAD - N_MIX), w.dtype)]
    return jnp.concatenate(parts, axis=1).astype(BF16), w[:, N_MIX:].astype(BF16)


def _sigmoid(x):
    return 1.0 / (1.0 + jnp.exp(-x))


def _layer_norm_rows(z, g, b):
    mu = jnp.mean(z, axis=-1, keepdims=True)
    zc = z - mu
    var = jnp.mean(zc * zc, axis=-1, keepdims=True)
    return zc * lax.rsqrt(var + LN_EPS) * g + b


def _mod_matmul_kernel(x_ref, sc_ref, sh_ref, w_ref, o_ref, hb_ref):
    @pl.when(pl.program_id(2) == 0)
    def _():
        hb_ref[0] = (x_ref[0] * (1.0 + sc_ref[0]) + sh_ref[0]).astype(BF16)

    o_ref[0] = jnp.dot(hb_ref[0], w_ref[...], preferred_element_type=F32)


def mod_matmul(x, sc, sh, w, tm=1024, tn=1024):
    bsz, seq, d = x.shape
    n = w.shape[1]
    tm = min(tm, seq)
    return pl.pallas_call(
        _mod_matmul_kernel,
        grid=(bsz, seq // tm, n // tn),
        in_specs=[pl.BlockSpec((1, tm, d), lambda b, i, j: (b, i, 0)),
                  pl.BlockSpec((1, 1, d), lambda b, i, j: (b, 0, 0)),
                  pl.BlockSpec((1, 1, d), lambda b, i, j: (b, 0, 0)),
                  pl.BlockSpec((d, tn), lambda b, i, j: (0, j))],
        out_specs=[pl.BlockSpec((1, tm, tn), lambda b, i, j: (b, i, j)),
                   pl.BlockSpec((1, tm, d), lambda b, i, j: (b, i, 0))],
        out_shape=[jax.ShapeDtypeStruct((bsz, seq, n), F32), jax.ShapeDtypeStruct((bsz, seq, d), BF16)],
        compiler_params=pltpu.CompilerParams(
            dimension_semantics=("parallel", "parallel", "arbitrary"), vmem_limit_bytes=V7X_VMEM_LIMIT),
        name="in_proj",
    )(x, sc, sh, w)


def _merge_kernel(hb_ref, g0, g1, g2, g3, ya, yb, yc, yd, wb_ref, o_ref):
    hb = hb_ref[0]
    acc = None
    for n, (wg, y) in enumerate(((g0, ya), (g1, yb), (g2, yc), (g3, yd))):
        gate = _sigmoid(jnp.dot(hb, wg[...], preferred_element_type=F32))
        t = gate * jnp.dot(y[0].astype(BF16), wb_ref[n], preferred_element_type=F32)
        acc = t if acc is None else acc + t
    o_ref[0] = acc.astype(BF16)


def merge_branches(hb, w_gates, branches, wb, tm=512, tn=512):
    bsz, seq, dk = hb.shape
    d = wb.shape[2]
    tm = min(tm, seq)
    nj = d // tn
    g_specs = [pl.BlockSpec((dk, tn), functools.partial(lambda b, i, j, n: (0, n * nj + j), n=n))
               for n in range(N_BRANCH)]
    y_specs = [pl.BlockSpec((1, tm, BRANCH_W), lambda b, i, j: (b, i, 0)) for _ in range(N_BRANCH)]
    return pl.pallas_call(
        _merge_kernel,
        grid=(bsz, seq // tm, nj),
        in_specs=[pl.BlockSpec((1, tm, dk), lambda b, i, j: (b, i, 0))] + g_specs + y_specs
                 + [pl.BlockSpec((N_BRANCH, BRANCH_W, tn), lambda b, i, j: (0, 0, j))],
        out_specs=pl.BlockSpec((1, tm, tn), lambda b, i, j: (b, i, j)),
        out_shape=jax.ShapeDtypeStruct((bsz, seq, d), BF16),
        compiler_params=pltpu.CompilerParams(
            dimension_semantics=("parallel", "parallel", "arbitrary"), vmem_limit_bytes=V7X_VMEM_LIMIT),
        name="merge",
    )(hb, w_gates, w_gates, w_gates, w_gates, *branches, wb)


def _proj_ln_kernel(m_ref, w_ref, x_ref, gt_ref, g_ref, b_ref, o_ref, *, alpha):
    y = jnp.dot(m_ref[0], w_ref[...], preferred_element_type=F32)
    o_ref[0] = _layer_norm_rows(alpha * x_ref[0] + gt_ref[0] * y, g_ref[...], b_ref[...])


def proj_residual_ln(m, w, x, gate, ln_g, ln_b, alpha, tm=512):
    bsz, seq, k = m.shape
    d = w.shape[1]
    tm = min(tm, seq)
    return pl.pallas_call(
        functools.partial(_proj_ln_kernel, alpha=alpha),
        grid=(bsz, seq // tm),
        in_specs=[pl.BlockSpec((1, tm, k), lambda b, i: (b, i, 0)),
                  pl.BlockSpec((k, d), lambda b, i: (0, 0)),
                  pl.BlockSpec((1, tm, d), lambda b, i: (b, i, 0)),
                  pl.BlockSpec((1, 1, d), lambda b, i: (b, 0, 0)),
                  pl.BlockSpec((1, d), lambda b, i: (0, 0)),
                  pl.BlockSpec((1, d), lambda b, i: (0, 0))],
        out_specs=pl.BlockSpec((1, tm, d), lambda b, i: (b, i, 0)),
        out_shape=jax.ShapeDtypeStruct((bsz, seq, d), F32),
        compiler_params=pltpu.CompilerParams(
            dimension_semantics=("parallel", "parallel"), vmem_limit_bytes=V7X_VMEM_LIMIT),
        name="out_proj_ln",
    )(m, w, x, gate, ln_g.reshape(1, d), ln_b.reshape(1, d))


def _ffn_kernel(x_ref, sc_ref, sh_ref, gt_ref, wg_ref, wu_ref, wd_ref, lg_ref, lb_ref, o_ref, hb_ref, acc_ref, *, alpha):
    j = pl.program_id(2)

    @pl.when(j == 0)
    def _():
        hb_ref[...] = (x_ref[0] * (1.0 + sc_ref[0]) + sh_ref[0]).astype(BF16)
        acc_ref[...] = jnp.zeros_like(acc_ref)

    hb = hb_ref[...]
    g = jnp.dot(hb, wg_ref[...], preferred_element_type=F32)
    u = jnp.dot(hb, wu_ref[...], preferred_element_type=F32)
    acc_ref[...] += jnp.dot((_silu(g) * u).astype(BF16), wd_ref[...], preferred_element_type=F32)

    @pl.when(j == pl.num_programs(2) - 1)
    def _():
        o_ref[0] = _layer_norm_rows(alpha * x_ref[0] + gt_ref[0] * acc_ref[...], lg_ref[...], lb_ref[...])


def ffn_residual_ln(x, sc, sh, gate, wg, wu, wd, ln_g, ln_b, alpha, tm=512, tf=512):
    bsz, seq, d = x.shape
    f = wg.shape[1]
    tm = min(tm, seq)
    vec = pl.BlockSpec((1, 1, d), lambda b, i, j: (b, 0, 0))
    par = pl.BlockSpec((1, d), lambda b, i, j: (0, 0))
    return pl.pallas_call(
        functools.partial(_ffn_kernel, alpha=alpha),
        grid=(bsz, seq // tm, f // tf),
        in_specs=[pl.BlockSpec((1, tm, d), lambda b, i, j: (b, i, 0)), vec, vec, vec,
                  pl.BlockSpec((d, tf), lambda b, i, j: (0, j)),
                  pl.BlockSpec((d, tf), lambda b, i, j: (0, j)),
                  pl.BlockSpec((tf, d), lambda b, i, j: (j, 0)), par, par],
        out_specs=pl.BlockSpec((1, tm, d), lambda b, i, j: (b, i, 0)),
        out_shape=jax.ShapeDtypeStruct((bsz, seq, d), F32),
        scratch_shapes=[pltpu.VMEM((tm, d), BF16), pltpu.VMEM((tm, d), F32)],
        compiler_params=pltpu.CompilerParams(
            dimension_semantics=("parallel", "parallel", "arbitrary"), vmem_limit_bytes=V7X_VMEM_LIMIT),
        name="ffn_ln",
    )(x, sc, sh, gate, wg, wu, wd, ln_g.reshape(1, d), ln_b.reshape(1, d))


def _ada_kernel(c_ref, w_ref, b_ref, o_ref):
    cv = c_ref[...]
    o_ref[0] = jnp.dot(cv * _sigmoid(cv), w_ref[0], precision=lax.Precision.HIGHEST,
                       preferred_element_type=F32) + b_ref[0]


def ada_modulation(cvecs, w, b, tn=1536):
    r, d = cvecs.shape
    depth, _, n = w.shape
    return pl.pallas_call(
        _ada_kernel,
        grid=(depth, n // tn),
        in_specs=[pl.BlockSpec((r, d), lambda l, j: (0, 0)), pl.BlockSpec((1, d, tn), lambda l, j: (l, 0, j)),
                  pl.BlockSpec((1, 1, tn), lambda l, j: (l, 0, j))],
        out_specs=pl.BlockSpec((1, r, tn), lambda l, j: (l, 0, j)),
        out_shape=jax.ShapeDtypeStruct((depth, r, n), F32),
        compiler_params=pltpu.CompilerParams(
            dimension_semantics=("parallel", "parallel"), vmem_limit_bytes=V7X_VMEM_LIMIT),
        name="ada_mod",
    )(cvecs, w, b.reshape(depth, 1, n))


NA_NEG = -1e30


def _na_bias_slabs(rpb):
    w = jnp.arange(GRID_W)
    cs = jnp.clip(w - NA_KW // 2, 0, GRID_W - NA_KW)
    ok = (w[None, :] >= cs[:, None]) & (w[None, :] < cs[:, None] + NA_KW)
    dc = jnp.clip(w[None, :] - w[:, None] + (NA_KW - 1), 0, 2 * NA_KW - 2)
    tab = jnp.where(ok, rpb[:, :, dc], NA_NEG)
    idx = jnp.arange(NA_KH)[:, None] + jnp.arange(NA_KH)[None, :]
    slab = tab[:, idx]
    return slab.transpose(1, 0, 3, 2, 4).reshape(NA_KH, NA_HEADS, GRID_W, NA_KH * GRID_W)


def _softmax2(s_a, s_b):
    m = jnp.maximum(jnp.max(s_a, axis=-1, keepdims=True), jnp.max(s_b, axis=-1, keepdims=True))
    e_a = jnp.exp(s_a - m)
    e_b = jnp.exp(s_b - m)
    inv = 1.0 / (jnp.sum(e_a, axis=-1, keepdims=True) + jnp.sum(e_b, axis=-1, keepdims=True))
    return e_a * inv, e_b * inv


_NT = (((1,), (1,)), ((), ()))


def _na_kernel(q_ref, kp_ref, kc_ref, kn_ref, vp_ref, vc_ref, vn_ref, ck_ref, cv_ref, bias_ref, o_ref,
               kw_ref, vw_ref, ckb_ref, cvb_ref, *, rows):
    i = pl.program_id(1)
    blk = NA_KH * GRID_W
    for n, (kr, vr) in enumerate(((kp_ref, vp_ref), (kc_ref, vc_ref), (kn_ref, vn_ref))):
        kw_ref[n * blk:(n + 1) * blk, :] = kr[0].astype(BF16)
        vw_ref[n * blk:(n + 1) * blk, :] = vr[0].astype(BF16)
    ckb_ref[...] = ck_ref[0].astype(BF16)
    cvb_ref[...] = cv_ref[0].astype(BF16)
    scale = NA_HD ** -0.5

    def body(rr, carry):
        r = i * NA_KH + rr
        rs = jnp.clip(r - NA_KH // 2, 0, rows - NA_KH)
        off = pl.multiple_of((rs - (i - 1) * NA_KH) * GRID_W, GRID_W)
        v = rs - r + (NA_KH - 1)
        q_all = (q_ref[0, pl.ds(pl.multiple_of(rr * GRID_W, GRID_W), GRID_W), :] * scale).astype(BF16)
        kwin = kw_ref[pl.ds(off, blk), :]
        vwin = vw_ref[pl.ds(off, blk), :]
        first = lax.broadcasted_iota(jnp.int32, (GRID_W, 2 * NA_HD), 1) < NA_HD
        pairs = [slice(n * 2 * NA_HD, (n + 1) * 2 * NA_HD) for n in range(NA_HEADS // 2)]
        scores = []
        for h in range(NA_HEADS):
            ps = pairs[h // 2]
            q = jnp.where(first if h % 2 == 0 else ~first, q_all[:, ps], jnp.zeros((), BF16))
            scores.append((lax.dot_general(q, kwin[:, ps], _NT, preferred_element_type=F32) + bias_ref[v, h],
                           lax.dot_general(q, ckb_ref[:, ps], _NT, preferred_element_type=F32)))
        probs = [_softmax2(s_w, s_c) for s_w, s_c in scores]
        both = [jnp.dot(p_w.astype(BF16), vwin[:, pairs[h // 2]], preferred_element_type=F32)
                + jnp.dot(p_c.astype(BF16), cvb_ref[:, pairs[h // 2]], preferred_element_type=F32)
                for h, (p_w, p_c) in enumerate(probs)]
        outs = [jnp.where(first, both[2 * n], both[2 * n + 1]) for n in range(NA_HEADS // 2)]
        o_ref[0, pl.ds(pl.multiple_of(rr * GRID_W, GRID_W), GRID_W), :] = jnp.concatenate(outs, axis=-1)
        return carry

    lax.fori_loop(0, NA_KH, body, 0)


def _na_ctx_kernel(q_ref, k_ref, v_ref, o_ref):
    scale = NA_HD ** -0.5
    q_all = (q_ref[0] * scale).astype(BF16)
    k_all = k_ref[0].astype(BF16)
    v_all = v_ref[0].astype(BF16)
    outs = []
    for h in range(NA_HEADS):
        hs = slice(h * NA_HD, (h + 1) * NA_HD)
        s = lax.dot_general(q_all[:, hs], k_all[:, hs], _NT, preferred_element_type=F32)
        e = jnp.exp(s - jnp.max(s, axis=-1, keepdims=True))
        p = e * (1.0 / jnp.sum(e, axis=-1, keepdims=True))
        outs.append(jnp.dot(p.astype(BF16), v_all[:, hs], preferred_element_type=F32))
    o_ref[0] = jnp.concatenate(outs, axis=-1)


def na_branch(p_l, p_c, rpb, need_ctx_out):
    bsz, seq, _ = p_l.shape
    n_ctx = p_c.shape[1]
    rows = seq // GRID_W
    assert rows % NA_KH == 0 and rows >= 2 * NA_KH
    w = BRANCH_W
    blk = NA_KH * GRID_W
    nblk = rows // NA_KH
    cq = COL_NA // w

    def shifted(col, delta):
        return pl.BlockSpec((1, blk, w), lambda b, i: (b, jnp.clip(i + delta, 0, nblk - 1), col))

    y_l = pl.pallas_call(
        functools.partial(_na_kernel, rows=rows),
        grid=(bsz, nblk),
        in_specs=[shifted(cq, 0), shifted(cq + 1, -1), shifted(cq + 1, 0), shifted(cq + 1, 1),
                  shifted(cq + 2, -1), shifted(cq + 2, 0), shifted(cq + 2, 1),
                  pl.BlockSpec((1, n_ctx, w), lambda b, i: (b, 0, cq + 1)),
                  pl.BlockSpec((1, n_ctx, w), lambda b, i: (b, 0, cq + 2)),
                  pl.BlockSpec((NA_KH, NA_HEADS, GRID_W, blk), lambda b, i: (0, 0, 0, 0))],
        out_specs=pl.BlockSpec((1, blk, w), lambda b, i: (b, i, 0)),
        out_shape=jax.ShapeDtypeStruct((bsz, seq, w), F32),
        scratch_shapes=[pltpu.VMEM((3 * blk, w), BF16), pltpu.VMEM((3 * blk, w), BF16),
                        pltpu.VMEM((n_ctx, w), BF16), pltpu.VMEM((n_ctx, w), BF16)],
        compiler_params=pltpu.CompilerParams(
            dimension_semantics=("parallel", "arbitrary"), vmem_limit_bytes=V7X_VMEM_LIMIT),
        name="na_attn",
    )(p_l, p_l, p_l, p_l, p_l, p_l, p_l, p_c, p_c, _na_bias_slabs(rpb))
    y_c = None
    if need_ctx_out:
        y_c = pl.pallas_call(
            _na_ctx_kernel,
            grid=(bsz,),
            in_specs=[pl.BlockSpec((1, n_ctx, w), functools.partial(lambda b, c: (b, 0, c), c=cq + n)) for n in range(3)],
            out_specs=pl.BlockSpec((1, n_ctx, w), lambda b: (b, 0, 0)),
            out_shape=jax.ShapeDtypeStruct((bsz, n_ctx, w), F32),
            compiler_params=pltpu.CompilerParams(dimension_semantics=("parallel",)),
            name="na_ctx_attn",
        )(p_c, p_c, p_c)
    return y_l, y_c


def _gelu_tanh(x):
    return 0.5 * x * (1.0 + jnp.tanh(math.sqrt(2.0 / math.pi) * (x + 0.044715 * (x * x * x))))


def _silu(x):
    return x * _sigmoid(x)


def _softplus(x):
    return jnp.maximum(x, 0.0) + jnp.log1p(jnp.exp(-jnp.abs(x)))


def _rope_2d_tables(seq):
    t = jnp.arange(seq, dtype=jnp.int32)
    pos = jnp.stack([t // GRID_W, t % GRID_W], axis=-1).astype(F32)
    nf = ML_HD // 4
    inv_freq = ROPE_BASE ** (-jnp.arange(nf, dtype=F32) / nf)
    ang = jnp.broadcast_to(pos[:, :, None, None] * inv_freq, (seq, 2, 2, nf)).reshape(seq, ML_HD)
    return jnp.cos(ang), jnp.sin(ang)


def _chunk_specs(tt, nchunk, n8, reverse):
    def pos(j):
        return nchunk - 1 - j if reverse else j

    def chunk(width, col):
        return pl.BlockSpec((1, tt, width), lambda b, j: (b, pos(j), col))

    def halo(width, col, delta):
        if delta < 0:
            return pl.BlockSpec((1, 8, width), lambda b, j: (b, jnp.maximum(pos(j) * (tt // 8) - 1, 0), col))
        return pl.BlockSpec((1, 8, width), lambda b, j: (b, jnp.minimum((pos(j) + 1) * (tt // 8), n8 - 1), col))

    return chunk, halo


def _const_spec(shape):
    return pl.BlockSpec(shape, lambda b, j: (0,) * len(shape))


def _batch_spec(shape):
    return pl.BlockSpec((1,) + shape, lambda b, j: (b,) + (0,) * len(shape))


def _conv4(x, prev8, next8, has_prev, has_next, cw, cb):
    tt = x.shape[0]
    row = lax.broadcasted_iota(jnp.int32, x.shape, 0)
    p6 = prev8[6:7, :] * has_prev
    p7 = prev8[7:8, :] * has_prev
    n0 = next8[0:1, :] * has_next
    xm1 = jnp.where(row == 0, p7, pltpu.roll(x, 1, 0))
    xm2 = jnp.where(row == 0, p6, jnp.where(row == 1, p7, pltpu.roll(x, 2, 0)))
    xp1 = jnp.where(row == tt - 1, n0, pltpu.roll(x, tt - 1, 0))
    return cw[0:1, :] * xm2 + cw[1:2, :] * xm1 + cw[2:3, :] * x + cw[3:4, :] * xp1 + cb


def _cumsum_rows(x, reverse):
    n = x.shape[0]
    row = lax.broadcasted_iota(jnp.int32, x.shape, 0)
    s = 1
    while s < n:
        if reverse:
            x = x + jnp.where(row < n - s, pltpu.roll(x, n - s, 0), 0.0)
        else:
            x = x + jnp.where(row >= s, pltpu.roll(x, s, 0), 0.0)
        s *= 2
    return x


def _causal_mask(n, reverse):
    ii = lax.broadcasted_iota(jnp.int32, (n, n), 0)
    jj = lax.broadcasted_iota(jnp.int32, (n, n), 1)
    return (jj >= ii) if reverse else (jj <= ii)


MASK_NEG = -1e30


LRU_TT = 256


def _lru_kernel(*refs, reverse, finalize, nchunk):
    if finalize:
        (xp_ref, x_ref, xn_ref, g_ref, ho_ref, h0_ref, cw_ref, cb_ref, w_ref, bias_ref, sp_ref,
         o_ref, hl_ref, carry_ref) = refs
    else:
        (xp_ref, x_ref, xn_ref, h0_ref, cw_ref, cb_ref, w_ref, bias_ref, sp_ref, o_ref, hl_ref, carry_ref) = refs
    j = pl.program_id(1)
    c = (nchunk - 1 - j) if reverse else j

    @pl.when(j == 0)
    def _():
        carry_ref[...] = h0_ref[0]

    tt = x_ref.shape[1]
    w = x_ref.shape[2]
    xc = _conv4(x_ref[0], xp_ref[0], xn_ref[0], (c > 0).astype(F32), (c < nchunk - 1).astype(F32),
                cw_ref[...], cb_ref[...])
    g = jnp.dot(xc.astype(BF16), w_ref[...], preferred_element_type=F32)
    r = _sigmoid(g[:, :w] + bias_ref[0:1, :])
    ig = _sigmoid(g[:, w:] + bias_ref[1:2, :])
    log_a = -LRU_C * r * sp_ref[...]
    a = jnp.exp(log_a)
    u = jnp.sqrt(1.0 - jnp.exp(2.0 * log_a)) * (ig * xc)
    sub = lax.broadcasted_iota(jnp.int32, (tt, w), 0) % 8
    s = 1
    while s < 8:
        if reverse:
            keep = sub < 8 - s
            a_s = jnp.where(keep, pltpu.roll(a, tt - s, 0), 1.0)
            u_s = jnp.where(keep, pltpu.roll(u, tt - s, 0), 0.0)
        else:
            keep = sub >= s
            a_s = jnp.where(keep, pltpu.roll(a, s, 0), 1.0)
            u_s = jnp.where(keep, pltpu.roll(u, s, 0), 0.0)
        u = a * u_s + u
        a = a * a_s
        s *= 2
    new_carry = carry_ref[...]
    nblk = tt // 8
    hs = [None] * nblk
    for blk in (reversed(range(nblk)) if reverse else range(nblk)):
        hb = u[blk * 8:(blk + 1) * 8, :] + a[blk * 8:(blk + 1) * 8, :] * new_carry
        new_carry = hb[0:1, :] if reverse else hb[7:8, :]
        hs[blk] = hb
    h = jnp.concatenate(hs, axis=0)
    carry_ref[...] = new_carry
    hl_ref[0] = new_carry
    if finalize:
        o_ref[0] = (ho_ref[0] + h) * _gelu_tanh(g_ref[0])
    else:
        o_ref[0] = h


def _lru_pass(p, h0, other, cw, cb, wcat, bias, sp, reverse):
    bsz, seq, _ = p.shape
    w = BRANCH_W
    tt = min(LRU_TT, seq)
    nchunk = seq // tt
    chunk, halo = _chunk_specs(tt, nchunk, seq // 8, reverse)
    cx = COL_LRU_X // w
    finalize = other is not None
    in_specs = [halo(w, cx, -1), chunk(w, cx), halo(w, cx, 1)]
    args = [p, p, p]
    if finalize:
        in_specs += [chunk(w, COL_LRU_G // w), chunk(w, 0)]
        args += [p, other]
    in_specs += [_batch_spec((1, w)), _const_spec((CONV_W, w)), _const_spec((1, w)),
                 _const_spec((w, 2 * w)), _const_spec((2, w)), _const_spec((1, w))]
    args += [h0, cw, cb, wcat, bias, sp]
    return pl.pallas_call(
        functools.partial(_lru_kernel, reverse=reverse, finalize=finalize, nchunk=nchunk),
        grid=(bsz, nchunk),
        in_specs=in_specs,
        out_specs=[chunk(w, 0), _batch_spec((1, w))],
        out_shape=[jax.ShapeDtypeStruct((bsz, seq, w), F32), jax.ShapeDtypeStruct((bsz, 1, w), F32)],
        scratch_shapes=[pltpu.VMEM((1, w), F32)],
        compiler_params=pltpu.CompilerParams(dimension_semantics=("parallel", "arbitrary")),
        name="lru_scan",
    )(*args)


def _block_diag(wg):
    g, n, _ = wg.shape
    eye = jnp.eye(g, dtype=wg.dtype)
    return (wg[:, :, None, :] * eye[:, None, :, None]).reshape(g * n, g * n)


def lru_branch(p_l, p_c, conv_w, conv_b, wa, ba, wx, bx, lam, need_ctx_out):
    bsz = p_l.shape[0]
    w = BRANCH_W
    cb = conv_b.reshape(1, w)
    sp = jax.nn.softplus(-lam)
    zeros = jnp.zeros((bsz, 1, w), F32)
    h_c = h_l = None
    for d, reverse in ((0, False), (1, True)):
        wcat = jnp.concatenate([_block_diag(wa[d]), _block_diag(wx[d])], axis=1).astype(BF16)
        bias = jnp.stack([ba[d], bx[d]])
        other_c = h_c if (d == 1 and need_ctx_out) else None
        h_c, st = _lru_pass(p_c, zeros, other_c, conv_w, cb, wcat, bias, sp[d:d + 1], reverse)
        h_l, _ = _lru_pass(p_l, st, h_l if d == 1 else None, conv_w, cb, wcat, bias, sp[d:d + 1], reverse)
    return h_l, (h_c if need_ctx_out else None)


def _ssd_kernel(*refs, reverse, finalize, nchunk, d):
    (xp_ref, x_ref, xn_ref, bp_ref, bc_ref, bn_ref, dt_ref) = refs[:7]
    k = 7
    if finalize:
        z_ref, yp_ref = refs[k:k + 2]
        k += 2
    s0_ref, cwx_ref, cbx_ref, cwb_ref, cbb_ref, dtb_ref, a_ref = refs[k:k + 7]
    k += 7
    if finalize:
        dsk_ref, ng_ref = refs[k:k + 2]
        k += 2
    o_ref, so_ref, s_ref = refs[k:k + 3]
    j = pl.program_id(1)
    c = (nchunk - 1 - j) if reverse else j

    @pl.when(j == 0)
    def _():
        s_ref[...] = s0_ref[0]

    q = x_ref.shape[1]
    has_prev = (c > 0).astype(F32)
    has_next = (c < nchunk - 1).astype(F32)
    xs = _silu(_conv4(x_ref[0], xp_ref[0], xn_ref[0], has_prev, has_next, cwx_ref[...], cbx_ref[...]))
    bc = _silu(_conv4(bc_ref[0], bp_ref[0], bn_ref[0], has_prev, has_next, cwb_ref[...], cbb_ref[...]))
    dt = _softplus(dt_ref[0] + dtb_ref[...])
    cum = _cumsum_rows(dt * a_ref[...], reverse)
    tot = cum[0:1, :] if reverse else cum[q - 1:q, :]
    w_end = jnp.exp(tot - cum) * dt
    ecum = jnp.exp(cum)
    etot = jnp.exp(tot)
    dt_t = dt.T
    cum_t = cum.T
    b_t = bc[:, :SSD_GN].T
    mask = _causal_mask(q, reverse)
    xb = xs.astype(BF16)
    cgs = [bc[:, SSD_GN + g * SSD_STATE:SSD_GN + (g + 1) * SSD_STATE].astype(BF16) for g in range(SSD_GROUPS)]
    cbs = [lax.dot_general(cgs[g], bc[:, g * SSD_STATE:(g + 1) * SSD_STATE].astype(BF16), _NT,
                           preferred_element_type=F32) for g in range(SSD_GROUPS)]
    first = lax.broadcasted_iota(jnp.int32, (q, 2 * SSD_HD), 1) < SSD_HD
    npair = SSD_HEADS // 2
    pair_group = [(2 * n) // (SSD_HEADS // SSD_GROUPS) for n in range(npair)]
    pairs = [slice(n * 2 * SSD_HD, (n + 1) * 2 * SSD_HD) for n in range(npair)]

    def per_lane(t, n):
        c0 = d * SSD_HEADS + 2 * n
        return jnp.where(first[:t.shape[0]], t[:, c0:c0 + 1], t[:, c0 + 1:c0 + 2])

    intra = []
    for h in range(SSD_HEADS):
        col = d * SSD_HEADS + h
        diff = cum[:, col:col + 1] - cum_t[col:col + 1, :]
        m = cbs[pair_group[h // 2]] * jnp.exp(jnp.where(mask, diff, MASK_NEG)) * dt_t[col:col + 1, :]
        intra.append(jnp.dot(m.astype(BF16), xb[:, pairs[h // 2]], preferred_element_type=F32))
    s_old = [s_ref[n] for n in range(npair)]
    inter = [jnp.dot(cgs[pair_group[n]], s_old[n].astype(BF16), preferred_element_type=F32) for n in range(npair)]
    outs = []
    for n in range(npair):
        g = pair_group[n]
        outs.append(jnp.where(first, intra[2 * n], intra[2 * n + 1]) + per_lane(ecum, n) * inter[n])
        xw = (xs[:, pairs[n]] * per_lane(w_end, n)).astype(BF16)
        s_ref[n] = per_lane(etot, n) * s_old[n] + jnp.dot(
            b_t[g * SSD_STATE:(g + 1) * SSD_STATE, :].astype(BF16), xw, preferred_element_type=F32)
    y = jnp.concatenate(outs, axis=-1)
    so_ref[0] = s_ref[...]
    if finalize:
        yt = (xs * dsk_ref[...] + yp_ref[0] + y) * _silu(z_ref[0])
        o_ref[0] = yt * lax.rsqrt(jnp.mean(yt * yt, axis=-1, keepdims=True) + LN_EPS) * ng_ref[...]
    else:
        o_ref[0] = y


def _ssd_pass(p, s0, other, params, reverse, d):
    bsz, seq, _ = p.shape
    w = BRANCH_W
    q = SSD_CHUNK
    nchunk = seq // q
    chunk, halo = _chunk_specs(q, nchunk, seq // 8, reverse)
    cx, cb2, cdt = COL_SSD_X // w, COL_SSD_B // (2 * SSD_GN), COL_SSD_DT // 128
    finalize = other is not None
    cwx, cbx, cwb, cbb, dtb, arow, dsk, ng = params
    in_specs = [halo(w, cx, -1), chunk(w, cx), halo(w, cx, 1),
                halo(2 * SSD_GN, cb2, -1), chunk(2 * SSD_GN, cb2), halo(2 * SSD_GN, cb2, 1), chunk(128, cdt)]
    args = [p] * 7
    if finalize:
        in_specs += [chunk(w, COL_SSD_Z // w), chunk(w, 0)]
        args += [p, other]
    st_shape = (SSD_HEADS // 2, SSD_STATE, 2 * SSD_HD)
    in_specs += [_batch_spec(st_shape), _const_spec((CONV_W, w)), _const_spec((1, w)),
                 _const_spec((CONV_W, 2 * SSD_GN)), _const_spec((1, 2 * SSD_GN)), _const_spec((1, 128)),
                 _const_spec((1, 128))]
    args += [s0, cwx, cbx, cwb, cbb, dtb, arow]
    if finalize:
        in_specs += [_const_spec((1, w)), _const_spec((1, w))]
        args += [dsk, ng]
    return pl.pallas_call(
        functools.partial(_ssd_kernel, reverse=reverse, finalize=finalize, nchunk=nchunk, d=d),
        grid=(bsz, nchunk),
        in_specs=in_specs,
        out_specs=[chunk(w, 0), _batch_spec(st_shape)],
        out_shape=[jax.ShapeDtypeStruct((bsz, seq, w), F32), jax.ShapeDtypeStruct((bsz,) + st_shape, F32)],
        scratch_shapes=[pltpu.VMEM(st_shape, F32)],
        compiler_params=pltpu.CompilerParams(dimension_semantics=("parallel", "arbitrary")),
        name="ssd_scan",
    )(*args)


def _lane_row(vals, start):
    return jnp.zeros((128,), F32).at[start:start + vals.shape[0]].set(vals.astype(F32)).reshape(1, 128)


def ssd_branch(p_l, p_c, conv_w, conv_b, dt_bias, a_log, d_skip, norm_g, need_ctx_out):
    bsz = p_l.shape[0]
    w = BRANCH_W
    params = (conv_w[:, :w], conv_b[:w].reshape(1, w), conv_w[:, w:], conv_b[w:].reshape(1, 2 * SSD_GN),
              _lane_row(dt_bias.reshape(-1), 0), _lane_row(-jnp.exp(a_log.astype(F32)).reshape(-1), 0),
              jnp.repeat(d_skip, SSD_HD).reshape(1, w), norm_g.reshape(1, w))
    zeros = jnp.zeros((bsz, SSD_HEADS // 2, SSD_STATE, 2 * SSD_HD), F32)
    y_c = y_l = None
    for d, reverse in ((0, False), (1, True)):
        other_c = y_c if (d == 1 and need_ctx_out) else None
        y_c, st = _ssd_pass(p_c, zeros, other_c, params, reverse, d)
        y_l, _ = _ssd_pass(p_l, st, y_l if d == 1 else None, params, reverse, d)
    return y_l, (y_c if need_ctx_out else None)


def _rope_rotate(x):
    wl = x.shape[-1]
    half = ML_HD // 4
    lane = lax.broadcasted_iota(jnp.int32, x.shape, 1)
    return jnp.where(lane % (2 * half) < half, -pltpu.roll(x, wl - half, 1), pltpu.roll(x, half, 1))


def _log_sigmoid(x):
    return jnp.minimum(x, 0.0) - jnp.log1p(jnp.exp(-jnp.abs(x)))


def _mlstm_kernel(*refs, reverse, finalize, rope, nchunk, d):
    (qp_ref, q_ref, qn_ref, kp_ref, k_ref, kn_ref, v_ref, g_ref) = refs[:8]
    n = 8
    if rope:
        cos_ref, sin_ref = refs[n:n + 2]
        n += 2
    if finalize:
        og_ref, hp_ref = refs[n:n + 2]
        n += 2
    c0_ref, n0_ref, m0_ref, cwq_ref, cbq_ref, cwk_ref, cbk_ref, ib_ref, fb_ref = refs[n:n + 9]
    n += 9
    o_ref, co_ref, no_ref, mo_ref, c_ref, n_ref, m_ref = refs[n:n + 7]
    j = pl.program_id(1)
    c = (nchunk - 1 - j) if reverse else j

    @pl.when(j == 0)
    def _():
        c_ref[...] = c0_ref[0]
        n_ref[...] = n0_ref[0]
        m_ref[...] = m0_ref[0]

    qn = q_ref.shape[1]
    has_prev = (c > 0).astype(F32)
    has_next = (c < nchunk - 1).astype(F32)
    q = _silu(_conv4(q_ref[0], qp_ref[0], qn_ref[0], has_prev, has_next, cwq_ref[...], cbq_ref[...]))
    k = _silu(_conv4(k_ref[0], kp_ref[0], kn_ref[0], has_prev, has_next, cwk_ref[...], cbk_ref[...]))
    if rope:
        cos = cos_ref[...]
        sin = sin_ref[...]
        q = q * cos + _rope_rotate(q) * sin
        k = k * cos + _rope_rotate(k) * sin
    q = q * (ML_HD ** -0.5)
    v = v_ref[0]
    gb = g_ref[0]
    li = gb + ib_ref[...]
    b = _cumsum_rows(_log_sigmoid(gb + fb_ref[...]), reverse)
    tot = b[0:1, :] if reverse else b[qn - 1:qn, :]
    b_t = b.T
    li_t = li.T
    mask = _causal_mask(qn, reverse)
    heads = [slice(h * ML_HD, (h + 1) * ML_HD) for h in range(ML_HEADS)]
    qb = q.astype(BF16)
    kb = k.astype(BF16)
    vb = v.astype(BF16)
    qk = [lax.dot_general(qb[:, hs], kb[:, hs], _NT, preferred_element_type=F32) for hs in heads]
    c_old = [c_ref[h] for h in range(ML_HEADS)]
    n_old = [n_ref[h:h + 1, :] for h in range(ML_HEADS)]
    qc = [lax.dot_general(qb[:, hs], c_old[h].astype(BF16), _NT, preferred_element_type=F32)
          for h, hs in enumerate(heads)]
    gate = []
    for h in range(ML_HEADS):
        ci = 4 * ML_HEADS + d * 2 * ML_HEADS + h
        cf = ci + ML_HEADS
        b_c = b[:, cf:cf + 1]
        b_end = tot[:, cf:cf + 1]
        m_st = m_ref[h:h + 1, 0:1]
        end_log = b_end - b_c + li[:, ci:ci + 1]
        m_new = jnp.maximum(b_end + m_st, jnp.max(end_log, axis=0, keepdims=True))
        dlog = jnp.where(mask, b_c - b_t[cf:cf + 1, :] + li_t[ci:ci + 1, :], MASK_NEG)
        m_inter = b_c + m_st
        m_i = jnp.maximum(jnp.max(dlog, axis=1, keepdims=True), m_inter)
        gate.append((jnp.exp(end_log - m_new), jnp.exp(b_end + m_st - m_new), m_new,
                     jnp.exp(dlog - m_i), jnp.exp(m_inter - m_i), jnp.exp(-m_i)))
    s_all = [qk[h] * gate[h][3] for h in range(ML_HEADS)]
    sv = [jnp.dot(s_all[h].astype(BF16), vb[:, hs], preferred_element_type=F32) for h, hs in enumerate(heads)]
    upd = [jnp.dot((v[:, hs] * gate[h][0]).T.astype(BF16), kb[:, hs], preferred_element_type=F32)
           for h, hs in enumerate(heads)]
    outs = []
    for h, hs in enumerate(heads):
        w, carry_scale, m_new, _, w_in, floor = gate[h]
        num = sv[h] + w_in * qc[h]
        den = jnp.sum(s_all[h], axis=1, keepdims=True) + w_in * jnp.sum(q[:, hs] * n_old[h], axis=1, keepdims=True)
        outs.append(num / jnp.maximum(jnp.abs(den), floor))
        c_ref[h] = carry_scale * c_old[h] + upd[h]
        n_ref[h:h + 1, :] = carry_scale * n_old[h] + jnp.sum(k[:, hs] * w, axis=0, keepdims=True)
        m_ref[h:h + 1, :] = jnp.broadcast_to(m_new, (1, ML_HD))
    hout = jnp.concatenate(outs, axis=-1)
    co_ref[0] = c_ref[...]
    no_ref[0] = n_ref[...]
    mo_ref[0] = m_ref[...]
    if finalize:
        o_ref[0] = _sigmoid(og_ref[0]) * (hp_ref[0] + hout)
    else:
        o_ref[0] = hout


def _mlstm_pass(p, state, other, tables, params, reverse, d):
    bsz, seq, _ = p.shape
    w = BRANCH_W
    qn = ML_CHUNK
    nchunk = seq // qn
    chunk, halo = _chunk_specs(qn, nchunk, seq // 8, reverse)
    cq = COL_ML // w
    finalize = other is not None
    rope = tables is not None
    in_specs = [halo(w, cq, -1), chunk(w, cq), halo(w, cq, 1), halo(w, cq + 1, -1), chunk(w, cq + 1),
                halo(w, cq + 1, 1), chunk(w, cq + 2), chunk(128, COL_ML_G // 128)]
    args = [p] * 8
    if rope:
        tab = pl.BlockSpec((qn, w), (lambda b, j: (nchunk - 1 - j, 0)) if reverse else (lambda b, j: (j, 0)))
        in_specs += [tab, tab]
        args += list(tables)
    if finalize:
        in_specs += [chunk(w, cq + 3), chunk(w, 0)]
        args += [p, other]
    st_shapes = [(ML_HEADS, ML_HD, ML_HD), (ML_HEADS, ML_HD), (ML_HEADS, ML_HD)]
    in_specs += [_batch_spec(s) for s in st_shapes]
    in_specs += [_const_spec((CONV_W, w)), _const_spec((1, w)), _const_spec((CONV_W, w)), _const_spec((1, w)),
                 _const_spec((1, 128)), _const_spec((1, 128))]
    args += list(state) + list(params)
    res = pl.pallas_call(
        functools.partial(_mlstm_kernel, reverse=reverse, finalize=finalize, rope=rope, nchunk=nchunk, d=d),
        grid=(bsz, nchunk),
        in_specs=in_specs,
        out_specs=[chunk(w, 0)] + [_batch_spec(s) for s in st_shapes],
        out_shape=[jax.ShapeDtypeStruct((bsz, seq, w), F32)]
                  + [jax.ShapeDtypeStruct((bsz,) + s, F32) for s in st_shapes],
        scratch_shapes=[pltpu.VMEM(s, F32) for s in st_shapes],
        compiler_params=pltpu.CompilerParams(dimension_semantics=("parallel", "arbitrary")),
        name="mlstm_scan",
    )(*args)
    return res[0], tuple(res[1:])


def mlstm_branch(p_l, p_c, conv_w, conv_b, i_bias, f_bias, need_ctx_out):
    bsz, seq, _ = p_l.shape
    w = BRANCH_W
    cos, sin = _rope_2d_tables(seq)
    tables = (jnp.tile(cos, (1, ML_HEADS)), jnp.tile(sin, (1, ML_HEADS)))
    zero_h = jnp.zeros_like(i_bias)
    ib = _lane_row(jnp.concatenate([i_bias, zero_h], axis=1).reshape(-1), 4 * ML_HEADS)
    fb = _lane_row(jnp.concatenate([zero_h, f_bias], axis=1).reshape(-1), 4 * ML_HEADS)
    params = (conv_w[:, :w], conv_b[:w].reshape(1, w), conv_w[:, w:], conv_b[w:].reshape(1, w), ib, fb)
    state0 = (jnp.zeros((bsz, ML_HEADS, ML_HD, ML_HD), F32), jnp.zeros((bsz, ML_HEADS, ML_HD), F32),
              jnp.zeros((bsz, ML_HEADS, ML_HD), F32))
    h_c = h_l = None
    for d, reverse in ((0, False), (1, True)):
        other_c = h_c if (d == 1 and need_ctx_out) else None
        h_c, st = _mlstm_pass(p_c, state0, other_c, None, params, reverse, d)
        h_l, _ = _mlstm_pass(p_l, st, h_l if d == 1 else None, tables, params, reverse, d)
    return h_l, (h_c if need_ctx_out else None)


MOE_TM = 512
ROUTE_LANES = 128


def _moe_router_kernel(x_ref, sc_ref, sh_ref, wr_ref, br_ref, hb_ref, rt_ref):
    h = x_ref[0] * (1.0 + sc_ref[0]) + sh_ref[0]
    hb_ref[0] = h.astype(BF16)
    logits = jnp.dot(h, wr_ref[...], precision=lax.Precision.HIGHEST, preferred_element_type=F32) + br_ref[...]
    lane = lax.broadcasted_iota(jnp.int32, logits.shape, 1)
    m1 = jnp.max(logits, axis=-1, keepdims=True)
    i1 = jnp.min(jnp.where(logits == m1, lane, ROUTE_LANES), axis=-1, keepdims=True)
    rest = jnp.where(lane == i1, MASK_NEG, logits)
    m2 = jnp.max(rest, axis=-1, keepdims=True)
    i2 = jnp.min(jnp.where(rest == m2, lane, ROUTE_LANES), axis=-1, keepdims=True)
    e2 = jnp.exp(m2 - m1)
    p1 = 1.0 / (1.0 + e2)
    p2 = e2 * p1
    rt_ref[0] = jnp.where(lane == 0, i1.astype(F32), jnp.where(lane == 1, i2.astype(F32),
                          jnp.where(lane == 2, p1, jnp.where(lane == 3, p2, 0.0))))


def moe_router(x, sc, sh, w_router, b_router, tm=512):
    bsz, seq, d = x.shape
    n_e = w_router.shape[1]
    tm = min(tm, seq)
    wr = jnp.zeros((d, ROUTE_LANES), F32).at[:, :n_e].set(w_router)
    br = jnp.full((1, ROUTE_LANES), MASK_NEG, F32).at[0, :n_e].set(b_router)
    return pl.pallas_call(
        _moe_router_kernel,
        grid=(bsz, seq // tm),
        in_specs=[pl.BlockSpec((1, tm, d), lambda b, i: (b, i, 0)),
                  pl.BlockSpec((1, 1, d), lambda b, i: (b, 0, 0)),
                  pl.BlockSpec((1, 1, d), lambda b, i: (b, 0, 0)),
                  pl.BlockSpec((d, ROUTE_LANES), lambda b, i: (0, 0)),
                  pl.BlockSpec((1, ROUTE_LANES), lambda b, i: (0, 0))],
        out_specs=[pl.BlockSpec((1, tm, d), lambda b, i: (b, i, 0)),
                   pl.BlockSpec((1, tm, ROUTE_LANES), lambda b, i: (b, i, 0))],
        out_shape=[jax.ShapeDtypeStruct((bsz, seq, d), BF16), jax.ShapeDtypeStruct((bsz, seq, ROUTE_LANES), F32)],
        compiler_params=pltpu.CompilerParams(dimension_semantics=("parallel", "parallel")),
        name="moe_router",
    )(x, sc, sh, wr, br)


def _route_tables(idx, n_e, tmg):
    n_tok = idx.shape[0]
    e_flat = idx.reshape(-1)
    onehot = (e_flat[:, None] == jnp.arange(n_e, dtype=jnp.int32)[None, :]).astype(jnp.int32)
    csum = jnp.cumsum(onehot, axis=0)
    rank = jnp.take_along_axis(csum - onehot, e_flat[:, None], axis=1)[:, 0]
    padded = ((csum[-1] + tmg - 1) // tmg) * tmg
    ends = jnp.cumsum(padded)
    pos = (ends - padded)[e_flat] + rank
    n_rows = TOP_K * n_tok + n_e * tmg
    ntiles = n_rows // tmg
    tile_start = jnp.arange(ntiles, dtype=jnp.int32) * tmg
    tile_expert = jnp.minimum(jnp.sum(tile_start[:, None] >= ends[None, :], axis=1), n_e - 1)
    src = (jnp.arange(n_rows, dtype=jnp.int32) % n_tok).at[pos].set(
        jnp.arange(TOP_K * n_tok, dtype=jnp.int32) // TOP_K)
    meta = jnp.concatenate([tile_expert, ends[-1:] // tmg]).astype(jnp.int32)
    return pos.reshape(n_tok, TOP_K), src, meta


def _expert_changed(meta_ref, i):
    return (i == 0) | (meta_ref[i] != meta_ref[jnp.maximum(i - 1, 0)])


def _moe_up_kernel(meta_ref, x_ref, wg_ref, wu_ref, a_ref, wgb_ref, wub_ref, *, ntiles):
    i = pl.program_id(1)

    @pl.when(_expert_changed(meta_ref, i))
    def _():
        wgb_ref[...] = wg_ref[0].astype(BF16)
        wub_ref[...] = wu_ref[0].astype(BF16)

    @pl.when(i < meta_ref[ntiles])
    def _():
        x = x_ref[...]
        g = jnp.dot(x, wgb_ref[...], preferred_element_type=F32)
        u = jnp.dot(x, wub_ref[...], preferred_element_type=F32)
        a_ref[...] = (_silu(g) * u).astype(BF16)

    @pl.when(i >= meta_ref[ntiles])
    def _():
        a_ref[...] = jnp.zeros_like(a_ref)


def _moe_down_kernel(meta_ref, a_ref, wd_ref, y_ref, wdb_ref, *, ntiles):
    i = pl.program_id(0)

    @pl.when(_expert_changed(meta_ref, i))
    def _():
        wdb_ref[...] = wd_ref[0].astype(BF16)

    @pl.when(i < meta_ref[ntiles])
    def _():
        y_ref[...] = jnp.dot(a_ref[...], wdb_ref[...], preferred_element_type=F32).astype(y_ref.dtype)

    @pl.when(i >= meta_ref[ntiles])
    def _():
        y_ref[...] = jnp.zeros_like(y_ref)


def moe_experts(xs, meta, wg, wu, wd, tmg, tf):
    n_rows, d = xs.shape
    n_e, _, f = wg.shape
    ntiles = n_rows // tmg
    nf = f // tf
    once = pl.Buffered(1)
    a = pl.pallas_call(
        functools.partial(_moe_up_kernel, ntiles=ntiles),
        grid_spec=pltpu.PrefetchScalarGridSpec(
            num_scalar_prefetch=1,
            grid=(nf, ntiles),
            in_specs=[pl.BlockSpec((tmg, d), lambda j, i, m: (i, 0)),
                      pl.BlockSpec((1, d, tf), lambda j, i, m: (m[i], 0, j), pipeline_mode=once),
                      pl.BlockSpec((1, d, tf), lambda j, i, m: (m[i], 0, j), pipeline_mode=once)],
            out_specs=pl.BlockSpec((tmg, tf), lambda j, i, m: (i, j)),
            scratch_shapes=[pltpu.VMEM((d, tf), BF16), pltpu.VMEM((d, tf), BF16)]),
        out_shape=jax.ShapeDtypeStruct((n_rows, f), BF16),
        compiler_params=pltpu.CompilerParams(
            dimension_semantics=("arbitrary", "arbitrary"), vmem_limit_bytes=V7X_VMEM_LIMIT),
        name="moe_up",
    )(meta, xs, wg, wu)
    return pl.pallas_call(
        functools.partial(_moe_down_kernel, ntiles=ntiles),
        grid_spec=pltpu.PrefetchScalarGridSpec(
            num_scalar_prefetch=1,
            grid=(ntiles,),
            in_specs=[pl.BlockSpec((tmg, f), lambda i, m: (i, 0)),
                      pl.BlockSpec((1, f, d), lambda i, m: (m[i], 0, 0), pipeline_mode=once)],
            out_specs=pl.BlockSpec((tmg, d), lambda i, m: (i, 0)),
            scratch_shapes=[pltpu.VMEM((f, d), BF16)]),
        out_shape=jax.ShapeDtypeStruct((n_rows, d), BF16),
        compiler_params=pltpu.CompilerParams(
            dimension_semantics=("arbitrary",), vmem_limit_bytes=V7X_VMEM_LIMIT),
        name="moe_down",
    )(meta, a, wd)


def _combine_ln_kernel(x_ref, gt_ref, rt_ref, ya_ref, yb_ref, g_ref, b_ref, o_ref, *, alpha):
    rt = rt_ref[0]
    y = rt[:, 2:3] * ya_ref[0].astype(F32) + rt[:, 3:4] * yb_ref[0].astype(F32)
    o_ref[0] = _layer_norm_rows(alpha * x_ref[0] + gt_ref[0] * y, g_ref[...], b_ref[...])


def combine_residual_ln(x, gate, route, ya, yb, ln_g, ln_b, alpha, tm=512):
    bsz, seq, d = x.shape
    tm = min(tm, seq)
    row = pl.BlockSpec((1, tm, d), lambda b, i: (b, i, 0))
    return pl.pallas_call(
        functools.partial(_combine_ln_kernel, alpha=alpha),
        grid=(bsz, seq // tm),
        in_specs=[row, pl.BlockSpec((1, 1, d), lambda b, i: (b, 0, 0)),
                  pl.BlockSpec((1, tm, ROUTE_LANES), lambda b, i: (b, i, 0)), row, row,
                  pl.BlockSpec((1, d), lambda b, i: (0, 0)), pl.BlockSpec((1, d), lambda b, i: (0, 0))],
        out_specs=row,
        out_shape=jax.ShapeDtypeStruct((bsz, seq, d), F32),
        compiler_params=pltpu.CompilerParams(
            dimension_semantics=("parallel", "parallel"), vmem_limit_bytes=V7X_VMEM_LIMIT),
        name="moe_combine_ln",
    )(x, gate, route, ya, yb, ln_g.reshape(1, d), ln_b.reshape(1, d))


def moe_residual_ln(x, sc, sh, gate, w_router, b_router, wg, wu, wd, ln_g, ln_b, alpha):
    bsz, seq, d = x.shape
    n_e, _, f = wg.shape
    n_tok = bsz * seq
    tmg = min(MOE_TM, TOP_K * n_tok)
    hb, route = moe_router(x, sc, sh, w_router, b_router)
    idx = route[..., :TOP_K].astype(jnp.int32).reshape(n_tok, TOP_K)
    pos, src, meta = _route_tables(idx, n_e, tmg)
    xs = hb.reshape(n_tok, d).at[src].get(mode="promise_in_bounds")
    ys = moe_experts(xs, meta, wg, wu, wd, tmg, f // 2)
    ya = ys.at[pos[:, 0]].get(mode="promise_in_bounds").reshape(bsz, seq, d)
    yb = ys.at[pos[:, 1]].get(mode="promise_in_bounds").reshape(bsz, seq, d)
    return combine_residual_ln(x, gate, route, ya, yb, ln_g, ln_b, alpha)


def _mixers(p_l, p_c, need_ctx_out, lru_conv_w, lru_conv_b, lru_wa, lru_ba, lru_wx, lru_bx, lru_lambda,
            na_rpb, ssd_conv_w, ssd_conv_b, ssd_dt_bias, ssd_a_log, ssd_d, ssd_norm_g, ml_conv_w, ml_conv_b,
            ml_i_bias, ml_f_bias):
    ya = lru_branch(p_l, p_c, lru_conv_w, lru_conv_b, lru_wa, lru_ba, lru_wx, lru_bx, lru_lambda, need_ctx_out)
    yb = na_branch(p_l, p_c, na_rpb, need_ctx_out)
    yc = ssd_branch(p_l, p_c, ssd_conv_w, ssd_conv_b, ssd_dt_bias, ssd_a_log, ssd_d, ssd_norm_g, need_ctx_out)
    yd = mlstm_branch(p_l, p_c, ml_conv_w, ml_conv_b, ml_i_bias, ml_f_bias, need_ctx_out)
    return (ya[0], yb[0], yc[0], yd[0]), (ya[1], yb[1], yc[1], yd[1])


def kernel(x, c, ctx, c_ctx, w_ada, b_ada, w_in, lru_conv_w, lru_conv_b, lru_wa, lru_ba, lru_wx, lru_bx, lru_lambda, na_rpb, ssd_conv_w, ssd_conv_b, ssd_dt_bias, ssd_a_log, ssd_d, ssd_norm_g, ml_conv_w, ml_conv_b, ml_i_bias, ml_f_bias, w_branch, w_out, ln_g, ln_b, ffn_w_gate, ffn_w_up, ffn_w_down, moe_w_router, moe_b_router, moe_w_gate, moe_w_up, moe_w_down):
    depth = w_in.shape[0]
    bsz, seq, d = x.shape
    alpha = (2.0 * depth) ** 0.25
    cvecs = jnp.zeros((8, d), F32).at[:bsz].set(c).at[bsz].set(c_ctx)
    mods = ada_modulation(cvecs, w_ada, b_ada)
    xl, xc = x, ctx
    for l in range(depth):
        need_ctx_out = l < depth - 1
        mod = mods[l]
        mod_l = jnp.split(mod[:bsz, None, :], 6, axis=-1)
        mod_c = jnp.split(jnp.broadcast_to(mod[bsz:bsz + 1, None, :], (bsz, 1, 6 * d)), 6, axis=-1)
        w_mix, w_gates = _pack_w_in(w_in[l])
        p_l, hb_l = mod_matmul(xl, mod_l[1], mod_l[0], w_mix)
        p_c, hb_c = mod_matmul(xc, mod_c[1], mod_c[0], w_mix)
        br_l, br_c = _mixers(p_l, p_c, need_ctx_out, lru_conv_w[l], lru_conv_b[l], lru_wa[l], lru_ba[l],
                             lru_wx[l], lru_bx[l], lru_lambda[l], na_rpb[l], ssd_conv_w[l], ssd_conv_b[l],
                             ssd_dt_bias[l], ssd_a_log[l], ssd_d[l], ssd_norm_g[l], ml_conv_w[l], ml_conv_b[l],
                             ml_i_bias[l], ml_f_bias[l])
        wb = w_branch[l].astype(BF16)
        wo = w_out[l].astype(BF16)
        xl = proj_residual_ln(merge_branches(hb_l, w_gates, br_l, wb), wo, xl, mod_l[2], ln_g[l, 0], ln_b[l, 0],
                              alpha)
        if need_ctx_out:
            xc = proj_residual_ln(merge_branches(hb_c, w_gates, br_c, wb), wo, xc, mod_c[2], ln_g[l, 0],
                                  ln_b[l, 0], alpha)
        j = l // 2
        if l % 2 == 0:
            wg = ffn_w_gate[j].astype(BF16)
            wu = ffn_w_up[j].astype(BF16)
            wd = ffn_w_down[j].astype(BF16)

            def ffn(h, m, wg=wg, wu=wu, wd=wd):
                return ffn_residual_ln(h, m[4], m[3], m[5], wg, wu, wd, ln_g[l, 1], ln_b[l, 1], alpha)
        else:
            def ffn(h, m, j=j):
                return moe_residual_ln(h, m[4], m[3], m[5], moe_w_router[j], moe_b_router[j], moe_w_gate[j],
                                       moe_w_up[j], moe_w_down[j], ln_g[l, 1], ln_b[l, 1], alpha)

        xl = ffn(xl, mod_l)
        if need_ctx_out:
            xc = ffn(xc, mod_c)
    return xl
```

```python
import functools
import math

import numpy as np
import jax
import jax.numpy as jnp
from jax import lax
from jax.experimental import pallas as pl
from jax.experimental.pallas import tpu as pltpu

F32 = jnp.float32
BF16 = jnp.bfloat16

D_MODEL = 2048
GRID_W = 64
N_BRANCH = 4
BRANCH_W = D_MODEL // N_BRANCH
CONV_W = 4
LN_EPS = 1e-5
LRU_BLOCKS = 8
LRU_BW = BRANCH_W // LRU_BLOCKS
LRU_C = 8.0
NA_HEADS = 8
NA_HD = BRANCH_W // NA_HEADS
NA_KH = 8
NA_KW = 16
SSD_HEADS = 8
SSD_HD = BRANCH_W // SSD_HEADS
SSD_GROUPS = 2
SSD_STATE = 64
SSD_CHUNK = 128
SSD_GN = SSD_GROUPS * SSD_STATE
ML_HEADS = 4
ML_HD = BRANCH_W // ML_HEADS
ML_CHUNK = 128
ROPE_BASE = 10000.0
N_EXPERTS = 8
TOP_K = 2

V7X_VMEM_LIMIT = 52 * 1024 * 1024

COL_LRU_X = 0
COL_LRU_G = 512
COL_NA = 1024
COL_SSD_Z = 2560
COL_SSD_X = 3072
COL_ML = 3584
COL_SSD_B = 5632
COL_SSD_C = 5760
COL_SSD_DT = 5888
COL_ML_G = 5904
N_MIX = 5920
N_MIX_PAD = 6144


def _pack_w_in(w):
    parts = [w[:, 0:1024], w[:, 1024:2560], w[:, 2560:3584], w[:, 3856:5904],
             w[:, 3584:3840], w[:, 3840:3856], w[:, 5904:5920],
             jnp.zeros((w.shape[0], N_MIX_PAD - N_MIX), w.dtype)]
    return jnp.concatenate(parts, axis=1).astype(BF16), w[:, N_MIX:].astype(BF16)


def _sigmoid(x):
    return 1.0 / (1.0 + jnp.exp(-x))


def _layer_norm_rows(z, g, b):
    mu = jnp.mean(z, axis=-1, keepdims=True)
    zc = z - mu
    var = jnp.mean(zc * zc, axis=-1, keepdims=True)
    return zc * lax.rsqrt(var + LN_EPS) * g + b


def _mod_matmul_kernel(x_ref, sc_ref, sh_ref, w_ref, o_ref, hb_ref):
    @pl.when(pl.program_id(2) == 0)
    def _():
        hb_ref[0] = (x_ref[0] * (1.0 + sc_ref[0]) + sh_ref[0]).astype(BF16)

    o_ref[0] = jnp.dot(hb_ref[0], w_ref[...], preferred_element_type=F32)


def mod_matmul(x, sc, sh, w, tm=1024, tn=1024):
    bsz, seq, d = x.shape
    n = w.shape[1]
    tm = min(tm, seq)
    return pl.pallas_call(
        _mod_matmul_kernel,
        grid=(bsz, seq // tm, n // tn),
        in_specs=[pl.BlockSpec((1, tm, d), lambda b, i, j: (b, i, 0)),
                  pl.BlockSpec((1, 1, d), lambda b, i, j: (b, 0, 0)),
                  pl.BlockSpec((1, 1, d), lambda b, i, j: (b, 0, 0)),
                  pl.BlockSpec((d, tn), lambda b, i, j: (0, j))],
        out_specs=[pl.BlockSpec((1, tm, tn), lambda b, i, j: (b, i, j)),
                   pl.BlockSpec((1, tm, d), lambda b, i, j: (b, i, 0))],
        out_shape=[jax.ShapeDtypeStruct((bsz, seq, n), F32), jax.ShapeDtypeStruct((bsz, seq, d), BF16)],
        compiler_params=pltpu.CompilerParams(
            dimension_semantics=("parallel", "parallel", "arbitrary"), vmem_limit_bytes=V7X_VMEM_LIMIT),
        name="in_proj",
    )(x, sc, sh, w)


def _merge_kernel(hb_ref, g0, g1, g2, g3, ya, yb, yc, yd, wb_ref, o_ref):
    hb = hb_ref[0]
    acc = None
    for n, (wg, y) in enumerate(((g0, ya), (g1, yb), (g2, yc), (g3, yd))):
        gate = _sigmoid(jnp.dot(hb, wg[...], preferred_element_type=F32))
        t = gate * jnp.dot(y[0].astype(BF16), wb_ref[n], preferred_element_type=F32)
        acc = t if acc is None else acc + t
    o_ref[0] = acc.astype(BF16)


def merge_branches(hb, w_gates, branches, wb, tm=512, tn=512):
    bsz, seq, dk = hb.shape
    d = wb.shape[2]
    tm = min(tm, seq)
    nj = d // tn
    g_specs = [pl.BlockSpec((dk, tn), functools.partial(lambda b, i, j, n: (0, n * nj + j), n=n))
               for n in range(N_BRANCH)]
    y_specs = [pl.BlockSpec((1, tm, BRANCH_W), lambda b, i, j: (b, i, 0)) for _ in range(N_BRANCH)]
    return pl.pallas_call(
        _merge_kernel,
        grid=(bsz, seq // tm, nj),
        in_specs=[pl.BlockSpec((1, tm, dk), lambda b, i, j: (b, i, 0))] + g_specs + y_specs
                 + [pl.BlockSpec((N_BRANCH, BRANCH_W, tn), lambda b, i, j: (0, 0, j))],
        out_specs=pl.BlockSpec((1, tm, tn), lambda b, i, j: (b, i, j)),
        out_shape=jax.ShapeDtypeStruct((bsz, seq, d), BF16),
        compiler_params=pltpu.CompilerParams(
            dimension_semantics=("parallel", "parallel", "arbitrary"), vmem_limit_bytes=V7X_VMEM_LIMIT),
        name="merge",
    )(hb, w_gates, w_gates, w_gates, w_gates, *branches, wb)


def _proj_ln_kernel(m_ref, w_ref, x_ref, gt_ref, g_ref, b_ref, o_ref, *, alpha):
    y = jnp.dot(m_ref[0], w_ref[...], preferred_element_type=F32)
    o_ref[0] = _layer_norm_rows(alpha * x_ref[0] + gt_ref[0] * y, g_ref[...], b_ref[...])


def proj_residual_ln(m, w, x, gate, ln_g, ln_b, alpha, tm=512):
    bsz, seq, k = m.shape
    d = w.shape[1]
    tm = min(tm, seq)
    return pl.pallas_call(
        functools.partial(_proj_ln_kernel, alpha=alpha),
        grid=(bsz, seq // tm),
        in_specs=[pl.BlockSpec((1, tm, k), lambda b, i: (b, i, 0)),
                  pl.BlockSpec((k, d), lambda b, i: (0, 0)),
                  pl.BlockSpec((1, tm, d), lambda b, i: (b, i, 0)),
                  pl.BlockSpec((1, 1, d), lambda b, i: (b, 0, 0)),
                  pl.BlockSpec((1, d), lambda b, i: (0, 0)),
                  pl.BlockSpec((1, d), lambda b, i: (0, 0))],
        out_specs=pl.BlockSpec((1, tm, d), lambda b, i: (b, i, 0)),
        out_shape=jax.ShapeDtypeStruct((bsz, seq, d), F32),
        compiler_params=pltpu.CompilerParams(
            dimension_semantics=("parallel", "parallel"), vmem_limit_bytes=V7X_VMEM_LIMIT),
        name="out_proj_ln",
    )(m, w, x, gate, ln_g.reshape(1, d), ln_b.reshape(1, d))


def _ffn_kernel(x_ref, sc_ref, sh_ref, gt_ref, wg_ref, wu_ref, wd_ref, lg_ref, lb_ref, o_ref, hb_ref, acc_ref, *, alpha):
    j = pl.program_id(2)

    @pl.when(j == 0)
    def _():
        hb_ref[...] = (x_ref[0] * (1.0 + sc_ref[0]) + sh_ref[0]).astype(BF16)
        acc_ref[...] = jnp.zeros_like(acc_ref)

    hb = hb_ref[...]
    g = jnp.dot(hb, wg_ref[...], preferred_element_type=F32)
    u = jnp.dot(hb, wu_ref[...], preferred_element_type=F32)
    acc_ref[...] += jnp.dot((_silu(g) * u).astype(BF16), wd_ref[...], preferred_element_type=F32)

    @pl.when(j == pl.num_programs(2) - 1)
    def _():
        o_ref[0] = _layer_norm_rows(alpha * x_ref[0] + gt_ref[0] * acc_ref[...], lg_ref[...], lb_ref[...])


def ffn_residual_ln(x, sc, sh, gate, wg, wu, wd, ln_g, ln_b, alpha, tm=512, tf=512):
    bsz, seq, d = x.shape
    f = wg.shape[1]
    tm = min(tm, seq)
    vec = pl.BlockSpec((1, 1, d), lambda b, i, j: (b, 0, 0))
    par = pl.BlockSpec((1, d), lambda b, i, j: (0, 0))
    return pl.pallas_call(
        functools.partial(_ffn_kernel, alpha=alpha),
        grid=(bsz, seq // tm, f // tf),
        in_specs=[pl.BlockSpec((1, tm, d), lambda b, i, j: (b, i, 0)), vec, vec, vec,
                  pl.BlockSpec((d, tf), lambda b, i, j: (0, j)),
                  pl.BlockSpec((d, tf), lambda b, i, j: (0, j)),
                  pl.BlockSpec((tf, d), lambda b, i, j: (j, 0)), par, par],
        out_specs=pl.BlockSpec((1, tm, d), lambda b, i, j: (b, i, 0)),
        out_shape=jax.ShapeDtypeStruct((bsz, seq, d), F32),
        scratch_shapes=[pltpu.VMEM((tm, d), BF16), pltpu.VMEM((tm, d), F32)],
        compiler_params=pltpu.CompilerParams(
            dimension_semantics=("parallel", "parallel", "arbitrary"), vmem_limit_bytes=V7X_VMEM_LIMIT),
        name="ffn_ln",
    )(x, sc, sh, gate, wg, wu, wd, ln_g.reshape(1, d), ln_b.reshape(1, d))


def _ada_kernel(c_ref, w_ref, b_ref, o_ref):
    cv = c_ref[...]
    o_ref[0] = jnp.dot(cv * _sigmoid(cv), w_ref[0], precision=lax.Precision.HIGHEST,
                       preferred_element_type=F32) + b_ref[0]


def ada_modulation(cvecs, w, b, tn=1536):
    r, d = cvecs.shape
    depth, _, n = w.shape
    return pl.pallas_call(
        _ada_kernel,
        grid=(depth, n // tn),
        in_specs=[pl.BlockSpec((r, d), lambda l, j: (0, 0)), pl.BlockSpec((1, d, tn), lambda l, j: (l, 0, j)),
                  pl.BlockSpec((1, 1, tn), lambda l, j: (l, 0, j))],
        out_specs=pl.BlockSpec((1, r, tn), lambda l, j: (l, 0, j)),
        out_shape=jax.ShapeDtypeStruct((depth, r, n), F32),
        compiler_params=pltpu.CompilerParams(
            dimension_semantics=("parallel", "parallel"), vmem_limit_bytes=V7X_VMEM_LIMIT),
        name="ada_mod",
    )(cvecs, w, b.reshape(depth, 1, n))


NA_NEG = -1e30


def _na_bias_slabs(rpb):
    w = jnp.arange(GRID_W)
    cs = jnp.clip(w - NA_KW // 2, 0, GRID_W - NA_KW)
    ok = (w[None, :] >= cs[:, None]) & (w[None, :] < cs[:, None] + NA_KW)
    dc = jnp.clip(w[None, :] - w[:, None] + (NA_KW - 1), 0, 2 * NA_KW - 2)
    tab = jnp.where(ok, rpb[:, :, dc], NA_NEG)
    idx = jnp.arange(NA_KH)[:, None] + jnp.arange(NA_KH)[None, :]
    slab = tab[:, idx]
    return slab.transpose(1, 0, 3, 2, 4).reshape(NA_KH, NA_HEADS, GRID_W, NA_KH * GRID_W)


def _softmax2(s_a, s_b):
    m = jnp.maximum(jnp.max(s_a, axis=-1, keepdims=True), jnp.max(s_b, axis=-1, keepdims=True))
    e_a = jnp.exp(s_a - m)
    e_b = jnp.exp(s_b - m)
    inv = 1.0 / (jnp.sum(e_a, axis=-1, keepdims=True) + jnp.sum(e_b, axis=-1, keepdims=True))
    return e_a * inv, e_b * inv


_NT = (((1,), (1,)), ((), ()))


def _na_kernel(q_ref, kp_ref, kc_ref, kn_ref, vp_ref, vc_ref, vn_ref, ck_ref, cv_ref, bias_ref, o_ref,
               kw_ref, vw_ref, ckb_ref, cvb_ref, *, rows):
    i = pl.program_id(1)
    blk = NA_KH * GRID_W
    for n, (kr, vr) in enumerate(((kp_ref, vp_ref), (kc_ref, vc_ref), (kn_ref, vn_ref))):
        kw_ref[n * blk:(n + 1) * blk, :] = kr[0].astype(BF16)
        vw_ref[n * blk:(n + 1) * blk, :] = vr[0].astype(BF16)
    ckb_ref[...] = ck_ref[0].astype(BF16)
    cvb_ref[...] = cv_ref[0].astype(BF16)
    scale = NA_HD ** -0.5

    def body(rr, carry):
        r = i * NA_KH + rr
        rs = jnp.clip(r - NA_KH // 2, 0, rows - NA_KH)
        off = pl.multiple_of((rs - (i - 1) * NA_KH) * GRID_W, GRID_W)
        v = rs - r + (NA_KH - 1)
        q_all = (q_ref[0, pl.ds(pl.multiple_of(rr * GRID_W, GRID_W), GRID_W), :] * scale).astype(BF16)
        kwin = kw_ref[pl.ds(off, blk), :]
        vwin = vw_ref[pl.ds(off, blk), :]
        first = lax.broadcasted_iota(jnp.int32, (GRID_W, 2 * NA_HD), 1) < NA_HD
        pairs = [slice(n * 2 * NA_HD, (n + 1) * 2 * NA_HD) for n in range(NA_HEADS // 2)]
        scores = []
        for h in range(NA_HEADS):
            ps = pairs[h // 2]
            q = jnp.where(first if h % 2 == 0 else ~first, q_all[:, ps], jnp.zeros((), BF16))
            scores.append((lax.dot_general(q, kwin[:, ps], _NT, preferred_element_type=F32) + bias_ref[v, h],
                           lax.dot_general(q, ckb_ref[:, ps], _NT, preferred_element_type=F32)))
        probs = [_softmax2(s_w, s_c) for s_w, s_c in scores]
        both = [jnp.dot(p_w.astype(BF16), vwin[:, pairs[h // 2]], preferred_element_type=F32)
                + jnp.dot(p_c.astype(BF16), cvb_ref[:, pairs[h // 2]], preferred_element_type=F32)
                for h, (p_w, p_c) in enumerate(probs)]
        outs = [jnp.where(first, both[2 * n], both[2 * n + 1]) for n in range(NA_HEADS // 2)]
        o_ref[0, pl.ds(pl.multiple_of(rr * GRID_W, GRID_W), GRID_W), :] = jnp.concatenate(outs, axis=-1)
        return carry

    lax.fori_loop(0, NA_KH, body, 0)


def _na_ctx_kernel(q_ref, k_ref, v_ref, o_ref):
    scale = NA_HD ** -0.5
    q_all = (q_ref[0] * scale).astype(BF16)
    k_all = k_ref[0].astype(BF16)
    v_all = v_ref[0].astype(BF16)
    outs = []
    for h in range(NA_HEADS):
        hs = slice(h * NA_HD, (h + 1) * NA_HD)
        s = lax.dot_general(q_all[:, hs], k_all[:, hs], _NT, preferred_element_type=F32)
        e = jnp.exp(s - jnp.max(s, axis=-1, keepdims=True))
        p = e * (1.0 / jnp.sum(e, axis=-1, keepdims=True))
        outs.append(jnp.dot(p.astype(BF16), v_all[:, hs], preferred_element_type=F32))
    o_ref[0] = jnp.concatenate(outs, axis=-1)


def na_branch(p_l, p_c, rpb, need_ctx_out):
    bsz, seq, _ = p_l.shape
    n_ctx = p_c.shape[1]
    rows = seq // GRID_W
    assert rows % NA_KH == 0 and rows >= 2 * NA_KH
    w = BRANCH_W
    blk = NA_KH * GRID_W
    nblk = rows // NA_KH
    cq = COL_NA // w

    def shifted(col, delta):
        return pl.BlockSpec((1, blk, w), lambda b, i: (b, jnp.clip(i + delta, 0, nblk - 1), col))

    y_l = pl.pallas_call(
        functools.partial(_na_kernel, rows=rows),
        grid=(bsz, nblk),
        in_specs=[shifted(cq, 0), shifted(cq + 1, -1), shifted(cq + 1, 0), shifted(cq + 1, 1),
                  shifted(cq + 2, -1), shifted(cq + 2, 0), shifted(cq + 2, 1),
                  pl.BlockSpec((1, n_ctx, w), lambda b, i: (b, 0, cq + 1)),
                  pl.BlockSpec((1, n_ctx, w), lambda b, i: (b, 0, cq + 2)),
                  pl.BlockSpec((NA_KH, NA_HEADS, GRID_W, blk), lambda b, i: (0, 0, 0, 0))],
        out_specs=pl.BlockSpec((1, blk, w), lambda b, i: (b, i, 0)),
        out_shape=jax.ShapeDtypeStruct((bsz, seq, w), F32),
        scratch_shapes=[pltpu.VMEM((3 * blk, w), BF16), pltpu.VMEM((3 * blk, w), BF16),
                        pltpu.VMEM((n_ctx, w), BF16), pltpu.VMEM((n_ctx, w), BF16)],
        compiler_params=pltpu.CompilerParams(
            dimension_semantics=("parallel", "arbitrary"), vmem_limit_bytes=V7X_VMEM_LIMIT),
        name="na_attn",
    )(p_l, p_l, p_l, p_l, p_l, p_l, p_l, p_c, p_c, _na_bias_slabs(rpb))
    y_c = None
    if need_ctx_out:
        y_c = pl.pallas_call(
            _na_ctx_kernel,
            grid=(bsz,),
            in_specs=[pl.BlockSpec((1, n_ctx, w), functools.partial(lambda b, c: (b, 0, c), c=cq + n)) for n in range(3)],
            out_specs=pl.BlockSpec((1, n_ctx, w), lambda b: (b, 0, 0)),
            out_shape=jax.ShapeDtypeStruct((bsz, n_ctx, w), F32),
            compiler_params=pltpu.CompilerParams(dimension_semantics=("parallel",)),
            name="na_ctx_attn",
        )(p_c, p_c, p_c)
    return y_l, y_c


def _gelu_tanh(x):
    return 0.5 * x * (1.0 + jnp.tanh(math.sqrt(2.0 / math.pi) * (x + 0.044715 * (x * x * x))))


def _silu(x):
    return x * _sigmoid(x)


def _softplus(x):
    return jnp.maximum(x, 0.0) + jnp.log1p(jnp.exp(-jnp.abs(x)))


def _rope_2d_tables(seq):
    t = jnp.arange(seq, dtype=jnp.int32)
    pos = jnp.stack([t // GRID_W, t % GRID_W], axis=-1).astype(F32)
    nf = ML_HD // 4
    inv_freq = ROPE_BASE ** (-jnp.arange(nf, dtype=F32) / nf)
    ang = jnp.broadcast_to(pos[:, :, None, None] * inv_freq, (seq, 2, 2, nf)).reshape(seq, ML_HD)
    return jnp.cos(ang), jnp.sin(ang)


def _chunk_specs(tt, nchunk, n8, reverse, nb=1):
    def pos(j):
        return nchunk - 1 - j if reverse else j

    def chunk(width, col):
        return pl.BlockSpec((nb, tt, width), lambda b, j: (b, pos(j), col))

    def halo(width, col, delta):
        if delta < 0:
            return pl.BlockSpec((nb, 8, width), lambda b, j: (b, jnp.maximum(pos(j) * (tt // 8) - 1, 0), col))
        return pl.BlockSpec((nb, 8, width), lambda b, j: (b, jnp.minimum((pos(j) + 1) * (tt // 8), n8 - 1), col))

    return chunk, halo


def _const_spec(shape):
    return pl.BlockSpec(shape, lambda b, j: (0,) * len(shape))


def _batch_spec(shape, nb=1):
    return pl.BlockSpec((nb,) + shape, lambda b, j: (b,) + (0,) * len(shape))


def _conv4(x, prev8, next8, has_prev, has_next, cw, cb):
    tt = x.shape[0]
    row = lax.broadcasted_iota(jnp.int32, x.shape, 0)
    p6 = prev8[6:7, :] * has_prev
    p7 = prev8[7:8, :] * has_prev
    n0 = next8[0:1, :] * has_next
    xm1 = jnp.where(row == 0, p7, pltpu.roll(x, 1, 0))
    xm2 = jnp.where(row == 0, p6, jnp.where(row == 1, p7, pltpu.roll(x, 2, 0)))
    xp1 = jnp.where(row == tt - 1, n0, pltpu.roll(x, tt - 1, 0))
    return cw[0:1, :] * xm2 + cw[1:2, :] * xm1 + cw[2:3, :] * x + cw[3:4, :] * xp1 + cb


def _cumsum_rows(x, reverse):
    n = x.shape[0]
    row = lax.broadcasted_iota(jnp.int32, x.shape, 0)
    s = 1
    while s < n:
        if reverse:
            x = x + jnp.where(row < n - s, pltpu.roll(x, n - s, 0), 0.0)
        else:
            x = x + jnp.where(row >= s, pltpu.roll(x, s, 0), 0.0)
        s *= 2
    return x


def _causal_mask(n, reverse):
    ii = lax.broadcasted_iota(jnp.int32, (n, n), 0)
    jj = lax.broadcasted_iota(jnp.int32, (n, n), 1)
    return (jj >= ii) if reverse else (jj <= ii)


MASK_NEG = -1e30


LRU_TT = 256


def _lru_kernel(*refs, reverse, finalize, nchunk):
    if finalize:
        (xp_ref, x_ref, xn_ref, g_ref, ho_ref, h0_ref, cw_ref, cb_ref, w_ref, bias_ref, sp_ref,
         o_ref, hl_ref, carry_ref) = refs
    else:
        (xp_ref, x_ref, xn_ref, h0_ref, cw_ref, cb_ref, w_ref, bias_ref, sp_ref, o_ref, hl_ref, carry_ref) = refs
    j = pl.program_id(1)
    c = (nchunk - 1 - j) if reverse else j

    @pl.when(j == 0)
    def _():
        carry_ref[...] = h0_ref[0]

    tt = x_ref.shape[1]
    w = x_ref.shape[2]
    xc = _conv4(x_ref[0], xp_ref[0], xn_ref[0], (c > 0).astype(F32), (c < nchunk - 1).astype(F32),
                cw_ref[...], cb_ref[...])
    g = jnp.dot(xc.astype(BF16), w_ref[...], preferred_element_type=F32)
    r = _sigmoid(g[:, :w] + bias_ref[0:1, :])
    ig = _sigmoid(g[:, w:] + bias_ref[1:2, :])
    log_a = -LRU_C * r * sp_ref[...]
    a = jnp.exp(log_a)
    u = jnp.sqrt(1.0 - jnp.exp(2.0 * log_a)) * (ig * xc)
    sub = lax.broadcasted_iota(jnp.int32, (tt, w), 0) % 8
    s = 1
    while s < 8:
        if reverse:
            keep = sub < 8 - s
            a_s = jnp.where(keep, pltpu.roll(a, tt - s, 0), 1.0)
            u_s = jnp.where(keep, pltpu.roll(u, tt - s, 0), 0.0)
        else:
            keep = sub >= s
            a_s = jnp.where(keep, pltpu.roll(a, s, 0), 1.0)
            u_s = jnp.where(keep, pltpu.roll(u, s, 0), 0.0)
        u = a * u_s + u
        a = a * a_s
        s *= 2
    new_carry = carry_ref[...]
    nblk = tt // 8
    hs = [None] * nblk
    for blk in (reversed(range(nblk)) if reverse else range(nblk)):
        hb = u[blk * 8:(blk + 1) * 8, :] + a[blk * 8:(blk + 1) * 8, :] * new_carry
        new_carry = hb[0:1, :] if reverse else hb[7:8, :]
        hs[blk] = hb
    h = jnp.concatenate(hs, axis=0)
    carry_ref[...] = new_carry
    hl_ref[0] = new_carry
    if finalize:
        o_ref[0] = (ho_ref[0] + h) * _gelu_tanh(g_ref[0])
    else:
        o_ref[0] = h


def _lru_pass(p, h0, other, cw, cb, wcat, bias, sp, reverse):
    bsz, seq, _ = p.shape
    w = BRANCH_W
    tt = min(LRU_TT, seq)
    nchunk = seq // tt
    chunk, halo = _chunk_specs(tt, nchunk, seq // 8, reverse)
    cx = COL_LRU_X // w
    finalize = other is not None
    in_specs = [halo(w, cx, -1), chunk(w, cx), halo(w, cx, 1)]
    args = [p, p, p]
    if finalize:
        in_specs += [chunk(w, COL_LRU_G // w), chunk(w, 0)]
        args += [p, other]
    in_specs += [_batch_spec((1, w)), _const_spec((CONV_W, w)), _const_spec((1, w)),
                 _const_spec((w, 2 * w)), _const_spec((2, w)), _const_spec((1, w))]
    args += [h0, cw, cb, wcat, bias, sp]
    return pl.pallas_call(
        functools.partial(_lru_kernel, reverse=reverse, finalize=finalize, nchunk=nchunk),
        grid=(bsz, nchunk),
        in_specs=in_specs,
        out_specs=[chunk(w, 0), _batch_spec((1, w))],
        out_shape=[jax.ShapeDtypeStruct((bsz, seq, w), F32), jax.ShapeDtypeStruct((bsz, 1, w), F32)],
        scratch_shapes=[pltpu.VMEM((1, w), F32)],
        compiler_params=pltpu.CompilerParams(dimension_semantics=("parallel", "arbitrary")),
        name="lru_scan",
    )(*args)


def _block_diag(wg):
    g, n, _ = wg.shape
    eye = jnp.eye(g, dtype=wg.dtype)
    return (wg[:, :, None, :] * eye[:, None, :, None]).reshape(g * n, g * n)


def lru_branch(p_l, p_c, conv_w, conv_b, wa, ba, wx, bx, lam, need_ctx_out):
    bsz = p_l.shape[0]
    w = BRANCH_W
    cb = conv_b.reshape(1, w)
    sp = jax.nn.softplus(-lam)
    zeros = jnp.zeros((bsz, 1, w), F32)
    h_c = h_l = None
    for d, reverse in ((0, False), (1, True)):
        wcat = jnp.concatenate([_block_diag(wa[d]), _block_diag(wx[d])], axis=1).astype(BF16)
        bias = jnp.stack([ba[d], bx[d]])
        other_c = h_c if (d == 1 and need_ctx_out) else None
        h_c, st = _lru_pass(p_c, zeros, other_c, conv_w, cb, wcat, bias, sp[d:d + 1], reverse)
        h_l, _ = _lru_pass(p_l, st, h_l if d == 1 else None, conv_w, cb, wcat, bias, sp[d:d + 1], reverse)
    return h_l, (h_c if need_ctx_out else None)


def _ssd_kernel(*refs, reverse, finalize, nchunk, d):
    (xp_ref, x_ref, xn_ref, bp_ref, bc_ref, bn_ref, dt_ref) = refs[:7]
    k = 7
    if finalize:
        z_ref, yp_ref = refs[k:k + 2]
        k += 2
    s0_ref, cwx_ref, cbx_ref, cwb_ref, cbb_ref, dtb_ref, a_ref = refs[k:k + 7]
    k += 7
    if finalize:
        dsk_ref, ng_ref = refs[k:k + 2]
        k += 2
    o_ref, so_ref, s_ref = refs[k:k + 3]
    j = pl.program_id(1)
    c = (nchunk - 1 - j) if reverse else j

    @pl.when(j == 0)
    def _():
        s_ref[...] = s0_ref[...]

    nb = x_ref.shape[0]
    q = x_ref.shape[1]
    has_prev = (c > 0).astype(F32)
    has_next = (c < nchunk - 1).astype(F32)
    mask = _causal_mask(q, reverse)
    first = lax.broadcasted_iota(jnp.int32, (q, 2 * SSD_HD), 1) < SSD_HD
    npair = SSD_HEADS // 2
    pair_group = [(2 * n) // (SSD_HEADS // SSD_GROUPS) for n in range(npair)]
    pairs = [slice(n * 2 * SSD_HD, (n + 1) * 2 * SSD_HD) for n in range(npair)]

    def per_lane(t, n):
        c0 = d * SSD_HEADS + 2 * n
        return jnp.where(first[:t.shape[0]], t[:, c0:c0 + 1], t[:, c0 + 1:c0 + 2])

    xs, xb, dt_t, cum, cum_t, w_end, ecum, etot, b_t, cgs, cbs = [], [], [], [], [], [], [], [], [], [], []
    for bi in range(nb):
        xs.append(_silu(_conv4(x_ref[bi], xp_ref[bi], xn_ref[bi], has_prev, has_next, cwx_ref[...], cbx_ref[...])))
        bc = _silu(_conv4(bc_ref[bi], bp_ref[bi], bn_ref[bi], has_prev, has_next, cwb_ref[...], cbb_ref[...]))
        dt = _softplus(dt_ref[bi] + dtb_ref[...])
        cum.append(_cumsum_rows(dt * a_ref[...], reverse))
        tot = cum[bi][0:1, :] if reverse else cum[bi][q - 1:q, :]
        w_end.append(jnp.exp(tot - cum[bi]) * dt)
        ecum.append(jnp.exp(cum[bi]))
        etot.append(jnp.exp(tot))
        dt_t.append(dt.T)
        cum_t.append(cum[bi].T)
        b_t.append(bc[:, :SSD_GN].T)
        xb.append(xs[bi].astype(BF16))
        cgs.append([bc[:, SSD_GN + g * SSD_STATE:SSD_GN + (g + 1) * SSD_STATE].astype(BF16)
                    for g in range(SSD_GROUPS)])
        cbs.append([lax.dot_general(cgs[bi][g], bc[:, g * SSD_STATE:(g + 1) * SSD_STATE].astype(BF16), _NT,
                                    preferred_element_type=F32) for g in range(SSD_GROUPS)])
    intra = []
    for bi in range(nb):
        for h in range(SSD_HEADS):
            col = d * SSD_HEADS + h
            diff = cum[bi][:, col:col + 1] - cum_t[bi][col:col + 1, :]
            m = (cbs[bi][pair_group[h // 2]] * jnp.exp(jnp.where(mask, diff, MASK_NEG))
                 * dt_t[bi][col:col + 1, :])
            intra.append(jnp.dot(m.astype(BF16), xb[bi][:, pairs[h // 2]], preferred_element_type=F32))
    items = [(bi, n) for bi in range(nb) for n in range(npair)]
    s_old = [s_ref[bi, n] for bi, n in items]
    inter = [jnp.dot(cgs[bi][pair_group[n]], s_old[t].astype(BF16), preferred_element_type=F32)
             for t, (bi, n) in enumerate(items)]
    outs = []
    for t, (bi, n) in enumerate(items):
        g = pair_group[n]
        h0 = bi * SSD_HEADS + 2 * n
        outs.append(jnp.where(first, intra[h0], intra[h0 + 1]) + per_lane(ecum[bi], n) * inter[t])
        xw = (xs[bi][:, pairs[n]] * per_lane(w_end[bi], n)).astype(BF16)
        s_ref[bi, n] = per_lane(etot[bi], n) * s_old[t] + jnp.dot(
            b_t[bi][g * SSD_STATE:(g + 1) * SSD_STATE, :].astype(BF16), xw, preferred_element_type=F32)
    so_ref[...] = s_ref[...]
    for bi in range(nb):
        y = jnp.concatenate(outs[bi * npair:(bi + 1) * npair], axis=-1)
        if finalize:
            yt = (xs[bi] * dsk_ref[...] + yp_ref[bi] + y) * _silu(z_ref[bi])
            o_ref[bi] = yt * lax.rsqrt(jnp.mean(yt * yt, axis=-1, keepdims=True) + LN_EPS) * ng_ref[...]
        else:
            o_ref[bi] = y


def _ssd_pass(p, s0, other, params, reverse, d):
    bsz, seq, _ = p.shape
    w = BRANCH_W
    q = SSD_CHUNK
    nchunk = seq // q
    chunk, halo = _chunk_specs(q, nchunk, seq // 8, reverse, nb=bsz)
    cx, cb2, cdt = COL_SSD_X // w, COL_SSD_B // (2 * SSD_GN), COL_SSD_DT // 128
    finalize = other is not None
    cwx, cbx, cwb, cbb, dtb, arow, dsk, ng = params
    in_specs = [halo(w, cx, -1), chunk(w, cx), halo(w, cx, 1),
                halo(2 * SSD_GN, cb2, -1), chunk(2 * SSD_GN, cb2), halo(2 * SSD_GN, cb2, 1), chunk(128, cdt)]
    args = [p] * 7
    if finalize:
        in_specs += [chunk(w, COL_SSD_Z // w), chunk(w, 0)]
        args += [p, other]
    st_shape = (SSD_HEADS // 2, SSD_STATE, 2 * SSD_HD)
    in_specs += [_batch_spec(st_shape, bsz), _const_spec((CONV_W, w)), _const_spec((1, w)),
                 _const_spec((CONV_W, 2 * SSD_GN)), _const_spec((1, 2 * SSD_GN)), _const_spec((1, 128)),
                 _const_spec((1, 128))]
    args += [s0, cwx, cbx, cwb, cbb, dtb, arow]
    if finalize:
        in_specs += [_const_spec((1, w)), _const_spec((1, w))]
        args += [dsk, ng]
    return pl.pallas_call(
        functools.partial(_ssd_kernel, reverse=reverse, finalize=finalize, nchunk=nchunk, d=d),
        grid=(1, nchunk),
        in_specs=in_specs,
        out_specs=[chunk(w, 0), _batch_spec(st_shape, bsz)],
        out_shape=[jax.ShapeDtypeStruct((bsz, seq, w), F32), jax.ShapeDtypeStruct((bsz,) + st_shape, F32)],
        scratch_shapes=[pltpu.VMEM((bsz,) + st_shape, F32)],
        compiler_params=pltpu.CompilerParams(dimension_semantics=("parallel", "arbitrary")),
        name="ssd_scan",
    )(*args)


def _lane_row(vals, start):
    return jnp.zeros((128,), F32).at[start:start + vals.shape[0]].set(vals.astype(F32)).reshape(1, 128)


def ssd_branch(p_l, p_c, conv_w, conv_b, dt_bias, a_log, d_skip, norm_g, need_ctx_out):
    bsz = p_l.shape[0]
    w = BRANCH_W
    params = (conv_w[:, :w], conv_b[:w].reshape(1, w), conv_w[:, w:], conv_b[w:].reshape(1, 2 * SSD_GN),
              _lane_row(dt_bias.reshape(-1), 0), _lane_row(-jnp.exp(a_log.astype(F32)).reshape(-1), 0),
              jnp.repeat(d_skip, SSD_HD).reshape(1, w), norm_g.reshape(1, w))
    zeros = jnp.zeros((bsz, SSD_HEADS // 2, SSD_STATE, 2 * SSD_HD), F32)
    y_c = y_l = None
    for d, reverse in ((0, False), (1, True)):
        other_c = y_c if (d == 1 and need_ctx_out) else None
        y_c, st = _ssd_pass(p_c, zeros, other_c, params, reverse, d)
        y_l, _ = _ssd_pass(p_l, st, y_l if d == 1 else None, params, reverse, d)
    return y_l, (y_c if need_ctx_out else None)


def _rope_rotate(x):
    wl = x.shape[-1]
    half = ML_HD // 4
    lane = lax.broadcasted_iota(jnp.int32, x.shape, 1)
    return jnp.where(lane % (2 * half) < half, -pltpu.roll(x, wl - half, 1), pltpu.roll(x, half, 1))


def _log_sigmoid(x):
    return jnp.minimum(x, 0.0) - jnp.log1p(jnp.exp(-jnp.abs(x)))


def _mlstm_kernel(*refs, reverse, finalize, rope, nchunk, d):
    (qp_ref, q_ref, qn_ref, kp_ref, k_ref, kn_ref, v_ref, g_ref) = refs[:8]
    n = 8
    if rope:
        cos_ref, sin_ref = refs[n:n + 2]
        n += 2
    if finalize:
        og_ref, hp_ref = refs[n:n + 2]
        n += 2
    c0_ref, n0_ref, m0_ref, cwq_ref, cbq_ref, cwk_ref, cbk_ref, ib_ref, fb_ref = refs[n:n + 9]
    n += 9
    o_ref, co_ref, no_ref, mo_ref, c_ref, n_ref, m_ref = refs[n:n + 7]
    j = pl.program_id(1)
    c = (nchunk - 1 - j) if reverse else j

    @pl.when(j == 0)
    def _():
        c_ref[...] = c0_ref[...]
        n_ref[...] = n0_ref[...]
        m_ref[...] = m0_ref[...]

    nb = q_ref.shape[0]
    qn = q_ref.shape[1]
    has_prev = (c > 0).astype(F32)
    has_next = (c < nchunk - 1).astype(F32)
    mask = _causal_mask(qn, reverse)
    heads = [slice(h * ML_HD, (h + 1) * ML_HD) for h in range(ML_HEADS)]
    items = [(bi, h) for bi in range(nb) for h in range(ML_HEADS)]
    q, k, v, b, li, tot, b_t, li_t = [], [], [], [], [], [], [], []
    for bi in range(nb):
        qi = _silu(_conv4(q_ref[bi], qp_ref[bi], qn_ref[bi], has_prev, has_next, cwq_ref[...], cbq_ref[...]))
        ki = _silu(_conv4(k_ref[bi], kp_ref[bi], kn_ref[bi], has_prev, has_next, cwk_ref[...], cbk_ref[...]))
        if rope:
            qi = qi * cos_ref[...] + _rope_rotate(qi) * sin_ref[...]
            ki = ki * cos_ref[...] + _rope_rotate(ki) * sin_ref[...]
        q.append(qi * (ML_HD ** -0.5))
        k.append(ki)
        v.append(v_ref[bi])
        gb = g_ref[bi]
        li.append(gb + ib_ref[...])
        b.append(_cumsum_rows(_log_sigmoid(gb + fb_ref[...]), reverse))
        tot.append(b[bi][0:1, :] if reverse else b[bi][qn - 1:qn, :])
        b_t.append(b[bi].T)
        li_t.append(li[bi].T)
    qb = [t.astype(BF16) for t in q]
    kb = [t.astype(BF16) for t in k]
    vb = [t.astype(BF16) for t in v]
    qk = [lax.dot_general(qb[bi][:, heads[h]], kb[bi][:, heads[h]], _NT, preferred_element_type=F32)
          for bi, h in items]
    c_old = [c_ref[bi, h] for bi, h in items]
    n_old = [n_ref[bi, h:h + 1, :] for bi, h in items]
    qc = [lax.dot_general(qb[bi][:, heads[h]], c_old[t].astype(BF16), _NT, preferred_element_type=F32)
          for t, (bi, h) in enumerate(items)]
    gate = []
    for bi, h in items:
        ci = 4 * ML_HEADS + d * 2 * ML_HEADS + h
        cf = ci + ML_HEADS
        b_c = b[bi][:, cf:cf + 1]
        b_end = tot[bi][:, cf:cf + 1]
        m_st = m_ref[bi, h:h + 1, 0:1]
        end_log = b_end - b_c + li[bi][:, ci:ci + 1]
        m_new = jnp.maximum(b_end + m_st, jnp.max(end_log, axis=0, keepdims=True))
        dlog = jnp.where(mask, b_c - b_t[bi][cf:cf + 1, :] + li_t[bi][ci:ci + 1, :], MASK_NEG)
        m_inter = b_c + m_st
        m_i = jnp.maximum(jnp.max(dlog, axis=1, keepdims=True), m_inter)
        gate.append((jnp.exp(end_log - m_new), jnp.exp(b_end + m_st - m_new), m_new,
                     jnp.exp(dlog - m_i), jnp.exp(m_inter - m_i), jnp.exp(-m_i)))
    s_all = [qk[t] * gate[t][3] for t in range(len(items))]
    sv = [jnp.dot(s_all[t].astype(BF16), vb[bi][:, heads[h]], preferred_element_type=F32)
          for t, (bi, h) in enumerate(items)]
    upd = [jnp.dot((v[bi][:, heads[h]] * gate[t][0]).T.astype(BF16), kb[bi][:, heads[h]],
                   preferred_element_type=F32) for t, (bi, h) in enumerate(items)]
    outs = []
    for t, (bi, h) in enumerate(items):
        hs = heads[h]
        w, carry_scale, m_new, _, w_in, floor = gate[t]
        num = sv[t] + w_in * qc[t]
        den = (jnp.sum(s_all[t], axis=1, keepdims=True)
               + w_in * jnp.sum(q[bi][:, hs] * n_old[t], axis=1, keepdims=True))
        outs.append(num / jnp.maximum(jnp.abs(den), floor))
        c_ref[bi, h] = carry_scale * c_old[t] + upd[t]
        n_ref[bi, h:h + 1, :] = carry_scale * n_old[t] + jnp.sum(k[bi][:, hs] * w, axis=0, keepdims=True)
        m_ref[bi, h:h + 1, :] = jnp.broadcast_to(m_new, (1, ML_HD))
    co_ref[...] = c_ref[...]
    no_ref[...] = n_ref[...]
    mo_ref[...] = m_ref[...]
    for bi in range(nb):
        hout = jnp.concatenate(outs[bi * ML_HEADS:(bi + 1) * ML_HEADS], axis=-1)
        if finalize:
            o_ref[bi] = _sigmoid(og_ref[bi]) * (hp_ref[bi] + hout)
        else:
            o_ref[bi] = hout


def _mlstm_pass(p, state, other, tables, params, reverse, d):
    bsz, seq, _ = p.shape
    w = BRANCH_W
    qn = ML_CHUNK
    nchunk = seq // qn
    chunk, halo = _chunk_specs(qn, nchunk, seq // 8, reverse, nb=bsz)
    cq = COL_ML // w
    finalize = other is not None
    rope = tables is not None
    in_specs = [halo(w, cq, -1), chunk(w, cq), halo(w, cq, 1), halo(w, cq + 1, -1), chunk(w, cq + 1),
                halo(w, cq + 1, 1), chunk(w, cq + 2), chunk(128, COL_ML_G // 128)]
    args = [p] * 8
    if rope:
        tab = pl.BlockSpec((qn, w), (lambda b, j: (nchunk - 1 - j, 0)) if reverse else (lambda b, j: (j, 0)))
        in_specs += [tab, tab]
        args += list(tables)
    if finalize:
        in_specs += [chunk(w, cq + 3), chunk(w, 0)]
        args += [p, other]
    st_shapes = [(ML_HEADS, ML_HD, ML_HD), (ML_HEADS, ML_HD), (ML_HEADS, ML_HD)]
    in_specs += [_batch_spec(s, bsz) for s in st_shapes]
    in_specs += [_const_spec((CONV_W, w)), _const_spec((1, w)), _const_spec((CONV_W, w)), _const_spec((1, w)),
                 _const_spec((1, 128)), _const_spec((1, 128))]
    args += list(state) + list(params)
    res = pl.pallas_call(
        functools.partial(_mlstm_kernel, reverse=reverse, finalize=finalize, rope=rope, nchunk=nchunk, d=d),
        grid=(1, nchunk),
        in_specs=in_specs,
        out_specs=[chunk(w, 0)] + [_batch_spec(s, bsz) for s in st_shapes],
        out_shape=[jax.ShapeDtypeStruct((bsz, seq, w), F32)]
                  + [jax.ShapeDtypeStruct((bsz,) + s, F32) for s in st_shapes],
        scratch_shapes=[pltpu.VMEM((bsz,) + s, F32) for s in st_shapes],
        compiler_params=pltpu.CompilerParams(dimension_semantics=("parallel", "arbitrary")),
        name="mlstm_scan",
    )(*args)
    return res[0], tuple(res[1:])


def mlstm_branch(p_l, p_c, conv_w, conv_b, i_bias, f_bias, need_ctx_out):
    bsz, seq, _ = p_l.shape
    w = BRANCH_W
    cos, sin = _rope_2d_tables(seq)
    tables = (jnp.tile(cos, (1, ML_HEADS)), jnp.tile(sin, (1, ML_HEADS)))
    zero_h = jnp.zeros_like(i_bias)
    ib = _lane_row(jnp.concatenate([i_bias, zero_h], axis=1).reshape(-1), 4 * ML_HEADS)
    fb = _lane_row(jnp.concatenate([zero_h, f_bias], axis=1).reshape(-1), 4 * ML_HEADS)
    params = (conv_w[:, :w], conv_b[:w].reshape(1, w), conv_w[:, w:], conv_b[w:].reshape(1, w), ib, fb)
    state0 = (jnp.zeros((bsz, ML_HEADS, ML_HD, ML_HD), F32), jnp.zeros((bsz, ML_HEADS, ML_HD), F32),
              jnp.zeros((bsz, ML_HEADS, ML_HD), F32))
    h_c = h_l = None
    for d, reverse in ((0, False), (1, True)):
        other_c = h_c if (d == 1 and need_ctx_out) else None
        h_c, st = _mlstm_pass(p_c, state0, other_c, None, params, reverse, d)
        h_l, _ = _mlstm_pass(p_l, st, h_l if d == 1 else None, tables, params, reverse, d)
    return h_l, (h_c if need_ctx_out else None)


MOE_TM = 512
ROUTE_LANES = 128


def _moe_router_kernel(x_ref, sc_ref, sh_ref, wr_ref, br_ref, hb_ref, rt_ref):
    h = x_ref[0] * (1.0 + sc_ref[0]) + sh_ref[0]
    hb_ref[0] = h.astype(BF16)
    logits = jnp.dot(h, wr_ref[...], precision=lax.Precision.HIGHEST, preferred_element_type=F32) + br_ref[...]
    lane = lax.broadcasted_iota(jnp.int32, logits.shape, 1)
    m1 = jnp.max(logits, axis=-1, keepdims=True)
    i1 = jnp.min(jnp.where(logits == m1, lane, ROUTE_LANES), axis=-1, keepdims=True)
    rest = jnp.where(lane == i1, MASK_NEG, logits)
    m2 = jnp.max(rest, axis=-1, keepdims=True)
    i2 = jnp.min(jnp.where(rest == m2, lane, ROUTE_LANES), axis=-1, keepdims=True)
    e2 = jnp.exp(m2 - m1)
    p1 = 1.0 / (1.0 + e2)
    p2 = e2 * p1
    rt_ref[0] = jnp.where(lane == 0, i1.astype(F32), jnp.where(lane == 1, i2.astype(F32),
                          jnp.where(lane == 2, p1, jnp.where(lane == 3, p2, 0.0))))


def moe_router(x, sc, sh, w_router, b_router, tm=512):
    bsz, seq, d = x.shape
    n_e = w_router.shape[1]
    tm = min(tm, seq)
    wr = jnp.zeros((d, ROUTE_LANES), F32).at[:, :n_e].set(w_router)
    br = jnp.full((1, ROUTE_LANES), MASK_NEG, F32).at[0, :n_e].set(b_router)
    return pl.pallas_call(
        _moe_router_kernel,
        grid=(bsz, seq // tm),
        in_specs=[pl.BlockSpec((1, tm, d), lambda b, i: (b, i, 0)),
                  pl.BlockSpec((1, 1, d), lambda b, i: (b, 0, 0)),
                  pl.BlockSpec((1, 1, d), lambda b, i: (b, 0, 0)),
                  pl.BlockSpec((d, ROUTE_LANES), lambda b, i: (0, 0)),
                  pl.BlockSpec((1, ROUTE_LANES), lambda b, i: (0, 0))],
        out_specs=[pl.BlockSpec((1, tm, d), lambda b, i: (b, i, 0)),
                   pl.BlockSpec((1, tm, ROUTE_LANES), lambda b, i: (b, i, 0))],
        out_shape=[jax.ShapeDtypeStruct((bsz, seq, d), BF16), jax.ShapeDtypeStruct((bsz, seq, ROUTE_LANES), F32)],
        compiler_params=pltpu.CompilerParams(dimension_semantics=("parallel", "parallel")),
        name="moe_router",
    )(x, sc, sh, wr, br)


def _route_tables(idx, n_e, tmg):
    n_tok = idx.shape[0]
    e_flat = idx.reshape(-1)
    onehot = (e_flat[:, None] == jnp.arange(n_e, dtype=jnp.int32)[None, :]).astype(jnp.int32)
    csum = jnp.cumsum(onehot, axis=0)
    rank = jnp.take_along_axis(csum - onehot, e_flat[:, None], axis=1)[:, 0]
    padded = ((csum[-1] + tmg - 1) // tmg) * tmg
    ends = jnp.cumsum(padded)
    pos = (ends - padded)[e_flat] + rank
    n_rows = TOP_K * n_tok + n_e * tmg
    ntiles = n_rows // tmg
    tile_start = jnp.arange(ntiles, dtype=jnp.int32) * tmg
    tile_expert = jnp.minimum(jnp.sum(tile_start[:, None] >= ends[None, :], axis=1), n_e - 1)
    src = (jnp.arange(n_rows, dtype=jnp.int32) % n_tok).at[pos].set(
        jnp.arange(TOP_K * n_tok, dtype=jnp.int32) // TOP_K)
    meta = jnp.concatenate([tile_expert, ends[-1:] // tmg]).astype(jnp.int32)
    return pos.reshape(n_tok, TOP_K), src, meta


def _expert_changed(meta_ref, i):
    return (i == 0) | (meta_ref[i] != meta_ref[jnp.maximum(i - 1, 0)])


def _moe_up_kernel(meta_ref, x_ref, wg_ref, wu_ref, a_ref, wgb_ref, wub_ref, *, ntiles):
    i = pl.program_id(1)

    @pl.when(_expert_changed(meta_ref, i))
    def _():
        wgb_ref[...] = wg_ref[0].astype(BF16)
        wub_ref[...] = wu_ref[0].astype(BF16)

    @pl.when(i < meta_ref[ntiles])
    def _():
        x = x_ref[...]
        g = jnp.dot(x, wgb_ref[...], preferred_element_type=F32)
        u = jnp.dot(x, wub_ref[...], preferred_element_type=F32)
        a_ref[...] = (_silu(g) * u).astype(BF16)

    @pl.when(i >= meta_ref[ntiles])
    def _():
        a_ref[...] = jnp.zeros_like(a_ref)


def _moe_down_kernel(meta_ref, a_ref, wd_ref, y_ref, wdb_ref, *, ntiles):
    i = pl.program_id(0)

    @pl.when(_expert_changed(meta_ref, i))
    def _():
        wdb_ref[...] = wd_ref[0].astype(BF16)

    @pl.when(i < meta_ref[ntiles])
    def _():
        y_ref[...] = jnp.dot(a_ref[...], wdb_ref[...], preferred_element_type=F32).astype(y_ref.dtype)

    @pl.when(i >= meta_ref[ntiles])
    def _():
        y_ref[...] = jnp.zeros_like(y_ref)


def moe_experts(xs, meta, wg, wu, wd, tmg, tf):
    n_rows, d = xs.shape
    n_e, _, f = wg.shape
    ntiles = n_rows // tmg
    nf = f // tf
    once = pl.Buffered(1)
    a = pl.pallas_call(
        functools.partial(_moe_up_kernel, ntiles=ntiles),
        grid_spec=pltpu.PrefetchScalarGridSpec(
            num_scalar_prefetch=1,
            grid=(nf, ntiles),
            in_specs=[pl.BlockSpec((tmg, d), lambda j, i, m: (i, 0)),
                      pl.BlockSpec((1, d, tf), lambda j, i, m: (m[i], 0, j), pipeline_mode=once),
                      pl.BlockSpec((1, d, tf), lambda j, i, m: (m[i], 0, j), pipeline_mode=once)],
            out_specs=pl.BlockSpec((tmg, tf), lambda j, i, m: (i, j)),
            scratch_shapes=[pltpu.VMEM((d, tf), BF16), pltpu.VMEM((d, tf), BF16)]),
        out_shape=jax.ShapeDtypeStruct((n_rows, f), BF16),
        compiler_params=pltpu.CompilerParams(
            dimension_semantics=("arbitrary", "arbitrary"), vmem_limit_bytes=V7X_VMEM_LIMIT),
        name="moe_up",
    )(meta, xs, wg, wu)
    return pl.pallas_call(
        functools.partial(_moe_down_kernel, ntiles=ntiles),
        grid_spec=pltpu.PrefetchScalarGridSpec(
            num_scalar_prefetch=1,
            grid=(ntiles,),
            in_specs=[pl.BlockSpec((tmg, f), lambda i, m: (i, 0)),
                      pl.BlockSpec((1, f, d), lambda i, m: (m[i], 0, 0), pipeline_mode=once)],
            out_specs=pl.BlockSpec((tmg, d), lambda i, m: (i, 0)),
            scratch_shapes=[pltpu.VMEM((f, d), BF16)]),
        out_shape=jax.ShapeDtypeStruct((n_rows, d), BF16),
        compiler_params=pltpu.CompilerParams(
            dimension_semantics=("arbitrary",), vmem_limit_bytes=V7X_VMEM_LIMIT),
        name="moe_down",
    )(meta, a, wd)


def _combine_ln_kernel(x_ref, gt_ref, rt_ref, ya_ref, yb_ref, g_ref, b_ref, o_ref, *, alpha):
    rt = rt_ref[0]
    y = rt[:, 2:3] * ya_ref[0].astype(F32) + rt[:, 3:4] * yb_ref[0].astype(F32)
    o_ref[0] = _layer_norm_rows(alpha * x_ref[0] + gt_ref[0] * y, g_ref[...], b_ref[...])


def combine_residual_ln(x, gate, route, ya, yb, ln_g, ln_b, alpha, tm=512):
    bsz, seq, d = x.shape
    tm = min(tm, seq)
    row = pl.BlockSpec((1, tm, d), lambda b, i: (b, i, 0))
    return pl.pallas_call(
        functools.partial(_combine_ln_kernel, alpha=alpha),
        grid=(bsz, seq // tm),
        in_specs=[row, pl.BlockSpec((1, 1, d), lambda b, i: (b, 0, 0)),
                  pl.BlockSpec((1, tm, ROUTE_LANES), lambda b, i: (b, i, 0)), row, row,
                  pl.BlockSpec((1, d), lambda b, i: (0, 0)), pl.BlockSpec((1, d), lambda b, i: (0, 0))],
        out_specs=row,
        out_shape=jax.ShapeDtypeStruct((bsz, seq, d), F32),
        compiler_params=pltpu.CompilerParams(
            dimension_semantics=("parallel", "parallel"), vmem_limit_bytes=V7X_VMEM_LIMIT),
        name="moe_combine_ln",
    )(x, gate, route, ya, yb, ln_g.reshape(1, d), ln_b.reshape(1, d))


def moe_residual_ln(x, sc, sh, gate, w_router, b_router, wg, wu, wd, ln_g, ln_b, alpha):
    bsz, seq, d = x.shape
    n_e, _, f = wg.shape
    n_tok = bsz * seq
    tmg = min(MOE_TM, TOP_K * n_tok)
    hb, route = moe_router(x, sc, sh, w_router, b_router)
    idx = route[..., :TOP_K].astype(jnp.int32).reshape(n_tok, TOP_K)
    pos, src, meta = _route_tables(idx, n_e, tmg)
    xs = hb.reshape(n_tok, d).at[src].get(mode="promise_in_bounds")
    ys = moe_experts(xs, meta, wg, wu, wd, tmg, f // 2)
    ya = ys.at[pos[:, 0]].get(mode="promise_in_bounds").reshape(bsz, seq, d)
    yb = ys.at[pos[:, 1]].get(mode="promise_in_bounds").reshape(bsz, seq, d)
    return combine_residual_ln(x, gate, route, ya, yb, ln_g, ln_b, alpha)


def _mixers(p_l, p_c, need_ctx_out, lru_conv_w, lru_conv_b, lru_wa, lru_ba, lru_wx, lru_bx, lru_lambda,
            na_rpb, ssd_conv_w, ssd_conv_b, ssd_dt_bias, ssd_a_log, ssd_d, ssd_norm_g, ml_conv_w, ml_conv_b,
            ml_i_bias, ml_f_bias):
    ya = lru_branch(p_l, p_c, lru_conv_w, lru_conv_b, lru_wa, lru_ba, lru_wx, lru_bx, lru_lambda, need_ctx_out)
    yb = na_branch(p_l, p_c, na_rpb, need_ctx_out)
    yc = ssd_branch(p_l, p_c, ssd_conv_w, ssd_conv_b, ssd_dt_bias, ssd_a_log, ssd_d, ssd_norm_g, need_ctx_out)
    yd = mlstm_branch(p_l, p_c, ml_conv_w, ml_conv_b, ml_i_bias, ml_f_bias, need_ctx_out)
    return (ya[0], yb[0], yc[0], yd[0]), (ya[1], yb[1], yc[1], yd[1])


def kernel(x, c, ctx, c_ctx, w_ada, b_ada, w_in, lru_conv_w, lru_conv_b, lru_wa, lru_ba, lru_wx, lru_bx, lru_lambda, na_rpb, ssd_conv_w, ssd_conv_b, ssd_dt_bias, ssd_a_log, ssd_d, ssd_norm_g, ml_conv_w, ml_conv_b, ml_i_bias, ml_f_bias, w_branch, w_out, ln_g, ln_b, ffn_w_gate, ffn_w_up, ffn_w_down, moe_w_router, moe_b_router, moe_w_gate, moe_w_up, moe_w_down):
    depth = w_in.shape[0]
    bsz, seq, d = x.shape
    alpha = (2.0 * depth) ** 0.25
    cvecs = jnp.zeros((8, d), F32).at[:bsz].set(c).at[bsz].set(c_ctx)
    mods = ada_modulation(cvecs, w_ada, b_ada)
    xl, xc = x, ctx
    for l in range(depth):
        need_ctx_out = l < depth - 1
        mod = mods[l]
        mod_l = jnp.split(mod[:bsz, None, :], 6, axis=-1)
        mod_c = jnp.split(jnp.broadcast_to(mod[bsz:bsz + 1, None, :], (bsz, 1, 6 * d)), 6, axis=-1)
        w_mix, w_gates = _pack_w_in(w_in[l])
        p_l, hb_l = mod_matmul(xl, mod_l[1], mod_l[0], w_mix)
        p_c, hb_c = mod_matmul(xc, mod_c[1], mod_c[0], w_mix)
        br_l, br_c = _mixers(p_l, p_c, need_ctx_out, lru_conv_w[l], lru_conv_b[l], lru_wa[l], lru_ba[l],
                             lru_wx[l], lru_bx[l], lru_lambda[l], na_rpb[l], ssd_conv_w[l], ssd_conv_b[l],
                             ssd_dt_bias[l], ssd_a_log[l], ssd_d[l], ssd_norm_g[l], ml_conv_w[l], ml_conv_b[l],
                             ml_i_bias[l], ml_f_bias[l])
        wb = w_branch[l].astype(BF16)
        wo = w_out[l].astype(BF16)
        xl = proj_residual_ln(merge_branches(hb_l, w_gates, br_l, wb), wo, xl, mod_l[2], ln_g[l, 0], ln_b[l, 0],
                              alpha)
        if need_ctx_out:
            xc = proj_residual_ln(merge_branches(hb_c, w_gates, br_c, wb), wo, xc, mod_c[2], ln_g[l, 0],
                                  ln_b[l, 0], alpha)
        j = l // 2
        if l % 2 == 0:
            wg = ffn_w_gate[j].astype(BF16)
            wu = ffn_w_up[j].astype(BF16)
            wd = ffn_w_down[j].astype(BF16)

            def ffn(h, m, wg=wg, wu=wu, wd=wd):
                return ffn_residual_ln(h, m[4], m[3], m[5], wg, wu, wd, ln_g[l, 1], ln_b[l, 1], alpha)
        else:
            def ffn(h, m, j=j):
                return moe_residual_ln(h, m[4], m[3], m[5], moe_w_router[j], moe_b_router[j], moe_w_gate[j],
                                       moe_w_up[j], moe_w_down[j], ln_g[l, 1], ln_b[l, 1], alpha)

        xl = ffn(xl, mod_l)
        if need_ctx_out:
            xc = ffn(xc, mod_c)
    return xl
```

```python
import functools
import math

import numpy as np
import jax
import jax.numpy as jnp
from jax import lax
from jax.experimental import pallas as pl
from jax.experimental.pallas import tpu as pltpu

F32 = jnp.float32
BF16 = jnp.bfloat16

D_MODEL = 2048
GRID_W = 64
N_BRANCH = 4
BRANCH_W = D_MODEL // N_BRANCH
CONV_W = 4
LN_EPS = 1e-5
LRU_BLOCKS = 8
LRU_BW = BRANCH_W // LRU_BLOCKS
LRU_C = 8.0
NA_HEADS = 8
NA_HD = BRANCH_W // NA_HEADS
NA_KH = 8
NA_KW = 16
SSD_HEADS = 8
SSD_HD = BRANCH_W // SSD_HEADS
SSD_GROUPS = 2
SSD_STATE = 64
SSD_CHUNK = 128
SSD_GN = SSD_GROUPS * SSD_STATE
ML_HEADS = 4
ML_HD = BRANCH_W // ML_HEADS
ML_CHUNK = 128
ROPE_BASE = 10000.0
N_EXPERTS = 8
TOP_K = 2

V7X_VMEM_LIMIT = 52 * 1024 * 1024

COL_LRU_X = 0
COL_LRU_G = 512
COL_NA = 1024
COL_SSD_Z = 2560
COL_SSD_X = 3072
COL_ML = 3584
COL_SSD_B = 5632
COL_SSD_C = 5760
COL_SSD_DT = 5888
COL_ML_G = 5904
N_MIX = 5920
N_MIX_PAD = 6144


def _pack_w_in(w):
    parts = [w[:, 0:1024], w[:, 1024:2560], w[:, 2560:3584], w[:, 3856:5904],
             w[:, 3584:3840], w[:, 3840:3856], w[:, 5904:5920],
             jnp.zeros((w.shape[0], N_MIX_PAD - N_MIX), w.dtype)]
    return jnp.concatenate(parts, axis=1).astype(BF16), w[:, N_MIX:].astype(BF16)


def _sigmoid(x):
    return 1.0 / (1.0 + jnp.exp(-x))


def _layer_norm_rows(z, g, b):
    mu = jnp.mean(z, axis=-1, keepdims=True)
    zc = z - mu
    var = jnp.mean(zc * zc, axis=-1, keepdims=True)
    return zc * lax.rsqrt(var + LN_EPS) * g + b


def _mod_matmul_kernel(x_ref, sc_ref, sh_ref, w_ref, o_ref, hb_ref):
    @pl.when(pl.program_id(2) == 0)
    def _():
        hb_ref[0] = (x_ref[0] * (1.0 + sc_ref[0]) + sh_ref[0]).astype(BF16)

    o_ref[0] = jnp.dot(hb_ref[0], w_ref[...], preferred_element_type=F32)


def mod_matmul(x, sc, sh, w, tm=1024, tn=1024):
    bsz, seq, d = x.shape
    n = w.shape[1]
    tm = min(tm, seq)
    return pl.pallas_call(
        _mod_matmul_kernel,
        grid=(bsz, seq // tm, n // tn),
        in_specs=[pl.BlockSpec((1, tm, d), lambda b, i, j: (b, i, 0)),
                  pl.BlockSpec((1, 1, d), lambda b, i, j: (b, 0, 0)),
                  pl.BlockSpec((1, 1, d), lambda b, i, j: (b, 0, 0)),
                  pl.BlockSpec((d, tn), lambda b, i, j: (0, j))],
        out_specs=[pl.BlockSpec((1, tm, tn), lambda b, i, j: (b, i, j)),
                   pl.BlockSpec((1, tm, d), lambda b, i, j: (b, i, 0))],
        out_shape=[jax.ShapeDtypeStruct((bsz, seq, n), F32), jax.ShapeDtypeStruct((bsz, seq, d), BF16)],
        compiler_params=pltpu.CompilerParams(
            dimension_semantics=("parallel", "parallel", "arbitrary"), vmem_limit_bytes=V7X_VMEM_LIMIT),
        name="in_proj",
    )(x, sc, sh, w)


def _merge_kernel(hb_ref, g0, g1, g2, g3, ya, yb, yc, yd, wb_ref, o_ref):
    hb = hb_ref[0]
    acc = None
    for n, (wg, y) in enumerate(((g0, ya), (g1, yb), (g2, yc), (g3, yd))):
        gate = _sigmoid(jnp.dot(hb, wg[...], preferred_element_type=F32))
        t = gate * jnp.dot(y[0].astype(BF16), wb_ref[n], preferred_element_type=F32)
        acc = t if acc is None else acc + t
    o_ref[0] = acc.astype(BF16)


def merge_branches(hb, w_gates, branches, wb, tm=512, tn=512):
    bsz, seq, dk = hb.shape
    d = wb.shape[2]
    tm = min(tm, seq)
    nj = d // tn
    g_specs = [pl.BlockSpec((dk, tn), functools.partial(lambda b, i, j, n: (0, n * nj + j), n=n))
               for n in range(N_BRANCH)]
    y_specs = [pl.BlockSpec((1, tm, BRANCH_W), lambda b, i, j: (b, i, 0)) for _ in range(N_BRANCH)]
    return pl.pallas_call(
        _merge_kernel,
        grid=(bsz, seq // tm, nj),
        in_specs=[pl.BlockSpec((1, tm, dk), lambda b, i, j: (b, i, 0))] + g_specs + y_specs
                 + [pl.BlockSpec((N_BRANCH, BRANCH_W, tn), lambda b, i, j: (0, 0, j))],
        out_specs=pl.BlockSpec((1, tm, tn), lambda b, i, j: (b, i, j)),
        out_shape=jax.ShapeDtypeStruct((bsz, seq, d), BF16),
        compiler_params=pltpu.CompilerParams(
            dimension_semantics=("parallel", "parallel", "arbitrary"), vmem_limit_bytes=V7X_VMEM_LIMIT),
        name="merge",
    )(hb, w_gates, w_gates, w_gates, w_gates, *branches, wb)


def _proj_ln_kernel(m_ref, w_ref, x_ref, gt_ref, g_ref, b_ref, o_ref, *, alpha):
    y = jnp.dot(m_ref[0], w_ref[...], preferred_element_type=F32)
    o_ref[0] = _layer_norm_rows(alpha * x_ref[0] + gt_ref[0] * y, g_ref[...], b_ref[...])


def proj_residual_ln(m, w, x, gate, ln_g, ln_b, alpha, tm=512):
    bsz, seq, k = m.shape
    d = w.shape[1]
    tm = min(tm, seq)
    return pl.pallas_call(
        functools.partial(_proj_ln_kernel, alpha=alpha),
        grid=(bsz, seq // tm),
        in_specs=[pl.BlockSpec((1, tm, k), lambda b, i: (b, i, 0)),
                  pl.BlockSpec((k, d), lambda b, i: (0, 0)),
                  pl.BlockSpec((1, tm, d), lambda b, i: (b, i, 0)),
                  pl.BlockSpec((1, 1, d), lambda b, i: (b, 0, 0)),
                  pl.BlockSpec((1, d), lambda b, i: (0, 0)),
                  pl.BlockSpec((1, d), lambda b, i: (0, 0))],
        out_specs=pl.BlockSpec((1, tm, d), lambda b, i: (b, i, 0)),
        out_shape=jax.ShapeDtypeStruct((bsz, seq, d), F32),
        compiler_params=pltpu.CompilerParams(
            dimension_semantics=("parallel", "parallel"), vmem_limit_bytes=V7X_VMEM_LIMIT),
        name="out_proj_ln",
    )(m, w, x, gate, ln_g.reshape(1, d), ln_b.reshape(1, d))


def _ffn_kernel(x_ref, sc_ref, sh_ref, gt_ref, wg_ref, wu_ref, wd_ref, lg_ref, lb_ref, o_ref, hb_ref, acc_ref, *, alpha):
    j = pl.program_id(2)

    @pl.when(j == 0)
    def _():
        hb_ref[...] = (x_ref[0] * (1.0 + sc_ref[0]) + sh_ref[0]).astype(BF16)
        acc_ref[...] = jnp.zeros_like(acc_ref)

    hb = hb_ref[...]
    g = jnp.dot(hb, wg_ref[...], preferred_element_type=F32)
    u = jnp.dot(hb, wu_ref[...], preferred_element_type=F32)
    acc_ref[...] += jnp.dot((_silu(g) * u).astype(BF16), wd_ref[...], preferred_element_type=F32)

    @pl.when(j == pl.num_programs(2) - 1)
    def _():
        o_ref[0] = _layer_norm_rows(alpha * x_ref[0] + gt_ref[0] * acc_ref[...], lg_ref[...], lb_ref[...])


def ffn_residual_ln(x, sc, sh, gate, wg, wu, wd, ln_g, ln_b, alpha, tm=512, tf=512):
    bsz, seq, d = x.shape
    f = wg.shape[1]
    tm = min(tm, seq)
    vec = pl.BlockSpec((1, 1, d), lambda b, i, j: (b, 0, 0))
    par = pl.BlockSpec((1, d), lambda b, i, j: (0, 0))
    return pl.pallas_call(
        functools.partial(_ffn_kernel, alpha=alpha),
        grid=(bsz, seq // tm, f // tf),
        in_specs=[pl.BlockSpec((1, tm, d), lambda b, i, j: (b, i, 0)), vec, vec, vec,
                  pl.BlockSpec((d, tf), lambda b, i, j: (0, j)),
                  pl.BlockSpec((d, tf), lambda b, i, j: (0, j)),
                  pl.BlockSpec((tf, d), lambda b, i, j: (j, 0)), par, par],
        out_specs=pl.BlockSpec((1, tm, d), lambda b, i, j: (b, i, 0)),
        out_shape=jax.ShapeDtypeStruct((bsz, seq, d), F32),
        scratch_shapes=[pltpu.VMEM((tm, d), BF16), pltpu.VMEM((tm, d), F32)],
        compiler_params=pltpu.CompilerParams(
            dimension_semantics=("parallel", "parallel", "arbitrary"), vmem_limit_bytes=V7X_VMEM_LIMIT),
        name="ffn_ln",
    )(x, sc, sh, gate, wg, wu, wd, ln_g.reshape(1, d), ln_b.reshape(1, d))


def _ada_kernel(c_ref, w_ref, b_ref, o_ref):
    cv = c_ref[...]
    o_ref[0] = jnp.dot(cv * _sigmoid(cv), w_ref[0], precision=lax.Precision.HIGHEST,
                       preferred_element_type=F32) + b_ref[0]


def ada_modulation(cvecs, w, b, tn=1536):
    r, d = cvecs.shape
    depth, _, n = w.shape
    return pl.pallas_call(
        _ada_kernel,
        grid=(depth, n // tn),
        in_specs=[pl.BlockSpec((r, d), lambda l, j: (0, 0)), pl.BlockSpec((1, d, tn), lambda l, j: (l, 0, j)),
                  pl.BlockSpec((1, 1, tn), lambda l, j: (l, 0, j))],
        out_specs=pl.BlockSpec((1, r, tn), lambda l, j: (l, 0, j)),
        out_shape=jax.ShapeDtypeStruct((depth, r, n), F32),
        compiler_params=pltpu.CompilerParams(
            dimension_semantics=("parallel", "parallel"), vmem_limit_bytes=V7X_VMEM_LIMIT),
        name="ada_mod",
    )(cvecs, w, b.reshape(depth, 1, n))


NA_NEG = -1e30


def _na_bias_slabs(rpb):
    w = jnp.arange(GRID_W)
    cs = jnp.clip(w - NA_KW // 2, 0, GRID_W - NA_KW)
    ok = (w[None, :] >= cs[:, None]) & (w[None, :] < cs[:, None] + NA_KW)
    dc = jnp.clip(w[None, :] - w[:, None] + (NA_KW - 1), 0, 2 * NA_KW - 2)
    tab = jnp.where(ok, rpb[:, :, dc], NA_NEG)
    idx = jnp.arange(NA_KH)[:, None] + jnp.arange(NA_KH)[None, :]
    slab = tab[:, idx]
    return slab.transpose(1, 0, 3, 2, 4).reshape(NA_KH, NA_HEADS, GRID_W, NA_KH * GRID_W)


def _softmax2(s_a, s_b):
    m = jnp.maximum(jnp.max(s_a, axis=-1, keepdims=True), jnp.max(s_b, axis=-1, keepdims=True))
    e_a = jnp.exp(s_a - m)
    e_b = jnp.exp(s_b - m)
    inv = 1.0 / (jnp.sum(e_a, axis=-1, keepdims=True) + jnp.sum(e_b, axis=-1, keepdims=True))
    return e_a * inv, e_b * inv


_NT = (((1,), (1,)), ((), ()))


def _na_kernel(q_ref, kp_ref, kc_ref, kn_ref, vp_ref, vc_ref, vn_ref, ck_ref, cv_ref, bias_ref, o_ref,
               kw_ref, vw_ref, ckb_ref, cvb_ref, *, rows):
    i = pl.program_id(1)
    blk = NA_KH * GRID_W
    for n, (kr, vr) in enumerate(((kp_ref, vp_ref), (kc_ref, vc_ref), (kn_ref, vn_ref))):
        kw_ref[n * blk:(n + 1) * blk, :] = kr[0].astype(BF16)
        vw_ref[n * blk:(n + 1) * blk, :] = vr[0].astype(BF16)
    ckb_ref[...] = ck_ref[0].astype(BF16)
    cvb_ref[...] = cv_ref[0].astype(BF16)
    scale = NA_HD ** -0.5

    first = lax.broadcasted_iota(jnp.int32, (GRID_W, 2 * NA_HD), 1) < NA_HD
    pairs = [slice(n * 2 * NA_HD, (n + 1) * 2 * NA_HD) for n in range(NA_HEADS // 2)]
    rows_per_trip = 4

    def body(t, carry):
        trip = []
        for sub in range(rows_per_trip):
            rr = t * rows_per_trip + sub
            r = i * NA_KH + rr
            rs = jnp.clip(r - NA_KH // 2, 0, rows - NA_KH)
            off = pl.multiple_of((rs - (i - 1) * NA_KH) * GRID_W, GRID_W)
            v = rs - r + (NA_KH - 1)
            q_all = (q_ref[0, pl.ds(pl.multiple_of(rr * GRID_W, GRID_W), GRID_W), :] * scale).astype(BF16)
            trip.append((rr, v, q_all, kw_ref[pl.ds(off, blk), :], vw_ref[pl.ds(off, blk), :]))
        scores = []
        for rr, v, q_all, kwin, vwin in trip:
            for h in range(NA_HEADS):
                ps = pairs[h // 2]
                q = jnp.where(first if h % 2 == 0 else ~first, q_all[:, ps], jnp.zeros((), BF16))
                scores.append((lax.dot_general(q, kwin[:, ps], _NT, preferred_element_type=F32) + bias_ref[v, h],
                               lax.dot_general(q, ckb_ref[:, ps], _NT, preferred_element_type=F32)))
        probs = [_softmax2(s_w, s_c) for s_w, s_c in scores]
        both = [jnp.dot(p_w.astype(BF16), trip[n // NA_HEADS][4][:, pairs[(n % NA_HEADS) // 2]],
                        preferred_element_type=F32)
                + jnp.dot(p_c.astype(BF16), cvb_ref[:, pairs[(n % NA_HEADS) // 2]], preferred_element_type=F32)
                for n, (p_w, p_c) in enumerate(probs)]
        for sub, (rr, _, _, _, _) in enumerate(trip):
            base = sub * NA_HEADS
            outs = [jnp.where(first, both[base + 2 * n], both[base + 2 * n + 1]) for n in range(NA_HEADS // 2)]
            o_ref[0, pl.ds(pl.multiple_of(rr * GRID_W, GRID_W), GRID_W), :] = jnp.concatenate(outs, axis=-1)
        return carry

    lax.fori_loop(0, NA_KH // rows_per_trip, body, 0)


def _na_ctx_kernel(q_ref, k_ref, v_ref, o_ref):
    scale = NA_HD ** -0.5
    q_all = (q_ref[0] * scale).astype(BF16)
    k_all = k_ref[0].astype(BF16)
    v_all = v_ref[0].astype(BF16)
    outs = []
    for h in range(NA_HEADS):
        hs = slice(h * NA_HD, (h + 1) * NA_HD)
        s = lax.dot_general(q_all[:, hs], k_all[:, hs], _NT, preferred_element_type=F32)
        e = jnp.exp(s - jnp.max(s, axis=-1, keepdims=True))
        p = e * (1.0 / jnp.sum(e, axis=-1, keepdims=True))
        outs.append(jnp.dot(p.astype(BF16), v_all[:, hs], preferred_element_type=F32))
    o_ref[0] = jnp.concatenate(outs, axis=-1)


def na_branch(p_l, p_c, rpb, need_ctx_out):
    bsz, seq, _ = p_l.shape
    n_ctx = p_c.shape[1]
    rows = seq // GRID_W
    assert rows % NA_KH == 0 and rows >= 2 * NA_KH
    w = BRANCH_W
    blk = NA_KH * GRID_W
    nblk = rows // NA_KH
    cq = COL_NA // w

    def shifted(col, delta):
        return pl.BlockSpec((1, blk, w), lambda b, i: (b, jnp.clip(i + delta, 0, nblk - 1), col))

    y_l = pl.pallas_call(
        functools.partial(_na_kernel, rows=rows),
        grid=(bsz, nblk),
        in_specs=[shifted(cq, 0), shifted(cq + 1, -1), shifted(cq + 1, 0), shifted(cq + 1, 1),
                  shifted(cq + 2, -1), shifted(cq + 2, 0), shifted(cq + 2, 1),
                  pl.BlockSpec((1, n_ctx, w), lambda b, i: (b, 0, cq + 1)),
                  pl.BlockSpec((1, n_ctx, w), lambda b, i: (b, 0, cq + 2)),
                  pl.BlockSpec((NA_KH, NA_HEADS, GRID_W, blk), lambda b, i: (0, 0, 0, 0))],
        out_specs=pl.BlockSpec((1, blk, w), lambda b, i: (b, i, 0)),
        out_shape=jax.ShapeDtypeStruct((bsz, seq, w), F32),
        scratch_shapes=[pltpu.VMEM((3 * blk, w), BF16), pltpu.VMEM((3 * blk, w), BF16),
                        pltpu.VMEM((n_ctx, w), BF16), pltpu.VMEM((n_ctx, w), BF16)],
        compiler_params=pltpu.CompilerParams(
            dimension_semantics=("parallel", "arbitrary"), vmem_limit_bytes=V7X_VMEM_LIMIT),
        name="na_attn",
    )(p_l, p_l, p_l, p_l, p_l, p_l, p_l, p_c, p_c, _na_bias_slabs(rpb))
    y_c = None
    if need_ctx_out:
        y_c = pl.pallas_call(
            _na_ctx_kernel,
            grid=(bsz,),
            in_specs=[pl.BlockSpec((1, n_ctx, w), functools.partial(lambda b, c: (b, 0, c), c=cq + n)) for n in range(3)],
            out_specs=pl.BlockSpec((1, n_ctx, w), lambda b: (b, 0, 0)),
            out_shape=jax.ShapeDtypeStruct((bsz, n_ctx, w), F32),
            compiler_params=pltpu.CompilerParams(dimension_semantics=("parallel",)),
            name="na_ctx_attn",
        )(p_c, p_c, p_c)
    return y_l, y_c


def _gelu_tanh(x):
    return 0.5 * x * (1.0 + jnp.tanh(math.sqrt(2.0 / math.pi) * (x + 0.044715 * (x * x * x))))


def _silu(x):
    return x * _sigmoid(x)


def _softplus(x):
    return jnp.maximum(x, 0.0) + jnp.log1p(jnp.exp(-jnp.abs(x)))


def _rope_2d_tables(seq):
    t = jnp.arange(seq, dtype=jnp.int32)
    pos = jnp.stack([t // GRID_W, t % GRID_W], axis=-1).astype(F32)
    nf = ML_HD // 4
    inv_freq = ROPE_BASE ** (-jnp.arange(nf, dtype=F32) / nf)
    ang = jnp.broadcast_to(pos[:, :, None, None] * inv_freq, (seq, 2, 2, nf)).reshape(seq, ML_HD)
    return jnp.cos(ang), jnp.sin(ang)


def _chunk_specs(tt, nchunk, n8, reverse, nb=1):
    def pos(j):
        return nchunk - 1 - j if reverse else j

    def chunk(width, col):
        return pl.BlockSpec((nb, tt, width), lambda b, j: (b, pos(j), col))

    def halo(width, col, delta):
        if delta < 0:
            return pl.BlockSpec((nb, 8, width), lambda b, j: (b, jnp.maximum(pos(j) * (tt // 8) - 1, 0), col))
        return pl.BlockSpec((nb, 8, width), lambda b, j: (b, jnp.minimum((pos(j) + 1) * (tt // 8), n8 - 1), col))

    return chunk, halo


def _const_spec(shape):
    return pl.BlockSpec(shape, lambda b, j: (0,) * len(shape))


def _batch_spec(shape, nb=1):
    return pl.BlockSpec((nb,) + shape, lambda b, j: (b,) + (0,) * len(shape))


def _conv4(x, prev8, next8, has_prev, has_next, cw, cb):
    tt = x.shape[0]
    row = lax.broadcasted_iota(jnp.int32, x.shape, 0)
    p6 = prev8[6:7, :] * has_prev
    p7 = prev8[7:8, :] * has_prev
    n0 = next8[0:1, :] * has_next
    xm1 = jnp.where(row == 0, p7, pltpu.roll(x, 1, 0))
    xm2 = jnp.where(row == 0, p6, jnp.where(row == 1, p7, pltpu.roll(x, 2, 0)))
    xp1 = jnp.where(row == tt - 1, n0, pltpu.roll(x, tt - 1, 0))
    return cw[0:1, :] * xm2 + cw[1:2, :] * xm1 + cw[2:3, :] * x + cw[3:4, :] * xp1 + cb


def _cumsum_rows(x, reverse):
    n = x.shape[0]
    row = lax.broadcasted_iota(jnp.int32, x.shape, 0)
    s = 1
    while s < n:
        if reverse:
            x = x + jnp.where(row < n - s, pltpu.roll(x, n - s, 0), 0.0)
        else:
            x = x + jnp.where(row >= s, pltpu.roll(x, s, 0), 0.0)
        s *= 2
    return x


def _causal_mask(n, reverse):
    ii = lax.broadcasted_iota(jnp.int32, (n, n), 0)
    jj = lax.broadcasted_iota(jnp.int32, (n, n), 1)
    return (jj >= ii) if reverse else (jj <= ii)


MASK_NEG = -1e30


LRU_TT = 256


def _lru_kernel(*refs, reverse, finalize, nchunk):
    if finalize:
        (xp_ref, x_ref, xn_ref, g_ref, ho_ref, h0_ref, cw_ref, cb_ref, w_ref, bias_ref, sp_ref,
         o_ref, hl_ref, carry_ref) = refs
    else:
        (xp_ref, x_ref, xn_ref, h0_ref, cw_ref, cb_ref, w_ref, bias_ref, sp_ref, o_ref, hl_ref, carry_ref) = refs
    j = pl.program_id(1)
    c = (nchunk - 1 - j) if reverse else j

    @pl.when(j == 0)
    def _():
        carry_ref[...] = h0_ref[0]

    tt = x_ref.shape[1]
    w = x_ref.shape[2]
    xc = _conv4(x_ref[0], xp_ref[0], xn_ref[0], (c > 0).astype(F32), (c < nchunk - 1).astype(F32),
                cw_ref[...], cb_ref[...])
    g = jnp.dot(xc.astype(BF16), w_ref[...], preferred_element_type=F32)
    r = _sigmoid(g[:, :w] + bias_ref[0:1, :])
    ig = _sigmoid(g[:, w:] + bias_ref[1:2, :])
    log_a = -LRU_C * r * sp_ref[...]
    a = jnp.exp(log_a)
    u = jnp.sqrt(1.0 - jnp.exp(2.0 * log_a)) * (ig * xc)
    sub = lax.broadcasted_iota(jnp.int32, (tt, w), 0) % 8
    s = 1
    while s < 8:
        if reverse:
            keep = sub < 8 - s
            a_s = jnp.where(keep, pltpu.roll(a, tt - s, 0), 1.0)
            u_s = jnp.where(keep, pltpu.roll(u, tt - s, 0), 0.0)
        else:
            keep = sub >= s
            a_s = jnp.where(keep, pltpu.roll(a, s, 0), 1.0)
            u_s = jnp.where(keep, pltpu.roll(u, s, 0), 0.0)
        u = a * u_s + u
        a = a * a_s
        s *= 2
    new_carry = carry_ref[...]
    nblk = tt // 8
    hs = [None] * nblk
    for blk in (reversed(range(nblk)) if reverse else range(nblk)):
        hb = u[blk * 8:(blk + 1) * 8, :] + a[blk * 8:(blk + 1) * 8, :] * new_carry
        new_carry = hb[0:1, :] if reverse else hb[7:8, :]
        hs[blk] = hb
    h = jnp.concatenate(hs, axis=0)
    carry_ref[...] = new_carry
    hl_ref[0] = new_carry
    if finalize:
        o_ref[0] = (ho_ref[0] + h) * _gelu_tanh(g_ref[0])
    else:
        o_ref[0] = h


def _lru_pass(p, h0, other, cw, cb, wcat, bias, sp, reverse):
    bsz, seq, _ = p.shape
    w = BRANCH_W
    tt = min(LRU_TT, seq)
    nchunk = seq // tt
    chunk, halo = _chunk_specs(tt, nchunk, seq // 8, reverse)
    cx = COL_LRU_X // w
    finalize = other is not None
    in_specs = [halo(w, cx, -1), chunk(w, cx), halo(w, cx, 1)]
    args = [p, p, p]
    if finalize:
        in_specs += [chunk(w, COL_LRU_G // w), chunk(w, 0)]
        args += [p, other]
    in_specs += [_batch_spec((1, w)), _const_spec((CONV_W, w)), _const_spec((1, w)),
                 _const_spec((w, 2 * w)), _const_spec((2, w)), _const_spec((1, w))]
    args += [h0, cw, cb, wcat, bias, sp]
    return pl.pallas_call(
        functools.partial(_lru_kernel, reverse=reverse, finalize=finalize, nchunk=nchunk),
        grid=(bsz, nchunk),
        in_specs=in_specs,
        out_specs=[chunk(w, 0), _batch_spec((1, w))],
        out_shape=[jax.ShapeDtypeStruct((bsz, seq, w), F32), jax.ShapeDtypeStruct((bsz, 1, w), F32)],
        scratch_shapes=[pltpu.VMEM((1, w), F32)],
        compiler_params=pltpu.CompilerParams(dimension_semantics=("parallel", "arbitrary")),
        name="lru_scan",
    )(*args)


def _block_diag(wg):
    g, n, _ = wg.shape
    eye = jnp.eye(g, dtype=wg.dtype)
    return (wg[:, :, None, :] * eye[:, None, :, None]).reshape(g * n, g * n)


def lru_branch(p_l, p_c, conv_w, conv_b, wa, ba, wx, bx, lam, need_ctx_out):
    bsz = p_l.shape[0]
    w = BRANCH_W
    cb = conv_b.reshape(1, w)
    sp = jax.nn.softplus(-lam)
    zeros = jnp.zeros((bsz, 1, w), F32)
    h_c = h_l = None
    for d, reverse in ((0, False), (1, True)):
        wcat = jnp.concatenate([_block_diag(wa[d]), _block_diag(wx[d])], axis=1).astype(BF16)
        bias = jnp.stack([ba[d], bx[d]])
        other_c = h_c if (d == 1 and need_ctx_out) else None
        h_c, st = _lru_pass(p_c, zeros, other_c, conv_w, cb, wcat, bias, sp[d:d + 1], reverse)
        h_l, _ = _lru_pass(p_l, st, h_l if d == 1 else None, conv_w, cb, wcat, bias, sp[d:d + 1], reverse)
    return h_l, (h_c if need_ctx_out else None)


def _ssd_kernel(*refs, reverse, finalize, nchunk, d):
    (xp_ref, x_ref, xn_ref, bp_ref, bc_ref, bn_ref, dt_ref) = refs[:7]
    k = 7
    if finalize:
        z_ref, yp_ref = refs[k:k + 2]
        k += 2
    s0_ref, cwx_ref, cbx_ref, cwb_ref, cbb_ref, dtb_ref, a_ref = refs[k:k + 7]
    k += 7
    if finalize:
        dsk_ref, ng_ref = refs[k:k + 2]
        k += 2
    o_ref, so_ref, s_ref = refs[k:k + 3]
    j = pl.program_id(1)
    c = (nchunk - 1 - j) if reverse else j

    @pl.when(j == 0)
    def _():
        s_ref[...] = s0_ref[...]

    nb = x_ref.shape[0]
    q = x_ref.shape[1]
    has_prev = (c > 0).astype(F32)
    has_next = (c < nchunk - 1).astype(F32)
    mask = _causal_mask(q, reverse)
    first = lax.broadcasted_iota(jnp.int32, (q, 2 * SSD_HD), 1) < SSD_HD
    npair = SSD_HEADS // 2
    pair_group = [(2 * n) // (SSD_HEADS // SSD_GROUPS) for n in range(npair)]
    pairs = [slice(n * 2 * SSD_HD, (n + 1) * 2 * SSD_HD) for n in range(npair)]

    def per_lane(t, n):
        c0 = d * SSD_HEADS + 2 * n
        return jnp.where(first[:t.shape[0]], t[:, c0:c0 + 1], t[:, c0 + 1:c0 + 2])

    xs, xb, dt_t, cum, cum_t, w_end, ecum, etot, b_t, cgs, cbs = [], [], [], [], [], [], [], [], [], [], []
    for bi in range(nb):
        xs.append(_silu(_conv4(x_ref[bi], xp_ref[bi], xn_ref[bi], has_prev, has_next, cwx_ref[...], cbx_ref[...])))
        bc = _silu(_conv4(bc_ref[bi], bp_ref[bi], bn_ref[bi], has_prev, has_next, cwb_ref[...], cbb_ref[...]))
        dt = _softplus(dt_ref[bi] + dtb_ref[...])
        cum.append(_cumsum_rows(dt * a_ref[...], reverse))
        tot = cum[bi][0:1, :] if reverse else cum[bi][q - 1:q, :]
        w_end.append(jnp.exp(tot - cum[bi]) * dt)
        ecum.append(jnp.exp(cum[bi]))
        etot.append(jnp.exp(tot))
        dt_t.append(dt.T)
        cum_t.append(cum[bi].T)
        b_t.append(bc[:, :SSD_GN].T)
        xb.append(xs[bi].astype(BF16))
        cgs.append([bc[:, SSD_GN + g * SSD_STATE:SSD_GN + (g + 1) * SSD_STATE].astype(BF16)
                    for g in range(SSD_GROUPS)])
        cbs.append([lax.dot_general(cgs[bi][g], bc[:, g * SSD_STATE:(g + 1) * SSD_STATE].astype(BF16), _NT,
                                    preferred_element_type=F32) for g in range(SSD_GROUPS)])
    intra = []
    for bi in range(nb):
        for h in range(SSD_HEADS):
            col = d * SSD_HEADS + h
            diff = cum[bi][:, col:col + 1] - cum_t[bi][col:col + 1, :]
            m = (cbs[bi][pair_group[h // 2]] * jnp.exp(jnp.where(mask, diff, MASK_NEG))
                 * dt_t[bi][col:col + 1, :])
            intra.append(jnp.dot(m.astype(BF16), xb[bi][:, pairs[h // 2]], preferred_element_type=F32))
    items = [(bi, n) for bi in range(nb) for n in range(npair)]
    s_old = [s_ref[bi, n] for bi, n in items]
    inter = [jnp.dot(cgs[bi][pair_group[n]], s_old[t].astype(BF16), preferred_element_type=F32)
             for t, (bi, n) in enumerate(items)]
    outs = []
    for t, (bi, n) in enumerate(items):
        g = pair_group[n]
        h0 = bi * SSD_HEADS + 2 * n
        outs.append(jnp.where(first, intra[h0], intra[h0 + 1]) + per_lane(ecum[bi], n) * inter[t])
        xw = (xs[bi][:, pairs[n]] * per_lane(w_end[bi], n)).astype(BF16)
        s_ref[bi, n] = per_lane(etot[bi], n) * s_old[t] + jnp.dot(
            b_t[bi][g * SSD_STATE:(g + 1) * SSD_STATE, :].astype(BF16), xw, preferred_element_type=F32)
    so_ref[...] = s_ref[...]
    for bi in range(nb):
        y = jnp.concatenate(outs[bi * npair:(bi + 1) * npair], axis=-1)
        if finalize:
            yt = (xs[bi] * dsk_ref[...] + yp_ref[bi] + y) * _silu(z_ref[bi])
            o_ref[bi] = yt * lax.rsqrt(jnp.mean(yt * yt, axis=-1, keepdims=True) + LN_EPS) * ng_ref[...]
        else:
            o_ref[bi] = y


def _ssd_pass(p, s0, other, params, reverse, d):
    bsz, seq, _ = p.shape
    w = BRANCH_W
    q = SSD_CHUNK
    nchunk = seq // q
    chunk, halo = _chunk_specs(q, nchunk, seq // 8, reverse, nb=bsz)
    cx, cb2, cdt = COL_SSD_X // w, COL_SSD_B // (2 * SSD_GN), COL_SSD_DT // 128
    finalize = other is not None
    cwx, cbx, cwb, cbb, dtb, arow, dsk, ng = params
    in_specs = [halo(w, cx, -1), chunk(w, cx), halo(w, cx, 1),
                halo(2 * SSD_GN, cb2, -1), chunk(2 * SSD_GN, cb2), halo(2 * SSD_GN, cb2, 1), chunk(128, cdt)]
    args = [p] * 7
    if finalize:
        in_specs += [chunk(w, COL_SSD_Z // w), chunk(w, 0)]
        args += [p, other]
    st_shape = (SSD_HEADS // 2, SSD_STATE, 2 * SSD_HD)
    in_specs += [_batch_spec(st_shape, bsz), _const_spec((CONV_W, w)), _const_spec((1, w)),
                 _const_spec((CONV_W, 2 * SSD_GN)), _const_spec((1, 2 * SSD_GN)), _const_spec((1, 128)),
                 _const_spec((1, 128))]
    args += [s0, cwx, cbx, cwb, cbb, dtb, arow]
    if finalize:
        in_specs += [_const_spec((1, w)), _const_spec((1, w))]
        args += [dsk, ng]
    return pl.pallas_call(
        functools.partial(_ssd_kernel, reverse=reverse, finalize=finalize, nchunk=nchunk, d=d),
        grid=(1, nchunk),
        in_specs=in_specs,
        out_specs=[chunk(w, 0), _batch_spec(st_shape, bsz)],
        out_shape=[jax.ShapeDtypeStruct((bsz, seq, w), F32), jax.ShapeDtypeStruct((bsz,) + st_shape, F32)],
        scratch_shapes=[pltpu.VMEM((bsz,) + st_shape, F32)],
        compiler_params=pltpu.CompilerParams(dimension_semantics=("parallel", "arbitrary")),
        name="ssd_scan",
    )(*args)


def _lane_row(vals, start):
    return jnp.zeros((128,), F32).at[start:start + vals.shape[0]].set(vals.astype(F32)).reshape(1, 128)


def ssd_branch(p_l, p_c, conv_w, conv_b, dt_bias, a_log, d_skip, norm_g, need_ctx_out):
    bsz = p_l.shape[0]
    w = BRANCH_W
    params = (conv_w[:, :w], conv_b[:w].reshape(1, w), conv_w[:, w:], conv_b[w:].reshape(1, 2 * SSD_GN),
              _lane_row(dt_bias.reshape(-1), 0), _lane_row(-jnp.exp(a_log.astype(F32)).reshape(-1), 0),
              jnp.repeat(d_skip, SSD_HD).reshape(1, w), norm_g.reshape(1, w))
    zeros = jnp.zeros((bsz, SSD_HEADS // 2, SSD_STATE, 2 * SSD_HD), F32)
    y_c = y_l = None
    for d, reverse in ((0, False), (1, True)):
        other_c = y_c if (d == 1 and need_ctx_out) else None
        y_c, st = _ssd_pass(p_c, zeros, other_c, params, reverse, d)
        y_l, _ = _ssd_pass(p_l, st, y_l if d == 1 else None, params, reverse, d)
    return y_l, (y_c if need_ctx_out else None)


def _rope_rotate(x):
    wl = x.shape[-1]
    half = ML_HD // 4
    lane = lax.broadcasted_iota(jnp.int32, x.shape, 1)
    return jnp.where(lane % (2 * half) < half, -pltpu.roll(x, wl - half, 1), pltpu.roll(x, half, 1))


def _log_sigmoid(x):
    return jnp.minimum(x, 0.0) - jnp.log1p(jnp.exp(-jnp.abs(x)))


def _mlstm_kernel(*refs, reverse, finalize, rope, nchunk, d):
    (qp_ref, q_ref, qn_ref, kp_ref, k_ref, kn_ref, v_ref, g_ref) = refs[:8]
    n = 8
    if rope:
        cos_ref, sin_ref = refs[n:n + 2]
        n += 2
    if finalize:
        og_ref, hp_ref = refs[n:n + 2]
        n += 2
    c0_ref, n0_ref, m0_ref, cwq_ref, cbq_ref, cwk_ref, cbk_ref, ib_ref, fb_ref = refs[n:n + 9]
    n += 9
    o_ref, co_ref, no_ref, mo_ref, c_ref, n_ref, m_ref = refs[n:n + 7]
    j = pl.program_id(1)
    c = (nchunk - 1 - j) if reverse else j

    @pl.when(j == 0)
    def _():
        c_ref[...] = c0_ref[...]
        n_ref[...] = n0_ref[...]
        m_ref[...] = m0_ref[...]

    nb = q_ref.shape[0]
    qn = q_ref.shape[1]
    has_prev = (c > 0).astype(F32)
    has_next = (c < nchunk - 1).astype(F32)
    mask = _causal_mask(qn, reverse)
    heads = [slice(h * ML_HD, (h + 1) * ML_HD) for h in range(ML_HEADS)]
    items = [(bi, h) for bi in range(nb) for h in range(ML_HEADS)]
    q, k, v, b, li, tot, b_t, li_t = [], [], [], [], [], [], [], []
    for bi in range(nb):
        qi = _silu(_conv4(q_ref[bi], qp_ref[bi], qn_ref[bi], has_prev, has_next, cwq_ref[...], cbq_ref[...]))
        ki = _silu(_conv4(k_ref[bi], kp_ref[bi], kn_ref[bi], has_prev, has_next, cwk_ref[...], cbk_ref[...]))
        if rope:
            qi = qi * cos_ref[...] + _rope_rotate(qi) * sin_ref[...]
            ki = ki * cos_ref[...] + _rope_rotate(ki) * sin_ref[...]
        q.append(qi * (ML_HD ** -0.5))
        k.append(ki)
        v.append(v_ref[bi])
        gb = g_ref[bi]
        li.append(gb + ib_ref[...])
        b.append(_cumsum_rows(_log_sigmoid(gb + fb_ref[...]), reverse))
        tot.append(b[bi][0:1, :] if reverse else b[bi][qn - 1:qn, :])
        b_t.append(b[bi].T)
        li_t.append(li[bi].T)
    qb = [t.astype(BF16) for t in q]
    kb = [t.astype(BF16) for t in k]
    vb = [t.astype(BF16) for t in v]
    qk = [lax.dot_general(qb[bi][:, heads[h]], kb[bi][:, heads[h]], _NT, preferred_element_type=F32)
          for bi, h in items]
    c_old = [c_ref[bi, h] for bi, h in items]
    n_old = [n_ref[bi, h:h + 1, :] for bi, h in items]
    qc = [lax.dot_general(qb[bi][:, heads[h]], c_old[t].astype(BF16), _NT, preferred_element_type=F32)
          for t, (bi, h) in enumerate(items)]
    gate = []
    for bi, h in items:
        ci = 4 * ML_HEADS + d * 2 * ML_HEADS + h
        cf = ci + ML_HEADS
        b_c = b[bi][:, cf:cf + 1]
        b_end = tot[bi][:, cf:cf + 1]
        m_st = m_ref[bi, h:h + 1, 0:1]
        end_log = b_end - b_c + li[bi][:, ci:ci + 1]
        m_new = jnp.maximum(b_end + m_st, jnp.max(end_log, axis=0, keepdims=True))
        dlog = jnp.where(mask, b_c - b_t[bi][cf:cf + 1, :] + li_t[bi][ci:ci + 1, :], MASK_NEG)
        m_inter = b_c + m_st
        m_i = jnp.maximum(jnp.max(dlog, axis=1, keepdims=True), m_inter)
        gate.append((jnp.exp(end_log - m_new), jnp.exp(b_end + m_st - m_new), m_new,
                     jnp.exp(dlog - m_i), jnp.exp(m_inter - m_i), jnp.exp(-m_i)))
    s_all = [qk[t] * gate[t][3] for t in range(len(items))]
    sv = [jnp.dot(s_all[t].astype(BF16), vb[bi][:, heads[h]], preferred_element_type=F32)
          for t, (bi, h) in enumerate(items)]
    upd = [jnp.dot((v[bi][:, heads[h]] * gate[t][0]).T.astype(BF16), kb[bi][:, heads[h]],
                   preferred_element_type=F32) for t, (bi, h) in enumerate(items)]
    outs = []
    for t, (bi, h) in enumerate(items):
        hs = heads[h]
        w, carry_scale, m_new, _, w_in, floor = gate[t]
        num = sv[t] + w_in * qc[t]
        den = (jnp.sum(s_all[t], axis=1, keepdims=True)
               + w_in * jnp.sum(q[bi][:, hs] * n_old[t], axis=1, keepdims=True))
        outs.append(num / jnp.maximum(jnp.abs(den), floor))
        c_ref[bi, h] = carry_scale * c_old[t] + upd[t]
        n_ref[bi, h:h + 1, :] = carry_scale * n_old[t] + jnp.sum(k[bi][:, hs] * w, axis=0, keepdims=True)
        m_ref[bi, h:h + 1, :] = jnp.broadcast_to(m_new, (1, ML_HD))
    co_ref[...] = c_ref[...]
    no_ref[...] = n_ref[...]
    mo_ref[...] = m_ref[...]
    for bi in range(nb):
        hout = jnp.concatenate(outs[bi * ML_HEADS:(bi + 1) * ML_HEADS], axis=-1)
        if finalize:
            o_ref[bi] = _sigmoid(og_ref[bi]) * (hp_ref[bi] + hout)
        else:
            o_ref[bi] = hout


def _mlstm_pass(p, state, other, tables, params, reverse, d):
    bsz, seq, _ = p.shape
    w = BRANCH_W
    qn = ML_CHUNK
    nchunk = seq // qn
    chunk, halo = _chunk_specs(qn, nchunk, seq // 8, reverse, nb=bsz)
    cq = COL_ML // w
    finalize = other is not None
    rope = tables is not None
    in_specs = [halo(w, cq, -1), chunk(w, cq), halo(w, cq, 1), halo(w, cq + 1, -1), chunk(w, cq + 1),
                halo(w, cq + 1, 1), chunk(w, cq + 2), chunk(128, COL_ML_G // 128)]
    args = [p] * 8
    if rope:
        tab = pl.BlockSpec((qn, w), (lambda b, j: (nchunk - 1 - j, 0)) if reverse else (lambda b, j: (j, 0)))
        in_specs += [tab, tab]
        args += list(tables)
    if finalize:
        in_specs += [chunk(w, cq + 3), chunk(w, 0)]
        args += [p, other]
    st_shapes = [(ML_HEADS, ML_HD, ML_HD), (ML_HEADS, ML_HD), (ML_HEADS, ML_HD)]
    in_specs += [_batch_spec(s, bsz) for s in st_shapes]
    in_specs += [_const_spec((CONV_W, w)), _const_spec((1, w)), _const_spec((CONV_W, w)), _const_spec((1, w)),
                 _const_spec((1, 128)), _const_spec((1, 128))]
    args += list(state) + list(params)
    res = pl.pallas_call(
        functools.partial(_mlstm_kernel, reverse=reverse, finalize=finalize, rope=rope, nchunk=nchunk, d=d),
        grid=(1, nchunk),
        in_specs=in_specs,
        out_specs=[chunk(w, 0)] + [_batch_spec(s, bsz) for s in st_shapes],
        out_shape=[jax.ShapeDtypeStruct((bsz, seq, w), F32)]
                  + [jax.ShapeDtypeStruct((bsz,) + s, F32) for s in st_shapes],
        scratch_shapes=[pltpu.VMEM((bsz,) + s, F32) for s in st_shapes],
        compiler_params=pltpu.CompilerParams(dimension_semantics=("parallel", "arbitrary")),
        name="mlstm_scan",
    )(*args)
    return res[0], tuple(res[1:])


def mlstm_branch(p_l, p_c, conv_w, conv_b, i_bias, f_bias, need_ctx_out):
    bsz, seq, _ = p_l.shape
    w = BRANCH_W
    cos, sin = _rope_2d_tables(seq)
    tables = (jnp.tile(cos, (1, ML_HEADS)), jnp.tile(sin, (1, ML_HEADS)))
    zero_h = jnp.zeros_like(i_bias)
    ib = _lane_row(jnp.concatenate([i_bias, zero_h], axis=1).reshape(-1), 4 * ML_HEADS)
    fb = _lane_row(jnp.concatenate([zero_h, f_bias], axis=1).reshape(-1), 4 * ML_HEADS)
    params = (conv_w[:, :w], conv_b[:w].reshape(1, w), conv_w[:, w:], conv_b[w:].reshape(1, w), ib, fb)
    state0 = (jnp.zeros((bsz, ML_HEADS, ML_HD, ML_HD), F32), jnp.zeros((bsz, ML_HEADS, ML_HD), F32),
              jnp.zeros((bsz, ML_HEADS, ML_HD), F32))
    h_c = h_l = None
    for d, reverse in ((0, False), (1, True)):
        other_c = h_c if (d == 1 and need_ctx_out) else None
        h_c, st = _mlstm_pass(p_c, state0, other_c, None, params, reverse, d)
        h_l, _ = _mlstm_pass(p_l, st, h_l if d == 1 else None, tables, params, reverse, d)
    return h_l, (h_c if need_ctx_out else None)


MOE_TM = 512
ROUTE_LANES = 128


def _moe_router_kernel(x_ref, sc_ref, sh_ref, wr_ref, br_ref, hb_ref, rt_ref):
    h = x_ref[0] * (1.0 + sc_ref[0]) + sh_ref[0]
    hb_ref[0] = h.astype(BF16)
    logits = jnp.dot(h, wr_ref[...], precision=lax.Precision.HIGHEST, preferred_element_type=F32) + br_ref[...]
    lane = lax.broadcasted_iota(jnp.int32, logits.shape, 1)
    m1 = jnp.max(logits, axis=-1, keepdims=True)
    i1 = jnp.min(jnp.where(logits == m1, lane, ROUTE_LANES), axis=-1, keepdims=True)
    rest = jnp.where(lane == i1, MASK_NEG, logits)
    m2 = jnp.max(rest, axis=-1, keepdims=True)
    i2 = jnp.min(jnp.where(rest == m2, lane, ROUTE_LANES), axis=-1, keepdims=True)
    e2 = jnp.exp(m2 - m1)
    p1 = 1.0 / (1.0 + e2)
    p2 = e2 * p1
    rt_ref[0] = jnp.where(lane == 0, i1.astype(F32), jnp.where(lane == 1, i2.astype(F32),
                          jnp.where(lane == 2, p1, jnp.where(lane == 3, p2, 0.0))))


def moe_router(x, sc, sh, w_router, b_router, tm=512):
    bsz, seq, d = x.shape
    n_e = w_router.shape[1]
    tm = min(tm, seq)
    wr = jnp.zeros((d, ROUTE_LANES), F32).at[:, :n_e].set(w_router)
    br = jnp.full((1, ROUTE_LANES), MASK_NEG, F32).at[0, :n_e].set(b_router)
    return pl.pallas_call(
        _moe_router_kernel,
        grid=(bsz, seq // tm),
        in_specs=[pl.BlockSpec((1, tm, d), lambda b, i: (b, i, 0)),
                  pl.BlockSpec((1, 1, d), lambda b, i: (b, 0, 0)),
                  pl.BlockSpec((1, 1, d), lambda b, i: (b, 0, 0)),
                  pl.BlockSpec((d, ROUTE_LANES), lambda b, i: (0, 0)),
                  pl.BlockSpec((1, ROUTE_LANES), lambda b, i: (0, 0))],
        out_specs=[pl.BlockSpec((1, tm, d), lambda b, i: (b, i, 0)),
                   pl.BlockSpec((1, tm, ROUTE_LANES), lambda b, i: (b, i, 0))],
        out_shape=[jax.ShapeDtypeStruct((bsz, seq, d), BF16), jax.ShapeDtypeStruct((bsz, seq, ROUTE_LANES), F32)],
        compiler_params=pltpu.CompilerParams(dimension_semantics=("parallel", "parallel")),
        name="moe_router",
    )(x, sc, sh, wr, br)


def _route_tables(idx, n_e, tmg):
    n_tok = idx.shape[0]
    e_flat = idx.reshape(-1)
    onehot = (e_flat[:, None] == jnp.arange(n_e, dtype=jnp.int32)[None, :]).astype(jnp.int32)
    csum = jnp.cumsum(onehot, axis=0)
    rank = jnp.take_along_axis(csum - onehot, e_flat[:, None], axis=1)[:, 0]
    padded = ((csum[-1] + tmg - 1) // tmg) * tmg
    ends = jnp.cumsum(padded)
    pos = (ends - padded)[e_flat] + rank
    n_rows = TOP_K * n_tok + n_e * tmg
    ntiles = n_rows // tmg
    tile_start = jnp.arange(ntiles, dtype=jnp.int32) * tmg
    tile_expert = jnp.minimum(jnp.sum(tile_start[:, None] >= ends[None, :], axis=1), n_e - 1)
    src = (jnp.arange(n_rows, dtype=jnp.int32) % n_tok).at[pos].set(
        jnp.arange(TOP_K * n_tok, dtype=jnp.int32) // TOP_K)
    meta = jnp.concatenate([tile_expert, ends[-1:] // tmg]).astype(jnp.int32)
    return pos.reshape(n_tok, TOP_K), src, meta


def _expert_changed(meta_ref, i):
    return (i == 0) | (meta_ref[i] != meta_ref[jnp.maximum(i - 1, 0)])


def _moe_up_kernel(meta_ref, x_ref, wg_ref, wu_ref, a_ref, wgb_ref, wub_ref, *, ntiles):
    i = pl.program_id(1)

    @pl.when(_expert_changed(meta_ref, i))
    def _():
        wgb_ref[...] = wg_ref[0].astype(BF16)
        wub_ref[...] = wu_ref[0].astype(BF16)

    @pl.when(i < meta_ref[ntiles])
    def _():
        x = x_ref[...]
        g = jnp.dot(x, wgb_ref[...], preferred_element_type=F32)
        u = jnp.dot(x, wub_ref[...], preferred_element_type=F32)
        a_ref[...] = (_silu(g) * u).astype(BF16)

    @pl.when(i >= meta_ref[ntiles])
    def _():
        a_ref[...] = jnp.zeros_like(a_ref)


def _moe_down_kernel(meta_ref, a_ref, wd_ref, y_ref, wdb_ref, *, ntiles):
    i = pl.program_id(0)

    @pl.when(_expert_changed(meta_ref, i))
    def _():
        wdb_ref[...] = wd_ref[0].astype(BF16)

    @pl.when(i < meta_ref[ntiles])
    def _():
        y_ref[...] = jnp.dot(a_ref[...], wdb_ref[...], preferred_element_type=F32).astype(y_ref.dtype)

    @pl.when(i >= meta_ref[ntiles])
    def _():
        y_ref[...] = jnp.zeros_like(y_ref)


def moe_experts(xs, meta, wg, wu, wd, tmg, tf):
    n_rows, d = xs.shape
    n_e, _, f = wg.shape
    ntiles = n_rows // tmg
    nf = f // tf
    once = pl.Buffered(1)
    a = pl.pallas_call(
        functools.partial(_moe_up_kernel, ntiles=ntiles),
        grid_spec=pltpu.PrefetchScalarGridSpec(
            num_scalar_prefetch=1,
            grid=(nf, ntiles),
            in_specs=[pl.BlockSpec((tmg, d), lambda j, i, m: (i, 0)),
                      pl.BlockSpec((1, d, tf), lambda j, i, m: (m[i], 0, j), pipeline_mode=once),
                      pl.BlockSpec((1, d, tf), lambda j, i, m: (m[i], 0, j), pipeline_mode=once)],
            out_specs=pl.BlockSpec((tmg, tf), lambda j, i, m: (i, j)),
            scratch_shapes=[pltpu.VMEM((d, tf), BF16), pltpu.VMEM((d, tf), BF16)]),
        out_shape=jax.ShapeDtypeStruct((n_rows, f), BF16),
        compiler_params=pltpu.CompilerParams(
            dimension_semantics=("arbitrary", "arbitrary"), vmem_limit_bytes=V7X_VMEM_LIMIT),
        name="moe_up",
    )(meta, xs, wg, wu)
    return pl.pallas_call(
        functools.partial(_moe_down_kernel, ntiles=ntiles),
        grid_spec=pltpu.PrefetchScalarGridSpec(
            num_scalar_prefetch=1,
            grid=(ntiles,),
            in_specs=[pl.BlockSpec((tmg, f), lambda i, m: (i, 0)),
                      pl.BlockSpec((1, f, d), lambda i, m: (m[i], 0, 0), pipeline_mode=once)],
            out_specs=pl.BlockSpec((tmg, d), lambda i, m: (i, 0)),
            scratch_shapes=[pltpu.VMEM((f, d), BF16)]),
        out_shape=jax.ShapeDtypeStruct((n_rows, d), BF16),
        compiler_params=pltpu.CompilerParams(
            dimension_semantics=("arbitrary",), vmem_limit_bytes=V7X_VMEM_LIMIT),
        name="moe_down",
    )(meta, a, wd)


def _combine_ln_kernel(x_ref, gt_ref, rt_ref, ya_ref, yb_ref, g_ref, b_ref, o_ref, *, alpha):
    rt = rt_ref[0]
    y = rt[:, 2:3] * ya_ref[0].astype(F32) + rt[:, 3:4] * yb_ref[0].astype(F32)
    o_ref[0] = _layer_norm_rows(alpha * x_ref[0] + gt_ref[0] * y, g_ref[...], b_ref[...])


def combine_residual_ln(x, gate, route, ya, yb, ln_g, ln_b, alpha, tm=512):
    bsz, seq, d = x.shape
    tm = min(tm, seq)
    row = pl.BlockSpec((1, tm, d), lambda b, i: (b, i, 0))
    return pl.pallas_call(
        functools.partial(_combine_ln_kernel, alpha=alpha),
        grid=(bsz, seq // tm),
        in_specs=[row, pl.BlockSpec((1, 1, d), lambda b, i: (b, 0, 0)),
                  pl.BlockSpec((1, tm, ROUTE_LANES), lambda b, i: (b, i, 0)), row, row,
                  pl.BlockSpec((1, d), lambda b, i: (0, 0)), pl.BlockSpec((1, d), lambda b, i: (0, 0))],
        out_specs=row,
        out_shape=jax.ShapeDtypeStruct((bsz, seq, d), F32),
        compiler_params=pltpu.CompilerParams(
            dimension_semantics=("parallel", "parallel"), vmem_limit_bytes=V7X_VMEM_LIMIT),
        name="moe_combine_ln",
    )(x, gate, route, ya, yb, ln_g.reshape(1, d), ln_b.reshape(1, d))


def moe_residual_ln(x, sc, sh, gate, w_router, b_router, wg, wu, wd, ln_g, ln_b, alpha):
    bsz, seq, d = x.shape
    n_e, _, f = wg.shape
    n_tok = bsz * seq
    tmg = min(MOE_TM, TOP_K * n_tok)
    hb, route = moe_router(x, sc, sh, w_router, b_router)
    idx = route[..., :TOP_K].astype(jnp.int32).reshape(n_tok, TOP_K)
    pos, src, meta = _route_tables(idx, n_e, tmg)
    xs = hb.reshape(n_tok, d).at[src].get(mode="promise_in_bounds")
    ys = moe_experts(xs, meta, wg, wu, wd, tmg, f // 2)
    ya = ys.at[pos[:, 0]].get(mode="promise_in_bounds").reshape(bsz, seq, d)
    yb = ys.at[pos[:, 1]].get(mode="promise_in_bounds").reshape(bsz, seq, d)
    return combine_residual_ln(x, gate, route, ya, yb, ln_g, ln_b, alpha)


def _mixers(p_l, p_c, need_ctx_out, lru_conv_w, lru_conv_b, lru_wa, lru_ba, lru_wx, lru_bx, lru_lambda,
            na_rpb, ssd_conv_w, ssd_conv_b, ssd_dt_bias, ssd_a_log, ssd_d, ssd_norm_g, ml_conv_w, ml_conv_b,
            ml_i_bias, ml_f_bias):
    ya = lru_branch(p_l, p_c, lru_conv_w, lru_conv_b, lru_wa, lru_ba, lru_wx, lru_bx, lru_lambda, need_ctx_out)
    yb = na_branch(p_l, p_c, na_rpb, need_ctx_out)
    yc = ssd_branch(p_l, p_c, ssd_conv_w, ssd_conv_b, ssd_dt_bias, ssd_a_log, ssd_d, ssd_norm_g, need_ctx_out)
    yd = mlstm_branch(p_l, p_c, ml_conv_w, ml_conv_b, ml_i_bias, ml_f_bias, need_ctx_out)
    return (ya[0], yb[0], yc[0], yd[0]), (ya[1], yb[1], yc[1], yd[1])


def kernel(x, c, ctx, c_ctx, w_ada, b_ada, w_in, lru_conv_w, lru_conv_b, lru_wa, lru_ba, lru_wx, lru_bx, lru_lambda, na_rpb, ssd_conv_w, ssd_conv_b, ssd_dt_bias, ssd_a_log, ssd_d, ssd_norm_g, ml_conv_w, ml_conv_b, ml_i_bias, ml_f_bias, w_branch, w_out, ln_g, ln_b, ffn_w_gate, ffn_w_up, ffn_w_down, moe_w_router, moe_b_router, moe_w_gate, moe_w_up, moe_w_down):
    depth = w_in.shape[0]
    bsz, seq, d = x.shape
    alpha = (2.0 * depth) ** 0.25
    cvecs = jnp.zeros((8, d), F32).at[:bsz].set(c).at[bsz].set(c_ctx)
    mods = ada_modulation(cvecs, w_ada, b_ada)
    xl, xc = x, ctx
    for l in range(depth):
        need_ctx_out = l < depth - 1
        mod = mods[l]
        mod_l = jnp.split(mod[:bsz, None, :], 6, axis=-1)
        mod_c = jnp.split(jnp.broadcast_to(mod[bsz:bsz + 1, None, :], (bsz, 1, 6 * d)), 6, axis=-1)
        w_mix, w_gates = _pack_w_in(w_in[l])
        p_l, hb_l = mod_matmul(xl, mod_l[1], mod_l[0], w_mix)
        p_c, hb_c = mod_matmul(xc, mod_c[1], mod_c[0], w_mix)
        br_l, br_c = _mixers(p_l, p_c, need_ctx_out, lru_conv_w[l], lru_conv_b[l], lru_wa[l], lru_ba[l],
                             lru_wx[l], lru_bx[l], lru_lambda[l], na_rpb[l], ssd_conv_w[l], ssd_conv_b[l],
                             ssd_dt_bias[l], ssd_a_log[l], ssd_d[l], ssd_norm_g[l], ml_conv_w[l], ml_conv_b[l],
                             ml_i_bias[l], ml_f_bias[l])
        wb = w_branch[l].astype(BF16)
        wo = w_out[l].astype(BF16)
        xl = proj_residual_ln(merge_branches(hb_l, w_gates, br_l, wb), wo, xl, mod_l[2], ln_g[l, 0], ln_b[l, 0],
                              alpha)
        if need_ctx_out:
            xc = proj_residual_ln(merge_branches(hb_c, w_gates, br_c, wb), wo, xc, mod_c[2], ln_g[l, 0],
                                  ln_b[l, 0], alpha)
        j = l // 2
        if l % 2 == 0:
            wg = ffn_w_gate[j].astype(BF16)
            wu = ffn_w_up[j].astype(BF16)
            wd = ffn_w_down[j].astype(BF16)

            def ffn(h, m, wg=wg, wu=wu, wd=wd):
                return ffn_residual_ln(h, m[4], m[3], m[5], wg, wu, wd, ln_g[l, 1], ln_b[l, 1], alpha)
        else:
            def ffn(h, m, j=j):
                return moe_residual_ln(h, m[4], m[3], m[5], moe_w_router[j], moe_b_router[j], moe_w_gate[j],
                                       moe_w_up[j], moe_w_down[j], ln_g[l, 1], ln_b[l, 1], alpha)

        xl = ffn(xl, mod_l)
        if need_ctx_out:
            xc = ffn(xc, mod_c)
    return xl
```

```python
import functools
import math

import numpy as np
import jax
import jax.numpy as jnp
from jax import lax
from jax.experimental import pallas as pl
from jax.experimental.pallas import tpu as pltpu

F32 = jnp.float32
BF16 = jnp.bfloat16

D_MODEL = 2048
GRID_W = 64
N_BRANCH = 4
BRANCH_W = D_MODEL // N_BRANCH
CONV_W = 4
LN_EPS = 1e-5
LRU_BLOCKS = 8
LRU_BW = BRANCH_W // LRU_BLOCKS
LRU_C = 8.0
NA_HEADS = 8
NA_HD = BRANCH_W // NA_HEADS
NA_KH = 8
NA_KW = 16
SSD_HEADS = 8
SSD_HD = BRANCH_W // SSD_HEADS
SSD_GROUPS = 2
SSD_STATE = 64
SSD_CHUNK = 128
SSD_GN = SSD_GROUPS * SSD_STATE
ML_HEADS = 4
ML_HD = BRANCH_W // ML_HEADS
ML_CHUNK = 128
ROPE_BASE = 10000.0
N_EXPERTS = 8
TOP_K = 2

V7X_VMEM_LIMIT = 52 * 1024 * 1024

COL_LRU_X = 0
COL_LRU_G = 512
COL_NA = 1024
COL_SSD_Z = 2560
COL_SSD_X = 3072
COL_ML = 3584
COL_SSD_B = 5632
COL_SSD_C = 5760
COL_SSD_DT = 5888
COL_ML_G = 5904
N_MIX = 5920
N_MIX_PAD = 6144


def _pack_w_in(w):
    parts = [w[:, 0:1024], w[:, 1024:2560], w[:, 2560:3584], w[:, 3856:5904],
             w[:, 3584:3840], w[:, 3840:3856], w[:, 5904:5920],
             jnp.zeros((w.shape[0], N_MIX_PAD - N_MIX), w.dtype)]
    return jnp.concatenate(parts, axis=1).astype(BF16), w[:, N_MIX:].astype(BF16)


def _sigmoid(x):
    return 1.0 / (1.0 + jnp.exp(-x))


def _layer_norm_rows(z, g, b):
    mu = jnp.mean(z, axis=-1, keepdims=True)
    zc = z - mu
    var = jnp.mean(zc * zc, axis=-1, keepdims=True)
    return zc * lax.rsqrt(var + LN_EPS) * g + b


def _mod_matmul_kernel(x_ref, sc_ref, sh_ref, w_ref, o_ref, hb_ref):
    @pl.when(pl.program_id(2) == 0)
    def _():
        hb_ref[0] = (x_ref[0] * (1.0 + sc_ref[0]) + sh_ref[0]).astype(BF16)

    o_ref[0] = jnp.dot(hb_ref[0], w_ref[...], preferred_element_type=F32)


def mod_matmul(x, sc, sh, w, tm=1024, tn=1024):
    bsz, seq, d = x.shape
    n = w.shape[1]
    tm = min(tm, seq)
    return pl.pallas_call(
        _mod_matmul_kernel,
        grid=(bsz, seq // tm, n // tn),
        in_specs=[pl.BlockSpec((1, tm, d), lambda b, i, j: (b, i, 0)),
                  pl.BlockSpec((1, 1, d), lambda b, i, j: (b, 0, 0)),
                  pl.BlockSpec((1, 1, d), lambda b, i, j: (b, 0, 0)),
                  pl.BlockSpec((d, tn), lambda b, i, j: (0, j))],
        out_specs=[pl.BlockSpec((1, tm, tn), lambda b, i, j: (b, i, j)),
                   pl.BlockSpec((1, tm, d), lambda b, i, j: (b, i, 0))],
        out_shape=[jax.ShapeDtypeStruct((bsz, seq, n), F32), jax.ShapeDtypeStruct((bsz, seq, d), BF16)],
        compiler_params=pltpu.CompilerParams(
            dimension_semantics=("parallel", "parallel", "arbitrary"), vmem_limit_bytes=V7X_VMEM_LIMIT),
        name="in_proj",
    )(x, sc, sh, w)


def _merge_kernel(hb_ref, g0, g1, g2, g3, ya, yb, yc, yd, wb_ref, o_ref):
    hb = hb_ref[0]
    acc = None
    for n, (wg, y) in enumerate(((g0, ya), (g1, yb), (g2, yc), (g3, yd))):
        gate = _sigmoid(jnp.dot(hb, wg[...], preferred_element_type=F32))
        t = gate * jnp.dot(y[0].astype(BF16), wb_ref[n], preferred_element_type=F32)
        acc = t if acc is None else acc + t
    o_ref[0] = acc.astype(BF16)


def merge_branches(hb, w_gates, branches, wb, tm=512, tn=512):
    bsz, seq, dk = hb.shape
    d = wb.shape[2]
    tm = min(tm, seq)
    nj = d // tn
    g_specs = [pl.BlockSpec((dk, tn), functools.partial(lambda b, i, j, n: (0, n * nj + j), n=n))
               for n in range(N_BRANCH)]
    y_specs = [pl.BlockSpec((1, tm, BRANCH_W), lambda b, i, j: (b, i, 0)) for _ in range(N_BRANCH)]
    return pl.pallas_call(
        _merge_kernel,
        grid=(bsz, seq // tm, nj),
        in_specs=[pl.BlockSpec((1, tm, dk), lambda b, i, j: (b, i, 0))] + g_specs + y_specs
                 + [pl.BlockSpec((N_BRANCH, BRANCH_W, tn), lambda b, i, j: (0, 0, j))],
        out_specs=pl.BlockSpec((1, tm, tn), lambda b, i, j: (b, i, j)),
        out_shape=jax.ShapeDtypeStruct((bsz, seq, d), BF16),
        compiler_params=pltpu.CompilerParams(
            dimension_semantics=("parallel", "parallel", "arbitrary"), vmem_limit_bytes=V7X_VMEM_LIMIT),
        name="merge",
    )(hb, w_gates, w_gates, w_gates, w_gates, *branches, wb)


def _proj_ln_kernel(m_ref, w_ref, x_ref, gt_ref, g_ref, b_ref, o_ref, *, alpha):
    half = m_ref.shape[1] // 2
    ys = [jnp.dot(m_ref[0, n * half:(n + 1) * half, :], w_ref[...], preferred_element_type=F32) for n in range(2)]
    for n in range(2):
        rows = slice(n * half, (n + 1) * half)
        o_ref[0, rows, :] = _layer_norm_rows(alpha * x_ref[0, rows, :] + gt_ref[0] * ys[n], g_ref[...], b_ref[...])


def proj_residual_ln(m, w, x, gate, ln_g, ln_b, alpha, tm=512):
    bsz, seq, k = m.shape
    d = w.shape[1]
    tm = min(tm, seq)
    return pl.pallas_call(
        functools.partial(_proj_ln_kernel, alpha=alpha),
        grid=(bsz, seq // tm),
        in_specs=[pl.BlockSpec((1, tm, k), lambda b, i: (b, i, 0)),
                  pl.BlockSpec((k, d), lambda b, i: (0, 0)),
                  pl.BlockSpec((1, tm, d), lambda b, i: (b, i, 0)),
                  pl.BlockSpec((1, 1, d), lambda b, i: (b, 0, 0)),
                  pl.BlockSpec((1, d), lambda b, i: (0, 0)),
                  pl.BlockSpec((1, d), lambda b, i: (0, 0))],
        out_specs=pl.BlockSpec((1, tm, d), lambda b, i: (b, i, 0)),
        out_shape=jax.ShapeDtypeStruct((bsz, seq, d), F32),
        compiler_params=pltpu.CompilerParams(
            dimension_semantics=("parallel", "parallel"), vmem_limit_bytes=V7X_VMEM_LIMIT),
        name="out_proj_ln",
    )(m, w, x, gate, ln_g.reshape(1, d), ln_b.reshape(1, d))


def _ffn_kernel(x_ref, sc_ref, sh_ref, gt_ref, wg_ref, wu_ref, wd_ref, lg_ref, lb_ref, o_ref, hb_ref, acc_ref, *, alpha):
    j = pl.program_id(2)

    @pl.when(j == 0)
    def _():
        hb_ref[...] = (x_ref[0] * (1.0 + sc_ref[0]) + sh_ref[0]).astype(BF16)
        acc_ref[...] = jnp.zeros_like(acc_ref)

    hb = hb_ref[...]
    g = jnp.dot(hb, wg_ref[...], preferred_element_type=F32)
    u = jnp.dot(hb, wu_ref[...], preferred_element_type=F32)
    acc_ref[...] += jnp.dot((_silu(g) * u).astype(BF16), wd_ref[...], preferred_element_type=F32)

    @pl.when(j == pl.num_programs(2) - 1)
    def _():
        o_ref[0] = _layer_norm_rows(alpha * x_ref[0] + gt_ref[0] * acc_ref[...], lg_ref[...], lb_ref[...])


def ffn_residual_ln(x, sc, sh, gate, wg, wu, wd, ln_g, ln_b, alpha, tm=512, tf=512):
    bsz, seq, d = x.shape
    f = wg.shape[1]
    tm = min(tm, seq)
    vec = pl.BlockSpec((1, 1, d), lambda b, i, j: (b, 0, 0))
    par = pl.BlockSpec((1, d), lambda b, i, j: (0, 0))
    return pl.pallas_call(
        functools.partial(_ffn_kernel, alpha=alpha),
        grid=(bsz, seq // tm, f // tf),
        in_specs=[pl.BlockSpec((1, tm, d), lambda b, i, j: (b, i, 0)), vec, vec, vec,
                  pl.BlockSpec((d, tf), lambda b, i, j: (0, j)),
                  pl.BlockSpec((d, tf), lambda b, i, j: (0, j)),
                  pl.BlockSpec((tf, d), lambda b, i, j: (j, 0)), par, par],
        out_specs=pl.BlockSpec((1, tm, d), lambda b, i, j: (b, i, 0)),
        out_shape=jax.ShapeDtypeStruct((bsz, seq, d), F32),
        scratch_shapes=[pltpu.VMEM((tm, d), BF16), pltpu.VMEM((tm, d), F32)],
        compiler_params=pltpu.CompilerParams(
            dimension_semantics=("parallel", "parallel", "arbitrary"), vmem_limit_bytes=V7X_VMEM_LIMIT),
        name="ffn_ln",
    )(x, sc, sh, gate, wg, wu, wd, ln_g.reshape(1, d), ln_b.reshape(1, d))


def _ada_kernel(c_ref, w_ref, b_ref, o_ref):
    cv = c_ref[...]
    o_ref[0] = jnp.dot(cv * _sigmoid(cv), w_ref[0], precision=lax.Precision.HIGHEST,
                       preferred_element_type=F32) + b_ref[0]


def ada_modulation(cvecs, w, b, tn=1536):
    r, d = cvecs.shape
    depth, _, n = w.shape
    return pl.pallas_call(
        _ada_kernel,
        grid=(depth, n // tn),
        in_specs=[pl.BlockSpec((r, d), lambda l, j: (0, 0)), pl.BlockSpec((1, d, tn), lambda l, j: (l, 0, j)),
                  pl.BlockSpec((1, 1, tn), lambda l, j: (l, 0, j))],
        out_specs=pl.BlockSpec((1, r, tn), lambda l, j: (l, 0, j)),
        out_shape=jax.ShapeDtypeStruct((depth, r, n), F32),
        compiler_params=pltpu.CompilerParams(
            dimension_semantics=("parallel", "parallel"), vmem_limit_bytes=V7X_VMEM_LIMIT),
        name="ada_mod",
    )(cvecs, w, b.reshape(depth, 1, n))


NA_NEG = -1e30


def _na_bias_slabs(rpb):
    w = jnp.arange(GRID_W)
    cs = jnp.clip(w - NA_KW // 2, 0, GRID_W - NA_KW)
    ok = (w[None, :] >= cs[:, None]) & (w[None, :] < cs[:, None] + NA_KW)
    dc = jnp.clip(w[None, :] - w[:, None] + (NA_KW - 1), 0, 2 * NA_KW - 2)
    tab = jnp.where(ok, rpb[:, :, dc], NA_NEG)
    idx = jnp.arange(NA_KH)[:, None] + jnp.arange(NA_KH)[None, :]
    slab = tab[:, idx]
    return slab.transpose(1, 0, 3, 2, 4).reshape(NA_KH, NA_HEADS, GRID_W, NA_KH * GRID_W)


def _softmax2(s_a, s_b):
    m = jnp.maximum(jnp.max(s_a, axis=-1, keepdims=True), jnp.max(s_b, axis=-1, keepdims=True))
    e_a = jnp.exp(s_a - m)
    e_b = jnp.exp(s_b - m)
    inv = 1.0 / (jnp.sum(e_a, axis=-1, keepdims=True) + jnp.sum(e_b, axis=-1, keepdims=True))
    return e_a * inv, e_b * inv


_NT = (((1,), (1,)), ((), ()))


def _na_kernel(q_ref, kp_ref, kc_ref, kn_ref, vp_ref, vc_ref, vn_ref, ck_ref, cv_ref, bias_ref, o_ref,
               kw_ref, vw_ref, ckb_ref, cvb_ref, *, rows):
    i = pl.program_id(1)
    blk = NA_KH * GRID_W
    for n, (kr, vr) in enumerate(((kp_ref, vp_ref), (kc_ref, vc_ref), (kn_ref, vn_ref))):
        kw_ref[n * blk:(n + 1) * blk, :] = kr[0].astype(BF16)
        vw_ref[n * blk:(n + 1) * blk, :] = vr[0].astype(BF16)
    ckb_ref[...] = ck_ref[0].astype(BF16)
    cvb_ref[...] = cv_ref[0].astype(BF16)
    scale = NA_HD ** -0.5

    first = lax.broadcasted_iota(jnp.int32, (GRID_W, 2 * NA_HD), 1) < NA_HD
    pairs = [slice(n * 2 * NA_HD, (n + 1) * 2 * NA_HD) for n in range(NA_HEADS // 2)]
    rows_per_trip = 4

    def body(t, carry):
        trip = []
        for sub in range(rows_per_trip):
            rr = t * rows_per_trip + sub
            r = i * NA_KH + rr
            rs = jnp.clip(r - NA_KH // 2, 0, rows - NA_KH)
            off = pl.multiple_of((rs - (i - 1) * NA_KH) * GRID_W, GRID_W)
            v = rs - r + (NA_KH - 1)
            q_all = (q_ref[0, pl.ds(pl.multiple_of(rr * GRID_W, GRID_W), GRID_W), :] * scale).astype(BF16)
            trip.append((rr, v, q_all, kw_ref[pl.ds(off, blk), :], vw_ref[pl.ds(off, blk), :]))
        scores = []
        for rr, v, q_all, kwin, vwin in trip:
            for h in range(NA_HEADS):
                ps = pairs[h // 2]
                q = jnp.where(first if h % 2 == 0 else ~first, q_all[:, ps], jnp.zeros((), BF16))
                scores.append((lax.dot_general(q, kwin[:, ps], _NT, preferred_element_type=F32) + bias_ref[v, h],
                               lax.dot_general(q, ckb_ref[:, ps], _NT, preferred_element_type=F32)))
        probs = [_softmax2(s_w, s_c) for s_w, s_c in scores]
        both = [jnp.dot(p_w.astype(BF16), trip[n // NA_HEADS][4][:, pairs[(n % NA_HEADS) // 2]],
                        preferred_element_type=F32)
                + jnp.dot(p_c.astype(BF16), cvb_ref[:, pairs[(n % NA_HEADS) // 2]], preferred_element_type=F32)
                for n, (p_w, p_c) in enumerate(probs)]
        for sub, (rr, _, _, _, _) in enumerate(trip):
            base = sub * NA_HEADS
            outs = [jnp.where(first, both[base + 2 * n], both[base + 2 * n + 1]) for n in range(NA_HEADS // 2)]
            o_ref[0, pl.ds(pl.multiple_of(rr * GRID_W, GRID_W), GRID_W), :] = jnp.concatenate(outs, axis=-1)
        return carry

    lax.fori_loop(0, NA_KH // rows_per_trip, body, 0)


def _na_ctx_kernel(q_ref, k_ref, v_ref, o_ref):
    scale = NA_HD ** -0.5
    q_all = (q_ref[0] * scale).astype(BF16)
    k_all = k_ref[0].astype(BF16)
    v_all = v_ref[0].astype(BF16)
    outs = []
    for h in range(NA_HEADS):
        hs = slice(h * NA_HD, (h + 1) * NA_HD)
        s = lax.dot_general(q_all[:, hs], k_all[:, hs], _NT, preferred_element_type=F32)
        e = jnp.exp(s - jnp.max(s, axis=-1, keepdims=True))
        p = e * (1.0 / jnp.sum(e, axis=-1, keepdims=True))
        outs.append(jnp.dot(p.astype(BF16), v_all[:, hs], preferred_element_type=F32))
    o_ref[0] = jnp.concatenate(outs, axis=-1)


def na_branch(p_l, p_c, rpb, need_ctx_out):
    bsz, seq, _ = p_l.shape
    n_ctx = p_c.shape[1]
    rows = seq // GRID_W
    assert rows % NA_KH == 0 and rows >= 2 * NA_KH
    w = BRANCH_W
    blk = NA_KH * GRID_W
    nblk = rows // NA_KH
    cq = COL_NA // w

    def shifted(col, delta):
        return pl.BlockSpec((1, blk, w), lambda b, i: (b, jnp.clip(i + delta, 0, nblk - 1), col))

    y_l = pl.pallas_call(
        functools.partial(_na_kernel, rows=rows),
        grid=(bsz, nblk),
        in_specs=[shifted(cq, 0), shifted(cq + 1, -1), shifted(cq + 1, 0), shifted(cq + 1, 1),
                  shifted(cq + 2, -1), shifted(cq + 2, 0), shifted(cq + 2, 1),
                  pl.BlockSpec((1, n_ctx, w), lambda b, i: (b, 0, cq + 1)),
                  pl.BlockSpec((1, n_ctx, w), lambda b, i: (b, 0, cq + 2)),
                  pl.BlockSpec((NA_KH, NA_HEADS, GRID_W, blk), lambda b, i: (0, 0, 0, 0))],
        out_specs=pl.BlockSpec((1, blk, w), lambda b, i: (b, i, 0)),
        out_shape=jax.ShapeDtypeStruct((bsz, seq, w), F32),
        scratch_shapes=[pltpu.VMEM((3 * blk, w), BF16), pltpu.VMEM((3 * blk, w), BF16),
                        pltpu.VMEM((n_ctx, w), BF16), pltpu.VMEM((n_ctx, w), BF16)],
        compiler_params=pltpu.CompilerParams(
            dimension_semantics=("parallel", "arbitrary"), vmem_limit_bytes=V7X_VMEM_LIMIT),
        name="na_attn",
    )(p_l, p_l, p_l, p_l, p_l, p_l, p_l, p_c, p_c, _na_bias_slabs(rpb))
    y_c = None
    if need_ctx_out:
        y_c = pl.pallas_call(
            _na_ctx_kernel,
            grid=(bsz,),
            in_specs=[pl.BlockSpec((1, n_ctx, w), functools.partial(lambda b, c: (b, 0, c), c=cq + n)) for n in range(3)],
            out_specs=pl.BlockSpec((1, n_ctx, w), lambda b: (b, 0, 0)),
            out_shape=jax.ShapeDtypeStruct((bsz, n_ctx, w), F32),
            compiler_params=pltpu.CompilerParams(dimension_semantics=("parallel",)),
            name="na_ctx_attn",
        )(p_c, p_c, p_c)
    return y_l, y_c


def _gelu_tanh(x):
    return 0.5 * x * (1.0 + jnp.tanh(math.sqrt(2.0 / math.pi) * (x + 0.044715 * (x * x * x))))


def _silu(x):
    return x * _sigmoid(x)


def _softplus(x):
    return jnp.maximum(x, 0.0) + jnp.log1p(jnp.exp(-jnp.abs(x)))


def _rope_2d_tables(seq):
    t = jnp.arange(seq, dtype=jnp.int32)
    pos = jnp.stack([t // GRID_W, t % GRID_W], axis=-1).astype(F32)
    nf = ML_HD // 4
    inv_freq = ROPE_BASE ** (-jnp.arange(nf, dtype=F32) / nf)
    ang = jnp.broadcast_to(pos[:, :, None, None] * inv_freq, (seq, 2, 2, nf)).reshape(seq, ML_HD)
    return jnp.cos(ang), jnp.sin(ang)


def _chunk_specs(tt, nchunk, n8, reverse, nb=1):
    def pos(j):
        return nchunk - 1 - j if reverse else j

    def chunk(width, col):
        return pl.BlockSpec((nb, tt, width), lambda b, j: (b, pos(j), col))

    def halo(width, col, delta):
        if delta < 0:
            return pl.BlockSpec((nb, 8, width), lambda b, j: (b, jnp.maximum(pos(j) * (tt // 8) - 1, 0), col))
        return pl.BlockSpec((nb, 8, width), lambda b, j: (b, jnp.minimum((pos(j) + 1) * (tt // 8), n8 - 1), col))

    return chunk, halo


def _const_spec(shape):
    return pl.BlockSpec(shape, lambda b, j: (0,) * len(shape))


def _batch_spec(shape, nb=1):
    return pl.BlockSpec((nb,) + shape, lambda b, j: (b,) + (0,) * len(shape))


def _roll_in_blocks(x, shift):
    rows, w = x.shape
    return pltpu.roll(x.reshape(rows // 8, 8, w), shift, 1).reshape(rows, w)


def _conv4(x, prev8, next8, has_prev, has_next, cw, cb):
    tt = x.shape[0]
    row = lax.broadcasted_iota(jnp.int32, x.shape, 0)
    p6 = prev8[6:7, :] * has_prev
    p7 = prev8[7:8, :] * has_prev
    n0 = next8[0:1, :] * has_next
    xm1 = jnp.where(row == 0, p7, pltpu.roll(x, 1, 0))
    xm2 = jnp.where(row == 0, p6, jnp.where(row == 1, p7, pltpu.roll(x, 2, 0)))
    xp1 = jnp.where(row == tt - 1, n0, pltpu.roll(x, tt - 1, 0))
    return cw[0:1, :] * xm2 + cw[1:2, :] * xm1 + cw[2:3, :] * x + cw[3:4, :] * xp1 + cb


def _cumsum_rows(x, reverse):
    n = x.shape[0]
    row = lax.broadcasted_iota(jnp.int32, x.shape, 0)
    s = 1
    while s < n:
        if reverse:
            x = x + jnp.where(row < n - s, pltpu.roll(x, n - s, 0), 0.0)
        else:
            x = x + jnp.where(row >= s, pltpu.roll(x, s, 0), 0.0)
        s *= 2
    return x


def _causal_mask(n, reverse):
    ii = lax.broadcasted_iota(jnp.int32, (n, n), 0)
    jj = lax.broadcasted_iota(jnp.int32, (n, n), 1)
    return (jj >= ii) if reverse else (jj <= ii)


MASK_NEG = -1e30


LRU_TT = 256


def _lru_kernel(*refs, reverse, finalize, nchunk):
    if finalize:
        (xp_ref, x_ref, xn_ref, g_ref, ho_ref, h0_ref, cw_ref, cb_ref, w_ref, bias_ref, sp_ref,
         o_ref, hl_ref, carry_ref) = refs
    else:
        (xp_ref, x_ref, xn_ref, h0_ref, cw_ref, cb_ref, w_ref, bias_ref, sp_ref, o_ref, hl_ref, carry_ref) = refs
    j = pl.program_id(1)
    c = (nchunk - 1 - j) if reverse else j

    @pl.when(j == 0)
    def _():
        carry_ref[...] = h0_ref[0]

    tt = x_ref.shape[1]
    w = x_ref.shape[2]
    xc = _conv4(x_ref[0], xp_ref[0], xn_ref[0], (c > 0).astype(F32), (c < nchunk - 1).astype(F32),
                cw_ref[...], cb_ref[...])
    g = jnp.dot(xc.astype(BF16), w_ref[...], preferred_element_type=F32)
    r = _sigmoid(g[:, :w] + bias_ref[0:1, :])
    ig = _sigmoid(g[:, w:] + bias_ref[1:2, :])
    log_a = -LRU_C * r * sp_ref[...]
    a = jnp.exp(log_a)
    u = jnp.sqrt(1.0 - jnp.exp(2.0 * log_a)) * (ig * xc)
    sub = lax.broadcasted_iota(jnp.int32, (tt, w), 0) % 8
    s = 1
    while s < 8:
        if reverse:
            keep = sub < 8 - s
            a_s = jnp.where(keep, _roll_in_blocks(a, 8 - s), 1.0)
            u_s = jnp.where(keep, _roll_in_blocks(u, 8 - s), 0.0)
        else:
            keep = sub >= s
            a_s = jnp.where(keep, _roll_in_blocks(a, s), 1.0)
            u_s = jnp.where(keep, _roll_in_blocks(u, s), 0.0)
        u = a * u_s + u
        a = a * a_s
        s *= 2
    new_carry = carry_ref[...]
    nblk = tt // 8
    hs = [None] * nblk
    for blk in (reversed(range(nblk)) if reverse else range(nblk)):
        hb = u[blk * 8:(blk + 1) * 8, :] + a[blk * 8:(blk + 1) * 8, :] * new_carry
        new_carry = hb[0:1, :] if reverse else hb[7:8, :]
        hs[blk] = hb
    h = jnp.concatenate(hs, axis=0)
    carry_ref[...] = new_carry
    hl_ref[0] = new_carry
    if finalize:
        o_ref[0] = (ho_ref[0] + h) * _gelu_tanh(g_ref[0])
    else:
        o_ref[0] = h


def _lru_pass(p, h0, other, cw, cb, wcat, bias, sp, reverse):
    bsz, seq, _ = p.shape
    w = BRANCH_W
    tt = min(LRU_TT, seq)
    nchunk = seq // tt
    chunk, halo = _chunk_specs(tt, nchunk, seq // 8, reverse)
    cx = COL_LRU_X // w
    finalize = other is not None
    in_specs = [halo(w, cx, -1), chunk(w, cx), halo(w, cx, 1)]
    args = [p, p, p]
    if finalize:
        in_specs += [chunk(w, COL_LRU_G // w), chunk(w, 0)]
        args += [p, other]
    in_specs += [_batch_spec((1, w)), _const_spec((CONV_W, w)), _const_spec((1, w)),
                 _const_spec((w, 2 * w)), _const_spec((2, w)), _const_spec((1, w))]
    args += [h0, cw, cb, wcat, bias, sp]
    return pl.pallas_call(
        functools.partial(_lru_kernel, reverse=reverse, finalize=finalize, nchunk=nchunk),
        grid=(bsz, nchunk),
        in_specs=in_specs,
        out_specs=[chunk(w, 0), _batch_spec((1, w))],
        out_shape=[jax.ShapeDtypeStruct((bsz, seq, w), F32), jax.ShapeDtypeStruct((bsz, 1, w), F32)],
        scratch_shapes=[pltpu.VMEM((1, w), F32)],
        compiler_params=pltpu.CompilerParams(dimension_semantics=("parallel", "arbitrary")),
        name="lru_scan",
    )(*args)


def _block_diag(wg):
    g, n, _ = wg.shape
    eye = jnp.eye(g, dtype=wg.dtype)
    return (wg[:, :, None, :] * eye[:, None, :, None]).reshape(g * n, g * n)


def lru_branch(p_l, p_c, conv_w, conv_b, wa, ba, wx, bx, lam, need_ctx_out):
    bsz = p_l.shape[0]
    w = BRANCH_W
    cb = conv_b.reshape(1, w)
    sp = jax.nn.softplus(-lam)
    zeros = jnp.zeros((bsz, 1, w), F32)
    h_c = h_l = None
    for d, reverse in ((0, False), (1, True)):
        wcat = jnp.concatenate([_block_diag(wa[d]), _block_diag(wx[d])], axis=1).astype(BF16)
        bias = jnp.stack([ba[d], bx[d]])
        other_c = h_c if (d == 1 and need_ctx_out) else None
        h_c, st = _lru_pass(p_c, zeros, other_c, conv_w, cb, wcat, bias, sp[d:d + 1], reverse)
        h_l, _ = _lru_pass(p_l, st, h_l if d == 1 else None, conv_w, cb, wcat, bias, sp[d:d + 1], reverse)
    return h_l, (h_c if need_ctx_out else None)


def _ssd_kernel(*refs, reverse, finalize, nchunk, d):
    (xp_ref, x_ref, xn_ref, bp_ref, bc_ref, bn_ref, dt_ref) = refs[:7]
    k = 7
    if finalize:
        z_ref, yp_ref = refs[k:k + 2]
        k += 2
    s0_ref, cwx_ref, cbx_ref, cwb_ref, cbb_ref, dtb_ref, a_ref = refs[k:k + 7]
    k += 7
    if finalize:
        dsk_ref, ng_ref = refs[k:k + 2]
        k += 2
    o_ref, so_ref, s_ref = refs[k:k + 3]
    j = pl.program_id(1)
    c = (nchunk - 1 - j) if reverse else j

    @pl.when(j == 0)
    def _():
        s_ref[...] = s0_ref[...]

    nb = x_ref.shape[0]
    q = x_ref.shape[1]
    has_prev = (c > 0).astype(F32)
    has_next = (c < nchunk - 1).astype(F32)
    mask = _causal_mask(q, reverse)
    first = lax.broadcasted_iota(jnp.int32, (q, 2 * SSD_HD), 1) < SSD_HD
    npair = SSD_HEADS // 2
    pair_group = [(2 * n) // (SSD_HEADS // SSD_GROUPS) for n in range(npair)]
    pairs = [slice(n * 2 * SSD_HD, (n + 1) * 2 * SSD_HD) for n in range(npair)]

    def per_lane(t, n):
        c0 = d * SSD_HEADS + 2 * n
        return jnp.where(first[:t.shape[0]], t[:, c0:c0 + 1], t[:, c0 + 1:c0 + 2])

    xs, xb, dt_t, cum, cum_t, w_end, ecum, etot, b_t, cgs, cbs = [], [], [], [], [], [], [], [], [], [], []
    for bi in range(nb):
        xs.append(_silu(_conv4(x_ref[bi], xp_ref[bi], xn_ref[bi], has_prev, has_next, cwx_ref[...], cbx_ref[...])))
        bc = _silu(_conv4(bc_ref[bi], bp_ref[bi], bn_ref[bi], has_prev, has_next, cwb_ref[...], cbb_ref[...]))
        dt = _softplus(dt_ref[bi] + dtb_ref[...])
        cum.append(_cumsum_rows(dt * a_ref[...], reverse))
        tot = cum[bi][0:1, :] if reverse else cum[bi][q - 1:q, :]
        w_end.append(jnp.exp(tot - cum[bi]) * dt)
        ecum.append(jnp.exp(cum[bi]))
        etot.append(jnp.exp(tot))
        dt_t.append(dt.T)
        cum_t.append(cum[bi].T)
        b_t.append(bc[:, :SSD_GN].T)
        xb.append(xs[bi].astype(BF16))
        cgs.append([bc[:, SSD_GN + g * SSD_STATE:SSD_GN + (g + 1) * SSD_STATE].astype(BF16)
                    for g in range(SSD_GROUPS)])
        cbs.append([lax.dot_general(cgs[bi][g], bc[:, g * SSD_STATE:(g + 1) * SSD_STATE].astype(BF16), _NT,
                                    preferred_element_type=F32) for g in range(SSD_GROUPS)])
    intra = []
    for bi in range(nb):
        for h in range(SSD_HEADS):
            col = d * SSD_HEADS + h
            diff = cum[bi][:, col:col + 1] - cum_t[bi][col:col + 1, :]
            m = (cbs[bi][pair_group[h // 2]] * jnp.exp(jnp.where(mask, diff, MASK_NEG))
                 * dt_t[bi][col:col + 1, :])
            intra.append(jnp.dot(m.astype(BF16), xb[bi][:, pairs[h // 2]], preferred_element_type=F32))
    items = [(bi, n) for bi in range(nb) for n in range(npair)]
    s_old = [s_ref[bi, n] for bi, n in items]
    inter = [jnp.dot(cgs[bi][pair_group[n]], s_old[t].astype(BF16), preferred_element_type=F32)
             for t, (bi, n) in enumerate(items)]
    outs = []
    for t, (bi, n) in enumerate(items):
        g = pair_group[n]
        h0 = bi * SSD_HEADS + 2 * n
        outs.append(jnp.where(first, intra[h0], intra[h0 + 1]) + per_lane(ecum[bi], n) * inter[t])
        xw = (xs[bi][:, pairs[n]] * per_lane(w_end[bi], n)).astype(BF16)
        s_ref[bi, n] = per_lane(etot[bi], n) * s_old[t] + jnp.dot(
            b_t[bi][g * SSD_STATE:(g + 1) * SSD_STATE, :].astype(BF16), xw, preferred_element_type=F32)
    so_ref[...] = s_ref[...]
    for bi in range(nb):
        y = jnp.concatenate(outs[bi * npair:(bi + 1) * npair], axis=-1)
        if finalize:
            yt = (xs[bi] * dsk_ref[...] + yp_ref[bi] + y) * _silu(z_ref[bi])
            o_ref[bi] = yt * lax.rsqrt(jnp.mean(yt * yt, axis=-1, keepdims=True) + LN_EPS) * ng_ref[...]
        else:
            o_ref[bi] = y


def _ssd_pass(p, s0, other, params, reverse, d):
    bsz, seq, _ = p.shape
    w = BRANCH_W
    q = SSD_CHUNK
    nchunk = seq // q
    chunk, halo = _chunk_specs(q, nchunk, seq // 8, reverse, nb=bsz)
    cx, cb2, cdt = COL_SSD_X // w, COL_SSD_B // (2 * SSD_GN), COL_SSD_DT // 128
    finalize = other is not None
    cwx, cbx, cwb, cbb, dtb, arow, dsk, ng = params
    in_specs = [halo(w, cx, -1), chunk(w, cx), halo(w, cx, 1),
                halo(2 * SSD_GN, cb2, -1), chunk(2 * SSD_GN, cb2), halo(2 * SSD_GN, cb2, 1), chunk(128, cdt)]
    args = [p] * 7
    if finalize:
        in_specs += [chunk(w, COL_SSD_Z // w), chunk(w, 0)]
        args += [p, other]
    st_shape = (SSD_HEADS // 2, SSD_STATE, 2 * SSD_HD)
    in_specs += [_batch_spec(st_shape, bsz), _const_spec((CONV_W, w)), _const_spec((1, w)),
                 _const_spec((CONV_W, 2 * SSD_GN)), _const_spec((1, 2 * SSD_GN)), _const_spec((1, 128)),
                 _const_spec((1, 128))]
    args += [s0, cwx, cbx, cwb, cbb, dtb, arow]
    if finalize:
        in_specs += [_const_spec((1, w)), _const_spec((1, w))]
        args += [dsk, ng]
    return pl.pallas_call(
        functools.partial(_ssd_kernel, reverse=reverse, finalize=finalize, nchunk=nchunk, d=d),
        grid=(1, nchunk),
        in_specs=in_specs,
        out_specs=[chunk(w, 0), _batch_spec(st_shape, bsz)],
        out_shape=[jax.ShapeDtypeStruct((bsz, seq, w), F32), jax.ShapeDtypeStruct((bsz,) + st_shape, F32)],
        scratch_shapes=[pltpu.VMEM((bsz,) + st_shape, F32)],
        compiler_params=pltpu.CompilerParams(dimension_semantics=("parallel", "arbitrary")),
        name="ssd_scan",
    )(*args)


def _lane_row(vals, start):
    return jnp.zeros((128,), F32).at[start:start + vals.shape[0]].set(vals.astype(F32)).reshape(1, 128)


def ssd_branch(p_l, p_c, conv_w, conv_b, dt_bias, a_log, d_skip, norm_g, need_ctx_out):
    bsz = p_l.shape[0]
    w = BRANCH_W
    params = (conv_w[:, :w], conv_b[:w].reshape(1, w), conv_w[:, w:], conv_b[w:].reshape(1, 2 * SSD_GN),
              _lane_row(dt_bias.reshape(-1), 0), _lane_row(-jnp.exp(a_log.astype(F32)).reshape(-1), 0),
              jnp.repeat(d_skip, SSD_HD).reshape(1, w), norm_g.reshape(1, w))
    zeros = jnp.zeros((bsz, SSD_HEADS // 2, SSD_STATE, 2 * SSD_HD), F32)
    y_c = y_l = None
    for d, reverse in ((0, False), (1, True)):
        other_c = y_c if (d == 1 and need_ctx_out) else None
        y_c, st = _ssd_pass(p_c, zeros, other_c, params, reverse, d)
        y_l, _ = _ssd_pass(p_l, st, y_l if d == 1 else None, params, reverse, d)
    return y_l, (y_c if need_ctx_out else None)


def _rope_rotate(x):
    wl = x.shape[-1]
    half = ML_HD // 4
    lane = lax.broadcasted_iota(jnp.int32, x.shape, 1)
    return jnp.where(lane % (2 * half) < half, -pltpu.roll(x, wl - half, 1), pltpu.roll(x, half, 1))


def _log_sigmoid(x):
    return jnp.minimum(x, 0.0) - jnp.log1p(jnp.exp(-jnp.abs(x)))


def _mlstm_kernel(*refs, reverse, finalize, rope, nchunk, d):
    (qp_ref, q_ref, qn_ref, kp_ref, k_ref, kn_ref, v_ref, g_ref) = refs[:8]
    n = 8
    if rope:
        cos_ref, sin_ref = refs[n:n + 2]
        n += 2
    if finalize:
        og_ref, hp_ref = refs[n:n + 2]
        n += 2
    c0_ref, n0_ref, m0_ref, cwq_ref, cbq_ref, cwk_ref, cbk_ref, ib_ref, fb_ref = refs[n:n + 9]
    n += 9
    o_ref, co_ref, no_ref, mo_ref, c_ref, n_ref, m_ref = refs[n:n + 7]
    j = pl.program_id(1)
    c = (nchunk - 1 - j) if reverse else j

    @pl.when(j == 0)
    def _():
        c_ref[...] = c0_ref[...]
        n_ref[...] = n0_ref[...]
        m_ref[...] = m0_ref[...]

    nb = q_ref.shape[0]
    qn = q_ref.shape[1]
    has_prev = (c > 0).astype(F32)
    has_next = (c < nchunk - 1).astype(F32)
    mask = _causal_mask(qn, reverse)
    heads = [slice(h * ML_HD, (h + 1) * ML_HD) for h in range(ML_HEADS)]
    items = [(bi, h) for bi in range(nb) for h in range(ML_HEADS)]
    q, k, v, b, li, tot, b_t, li_t = [], [], [], [], [], [], [], []
    for bi in range(nb):
        qi = _silu(_conv4(q_ref[bi], qp_ref[bi], qn_ref[bi], has_prev, has_next, cwq_ref[...], cbq_ref[...]))
        ki = _silu(_conv4(k_ref[bi], kp_ref[bi], kn_ref[bi], has_prev, has_next, cwk_ref[...], cbk_ref[...]))
        if rope:
            qi = qi * cos_ref[...] + _rope_rotate(qi) * sin_ref[...]
            ki = ki * cos_ref[...] + _rope_rotate(ki) * sin_ref[...]
        q.append(qi * (ML_HD ** -0.5))
        k.append(ki)
        v.append(v_ref[bi])
        gb = g_ref[bi]
        li.append(gb + ib_ref[...])
        b.append(_cumsum_rows(_log_sigmoid(gb + fb_ref[...]), reverse))
        tot.append(b[bi][0:1, :] if reverse else b[bi][qn - 1:qn, :])
        b_t.append(b[bi].T)
        li_t.append(li[bi].T)
    qb = [t.astype(BF16) for t in q]
    kb = [t.astype(BF16) for t in k]
    vb = [t.astype(BF16) for t in v]
    qk = [lax.dot_general(qb[bi][:, heads[h]], kb[bi][:, heads[h]], _NT, preferred_element_type=F32)
          for bi, h in items]
    c_old = [c_ref[bi, h] for bi, h in items]
    n_old = [n_ref[bi, h:h + 1, :] for bi, h in items]
    qc = [lax.dot_general(qb[bi][:, heads[h]], c_old[t].astype(BF16), _NT, preferred_element_type=F32)
          for t, (bi, h) in enumerate(items)]
    gate = []
    for bi, h in items:
        ci = 4 * ML_HEADS + d * 2 * ML_HEADS + h
        cf = ci + ML_HEADS
        b_c = b[bi][:, cf:cf + 1]
        b_end = tot[bi][:, cf:cf + 1]
        m_st = m_ref[bi, h:h + 1, 0:1]
        end_log = b_end - b_c + li[bi][:, ci:ci + 1]
        m_new = jnp.maximum(b_end + m_st, jnp.max(end_log, axis=0, keepdims=True))
        dlog = jnp.where(mask, b_c - b_t[bi][cf:cf + 1, :] + li_t[bi][ci:ci + 1, :], MASK_NEG)
        m_inter = b_c + m_st
        m_i = jnp.maximum(jnp.max(dlog, axis=1, keepdims=True), m_inter)
        gate.append((jnp.exp(end_log - m_new), jnp.exp(b_end + m_st - m_new), m_new,
                     jnp.exp(dlog - m_i), jnp.exp(m_inter - m_i), jnp.exp(-m_i)))
    s_all = [qk[t] * gate[t][3] for t in range(len(items))]
    sv = [jnp.dot(s_all[t].astype(BF16), vb[bi][:, heads[h]], preferred_element_type=F32)
          for t, (bi, h) in enumerate(items)]
    upd = [jnp.dot((v[bi][:, heads[h]] * gate[t][0]).T.astype(BF16), kb[bi][:, heads[h]],
                   preferred_element_type=F32) for t, (bi, h) in enumerate(items)]
    outs = []
    for t, (bi, h) in enumerate(items):
        hs = heads[h]
        w, carry_scale, m_new, _, w_in, floor = gate[t]
        num = sv[t] + w_in * qc[t]
        den = (jnp.sum(s_all[t], axis=1, keepdims=True)
               + w_in * jnp.sum(q[bi][:, hs] * n_old[t], axis=1, keepdims=True))
        outs.append(num / jnp.maximum(jnp.abs(den), floor))
        c_ref[bi, h] = carry_scale * c_old[t] + upd[t]
        n_ref[bi, h:h + 1, :] = carry_scale * n_old[t] + jnp.sum(k[bi][:, hs] * w, axis=0, keepdims=True)
        m_ref[bi, h:h + 1, :] = jnp.broadcast_to(m_new, (1, ML_HD))
    co_ref[...] = c_ref[...]
    no_ref[...] = n_ref[...]
    mo_ref[...] = m_ref[...]
    for bi in range(nb):
        hout = jnp.concatenate(outs[bi * ML_HEADS:(bi + 1) * ML_HEADS], axis=-1)
        if finalize:
            o_ref[bi] = _sigmoid(og_ref[bi]) * (hp_ref[bi] + hout)
        else:
            o_ref[bi] = hout


def _mlstm_pass(p, state, other, tables, params, reverse, d):
    bsz, seq, _ = p.shape
    w = BRANCH_W
    qn = ML_CHUNK
    nchunk = seq // qn
    chunk, halo = _chunk_specs(qn, nchunk, seq // 8, reverse, nb=bsz)
    cq = COL_ML // w
    finalize = other is not None
    rope = tables is not None
    in_specs = [halo(w, cq, -1), chunk(w, cq), halo(w, cq, 1), halo(w, cq + 1, -1), chunk(w, cq + 1),
                halo(w, cq + 1, 1), chunk(w, cq + 2), chunk(128, COL_ML_G // 128)]
    args = [p] * 8
    if rope:
        tab = pl.BlockSpec((qn, w), (lambda b, j: (nchunk - 1 - j, 0)) if reverse else (lambda b, j: (j, 0)))
        in_specs += [tab, tab]
        args += list(tables)
    if finalize:
        in_specs += [chunk(w, cq + 3), chunk(w, 0)]
        args += [p, other]
    st_shapes = [(ML_HEADS, ML_HD, ML_HD), (ML_HEADS, ML_HD), (ML_HEADS, ML_HD)]
    in_specs += [_batch_spec(s, bsz) for s in st_shapes]
    in_specs += [_const_spec((CONV_W, w)), _const_spec((1, w)), _const_spec((CONV_W, w)), _const_spec((1, w)),
                 _const_spec((1, 128)), _const_spec((1, 128))]
    args += list(state) + list(params)
    res = pl.pallas_call(
        functools.partial(_mlstm_kernel, reverse=reverse, finalize=finalize, rope=rope, nchunk=nchunk, d=d),
        grid=(1, nchunk),
        in_specs=in_specs,
        out_specs=[chunk(w, 0)] + [_batch_spec(s, bsz) for s in st_shapes],
        out_shape=[jax.ShapeDtypeStruct((bsz, seq, w), F32)]
                  + [jax.ShapeDtypeStruct((bsz,) + s, F32) for s in st_shapes],
        scratch_shapes=[pltpu.VMEM((bsz,) + s, F32) for s in st_shapes],
        compiler_params=pltpu.CompilerParams(dimension_semantics=("parallel", "arbitrary")),
        name="mlstm_scan",
    )(*args)
    return res[0], tuple(res[1:])


def mlstm_branch(p_l, p_c, conv_w, conv_b, i_bias, f_bias, need_ctx_out):
    bsz, seq, _ = p_l.shape
    w = BRANCH_W
    cos, sin = _rope_2d_tables(seq)
    tables = (jnp.tile(cos, (1, ML_HEADS)), jnp.tile(sin, (1, ML_HEADS)))
    zero_h = jnp.zeros_like(i_bias)
    ib = _lane_row(jnp.concatenate([i_bias, zero_h], axis=1).reshape(-1), 4 * ML_HEADS)
    fb = _lane_row(jnp.concatenate([zero_h, f_bias], axis=1).reshape(-1), 4 * ML_HEADS)
    params = (conv_w[:, :w], conv_b[:w].reshape(1, w), conv_w[:, w:], conv_b[w:].reshape(1, w), ib, fb)
    state0 = (jnp.zeros((bsz, ML_HEADS, ML_HD, ML_HD), F32), jnp.zeros((bsz, ML_HEADS, ML_HD), F32),
              jnp.zeros((bsz, ML_HEADS, ML_HD), F32))
    h_c = h_l = None
    for d, reverse in ((0, False), (1, True)):
        other_c = h_c if (d == 1 and need_ctx_out) else None
        h_c, st = _mlstm_pass(p_c, state0, other_c, None, params, reverse, d)
        h_l, _ = _mlstm_pass(p_l, st, h_l if d == 1 else None, tables, params, reverse, d)
    return h_l, (h_c if need_ctx_out else None)


MOE_TM = 512
ROUTE_LANES = 128


def _moe_router_kernel(x_ref, sc_ref, sh_ref, wr_ref, br_ref, hb_ref, rt_ref):
    h = x_ref[0] * (1.0 + sc_ref[0]) + sh_ref[0]
    hb_ref[0] = h.astype(BF16)
    logits = jnp.dot(h, wr_ref[...], precision=lax.Precision.HIGHEST, preferred_element_type=F32) + br_ref[...]
    lane = lax.broadcasted_iota(jnp.int32, logits.shape, 1)
    m1 = jnp.max(logits, axis=-1, keepdims=True)
    i1 = jnp.min(jnp.where(logits == m1, lane, ROUTE_LANES), axis=-1, keepdims=True)
    rest = jnp.where(lane == i1, MASK_NEG, logits)
    m2 = jnp.max(rest, axis=-1, keepdims=True)
    i2 = jnp.min(jnp.where(rest == m2, lane, ROUTE_LANES), axis=-1, keepdims=True)
    e2 = jnp.exp(m2 - m1)
    p1 = 1.0 / (1.0 + e2)
    p2 = e2 * p1
    rt_ref[0] = jnp.where(lane == 0, i1.astype(F32), jnp.where(lane == 1, i2.astype(F32),
                          jnp.where(lane == 2, p1, jnp.where(lane == 3, p2, 0.0))))


def moe_router(x, sc, sh, w_router, b_router, tm=512):
    bsz, seq, d = x.shape
    n_e = w_router.shape[1]
    tm = min(tm, seq)
    wr = jnp.zeros((d, ROUTE_LANES), F32).at[:, :n_e].set(w_router)
    br = jnp.full((1, ROUTE_LANES), MASK_NEG, F32).at[0, :n_e].set(b_router)
    return pl.pallas_call(
        _moe_router_kernel,
        grid=(bsz, seq // tm),
        in_specs=[pl.BlockSpec((1, tm, d), lambda b, i: (b, i, 0)),
                  pl.BlockSpec((1, 1, d), lambda b, i: (b, 0, 0)),
                  pl.BlockSpec((1, 1, d), lambda b, i: (b, 0, 0)),
                  pl.BlockSpec((d, ROUTE_LANES), lambda b, i: (0, 0)),
                  pl.BlockSpec((1, ROUTE_LANES), lambda b, i: (0, 0))],
        out_specs=[pl.BlockSpec((1, tm, d), lambda b, i: (b, i, 0)),
                   pl.BlockSpec((1, tm, ROUTE_LANES), lambda b, i: (b, i, 0))],
        out_shape=[jax.ShapeDtypeStruct((bsz, seq, d), BF16), jax.ShapeDtypeStruct((bsz, seq, ROUTE_LANES), F32)],
        compiler_params=pltpu.CompilerParams(dimension_semantics=("parallel", "parallel")),
        name="moe_router",
    )(x, sc, sh, wr, br)


def _route_tables(idx, n_e, tmg):
    n_tok = idx.shape[0]
    e_flat = idx.reshape(-1)
    onehot = (e_flat[:, None] == jnp.arange(n_e, dtype=jnp.int32)[None, :]).astype(jnp.int32)
    csum = jnp.cumsum(onehot, axis=0)
    rank = jnp.take_along_axis(csum - onehot, e_flat[:, None], axis=1)[:, 0]
    padded = ((csum[-1] + tmg - 1) // tmg) * tmg
    ends = jnp.cumsum(padded)
    pos = (ends - padded)[e_flat] + rank
    n_rows = TOP_K * n_tok + n_e * tmg
    ntiles = n_rows // tmg
    tile_start = jnp.arange(ntiles, dtype=jnp.int32) * tmg
    tile_expert = jnp.minimum(jnp.sum(tile_start[:, None] >= ends[None, :], axis=1), n_e - 1)
    src = (jnp.arange(n_rows, dtype=jnp.int32) % n_tok).at[pos].set(
        jnp.arange(TOP_K * n_tok, dtype=jnp.int32) // TOP_K)
    meta = jnp.concatenate([tile_expert, ends[-1:] // tmg]).astype(jnp.int32)
    return pos.reshape(n_tok, TOP_K), src, meta


def _expert_changed(meta_ref, i):
    return (i == 0) | (meta_ref[i] != meta_ref[jnp.maximum(i - 1, 0)])


def _moe_up_kernel(meta_ref, x_ref, wg_ref, wu_ref, a_ref, wgb_ref, wub_ref, *, ntiles):
    i = pl.program_id(1)

    @pl.when(_expert_changed(meta_ref, i))
    def _():
        wgb_ref[...] = wg_ref[0].astype(BF16)
        wub_ref[...] = wu_ref[0].astype(BF16)

    @pl.when(i < meta_ref[ntiles])
    def _():
        x = x_ref[...]
        g = jnp.dot(x, wgb_ref[...], preferred_element_type=F32)
        u = jnp.dot(x, wub_ref[...], preferred_element_type=F32)
        a_ref[...] = (_silu(g) * u).astype(BF16)

    @pl.when(i >= meta_ref[ntiles])
    def _():
        a_ref[...] = jnp.zeros_like(a_ref)


def _moe_down_kernel(meta_ref, a_ref, wd_ref, y_ref, wdb_ref, *, ntiles):
    i = pl.program_id(0)

    @pl.when(_expert_changed(meta_ref, i))
    def _():
        wdb_ref[...] = wd_ref[0].astype(BF16)

    @pl.when(i < meta_ref[ntiles])
    def _():
        y_ref[...] = jnp.dot(a_ref[...], wdb_ref[...], preferred_element_type=F32).astype(y_ref.dtype)

    @pl.when(i >= meta_ref[ntiles])
    def _():
        y_ref[...] = jnp.zeros_like(y_ref)


def moe_experts(xs, meta, wg, wu, wd, tmg, tf):
    n_rows, d = xs.shape
    n_e, _, f = wg.shape
    ntiles = n_rows // tmg
    nf = f // tf
    once = pl.Buffered(1)
    a = pl.pallas_call(
        functools.partial(_moe_up_kernel, ntiles=ntiles),
        grid_spec=pltpu.PrefetchScalarGridSpec(
            num_scalar_prefetch=1,
            grid=(nf, ntiles),
            in_specs=[pl.BlockSpec((tmg, d), lambda j, i, m: (i, 0)),
                      pl.BlockSpec((1, d, tf), lambda j, i, m: (m[i], 0, j), pipeline_mode=once),
                      pl.BlockSpec((1, d, tf), lambda j, i, m: (m[i], 0, j), pipeline_mode=once)],
            out_specs=pl.BlockSpec((tmg, tf), lambda j, i, m: (i, j)),
            scratch_shapes=[pltpu.VMEM((d, tf), BF16), pltpu.VMEM((d, tf), BF16)]),
        out_shape=jax.ShapeDtypeStruct((n_rows, f), BF16),
        compiler_params=pltpu.CompilerParams(
            dimension_semantics=("arbitrary", "arbitrary"), vmem_limit_bytes=V7X_VMEM_LIMIT),
        name="moe_up",
    )(meta, xs, wg, wu)
    return pl.pallas_call(
        functools.partial(_moe_down_kernel, ntiles=ntiles),
        grid_spec=pltpu.PrefetchScalarGridSpec(
            num_scalar_prefetch=1,
            grid=(ntiles,),
            in_specs=[pl.BlockSpec((tmg, f), lambda i, m: (i, 0)),
                      pl.BlockSpec((1, f, d), lambda i, m: (m[i], 0, 0), pipeline_mode=once)],
            out_specs=pl.BlockSpec((tmg, d), lambda i, m: (i, 0)),
            scratch_shapes=[pltpu.VMEM((f, d), BF16)]),
        out_shape=jax.ShapeDtypeStruct((n_rows, d), BF16),
        compiler_params=pltpu.CompilerParams(
            dimension_semantics=("arbitrary",), vmem_limit_bytes=V7X_VMEM_LIMIT),
        name="moe_down",
    )(meta, a, wd)


def _combine_ln_kernel(x_ref, gt_ref, rt_ref, ya_ref, yb_ref, g_ref, b_ref, o_ref, *, alpha):
    rt = rt_ref[0]
    y = rt[:, 2:3] * ya_ref[0].astype(F32) + rt[:, 3:4] * yb_ref[0].astype(F32)
    o_ref[0] = _layer_norm_rows(alpha * x_ref[0] + gt_ref[0] * y, g_ref[...], b_ref[...])


def combine_residual_ln(x, gate, route, ya, yb, ln_g, ln_b, alpha, tm=512):
    bsz, seq, d = x.shape
    tm = min(tm, seq)
    row = pl.BlockSpec((1, tm, d), lambda b, i: (b, i, 0))
    return pl.pallas_call(
        functools.partial(_combine_ln_kernel, alpha=alpha),
        grid=(bsz, seq // tm),
        in_specs=[row, pl.BlockSpec((1, 1, d), lambda b, i: (b, 0, 0)),
                  pl.BlockSpec((1, tm, ROUTE_LANES), lambda b, i: (b, i, 0)), row, row,
                  pl.BlockSpec((1, d), lambda b, i: (0, 0)), pl.BlockSpec((1, d), lambda b, i: (0, 0))],
        out_specs=row,
        out_shape=jax.ShapeDtypeStruct((bsz, seq, d), F32),
        compiler_params=pltpu.CompilerParams(
            dimension_semantics=("parallel", "parallel"), vmem_limit_bytes=V7X_VMEM_LIMIT),
        name="moe_combine_ln",
    )(x, gate, route, ya, yb, ln_g.reshape(1, d), ln_b.reshape(1, d))


def moe_residual_ln(x, sc, sh, gate, w_router, b_router, wg, wu, wd, ln_g, ln_b, alpha):
    bsz, seq, d = x.shape
    n_e, _, f = wg.shape
    n_tok = bsz * seq
    tmg = min(MOE_TM, TOP_K * n_tok)
    hb, route = moe_router(x, sc, sh, w_router, b_router)
    idx = route[..., :TOP_K].astype(jnp.int32).reshape(n_tok, TOP_K)
    pos, src, meta = _route_tables(idx, n_e, tmg)
    xs = hb.reshape(n_tok, d).at[src].get(mode="promise_in_bounds")
    ys = moe_experts(xs, meta, wg, wu, wd, tmg, f // 2)
    ya = ys.at[pos[:, 0]].get(mode="promise_in_bounds").reshape(bsz, seq, d)
    yb = ys.at[pos[:, 1]].get(mode="promise_in_bounds").reshape(bsz, seq, d)
    return combine_residual_ln(x, gate, route, ya, yb, ln_g, ln_b, alpha)


def _mixers(p_l, p_c, need_ctx_out, lru_conv_w, lru_conv_b, lru_wa, lru_ba, lru_wx, lru_bx, lru_lambda,
            na_rpb, ssd_conv_w, ssd_conv_b, ssd_dt_bias, ssd_a_log, ssd_d, ssd_norm_g, ml_conv_w, ml_conv_b,
            ml_i_bias, ml_f_bias):
    ya = lru_branch(p_l, p_c, lru_conv_w, lru_conv_b, lru_wa, lru_ba, lru_wx, lru_bx, lru_lambda, need_ctx_out)
    yb = na_branch(p_l, p_c, na_rpb, need_ctx_out)
    yc = ssd_branch(p_l, p_c, ssd_conv_w, ssd_conv_b, ssd_dt_bias, ssd_a_log, ssd_d, ssd_norm_g, need_ctx_out)
    yd = mlstm_branch(p_l, p_c, ml_conv_w, ml_conv_b, ml_i_bias, ml_f_bias, need_ctx_out)
    return (ya[0], yb[0], yc[0], yd[0]), (ya[1], yb[1], yc[1], yd[1])


def kernel(x, c, ctx, c_ctx, w_ada, b_ada, w_in, lru_conv_w, lru_conv_b, lru_wa, lru_ba, lru_wx, lru_bx, lru_lambda, na_rpb, ssd_conv_w, ssd_conv_b, ssd_dt_bias, ssd_a_log, ssd_d, ssd_norm_g, ml_conv_w, ml_conv_b, ml_i_bias, ml_f_bias, w_branch, w_out, ln_g, ln_b, ffn_w_gate, ffn_w_up, ffn_w_down, moe_w_router, moe_b_router, moe_w_gate, moe_w_up, moe_w_down):
    depth = w_in.shape[0]
    bsz, seq, d = x.shape
    alpha = (2.0 * depth) ** 0.25
    cvecs = jnp.zeros((8, d), F32).at[:bsz].set(c).at[bsz].set(c_ctx)
    mods = ada_modulation(cvecs, w_ada, b_ada)
    xl, xc = x, ctx
    for l in range(depth):
        need_ctx_out = l < depth - 1
        mod = mods[l]
        mod_l = jnp.split(mod[:bsz, None, :], 6, axis=-1)
        mod_c = jnp.split(jnp.broadcast_to(mod[bsz:bsz + 1, None, :], (bsz, 1, 6 * d)), 6, axis=-1)
        w_mix, w_gates = _pack_w_in(w_in[l])
        p_l, hb_l = mod_matmul(xl, mod_l[1], mod_l[0], w_mix)
        p_c, hb_c = mod_matmul(xc, mod_c[1], mod_c[0], w_mix)
        br_l, br_c = _mixers(p_l, p_c, need_ctx_out, lru_conv_w[l], lru_conv_b[l], lru_wa[l], lru_ba[l],
                             lru_wx[l], lru_bx[l], lru_lambda[l], na_rpb[l], ssd_conv_w[l], ssd_conv_b[l],
                             ssd_dt_bias[l], ssd_a_log[l], ssd_d[l], ssd_norm_g[l], ml_conv_w[l], ml_conv_b[l],
                             ml_i_bias[l], ml_f_bias[l])
        wb = w_branch[l].astype(BF16)
        wo = w_out[l].astype(BF16)
        xl = proj_residual_ln(merge_branches(hb_l, w_gates, br_l, wb), wo, xl, mod_l[2], ln_g[l, 0], ln_b[l, 0],
                              alpha)
        if need_ctx_out:
            xc = proj_residual_ln(merge_branches(hb_c, w_gates, br_c, wb), wo, xc, mod_c[2], ln_g[l, 0],
                                  ln_b[l, 0], alpha)
        j = l // 2
        if l % 2 == 0:
            wg = ffn_w_gate[j].astype(BF16)
            wu = ffn_w_up[j].astype(BF16)
            wd = ffn_w_down[j].astype(BF16)

            def ffn(h, m, wg=wg, wu=wu, wd=wd):
                return ffn_residual_ln(h, m[4], m[3], m[5], wg, wu, wd, ln_g[l, 1], ln_b[l, 1], alpha)
        else:
            def ffn(h, m, j=j):
                return moe_residual_ln(h, m[4], m[3], m[5], moe_w_router[j], moe_b_router[j], moe_w_gate[j],
                                       moe_w_up[j], moe_w_down[j], ln_g[l, 1], ln_b[l, 1], alpha)

        xl = ffn(xl, mod_l)
        if need_ctx_out:
            xc = ffn(xc, mod_c)
    return xl
```

```python
import functools
import math

import numpy as np
import jax
import jax.numpy as jnp
from jax import lax
from jax.experimental import pallas as pl
from jax.experimental.pallas import tpu as pltpu

F32 = jnp.float32
BF16 = jnp.bfloat16

D_MODEL = 2048
GRID_W = 64
N_BRANCH = 4
BRANCH_W = D_MODEL // N_BRANCH
CONV_W = 4
LN_EPS = 1e-5
LRU_BLOCKS = 8
LRU_BW = BRANCH_W // LRU_BLOCKS
LRU_C = 8.0
NA_HEADS = 8
NA_HD = BRANCH_W // NA_HEADS
NA_KH = 8
NA_KW = 16
SSD_HEADS = 8
SSD_HD = BRANCH_W // SSD_HEADS
SSD_GROUPS = 2
SSD_STATE = 64
SSD_CHUNK = 128
SSD_GN = SSD_GROUPS * SSD_STATE
ML_HEADS = 4
ML_HD = BRANCH_W // ML_HEADS
ML_CHUNK = 128
ROPE_BASE = 10000.0
N_EXPERTS = 8
TOP_K = 2

V7X_VMEM_LIMIT = 52 * 1024 * 1024

COL_LRU_X = 0
COL_LRU_G = 512
COL_NA = 1024
COL_SSD_Z = 2560
COL_SSD_X = 3072
COL_ML = 3584
COL_SSD_B = 5632
COL_SSD_C = 5760
COL_SSD_DT = 5888
COL_ML_G = 5904
N_MIX = 5920
N_MIX_PAD = 6144


def _pack_w_in(w):
    parts = [w[:, 0:1024], w[:, 1024:2560], w[:, 2560:3584], w[:, 3856:5904],
             w[:, 3584:3840], w[:, 3840:3856], w[:, 5904:5920],
             jnp.zeros((w.shape[0], N_MIX_PAD - N_MIX), w.dtype)]
    return jnp.concatenate(parts, axis=1).astype(BF16), w[:, N_MIX:].astype(BF16)


def _sigmoid(x):
    return 1.0 / (1.0 + jnp.exp(-x))


def _layer_norm_rows(z, g, b):
    mu = jnp.mean(z, axis=-1, keepdims=True)
    zc = z - mu
    var = jnp.mean(zc * zc, axis=-1, keepdims=True)
    return zc * lax.rsqrt(var + LN_EPS) * g + b


def _mod_matmul_kernel(x_ref, sc_ref, sh_ref, w_ref, o_ref, hb_ref):
    @pl.when(pl.program_id(2) == 0)
    def _():
        hb_ref[0] = (x_ref[0] * (1.0 + sc_ref[0]) + sh_ref[0]).astype(BF16)

    o_ref[0] = jnp.dot(hb_ref[0], w_ref[...], preferred_element_type=F32)


def mod_matmul(x, sc, sh, w, tm=1024, tn=1024):
    bsz, seq, d = x.shape
    n = w.shape[1]
    tm = min(tm, seq)
    return pl.pallas_call(
        _mod_matmul_kernel,
        grid=(bsz, seq // tm, n // tn),
        in_specs=[pl.BlockSpec((1, tm, d), lambda b, i, j: (b, i, 0)),
                  pl.BlockSpec((1, 1, d), lambda b, i, j: (b, 0, 0)),
                  pl.BlockSpec((1, 1, d), lambda b, i, j: (b, 0, 0)),
                  pl.BlockSpec((d, tn), lambda b, i, j: (0, j))],
        out_specs=[pl.BlockSpec((1, tm, tn), lambda b, i, j: (b, i, j)),
                   pl.BlockSpec((1, tm, d), lambda b, i, j: (b, i, 0))],
        out_shape=[jax.ShapeDtypeStruct((bsz, seq, n), F32), jax.ShapeDtypeStruct((bsz, seq, d), BF16)],
        compiler_params=pltpu.CompilerParams(
            dimension_semantics=("parallel", "parallel", "arbitrary"), vmem_limit_bytes=V7X_VMEM_LIMIT),
        name="in_proj",
    )(x, sc, sh, w)


def _merge_kernel(hb_ref, g0, g1, g2, g3, ya, yb, yc, yd, wb_ref, o_ref):
    hb = hb_ref[0]
    acc = None
    for n, (wg, y) in enumerate(((g0, ya), (g1, yb), (g2, yc), (g3, yd))):
        gate = _sigmoid(jnp.dot(hb, wg[...], preferred_element_type=F32))
        t = gate * jnp.dot(y[0].astype(BF16), wb_ref[n], preferred_element_type=F32)
        acc = t if acc is None else acc + t
    o_ref[0] = acc.astype(BF16)


def merge_branches(hb, w_gates, branches, wb, tm=512, tn=512):
    bsz, seq, dk = hb.shape
    d = wb.shape[2]
    tm = min(tm, seq)
    nj = d // tn
    g_specs = [pl.BlockSpec((dk, tn), functools.partial(lambda b, i, j, n: (0, n * nj + j), n=n))
               for n in range(N_BRANCH)]
    y_specs = [pl.BlockSpec((1, tm, BRANCH_W), lambda b, i, j: (b, i, 0)) for _ in range(N_BRANCH)]
    return pl.pallas_call(
        _merge_kernel,
        grid=(bsz, seq // tm, nj),
        in_specs=[pl.BlockSpec((1, tm, dk), lambda b, i, j: (b, i, 0))] + g_specs + y_specs
                 + [pl.BlockSpec((N_BRANCH, BRANCH_W, tn), lambda b, i, j: (0, 0, j))],
        out_specs=pl.BlockSpec((1, tm, tn), lambda b, i, j: (b, i, j)),
        out_shape=jax.ShapeDtypeStruct((bsz, seq, d), BF16),
        compiler_params=pltpu.CompilerParams(
            dimension_semantics=("parallel", "parallel", "arbitrary"), vmem_limit_bytes=V7X_VMEM_LIMIT),
        name="merge",
    )(hb, w_gates, w_gates, w_gates, w_gates, *branches, wb)


def _proj_ln_kernel(m_ref, w_ref, x_ref, gt_ref, g_ref, b_ref, o_ref, *, alpha):
    half = m_ref.shape[1] // 2
    ys = [jnp.dot(m_ref[0, n * half:(n + 1) * half, :], w_ref[...], preferred_element_type=F32) for n in range(2)]
    for n in range(2):
        rows = slice(n * half, (n + 1) * half)
        o_ref[0, rows, :] = _layer_norm_rows(alpha * x_ref[0, rows, :] + gt_ref[0] * ys[n], g_ref[...], b_ref[...])


def proj_residual_ln(m, w, x, gate, ln_g, ln_b, alpha, tm=512):
    bsz, seq, k = m.shape
    d = w.shape[1]
    tm = min(tm, seq)
    return pl.pallas_call(
        functools.partial(_proj_ln_kernel, alpha=alpha),
        grid=(bsz, seq // tm),
        in_specs=[pl.BlockSpec((1, tm, k), lambda b, i: (b, i, 0)),
                  pl.BlockSpec((k, d), lambda b, i: (0, 0)),
                  pl.BlockSpec((1, tm, d), lambda b, i: (b, i, 0)),
                  pl.BlockSpec((1, 1, d), lambda b, i: (b, 0, 0)),
                  pl.BlockSpec((1, d), lambda b, i: (0, 0)),
                  pl.BlockSpec((1, d), lambda b, i: (0, 0))],
        out_specs=pl.BlockSpec((1, tm, d), lambda b, i: (b, i, 0)),
        out_shape=jax.ShapeDtypeStruct((bsz, seq, d), F32),
        compiler_params=pltpu.CompilerParams(
            dimension_semantics=("parallel", "parallel"), vmem_limit_bytes=V7X_VMEM_LIMIT),
        name="out_proj_ln",
    )(m, w, x, gate, ln_g.reshape(1, d), ln_b.reshape(1, d))


def _ffn_kernel(x_ref, sc_ref, sh_ref, gt_ref, wg_ref, wu_ref, wd_ref, lg_ref, lb_ref, o_ref, hb_ref, acc_ref, *, alpha):
    j = pl.program_id(2)

    @pl.when(j == 0)
    def _():
        hb_ref[...] = (x_ref[0] * (1.0 + sc_ref[0]) + sh_ref[0]).astype(BF16)
        acc_ref[...] = jnp.zeros_like(acc_ref)

    hb = hb_ref[...]
    g = jnp.dot(hb, wg_ref[...], preferred_element_type=F32)
    u = jnp.dot(hb, wu_ref[...], preferred_element_type=F32)
    acc_ref[...] += jnp.dot((_silu(g) * u).astype(BF16), wd_ref[...], preferred_element_type=F32)

    @pl.when(j == pl.num_programs(2) - 1)
    def _():
        o_ref[0] = _layer_norm_rows(alpha * x_ref[0] + gt_ref[0] * acc_ref[...], lg_ref[...], lb_ref[...])


def ffn_residual_ln(x, sc, sh, gate, wg, wu, wd, ln_g, ln_b, alpha, tm=512, tf=512):
    bsz, seq, d = x.shape
    f = wg.shape[1]
    tm = min(tm, seq)
    vec = pl.BlockSpec((1, 1, d), lambda b, i, j: (b, 0, 0))
    par = pl.BlockSpec((1, d), lambda b, i, j: (0, 0))
    return pl.pallas_call(
        functools.partial(_ffn_kernel, alpha=alpha),
        grid=(bsz, seq // tm, f // tf),
        in_specs=[pl.BlockSpec((1, tm, d), lambda b, i, j: (b, i, 0)), vec, vec, vec,
                  pl.BlockSpec((d, tf), lambda b, i, j: (0, j)),
                  pl.BlockSpec((d, tf), lambda b, i, j: (0, j)),
                  pl.BlockSpec((tf, d), lambda b, i, j: (j, 0)), par, par],
        out_specs=pl.BlockSpec((1, tm, d), lambda b, i, j: (b, i, 0)),
        out_shape=jax.ShapeDtypeStruct((bsz, seq, d), F32),
        scratch_shapes=[pltpu.VMEM((tm, d), BF16), pltpu.VMEM((tm, d), F32)],
        compiler_params=pltpu.CompilerParams(
            dimension_semantics=("parallel", "parallel", "arbitrary"), vmem_limit_bytes=V7X_VMEM_LIMIT),
        name="ffn_ln",
    )(x, sc, sh, gate, wg, wu, wd, ln_g.reshape(1, d), ln_b.reshape(1, d))


def _ada_kernel(c_ref, w_ref, b_ref, o_ref):
    cv = c_ref[...]
    o_ref[0] = jnp.dot(cv * _sigmoid(cv), w_ref[0], precision=lax.Precision.HIGHEST,
                       preferred_element_type=F32) + b_ref[0]


def ada_modulation(cvecs, w, b, tn=1536):
    r, d = cvecs.shape
    depth, _, n = w.shape
    return pl.pallas_call(
        _ada_kernel,
        grid=(depth, n // tn),
        in_specs=[pl.BlockSpec((r, d), lambda l, j: (0, 0)), pl.BlockSpec((1, d, tn), lambda l, j: (l, 0, j)),
                  pl.BlockSpec((1, 1, tn), lambda l, j: (l, 0, j))],
        out_specs=pl.BlockSpec((1, r, tn), lambda l, j: (l, 0, j)),
        out_shape=jax.ShapeDtypeStruct((depth, r, n), F32),
        compiler_params=pltpu.CompilerParams(
            dimension_semantics=("parallel", "parallel"), vmem_limit_bytes=V7X_VMEM_LIMIT),
        name="ada_mod",
    )(cvecs, w, b.reshape(depth, 1, n))


NA_NEG = -1e30


def _na_bias_slabs(rpb):
    w = jnp.arange(GRID_W)
    cs = jnp.clip(w - NA_KW // 2, 0, GRID_W - NA_KW)
    ok = (w[None, :] >= cs[:, None]) & (w[None, :] < cs[:, None] + NA_KW)
    dc = jnp.clip(w[None, :] - w[:, None] + (NA_KW - 1), 0, 2 * NA_KW - 2)
    tab = jnp.where(ok, rpb[:, :, dc], NA_NEG)
    idx = jnp.arange(NA_KH)[:, None] + jnp.arange(NA_KH)[None, :]
    slab = tab[:, idx]
    return slab.transpose(1, 0, 3, 2, 4).reshape(NA_KH, NA_HEADS, GRID_W, NA_KH * GRID_W)


def _softmax2(s_a, s_b):
    m = jnp.maximum(jnp.max(s_a, axis=-1, keepdims=True), jnp.max(s_b, axis=-1, keepdims=True))
    e_a = jnp.exp(s_a - m)
    e_b = jnp.exp(s_b - m)
    inv = 1.0 / (jnp.sum(e_a, axis=-1, keepdims=True) + jnp.sum(e_b, axis=-1, keepdims=True))
    return e_a * inv, e_b * inv


_NT = (((1,), (1,)), ((), ()))


def _na_kernel(q_ref, kp_ref, kc_ref, kn_ref, vp_ref, vc_ref, vn_ref, ck_ref, cv_ref, bias_ref, o_ref,
               kw_ref, vw_ref, ckb_ref, cvb_ref, *, rows):
    i = pl.program_id(1)
    blk = NA_KH * GRID_W
    for n, (kr, vr) in enumerate(((kp_ref, vp_ref), (kc_ref, vc_ref), (kn_ref, vn_ref))):
        kw_ref[n * blk:(n + 1) * blk, :] = kr[0].astype(BF16)
        vw_ref[n * blk:(n + 1) * blk, :] = vr[0].astype(BF16)
    ckb_ref[...] = ck_ref[0].astype(BF16)
    cvb_ref[...] = cv_ref[0].astype(BF16)
    scale = NA_HD ** -0.5

    first = lax.broadcasted_iota(jnp.int32, (GRID_W, 2 * NA_HD), 1) < NA_HD
    pairs = [slice(n * 2 * NA_HD, (n + 1) * 2 * NA_HD) for n in range(NA_HEADS // 2)]
    rows_per_trip = 4

    def body(t, carry):
        trip = []
        for sub in range(rows_per_trip):
            rr = t * rows_per_trip + sub
            r = i * NA_KH + rr
            rs = jnp.clip(r - NA_KH // 2, 0, rows - NA_KH)
            off = pl.multiple_of((rs - (i - 1) * NA_KH) * GRID_W, GRID_W)
            v = rs - r + (NA_KH - 1)
            q_all = (q_ref[0, pl.ds(pl.multiple_of(rr * GRID_W, GRID_W), GRID_W), :] * scale).astype(BF16)
            trip.append((rr, v, q_all, kw_ref[pl.ds(off, blk), :], vw_ref[pl.ds(off, blk), :]))
        scores = []
        for rr, v, q_all, kwin, vwin in trip:
            for h in range(NA_HEADS):
                ps = pairs[h // 2]
                q = jnp.where(first if h % 2 == 0 else ~first, q_all[:, ps], jnp.zeros((), BF16))
                scores.append((lax.dot_general(q, kwin[:, ps], _NT, preferred_element_type=F32) + bias_ref[v, h],
                               lax.dot_general(q, ckb_ref[:, ps], _NT, preferred_element_type=F32)))
        probs = [_softmax2(s_w, s_c) for s_w, s_c in scores]
        both = [jnp.dot(p_w.astype(BF16), trip[n // NA_HEADS][4][:, pairs[(n % NA_HEADS) // 2]],
                        preferred_element_type=F32)
                + jnp.dot(p_c.astype(BF16), cvb_ref[:, pairs[(n % NA_HEADS) // 2]], preferred_element_type=F32)
                for n, (p_w, p_c) in enumerate(probs)]
        for sub, (rr, _, _, _, _) in enumerate(trip):
            base = sub * NA_HEADS
            outs = [jnp.where(first, both[base + 2 * n], both[base + 2 * n + 1]) for n in range(NA_HEADS // 2)]
            o_ref[0, pl.ds(pl.multiple_of(rr * GRID_W, GRID_W), GRID_W), :] = jnp.concatenate(outs, axis=-1)
        return carry

    lax.fori_loop(0, NA_KH // rows_per_trip, body, 0)


def _na_ctx_kernel(q_ref, k_ref, v_ref, o_ref):
    scale = NA_HD ** -0.5
    q_all = (q_ref[0] * scale).astype(BF16)
    k_all = k_ref[0].astype(BF16)
    v_all = v_ref[0].astype(BF16)
    outs = []
    for h in range(NA_HEADS):
        hs = slice(h * NA_HD, (h + 1) * NA_HD)
        s = lax.dot_general(q_all[:, hs], k_all[:, hs], _NT, preferred_element_type=F32)
        e = jnp.exp(s - jnp.max(s, axis=-1, keepdims=True))
        p = e * (1.0 / jnp.sum(e, axis=-1, keepdims=True))
        outs.append(jnp.dot(p.astype(BF16), v_all[:, hs], preferred_element_type=F32))
    o_ref[0] = jnp.concatenate(outs, axis=-1)


def na_branch(p_l, p_c, rpb, need_ctx_out):
    bsz, seq, _ = p_l.shape
    n_ctx = p_c.shape[1]
    rows = seq // GRID_W
    assert rows % NA_KH == 0 and rows >= 2 * NA_KH
    w = BRANCH_W
    blk = NA_KH * GRID_W
    nblk = rows // NA_KH
    cq = COL_NA // w

    def shifted(col, delta):
        return pl.BlockSpec((1, blk, w), lambda b, i: (b, jnp.clip(i + delta, 0, nblk - 1), col))

    y_l = pl.pallas_call(
        functools.partial(_na_kernel, rows=rows),
        grid=(bsz, nblk),
        in_specs=[shifted(cq, 0), shifted(cq + 1, -1), shifted(cq + 1, 0), shifted(cq + 1, 1),
                  shifted(cq + 2, -1), shifted(cq + 2, 0), shifted(cq + 2, 1),
                  pl.BlockSpec((1, n_ctx, w), lambda b, i: (b, 0, cq + 1)),
                  pl.BlockSpec((1, n_ctx, w), lambda b, i: (b, 0, cq + 2)),
                  pl.BlockSpec((NA_KH, NA_HEADS, GRID_W, blk), lambda b, i: (0, 0, 0, 0))],
        out_specs=pl.BlockSpec((1, blk, w), lambda b, i: (b, i, 0)),
        out_shape=jax.ShapeDtypeStruct((bsz, seq, w), F32),
        scratch_shapes=[pltpu.VMEM((3 * blk, w), BF16), pltpu.VMEM((3 * blk, w), BF16),
                        pltpu.VMEM((n_ctx, w), BF16), pltpu.VMEM((n_ctx, w), BF16)],
        compiler_params=pltpu.CompilerParams(
            dimension_semantics=("parallel", "arbitrary"), vmem_limit_bytes=V7X_VMEM_LIMIT),
        name="na_attn",
    )(p_l, p_l, p_l, p_l, p_l, p_l, p_l, p_c, p_c, _na_bias_slabs(rpb))
    y_c = None
    if need_ctx_out:
        y_c = pl.pallas_call(
            _na_ctx_kernel,
            grid=(bsz,),
            in_specs=[pl.BlockSpec((1, n_ctx, w), functools.partial(lambda b, c: (b, 0, c), c=cq + n)) for n in range(3)],
            out_specs=pl.BlockSpec((1, n_ctx, w), lambda b: (b, 0, 0)),
            out_shape=jax.ShapeDtypeStruct((bsz, n_ctx, w), F32),
            compiler_params=pltpu.CompilerParams(dimension_semantics=("parallel",)),
            name="na_ctx_attn",
        )(p_c, p_c, p_c)
    return y_l, y_c


def _gelu_tanh(x):
    return 0.5 * x * (1.0 + jnp.tanh(math.sqrt(2.0 / math.pi) * (x + 0.044715 * (x * x * x))))


def _silu(x):
    return x * _sigmoid(x)


def _softplus(x):
    return jnp.maximum(x, 0.0) + jnp.log1p(jnp.exp(-jnp.abs(x)))


def _rope_2d_tables(seq):
    t = jnp.arange(seq, dtype=jnp.int32)
    pos = jnp.stack([t // GRID_W, t % GRID_W], axis=-1).astype(F32)
    nf = ML_HD // 4
    inv_freq = ROPE_BASE ** (-jnp.arange(nf, dtype=F32) / nf)
    ang = jnp.broadcast_to(pos[:, :, None, None] * inv_freq, (seq, 2, 2, nf)).reshape(seq, ML_HD)
    return jnp.cos(ang), jnp.sin(ang)


def _chunk_specs(tt, nchunk, n8, reverse, nb=1):
    def pos(j):
        return nchunk - 1 - j if reverse else j

    def chunk(width, col):
        return pl.BlockSpec((nb, tt, width), lambda b, j: (b, pos(j), col))

    def halo(width, col, delta):
        if delta < 0:
            return pl.BlockSpec((nb, 8, width), lambda b, j: (b, jnp.maximum(pos(j) * (tt // 8) - 1, 0), col))
        return pl.BlockSpec((nb, 8, width), lambda b, j: (b, jnp.minimum((pos(j) + 1) * (tt // 8), n8 - 1), col))

    return chunk, halo


def _const_spec(shape):
    return pl.BlockSpec(shape, lambda b, j: (0,) * len(shape))


def _batch_spec(shape, nb=1):
    return pl.BlockSpec((nb,) + shape, lambda b, j: (b,) + (0,) * len(shape))


def _roll_in_blocks(x, shift):
    rows, w = x.shape
    return pltpu.roll(x.reshape(rows // 8, 8, w), shift, 1).reshape(rows, w)


def _conv4(x, prev8, next8, has_prev, has_next, cw, cb):
    tt = x.shape[0]
    row = lax.broadcasted_iota(jnp.int32, x.shape, 0)
    p6 = prev8[6:7, :] * has_prev
    p7 = prev8[7:8, :] * has_prev
    n0 = next8[0:1, :] * has_next
    xm1 = jnp.where(row == 0, p7, pltpu.roll(x, 1, 0))
    xm2 = jnp.where(row == 0, p6, jnp.where(row == 1, p7, pltpu.roll(x, 2, 0)))
    xp1 = jnp.where(row == tt - 1, n0, pltpu.roll(x, tt - 1, 0))
    return cw[0:1, :] * xm2 + cw[1:2, :] * xm1 + cw[2:3, :] * x + cw[3:4, :] * xp1 + cb


def _cumsum_rows(x, reverse):
    n = x.shape[0]
    row = lax.broadcasted_iota(jnp.int32, x.shape, 0)
    s = 1
    while s < n:
        if reverse:
            x = x + jnp.where(row < n - s, pltpu.roll(x, n - s, 0), 0.0)
        else:
            x = x + jnp.where(row >= s, pltpu.roll(x, s, 0), 0.0)
        s *= 2
    return x


def _causal_mask(n, reverse):
    ii = lax.broadcasted_iota(jnp.int32, (n, n), 0)
    jj = lax.broadcasted_iota(jnp.int32, (n, n), 1)
    return (jj >= ii) if reverse else (jj <= ii)


MASK_NEG = -1e30


LRU_TT = 256


def _lru_kernel(*refs, reverse, finalize, nchunk):
    if finalize:
        (xp_ref, x_ref, xn_ref, g_ref, ho_ref, h0_ref, cw_ref, cb_ref, w_ref, bias_ref, sp_ref,
         o_ref, hl_ref, carry_ref) = refs
    else:
        (xp_ref, x_ref, xn_ref, h0_ref, cw_ref, cb_ref, w_ref, bias_ref, sp_ref, o_ref, hl_ref, carry_ref) = refs
    j = pl.program_id(1)
    c = (nchunk - 1 - j) if reverse else j

    @pl.when(j == 0)
    def _():
        carry_ref[...] = h0_ref[0]

    tt = x_ref.shape[1]
    w = x_ref.shape[2]
    xc = _conv4(x_ref[0], xp_ref[0], xn_ref[0], (c > 0).astype(F32), (c < nchunk - 1).astype(F32),
                cw_ref[...], cb_ref[...])
    g = jnp.dot(xc.astype(BF16), w_ref[...], preferred_element_type=F32)
    r = _sigmoid(g[:, :w] + bias_ref[0:1, :])
    ig = _sigmoid(g[:, w:] + bias_ref[1:2, :])
    log_a = -LRU_C * r * sp_ref[...]
    a = jnp.exp(log_a)
    u = jnp.sqrt(1.0 - jnp.exp(2.0 * log_a)) * (ig * xc)
    sub = lax.broadcasted_iota(jnp.int32, (tt, w), 0) % 8
    s = 1
    while s < 8:
        if reverse:
            keep = sub < 8 - s
            a_s = jnp.where(keep, _roll_in_blocks(a, 8 - s), 1.0)
            u_s = jnp.where(keep, _roll_in_blocks(u, 8 - s), 0.0)
        else:
            keep = sub >= s
            a_s = jnp.where(keep, _roll_in_blocks(a, s), 1.0)
            u_s = jnp.where(keep, _roll_in_blocks(u, s), 0.0)
        u = a * u_s + u
        a = a * a_s
        s *= 2
    new_carry = carry_ref[...]
    nblk = tt // 8
    hs = [None] * nblk
    for blk in (reversed(range(nblk)) if reverse else range(nblk)):
        hb = u[blk * 8:(blk + 1) * 8, :] + a[blk * 8:(blk + 1) * 8, :] * new_carry
        new_carry = hb[0:1, :] if reverse else hb[7:8, :]
        hs[blk] = hb
    h = jnp.concatenate(hs, axis=0)
    carry_ref[...] = new_carry
    hl_ref[0] = new_carry
    if finalize:
        o_ref[0] = (ho_ref[0] + h) * _gelu_tanh(g_ref[0])
    else:
        o_ref[0] = h


def _lru_pass(p, h0, other, cw, cb, wcat, bias, sp, reverse):
    bsz, seq, _ = p.shape
    w = BRANCH_W
    tt = min(LRU_TT, seq)
    nchunk = seq // tt
    chunk, halo = _chunk_specs(tt, nchunk, seq // 8, reverse)
    cx = COL_LRU_X // w
    finalize = other is not None
    in_specs = [halo(w, cx, -1), chunk(w, cx), halo(w, cx, 1)]
    args = [p, p, p]
    if finalize:
        in_specs += [chunk(w, COL_LRU_G // w), chunk(w, 0)]
        args += [p, other]
    in_specs += [_batch_spec((1, w)), _const_spec((CONV_W, w)), _const_spec((1, w)),
                 _const_spec((w, 2 * w)), _const_spec((2, w)), _const_spec((1, w))]
    args += [h0, cw, cb, wcat, bias, sp]
    return pl.pallas_call(
        functools.partial(_lru_kernel, reverse=reverse, finalize=finalize, nchunk=nchunk),
        grid=(bsz, nchunk),
        in_specs=in_specs,
        out_specs=[chunk(w, 0), _batch_spec((1, w))],
        out_shape=[jax.ShapeDtypeStruct((bsz, seq, w), F32), jax.ShapeDtypeStruct((bsz, 1, w), F32)],
        scratch_shapes=[pltpu.VMEM((1, w), F32)],
        compiler_params=pltpu.CompilerParams(dimension_semantics=("parallel", "arbitrary")),
        name="lru_scan",
    )(*args)


def _block_diag(wg):
    g, n, _ = wg.shape
    eye = jnp.eye(g, dtype=wg.dtype)
    return (wg[:, :, None, :] * eye[:, None, :, None]).reshape(g * n, g * n)


def lru_branch(p_l, p_c, conv_w, conv_b, wa, ba, wx, bx, lam, need_ctx_out):
    bsz = p_l.shape[0]
    w = BRANCH_W
    cb = conv_b.reshape(1, w)
    sp = jax.nn.softplus(-lam)
    zeros = jnp.zeros((bsz, 1, w), F32)
    h_c = h_l = None
    for d, reverse in ((0, False), (1, True)):
        wcat = jnp.concatenate([_block_diag(wa[d]), _block_diag(wx[d])], axis=1).astype(BF16)
        bias = jnp.stack([ba[d], bx[d]])
        other_c = h_c if (d == 1 and need_ctx_out) else None
        h_c, st = _lru_pass(p_c, zeros, other_c, conv_w, cb, wcat, bias, sp[d:d + 1], reverse)
        h_l, _ = _lru_pass(p_l, st, h_l if d == 1 else None, conv_w, cb, wcat, bias, sp[d:d + 1], reverse)
    return h_l, (h_c if need_ctx_out else None)


def _ssd_kernel(*refs, reverse, finalize, nchunk, d):
    (xp_ref, x_ref, xn_ref, bp_ref, bc_ref, bn_ref, dt_ref) = refs[:7]
    k = 7
    if finalize:
        z_ref, yp_ref = refs[k:k + 2]
        k += 2
    s0_ref, cwx_ref, cbx_ref, cwb_ref, cbb_ref, dtb_ref, a_ref = refs[k:k + 7]
    k += 7
    if finalize:
        dsk_ref, ng_ref = refs[k:k + 2]
        k += 2
    o_ref, so_ref, s_ref = refs[k:k + 3]
    j = pl.program_id(1)
    c = (nchunk - 1 - j) if reverse else j

    @pl.when(j == 0)
    def _():
        s_ref[...] = s0_ref[...]

    nb = x_ref.shape[0]
    q = x_ref.shape[1]
    has_prev = (c > 0).astype(F32)
    has_next = (c < nchunk - 1).astype(F32)
    mask = _causal_mask(q, reverse)
    first = lax.broadcasted_iota(jnp.int32, (q, 2 * SSD_HD), 1) < SSD_HD
    npair = SSD_HEADS // 2
    pair_group = [(2 * n) // (SSD_HEADS // SSD_GROUPS) for n in range(npair)]
    pairs = [slice(n * 2 * SSD_HD, (n + 1) * 2 * SSD_HD) for n in range(npair)]

    def per_lane(t, n):
        c0 = d * SSD_HEADS + 2 * n
        return jnp.where(first[:t.shape[0]], t[:, c0:c0 + 1], t[:, c0 + 1:c0 + 2])

    xs, xb, dt_t, cum, cum_t, w_end, ecum, etot, b_t, cgs, cbs = [], [], [], [], [], [], [], [], [], [], []
    for bi in range(nb):
        xs.append(_silu(_conv4(x_ref[bi], xp_ref[bi], xn_ref[bi], has_prev, has_next, cwx_ref[...], cbx_ref[...])))
        bc = _silu(_conv4(bc_ref[bi], bp_ref[bi], bn_ref[bi], has_prev, has_next, cwb_ref[...], cbb_ref[...]))
        dt = _softplus(dt_ref[bi] + dtb_ref[...])
        cum.append(_cumsum_rows(dt * a_ref[...], reverse))
        tot = cum[bi][0:1, :] if reverse else cum[bi][q - 1:q, :]
        w_end.append(jnp.exp(tot - cum[bi]) * dt)
        ecum.append(jnp.exp(cum[bi]))
        etot.append(jnp.exp(tot))
        dt_t.append(dt.T)
        cum_t.append(cum[bi].T)
        b_t.append(bc[:, :SSD_GN].T)
        xb.append(xs[bi].astype(BF16))
        cgs.append([bc[:, SSD_GN + g * SSD_STATE:SSD_GN + (g + 1) * SSD_STATE].astype(BF16)
                    for g in range(SSD_GROUPS)])
        cbs.append([lax.dot_general(cgs[bi][g], bc[:, g * SSD_STATE:(g + 1) * SSD_STATE].astype(BF16), _NT,
                                    preferred_element_type=F32) for g in range(SSD_GROUPS)])
    intra = []
    for bi in range(nb):
        for h in range(SSD_HEADS):
            col = d * SSD_HEADS + h
            diff = cum[bi][:, col:col + 1] - cum_t[bi][col:col + 1, :]
            m = (cbs[bi][pair_group[h // 2]] * jnp.exp(jnp.where(mask, diff, MASK_NEG))
                 * dt_t[bi][col:col + 1, :])
            intra.append(jnp.dot(m.astype(BF16), xb[bi][:, pairs[h // 2]], preferred_element_type=F32))
    items = [(bi, n) for bi in range(nb) for n in range(npair)]
    s_old = [s_ref[bi, n] for bi, n in items]
    inter = [jnp.dot(cgs[bi][pair_group[n]], s_old[t].astype(BF16), preferred_element_type=F32)
             for t, (bi, n) in enumerate(items)]
    outs = []
    for t, (bi, n) in enumerate(items):
        g = pair_group[n]
        h0 = bi * SSD_HEADS + 2 * n
        outs.append(jnp.where(first, intra[h0], intra[h0 + 1]) + per_lane(ecum[bi], n) * inter[t])
        xw = (xs[bi][:, pairs[n]] * per_lane(w_end[bi], n)).astype(BF16)
        s_ref[bi, n] = per_lane(etot[bi], n) * s_old[t] + jnp.dot(
            b_t[bi][g * SSD_STATE:(g + 1) * SSD_STATE, :].astype(BF16), xw, preferred_element_type=F32)
    so_ref[...] = s_ref[...]
    for bi in range(nb):
        y = jnp.concatenate(outs[bi * npair:(bi + 1) * npair], axis=-1)
        if finalize:
            yt = (xs[bi] * dsk_ref[...] + yp_ref[bi] + y) * _silu(z_ref[bi])
            o_ref[bi] = yt * lax.rsqrt(jnp.mean(yt * yt, axis=-1, keepdims=True) + LN_EPS) * ng_ref[...]
        else:
            o_ref[bi] = y


def _ssd_pass(p, s0, other, params, reverse, d):
    bsz, seq, _ = p.shape
    w = BRANCH_W
    q = SSD_CHUNK
    nchunk = seq // q
    chunk, halo = _chunk_specs(q, nchunk, seq // 8, reverse, nb=bsz)
    cx, cb2, cdt = COL_SSD_X // w, COL_SSD_B // (2 * SSD_GN), COL_SSD_DT // 128
    finalize = other is not None
    cwx, cbx, cwb, cbb, dtb, arow, dsk, ng = params
    in_specs = [halo(w, cx, -1), chunk(w, cx), halo(w, cx, 1),
                halo(2 * SSD_GN, cb2, -1), chunk(2 * SSD_GN, cb2), halo(2 * SSD_GN, cb2, 1), chunk(128, cdt)]
    args = [p] * 7
    if finalize:
        in_specs += [chunk(w, COL_SSD_Z // w), chunk(w, 0)]
        args += [p, other]
    st_shape = (SSD_HEADS // 2, SSD_STATE, 2 * SSD_HD)
    in_specs += [_batch_spec(st_shape, bsz), _const_spec((CONV_W, w)), _const_spec((1, w)),
                 _const_spec((CONV_W, 2 * SSD_GN)), _const_spec((1, 2 * SSD_GN)), _const_spec((1, 128)),
                 _const_spec((1, 128))]
    args += [s0, cwx, cbx, cwb, cbb, dtb, arow]
    if finalize:
        in_specs += [_const_spec((1, w)), _const_spec((1, w))]
        args += [dsk, ng]
    return pl.pallas_call(
        functools.partial(_ssd_kernel, reverse=reverse, finalize=finalize, nchunk=nchunk, d=d),
        grid=(1, nchunk),
        in_specs=in_specs,
        out_specs=[chunk(w, 0), _batch_spec(st_shape, bsz)],
        out_shape=[jax.ShapeDtypeStruct((bsz, seq, w), F32), jax.ShapeDtypeStruct((bsz,) + st_shape, F32)],
        scratch_shapes=[pltpu.VMEM((bsz,) + st_shape, F32)],
        compiler_params=pltpu.CompilerParams(dimension_semantics=("parallel", "arbitrary")),
        name="ssd_scan",
    )(*args)


def _lane_row(vals, start):
    return jnp.zeros((128,), F32).at[start:start + vals.shape[0]].set(vals.astype(F32)).reshape(1, 128)


def ssd_branch(p_l, p_c, conv_w, conv_b, dt_bias, a_log, d_skip, norm_g, need_ctx_out):
    bsz = p_l.shape[0]
    w = BRANCH_W
    params = (conv_w[:, :w], conv_b[:w].reshape(1, w), conv_w[:, w:], conv_b[w:].reshape(1, 2 * SSD_GN),
              _lane_row(dt_bias.reshape(-1), 0), _lane_row(-jnp.exp(a_log.astype(F32)).reshape(-1), 0),
              jnp.repeat(d_skip, SSD_HD).reshape(1, w), norm_g.reshape(1, w))
    zeros = jnp.zeros((bsz, SSD_HEADS // 2, SSD_STATE, 2 * SSD_HD), F32)
    y_c = y_l = None
    for d, reverse in ((0, False), (1, True)):
        other_c = y_c if (d == 1 and need_ctx_out) else None
        y_c, st = _ssd_pass(p_c, zeros, other_c, params, reverse, d)
        y_l, _ = _ssd_pass(p_l, st, y_l if d == 1 else None, params, reverse, d)
    return y_l, (y_c if need_ctx_out else None)


def _rope_rotate(x):
    wl = x.shape[-1]
    half = ML_HD // 4
    lane = lax.broadcasted_iota(jnp.int32, x.shape, 1)
    return jnp.where(lane % (2 * half) < half, -pltpu.roll(x, wl - half, 1), pltpu.roll(x, half, 1))


def _log_sigmoid(x):
    return jnp.minimum(x, 0.0) - jnp.log1p(jnp.exp(-jnp.abs(x)))


def _mlstm_kernel(*refs, reverse, finalize, rope, nchunk, d):
    (qp_ref, q_ref, qn_ref, kp_ref, k_ref, kn_ref, v_ref, g_ref) = refs[:8]
    n = 8
    if rope:
        cos_ref, sin_ref = refs[n:n + 2]
        n += 2
    if finalize:
        og_ref, hp_ref = refs[n:n + 2]
        n += 2
    c0_ref, n0_ref, m0_ref, cwq_ref, cbq_ref, cwk_ref, cbk_ref, ib_ref, fb_ref = refs[n:n + 9]
    n += 9
    o_ref, co_ref, no_ref, mo_ref, c_ref, n_ref, m_ref = refs[n:n + 7]
    j = pl.program_id(1)
    c = (nchunk - 1 - j) if reverse else j

    @pl.when(j == 0)
    def _():
        c_ref[...] = c0_ref[...]
        n_ref[...] = n0_ref[...]
        m_ref[...] = m0_ref[...]

    nb = q_ref.shape[0]
    qn = q_ref.shape[1]
    has_prev = (c > 0).astype(F32)
    has_next = (c < nchunk - 1).astype(F32)
    mask = _causal_mask(qn, reverse)
    heads = [slice(h * ML_HD, (h + 1) * ML_HD) for h in range(ML_HEADS)]
    items = [(bi, h) for bi in range(nb) for h in range(ML_HEADS)]
    q, k, v, b, li, tot, b_t, li_t = [], [], [], [], [], [], [], []
    for bi in range(nb):
        qi = _silu(_conv4(q_ref[bi], qp_ref[bi], qn_ref[bi], has_prev, has_next, cwq_ref[...], cbq_ref[...]))
        ki = _silu(_conv4(k_ref[bi], kp_ref[bi], kn_ref[bi], has_prev, has_next, cwk_ref[...], cbk_ref[...]))
        if rope:
            qi = qi * cos_ref[...] + _rope_rotate(qi) * sin_ref[...]
            ki = ki * cos_ref[...] + _rope_rotate(ki) * sin_ref[...]
        q.append(qi * (ML_HD ** -0.5))
        k.append(ki)
        v.append(v_ref[bi])
        gb = g_ref[bi]
        li.append(gb + ib_ref[...])
        b.append(_cumsum_rows(_log_sigmoid(gb + fb_ref[...]), reverse))
        tot.append(b[bi][0:1, :] if reverse else b[bi][qn - 1:qn, :])
        b_t.append(b[bi].T)
        li_t.append(li[bi].T)
    qb = [t.astype(BF16) for t in q]
    kb = [t.astype(BF16) for t in k]
    vb = [t.astype(BF16) for t in v]
    qk = [lax.dot_general(qb[bi][:, heads[h]], kb[bi][:, heads[h]], _NT, preferred_element_type=F32)
          for bi, h in items]
    c_old = [c_ref[bi, h] for bi, h in items]
    n_old = [n_ref[bi, h:h + 1, :] for bi, h in items]
    qc = [lax.dot_general(qb[bi][:, heads[h]], c_old[t].astype(BF16), _NT, preferred_element_type=F32)
          for t, (bi, h) in enumerate(items)]
    gate = []
    for bi, h in items:
        ci = 4 * ML_HEADS + d * 2 * ML_HEADS + h
        cf = ci + ML_HEADS
        b_c = b[bi][:, cf:cf + 1]
        b_end = tot[bi][:, cf:cf + 1]
        m_st = m_ref[bi, h:h + 1, 0:1]
        end_log = b_end - b_c + li[bi][:, ci:ci + 1]
        m_new = jnp.maximum(b_end + m_st, jnp.max(end_log, axis=0, keepdims=True))
        dlog = jnp.where(mask, b_c - b_t[bi][cf:cf + 1, :] + li_t[bi][ci:ci + 1, :], MASK_NEG)
        m_inter = b_c + m_st
        m_i = jnp.maximum(jnp.max(dlog, axis=1, keepdims=True), m_inter)
        gate.append((jnp.exp(end_log - m_new), jnp.exp(b_end + m_st - m_new), m_new,
                     jnp.exp(dlog - m_i), jnp.exp(m_inter - m_i), jnp.exp(-m_i)))
    s_all = [qk[t] * gate[t][3] for t in range(len(items))]
    sv = [jnp.dot(s_all[t].astype(BF16), vb[bi][:, heads[h]], preferred_element_type=F32)
          for t, (bi, h) in enumerate(items)]
    upd = [jnp.dot((v[bi][:, heads[h]] * gate[t][0]).T.astype(BF16), kb[bi][:, heads[h]],
                   preferred_element_type=F32) for t, (bi, h) in enumerate(items)]
    outs = []
    for t, (bi, h) in enumerate(items):
        hs = heads[h]
        w, carry_scale, m_new, _, w_in, floor = gate[t]
        num = sv[t] + w_in * qc[t]
        den = (jnp.sum(s_all[t], axis=1, keepdims=True)
               + w_in * jnp.sum(q[bi][:, hs] * n_old[t], axis=1, keepdims=True))
        outs.append(num / jnp.maximum(jnp.abs(den), floor))
        c_ref[bi, h] = carry_scale * c_old[t] + upd[t]
        n_ref[bi, h:h + 1, :] = carry_scale * n_old[t] + jnp.sum(k[bi][:, hs] * w, axis=0, keepdims=True)
        m_ref[bi, h:h + 1, :] = jnp.broadcast_to(m_new, (1, ML_HD))
    co_ref[...] = c_ref[...]
    no_ref[...] = n_ref[...]
    mo_ref[...] = m_ref[...]
    for bi in range(nb):
        hout = jnp.concatenate(outs[bi * ML_HEADS:(bi + 1) * ML_HEADS], axis=-1)
        if finalize:
            o_ref[bi] = _sigmoid(og_ref[bi]) * (hp_ref[bi] + hout)
        else:
            o_ref[bi] = hout


def _mlstm_pass(p, state, other, tables, params, reverse, d):
    bsz, seq, _ = p.shape
    w = BRANCH_W
    qn = ML_CHUNK
    nchunk = seq // qn
    chunk, halo = _chunk_specs(qn, nchunk, seq // 8, reverse, nb=bsz)
    cq = COL_ML // w
    finalize = other is not None
    rope = tables is not None
    in_specs = [halo(w, cq, -1), chunk(w, cq), halo(w, cq, 1), halo(w, cq + 1, -1), chunk(w, cq + 1),
                halo(w, cq + 1, 1), chunk(w, cq + 2), chunk(128, COL_ML_G // 128)]
    args = [p] * 8
    if rope:
        tab = pl.BlockSpec((qn, w), (lambda b, j: (nchunk - 1 - j, 0)) if reverse else (lambda b, j: (j, 0)))
        in_specs += [tab, tab]
        args += list(tables)
    if finalize:
        in_specs += [chunk(w, cq + 3), chunk(w, 0)]
        args += [p, other]
    st_shapes = [(ML_HEADS, ML_HD, ML_HD), (ML_HEADS, ML_HD), (ML_HEADS, ML_HD)]
    in_specs += [_batch_spec(s, bsz) for s in st_shapes]
    in_specs += [_const_spec((CONV_W, w)), _const_spec((1, w)), _const_spec((CONV_W, w)), _const_spec((1, w)),
                 _const_spec((1, 128)), _const_spec((1, 128))]
    args += list(state) + list(params)
    res = pl.pallas_call(
        functools.partial(_mlstm_kernel, reverse=reverse, finalize=finalize, rope=rope, nchunk=nchunk, d=d),
        grid=(1, nchunk),
        in_specs=in_specs,
        out_specs=[chunk(w, 0)] + [_batch_spec(s, bsz) for s in st_shapes],
        out_shape=[jax.ShapeDtypeStruct((bsz, seq, w), F32)]
                  + [jax.ShapeDtypeStruct((bsz,) + s, F32) for s in st_shapes],
        scratch_shapes=[pltpu.VMEM((bsz,) + s, F32) for s in st_shapes],
        compiler_params=pltpu.CompilerParams(dimension_semantics=("parallel", "arbitrary")),
        name="mlstm_scan",
    )(*args)
    return res[0], tuple(res[1:])


def mlstm_branch(p_l, p_c, conv_w, conv_b, i_bias, f_bias, need_ctx_out):
    bsz, seq, _ = p_l.shape
    w = BRANCH_W
    cos, sin = _rope_2d_tables(seq)
    tables = (jnp.tile(cos, (1, ML_HEADS)), jnp.tile(sin, (1, ML_HEADS)))
    zero_h = jnp.zeros_like(i_bias)
    ib = _lane_row(jnp.concatenate([i_bias, zero_h], axis=1).reshape(-1), 4 * ML_HEADS)
    fb = _lane_row(jnp.concatenate([zero_h, f_bias], axis=1).reshape(-1), 4 * ML_HEADS)
    params = (conv_w[:, :w], conv_b[:w].reshape(1, w), conv_w[:, w:], conv_b[w:].reshape(1, w), ib, fb)
    state0 = (jnp.zeros((bsz, ML_HEADS, ML_HD, ML_HD), F32), jnp.zeros((bsz, ML_HEADS, ML_HD), F32),
              jnp.zeros((bsz, ML_HEADS, ML_HD), F32))
    h_c = h_l = None
    for d, reverse in ((0, False), (1, True)):
        other_c = h_c if (d == 1 and need_ctx_out) else None
        h_c, st = _mlstm_pass(p_c, state0, other_c, None, params, reverse, d)
        h_l, _ = _mlstm_pass(p_l, st, h_l if d == 1 else None, tables, params, reverse, d)
    return h_l, (h_c if need_ctx_out else None)


MOE_TM = 512
ROUTE_LANES = 128


def _moe_router_kernel(x_ref, sc_ref, sh_ref, wh_ref, wl_ref, br_ref, hb_ref, rt_ref):
    h = x_ref[0] * (1.0 + sc_ref[0]) + sh_ref[0]
    h_hi = h.astype(BF16)
    hb_ref[0] = h_hi
    h_lo = (h - h_hi.astype(F32)).astype(BF16)
    logits = (jnp.dot(h_hi, wh_ref[...], preferred_element_type=F32)
              + jnp.dot(h_hi, wl_ref[...], preferred_element_type=F32)
              + jnp.dot(h_lo, wh_ref[...], preferred_element_type=F32)) + br_ref[...]
    lane = lax.broadcasted_iota(jnp.int32, logits.shape, 1)
    m1 = jnp.max(logits, axis=-1, keepdims=True)
    i1 = jnp.min(jnp.where(logits == m1, lane, ROUTE_LANES), axis=-1, keepdims=True)
    rest = jnp.where(lane == i1, MASK_NEG, logits)
    m2 = jnp.max(rest, axis=-1, keepdims=True)
    i2 = jnp.min(jnp.where(rest == m2, lane, ROUTE_LANES), axis=-1, keepdims=True)
    e2 = jnp.exp(m2 - m1)
    p1 = 1.0 / (1.0 + e2)
    p2 = e2 * p1
    rt_ref[0] = jnp.where(lane == 0, i1.astype(F32), jnp.where(lane == 1, i2.astype(F32),
                          jnp.where(lane == 2, p1, jnp.where(lane == 3, p2, 0.0))))


def moe_router(x, sc, sh, w_router, b_router, tm=512):
    bsz, seq, d = x.shape
    n_e = w_router.shape[1]
    tm = min(tm, seq)
    wr = jnp.zeros((d, ROUTE_LANES), F32).at[:, :n_e].set(w_router)
    wr_hi = wr.astype(BF16)
    wr_lo = (wr - wr_hi.astype(F32)).astype(BF16)
    br = jnp.full((1, ROUTE_LANES), MASK_NEG, F32).at[0, :n_e].set(b_router)
    return pl.pallas_call(
        _moe_router_kernel,
        grid=(bsz, seq // tm),
        in_specs=[pl.BlockSpec((1, tm, d), lambda b, i: (b, i, 0)),
                  pl.BlockSpec((1, 1, d), lambda b, i: (b, 0, 0)),
                  pl.BlockSpec((1, 1, d), lambda b, i: (b, 0, 0)),
                  pl.BlockSpec((d, ROUTE_LANES), lambda b, i: (0, 0)),
                  pl.BlockSpec((d, ROUTE_LANES), lambda b, i: (0, 0)),
                  pl.BlockSpec((1, ROUTE_LANES), lambda b, i: (0, 0))],
        out_specs=[pl.BlockSpec((1, tm, d), lambda b, i: (b, i, 0)),
                   pl.BlockSpec((1, tm, ROUTE_LANES), lambda b, i: (b, i, 0))],
        out_shape=[jax.ShapeDtypeStruct((bsz, seq, d), BF16), jax.ShapeDtypeStruct((bsz, seq, ROUTE_LANES), F32)],
        compiler_params=pltpu.CompilerParams(dimension_semantics=("parallel", "parallel")),
        name="moe_router",
    )(x, sc, sh, wr_hi, wr_lo, br)


def _route_tables(idx, n_e, tmg):
    n_tok = idx.shape[0]
    e_flat = idx.reshape(-1)
    onehot = (e_flat[:, None] == jnp.arange(n_e, dtype=jnp.int32)[None, :]).astype(jnp.int32)
    csum = jnp.cumsum(onehot, axis=0)
    rank = jnp.take_along_axis(csum - onehot, e_flat[:, None], axis=1)[:, 0]
    padded = ((csum[-1] + tmg - 1) // tmg) * tmg
    ends = jnp.cumsum(padded)
    pos = (ends - padded)[e_flat] + rank
    n_rows = TOP_K * n_tok + n_e * tmg
    ntiles = n_rows // tmg
    tile_start = jnp.arange(ntiles, dtype=jnp.int32) * tmg
    tile_expert = jnp.minimum(jnp.sum(tile_start[:, None] >= ends[None, :], axis=1), n_e - 1)
    src = (jnp.arange(n_rows, dtype=jnp.int32) % n_tok).at[pos].set(
        jnp.arange(TOP_K * n_tok, dtype=jnp.int32) // TOP_K)
    meta = jnp.concatenate([tile_expert, ends[-1:] // tmg]).astype(jnp.int32)
    return pos.reshape(n_tok, TOP_K), src, meta


def _expert_changed(meta_ref, i):
    return (i == 0) | (meta_ref[i] != meta_ref[jnp.maximum(i - 1, 0)])


def _moe_up_kernel(meta_ref, x_ref, wg_ref, wu_ref, a_ref, wgb_ref, wub_ref, *, ntiles):
    i = pl.program_id(1)

    @pl.when(_expert_changed(meta_ref, i))
    def _():
        wgb_ref[...] = wg_ref[0].astype(BF16)
        wub_ref[...] = wu_ref[0].astype(BF16)

    @pl.when(i < meta_ref[ntiles])
    def _():
        x = x_ref[...]
        g = jnp.dot(x, wgb_ref[...], preferred_element_type=F32)
        u = jnp.dot(x, wub_ref[...], preferred_element_type=F32)
        a_ref[...] = (_silu(g) * u).astype(BF16)

    @pl.when(i >= meta_ref[ntiles])
    def _():
        a_ref[...] = jnp.zeros_like(a_ref)


def _moe_down_kernel(meta_ref, a_ref, wd_ref, y_ref, wdb_ref, *, ntiles):
    i = pl.program_id(0)

    @pl.when(_expert_changed(meta_ref, i))
    def _():
        wdb_ref[...] = wd_ref[0].astype(BF16)

    @pl.when(i < meta_ref[ntiles])
    def _():
        y_ref[...] = jnp.dot(a_ref[...], wdb_ref[...], preferred_element_type=F32).astype(y_ref.dtype)

    @pl.when(i >= meta_ref[ntiles])
    def _():
        y_ref[...] = jnp.zeros_like(y_ref)


def moe_experts(xs, meta, wg, wu, wd, tmg, tf):
    n_rows, d = xs.shape
    n_e, _, f = wg.shape
    ntiles = n_rows // tmg
    nf = f // tf
    once = pl.Buffered(1)
    a = pl.pallas_call(
        functools.partial(_moe_up_kernel, ntiles=ntiles),
        grid_spec=pltpu.PrefetchScalarGridSpec(
            num_scalar_prefetch=1,
            grid=(nf, ntiles),
            in_specs=[pl.BlockSpec((tmg, d), lambda j, i, m: (i, 0)),
                      pl.BlockSpec((1, d, tf), lambda j, i, m: (m[i], 0, j), pipeline_mode=once),
                      pl.BlockSpec((1, d, tf), lambda j, i, m: (m[i], 0, j), pipeline_mode=once)],
            out_specs=pl.BlockSpec((tmg, tf), lambda j, i, m: (i, j)),
            scratch_shapes=[pltpu.VMEM((d, tf), BF16), pltpu.VMEM((d, tf), BF16)]),
        out_shape=jax.ShapeDtypeStruct((n_rows, f), BF16),
        compiler_params=pltpu.CompilerParams(
            dimension_semantics=("arbitrary", "arbitrary"), vmem_limit_bytes=V7X_VMEM_LIMIT),
        name="moe_up",
    )(meta, xs, wg, wu)
    return pl.pallas_call(
        functools.partial(_moe_down_kernel, ntiles=ntiles),
        grid_spec=pltpu.PrefetchScalarGridSpec(
            num_scalar_prefetch=1,
            grid=(ntiles,),
            in_specs=[pl.BlockSpec((tmg, f), lambda i, m: (i, 0)),
                      pl.BlockSpec((1, f, d), lambda i, m: (m[i], 0, 0), pipeline_mode=once)],
            out_specs=pl.BlockSpec((tmg, d), lambda i, m: (i, 0)),
            scratch_shapes=[pltpu.VMEM((f, d), BF16)]),
        out_shape=jax.ShapeDtypeStruct((n_rows, d), BF16),
        compiler_params=pltpu.CompilerParams(
            dimension_semantics=("arbitrary",), vmem_limit_bytes=V7X_VMEM_LIMIT),
        name="moe_down",
    )(meta, a, wd)


def _combine_ln_kernel(x_ref, gt_ref, rt_ref, ya_ref, yb_ref, g_ref, b_ref, o_ref, *, alpha):
    rt = rt_ref[0]
    y = rt[:, 2:3] * ya_ref[0].astype(F32) + rt[:, 3:4] * yb_ref[0].astype(F32)
    o_ref[0] = _layer_norm_rows(alpha * x_ref[0] + gt_ref[0] * y, g_ref[...], b_ref[...])


def combine_residual_ln(x, gate, route, ya, yb, ln_g, ln_b, alpha, tm=512):
    bsz, seq, d = x.shape
    tm = min(tm, seq)
    row = pl.BlockSpec((1, tm, d), lambda b, i: (b, i, 0))
    return pl.pallas_call(
        functools.partial(_combine_ln_kernel, alpha=alpha),
        grid=(bsz, seq // tm),
        in_specs=[row, pl.BlockSpec((1, 1, d), lambda b, i: (b, 0, 0)),
                  pl.BlockSpec((1, tm, ROUTE_LANES), lambda b, i: (b, i, 0)), row, row,
                  pl.BlockSpec((1, d), lambda b, i: (0, 0)), pl.BlockSpec((1, d), lambda b, i: (0, 0))],
        out_specs=row,
        out_shape=jax.ShapeDtypeStruct((bsz, seq, d), F32),
        compiler_params=pltpu.CompilerParams(
            dimension_semantics=("parallel", "parallel"), vmem_limit_bytes=V7X_VMEM_LIMIT),
        name="moe_combine_ln",
    )(x, gate, route, ya, yb, ln_g.reshape(1, d), ln_b.reshape(1, d))


def moe_residual_ln(x, sc, sh, gate, w_router, b_router, wg, wu, wd, ln_g, ln_b, alpha):
    bsz, seq, d = x.shape
    n_e, _, f = wg.shape
    n_tok = bsz * seq
    tmg = min(MOE_TM, TOP_K * n_tok)
    hb, route = moe_router(x, sc, sh, w_router, b_router)
    idx = route[..., :TOP_K].astype(jnp.int32).reshape(n_tok, TOP_K)
    pos, src, meta = _route_tables(idx, n_e, tmg)
    xs = hb.reshape(n_tok, d).at[src].get(mode="promise_in_bounds")
    ys = moe_experts(xs, meta, wg, wu, wd, tmg, f // 2)
    ya = ys.at[pos[:, 0]].get(mode="promise_in_bounds").reshape(bsz, seq, d)
    yb = ys.at[pos[:, 1]].get(mode="promise_in_bounds").reshape(bsz, seq, d)
    return combine_residual_ln(x, gate, route, ya, yb, ln_g, ln_b, alpha)


def _mixers(p_l, p_c, need_ctx_out, lru_conv_w, lru_conv_b, lru_wa, lru_ba, lru_wx, lru_bx, lru_lambda,
            na_rpb, ssd_conv_w, ssd_conv_b, ssd_dt_bias, ssd_a_log, ssd_d, ssd_norm_g, ml_conv_w, ml_conv_b,
            ml_i_bias, ml_f_bias):
    ya = lru_branch(p_l, p_c, lru_conv_w, lru_conv_b, lru_wa, lru_ba, lru_wx, lru_bx, lru_lambda, need_ctx_out)
    yb = na_branch(p_l, p_c, na_rpb, need_ctx_out)
    yc = ssd_branch(p_l, p_c, ssd_conv_w, ssd_conv_b, ssd_dt_bias, ssd_a_log, ssd_d, ssd_norm_g, need_ctx_out)
    yd = mlstm_branch(p_l, p_c, ml_conv_w, ml_conv_b, ml_i_bias, ml_f_bias, need_ctx_out)
    return (ya[0], yb[0], yc[0], yd[0]), (ya[1], yb[1], yc[1], yd[1])


def kernel(x, c, ctx, c_ctx, w_ada, b_ada, w_in, lru_conv_w, lru_conv_b, lru_wa, lru_ba, lru_wx, lru_bx, lru_lambda, na_rpb, ssd_conv_w, ssd_conv_b, ssd_dt_bias, ssd_a_log, ssd_d, ssd_norm_g, ml_conv_w, ml_conv_b, ml_i_bias, ml_f_bias, w_branch, w_out, ln_g, ln_b, ffn_w_gate, ffn_w_up, ffn_w_down, moe_w_router, moe_b_router, moe_w_gate, moe_w_up, moe_w_down):
    depth = w_in.shape[0]
    bsz, seq, d = x.shape
    alpha = (2.0 * depth) ** 0.25
    cvecs = jnp.zeros((8, d), F32).at[:bsz].set(c).at[bsz].set(c_ctx)
    mods = ada_modulation(cvecs, w_ada, b_ada)
    xl, xc = x, ctx
    for l in range(depth):
        need_ctx_out = l < depth - 1
        mod = mods[l]
        mod_l = jnp.split(mod[:bsz, None, :], 6, axis=-1)
        mod_c = jnp.split(jnp.broadcast_to(mod[bsz:bsz + 1, None, :], (bsz, 1, 6 * d)), 6, axis=-1)
        w_mix, w_gates = _pack_w_in(w_in[l])
        p_l, hb_l = mod_matmul(xl, mod_l[1], mod_l[0], w_mix)
        p_c, hb_c = mod_matmul(xc, mod_c[1], mod_c[0], w_mix)
        br_l, br_c = _mixers(p_l, p_c, need_ctx_out, lru_conv_w[l], lru_conv_b[l], lru_wa[l], lru_ba[l],
                             lru_wx[l], lru_bx[l], lru_lambda[l], na_rpb[l], ssd_conv_w[l], ssd_conv_b[l],
                             ssd_dt_bias[l], ssd_a_log[l], ssd_d[l], ssd_norm_g[l], ml_conv_w[l], ml_conv_b[l],
                             ml_i_bias[l], ml_f_bias[l])
        wb = w_branch[l].astype(BF16)
        wo = w_out[l].astype(BF16)
        xl = proj_residual_ln(merge_branches(hb_l, w_gates, br_l, wb), wo, xl, mod_l[2], ln_g[l, 0], ln_b[l, 0],
                              alpha)
        if need_ctx_out:
            xc = proj_residual_ln(merge_branches(hb_c, w_gates, br_c, wb), wo, xc, mod_c[2], ln_g[l, 0],
                                  ln_b[l, 0], alpha)
        j = l // 2
        if l % 2 == 0:
            wg = ffn_w_gate[j].astype(BF16)
            wu = ffn_w_up[j].astype(BF16)
            wd = ffn_w_down[j].astype(BF16)

            def ffn(h, m, wg=wg, wu=wu, wd=wd):
                return ffn_residual_ln(h, m[4], m[3], m[5], wg, wu, wd, ln_g[l, 1], ln_b[l, 1], alpha)
        else:
            def ffn(h, m, j=j):
                return moe_residual_ln(h, m[4], m[3], m[5], moe_w_router[j], moe_b_router[j], moe_w_gate[j],
                                       moe_w_up[j], moe_w_down[j], ln_g[l, 1], ln_b[l, 1], alpha)

        xl = ffn(xl, mod_l)
        if need_ctx_out:
            xc = ffn(xc, mod_c)
    return xl
```

```python
import functools
import math

import numpy as np
import jax
import jax.numpy as jnp
from jax import lax
from jax.experimental import pallas as pl
from jax.experimental.pallas import tpu as pltpu

F32 = jnp.float32
BF16 = jnp.bfloat16

D_MODEL = 2048
GRID_W = 64
N_BRANCH = 4
BRANCH_W = D_MODEL // N_BRANCH
CONV_W = 4
LN_EPS = 1e-5
LRU_BLOCKS = 8
LRU_BW = BRANCH_W // LRU_BLOCKS
LRU_C = 8.0
NA_HEADS = 8
NA_HD = BRANCH_W // NA_HEADS
NA_KH = 8
NA_KW = 16
SSD_HEADS = 8
SSD_HD = BRANCH_W // SSD_HEADS
SSD_GROUPS = 2
SSD_STATE = 64
SSD_CHUNK = 128
SSD_GN = SSD_GROUPS * SSD_STATE
ML_HEADS = 4
ML_HD = BRANCH_W // ML_HEADS
ML_CHUNK = 128
ROPE_BASE = 10000.0
N_EXPERTS = 8
TOP_K = 2

V7X_VMEM_LIMIT = 52 * 1024 * 1024

COL_LRU_X = 0
COL_LRU_G = 512
COL_NA = 1024
COL_SSD_Z = 2560
COL_SSD_X = 3072
COL_ML = 3584
COL_SSD_B = 5632
COL_SSD_C = 5760
COL_SSD_DT = 5888
COL_ML_G = 5904
N_MIX = 5920
N_MIX_PAD = 6144


def _pack_w_in(w):
    parts = [w[:, 0:1024], w[:, 1024:2560], w[:, 2560:3584], w[:, 3856:5904],
             w[:, 3584:3840], w[:, 3840:3856], w[:, 5904:5920],
             jnp.zeros((w.shape[0], N_MIX_PAD - N_MIX), w.dtype)]
    return jnp.concatenate(parts, axis=1).astype(BF16), w[:, N_MIX:].astype(BF16)


def _sigmoid(x):
    return 1.0 / (1.0 + jnp.exp(-x))


def _layer_norm_rows(z, g, b):
    mu = jnp.mean(z, axis=-1, keepdims=True)
    zc = z - mu
    var = jnp.mean(zc * zc, axis=-1, keepdims=True)
    return zc * lax.rsqrt(var + LN_EPS) * g + b


def _mod_matmul_kernel(x_ref, sc_ref, sh_ref, w_ref, o_ref, hb_ref):
    @pl.when(pl.program_id(2) == 0)
    def _():
        hb_ref[0] = (x_ref[0] * (1.0 + sc_ref[0]) + sh_ref[0]).astype(BF16)

    o_ref[0] = jnp.dot(hb_ref[0], w_ref[...], preferred_element_type=F32)


def mod_matmul(x, sc, sh, w, tm=1024, tn=1024):
    bsz, seq, d = x.shape
    n = w.shape[1]
    tm = min(tm, seq)
    return pl.pallas_call(
        _mod_matmul_kernel,
        grid=(bsz, seq // tm, n // tn),
        in_specs=[pl.BlockSpec((1, tm, d), lambda b, i, j: (b, i, 0)),
                  pl.BlockSpec((1, 1, d), lambda b, i, j: (b, 0, 0)),
                  pl.BlockSpec((1, 1, d), lambda b, i, j: (b, 0, 0)),
                  pl.BlockSpec((d, tn), lambda b, i, j: (0, j))],
        out_specs=[pl.BlockSpec((1, tm, tn), lambda b, i, j: (b, i, j)),
                   pl.BlockSpec((1, tm, d), lambda b, i, j: (b, i, 0))],
        out_shape=[jax.ShapeDtypeStruct((bsz, seq, n), F32), jax.ShapeDtypeStruct((bsz, seq, d), BF16)],
        compiler_params=pltpu.CompilerParams(
            dimension_semantics=("parallel", "parallel", "arbitrary"), vmem_limit_bytes=V7X_VMEM_LIMIT),
        name="in_proj",
    )(x, sc, sh, w)


def _merge_kernel(hb_ref, g0, g1, g2, g3, ya, yb, yc, yd, wb_ref, o_ref):
    hb = hb_ref[0]
    acc = None
    for n, (wg, y) in enumerate(((g0, ya), (g1, yb), (g2, yc), (g3, yd))):
        gate = _sigmoid(jnp.dot(hb, wg[...], preferred_element_type=F32))
        t = gate * jnp.dot(y[0].astype(BF16), wb_ref[n], preferred_element_type=F32)
        acc = t if acc is None else acc + t
    o_ref[0] = acc.astype(BF16)


def merge_branches(hb, w_gates, branches, wb, tm=512, tn=512):
    bsz, seq, dk = hb.shape
    d = wb.shape[2]
    tm = min(tm, seq)
    nj = d // tn
    g_specs = [pl.BlockSpec((dk, tn), functools.partial(lambda b, i, j, n: (0, n * nj + j), n=n))
               for n in range(N_BRANCH)]
    y_specs = [pl.BlockSpec((1, tm, BRANCH_W), lambda b, i, j: (b, i, 0)) for _ in range(N_BRANCH)]
    return pl.pallas_call(
        _merge_kernel,
        grid=(bsz, seq // tm, nj),
        in_specs=[pl.BlockSpec((1, tm, dk), lambda b, i, j: (b, i, 0))] + g_specs + y_specs
                 + [pl.BlockSpec((N_BRANCH, BRANCH_W, tn), lambda b, i, j: (0, 0, j))],
        out_specs=pl.BlockSpec((1, tm, tn), lambda b, i, j: (b, i, j)),
        out_shape=jax.ShapeDtypeStruct((bsz, seq, d), BF16),
        compiler_params=pltpu.CompilerParams(
            dimension_semantics=("parallel", "parallel", "arbitrary"), vmem_limit_bytes=V7X_VMEM_LIMIT),
        name="merge",
    )(hb, w_gates, w_gates, w_gates, w_gates, *branches, wb)


def _proj_ln_kernel(m_ref, w_ref, x_ref, gt_ref, g_ref, b_ref, o_ref, *, alpha):
    half = m_ref.shape[1] // 2
    ys = [jnp.dot(m_ref[0, n * half:(n + 1) * half, :], w_ref[...], preferred_element_type=F32) for n in range(2)]
    for n in range(2):
        rows = slice(n * half, (n + 1) * half)
        o_ref[0, rows, :] = _layer_norm_rows(alpha * x_ref[0, rows, :] + gt_ref[0] * ys[n], g_ref[...], b_ref[...])


def proj_residual_ln(m, w, x, gate, ln_g, ln_b, alpha, tm=512):
    bsz, seq, k = m.shape
    d = w.shape[1]
    tm = min(tm, seq)
    return pl.pallas_call(
        functools.partial(_proj_ln_kernel, alpha=alpha),
        grid=(bsz, seq // tm),
        in_specs=[pl.BlockSpec((1, tm, k), lambda b, i: (b, i, 0)),
                  pl.BlockSpec((k, d), lambda b, i: (0, 0)),
                  pl.BlockSpec((1, tm, d), lambda b, i: (b, i, 0)),
                  pl.BlockSpec((1, 1, d), lambda b, i: (b, 0, 0)),
                  pl.BlockSpec((1, d), lambda b, i: (0, 0)),
                  pl.BlockSpec((1, d), lambda b, i: (0, 0))],
        out_specs=pl.BlockSpec((1, tm, d), lambda b, i: (b, i, 0)),
        out_shape=jax.ShapeDtypeStruct((bsz, seq, d), F32),
        compiler_params=pltpu.CompilerParams(
            dimension_semantics=("parallel", "parallel"), vmem_limit_bytes=V7X_VMEM_LIMIT),
        name="out_proj_ln",
    )(m, w, x, gate, ln_g.reshape(1, d), ln_b.reshape(1, d))


def _ffn_kernel(x_ref, sc_ref, sh_ref, gt_ref, wg_ref, wu_ref, wd_ref, lg_ref, lb_ref, o_ref, hb_ref, acc_ref, *, alpha):
    j = pl.program_id(2)

    @pl.when(j == 0)
    def _():
        hb_ref[...] = (x_ref[0] * (1.0 + sc_ref[0]) + sh_ref[0]).astype(BF16)
        acc_ref[...] = jnp.zeros_like(acc_ref)

    hb = hb_ref[...]
    g = jnp.dot(hb, wg_ref[...], preferred_element_type=F32)
    u = jnp.dot(hb, wu_ref[...], preferred_element_type=F32)
    acc_ref[...] += jnp.dot((_silu(g) * u).astype(BF16), wd_ref[...], preferred_element_type=F32)

    @pl.when(j == pl.num_programs(2) - 1)
    def _():
        o_ref[0] = _layer_norm_rows(alpha * x_ref[0] + gt_ref[0] * acc_ref[...], lg_ref[...], lb_ref[...])


def ffn_residual_ln(x, sc, sh, gate, wg, wu, wd, ln_g, ln_b, alpha, tm=512, tf=512):
    bsz, seq, d = x.shape
    f = wg.shape[1]
    tm = min(tm, seq)
    vec = pl.BlockSpec((1, 1, d), lambda b, i, j: (b, 0, 0))
    par = pl.BlockSpec((1, d), lambda b, i, j: (0, 0))
    return pl.pallas_call(
        functools.partial(_ffn_kernel, alpha=alpha),
        grid=(bsz, seq // tm, f // tf),
        in_specs=[pl.BlockSpec((1, tm, d), lambda b, i, j: (b, i, 0)), vec, vec, vec,
                  pl.BlockSpec((d, tf), lambda b, i, j: (0, j)),
                  pl.BlockSpec((d, tf), lambda b, i, j: (0, j)),
                  pl.BlockSpec((tf, d), lambda b, i, j: (j, 0)), par, par],
        out_specs=pl.BlockSpec((1, tm, d), lambda b, i, j: (b, i, 0)),
        out_shape=jax.ShapeDtypeStruct((bsz, seq, d), F32),
        scratch_shapes=[pltpu.VMEM((tm, d), BF16), pltpu.VMEM((tm, d), F32)],
        compiler_params=pltpu.CompilerParams(
            dimension_semantics=("parallel", "parallel", "arbitrary"), vmem_limit_bytes=V7X_VMEM_LIMIT),
        name="ffn_ln",
    )(x, sc, sh, gate, wg, wu, wd, ln_g.reshape(1, d), ln_b.reshape(1, d))


def _ada_kernel(c_ref, w_ref, b_ref, o_ref):
    cv = c_ref[...]
    o_ref[0] = jnp.dot(cv * _sigmoid(cv), w_ref[0], precision=lax.Precision.HIGHEST,
                       preferred_element_type=F32) + b_ref[0]


def ada_modulation(cvecs, w, b, tn=1536):
    r, d = cvecs.shape
    depth, _, n = w.shape
    return pl.pallas_call(
        _ada_kernel,
        grid=(depth, n // tn),
        in_specs=[pl.BlockSpec((r, d), lambda l, j: (0, 0)), pl.BlockSpec((1, d, tn), lambda l, j: (l, 0, j)),
                  pl.BlockSpec((1, 1, tn), lambda l, j: (l, 0, j))],
        out_specs=pl.BlockSpec((1, r, tn), lambda l, j: (l, 0, j)),
        out_shape=jax.ShapeDtypeStruct((depth, r, n), F32),
        compiler_params=pltpu.CompilerParams(
            dimension_semantics=("parallel", "parallel"), vmem_limit_bytes=V7X_VMEM_LIMIT),
        name="ada_mod",
    )(cvecs, w, b.reshape(depth, 1, n))


NA_NEG = -1e30


def _na_bias_slabs(rpb):
    w = jnp.arange(GRID_W)
    cs = jnp.clip(w - NA_KW // 2, 0, GRID_W - NA_KW)
    ok = (w[None, :] >= cs[:, None]) & (w[None, :] < cs[:, None] + NA_KW)
    dc = jnp.clip(w[None, :] - w[:, None] + (NA_KW - 1), 0, 2 * NA_KW - 2)
    tab = jnp.where(ok, rpb[:, :, dc], NA_NEG)
    idx = jnp.arange(NA_KH)[:, None] + jnp.arange(NA_KH)[None, :]
    slab = tab[:, idx]
    return slab.transpose(1, 0, 3, 2, 4).reshape(NA_KH, NA_HEADS, GRID_W, NA_KH * GRID_W)


def _softmax2(s_a, s_b):
    m = jnp.maximum(jnp.max(s_a, axis=-1, keepdims=True), jnp.max(s_b, axis=-1, keepdims=True))
    e_a = jnp.exp(s_a - m)
    e_b = jnp.exp(s_b - m)
    inv = 1.0 / (jnp.sum(e_a, axis=-1, keepdims=True) + jnp.sum(e_b, axis=-1, keepdims=True))
    return e_a * inv, e_b * inv


_NT = (((1,), (1,)), ((), ()))


def _na_kernel(q_ref, kp_ref, kc_ref, kn_ref, vp_ref, vc_ref, vn_ref, ck_ref, cv_ref, bias_ref, o_ref,
               kw_ref, vw_ref, ckb_ref, cvb_ref, *, rows):
    i = pl.program_id(1)
    blk = NA_KH * GRID_W
    for n, (kr, vr) in enumerate(((kp_ref, vp_ref), (kc_ref, vc_ref), (kn_ref, vn_ref))):
        kw_ref[n * blk:(n + 1) * blk, :] = kr[0].astype(BF16)
        vw_ref[n * blk:(n + 1) * blk, :] = vr[0].astype(BF16)
    ckb_ref[...] = ck_ref[0].astype(BF16)
    cvb_ref[...] = cv_ref[0].astype(BF16)
    scale = NA_HD ** -0.5

    first = lax.broadcasted_iota(jnp.int32, (GRID_W, 2 * NA_HD), 1) < NA_HD
    pairs = [slice(n * 2 * NA_HD, (n + 1) * 2 * NA_HD) for n in range(NA_HEADS // 2)]
    rows_per_trip = 8

    def body(t, carry):
        trip = []
        for sub in range(rows_per_trip):
            rr = t * rows_per_trip + sub
            r = i * NA_KH + rr
            rs = jnp.clip(r - NA_KH // 2, 0, rows - NA_KH)
            off = pl.multiple_of((rs - (i - 1) * NA_KH) * GRID_W, GRID_W)
            v = rs - r + (NA_KH - 1)
            q_all = (q_ref[0, pl.ds(pl.multiple_of(rr * GRID_W, GRID_W), GRID_W), :] * scale).astype(BF16)
            trip.append((rr, v, q_all, kw_ref[pl.ds(off, blk), :], vw_ref[pl.ds(off, blk), :]))
        scores = []
        for rr, v, q_all, kwin, vwin in trip:
            for h in range(NA_HEADS):
                ps = pairs[h // 2]
                q = jnp.where(first if h % 2 == 0 else ~first, q_all[:, ps], jnp.zeros((), BF16))
                scores.append((lax.dot_general(q, kwin[:, ps], _NT, preferred_element_type=F32) + bias_ref[v, h],
                               lax.dot_general(q, ckb_ref[:, ps], _NT, preferred_element_type=F32)))
        probs = [_softmax2(s_w, s_c) for s_w, s_c in scores]
        both = [jnp.dot(p_w.astype(BF16), trip[n // NA_HEADS][4][:, pairs[(n % NA_HEADS) // 2]],
                        preferred_element_type=F32)
                + jnp.dot(p_c.astype(BF16), cvb_ref[:, pairs[(n % NA_HEADS) // 2]], preferred_element_type=F32)
                for n, (p_w, p_c) in enumerate(probs)]
        for sub, (rr, _, _, _, _) in enumerate(trip):
            base = sub * NA_HEADS
            outs = [jnp.where(first, both[base + 2 * n], both[base + 2 * n + 1]) for n in range(NA_HEADS // 2)]
            o_ref[0, pl.ds(pl.multiple_of(rr * GRID_W, GRID_W), GRID_W), :] = jnp.concatenate(outs, axis=-1)
        return carry

    lax.fori_loop(0, NA_KH // rows_per_trip, body, 0)


def _na_ctx_kernel(q_ref, k_ref, v_ref, o_ref):
    scale = NA_HD ** -0.5
    q_all = (q_ref[0] * scale).astype(BF16)
    k_all = k_ref[0].astype(BF16)
    v_all = v_ref[0].astype(BF16)
    outs = []
    for h in range(NA_HEADS):
        hs = slice(h * NA_HD, (h + 1) * NA_HD)
        s = lax.dot_general(q_all[:, hs], k_all[:, hs], _NT, preferred_element_type=F32)
        e = jnp.exp(s - jnp.max(s, axis=-1, keepdims=True))
        p = e * (1.0 / jnp.sum(e, axis=-1, keepdims=True))
        outs.append(jnp.dot(p.astype(BF16), v_all[:, hs], preferred_element_type=F32))
    o_ref[0] = jnp.concatenate(outs, axis=-1)


def na_branch(p_l, p_c, rpb, need_ctx_out):
    bsz, seq, _ = p_l.shape
    n_ctx = p_c.shape[1]
    rows = seq // GRID_W
    assert rows % NA_KH == 0 and rows >= 2 * NA_KH
    w = BRANCH_W
    blk = NA_KH * GRID_W
    nblk = rows // NA_KH
    cq = COL_NA // w

    def shifted(col, delta):
        return pl.BlockSpec((1, blk, w), lambda b, i: (b, jnp.clip(i + delta, 0, nblk - 1), col))

    y_l = pl.pallas_call(
        functools.partial(_na_kernel, rows=rows),
        grid=(bsz, nblk),
        in_specs=[shifted(cq, 0), shifted(cq + 1, -1), shifted(cq + 1, 0), shifted(cq + 1, 1),
                  shifted(cq + 2, -1), shifted(cq + 2, 0), shifted(cq + 2, 1),
                  pl.BlockSpec((1, n_ctx, w), lambda b, i: (b, 0, cq + 1)),
                  pl.BlockSpec((1, n_ctx, w), lambda b, i: (b, 0, cq + 2)),
                  pl.BlockSpec((NA_KH, NA_HEADS, GRID_W, blk), lambda b, i: (0, 0, 0, 0))],
        out_specs=pl.BlockSpec((1, blk, w), lambda b, i: (b, i, 0)),
        out_shape=jax.ShapeDtypeStruct((bsz, seq, w), F32),
        scratch_shapes=[pltpu.VMEM((3 * blk, w), BF16), pltpu.VMEM((3 * blk, w), BF16),
                        pltpu.VMEM((n_ctx, w), BF16), pltpu.VMEM((n_ctx, w), BF16)],
        compiler_params=pltpu.CompilerParams(
            dimension_semantics=("parallel", "arbitrary"), vmem_limit_bytes=V7X_VMEM_LIMIT),
        name="na_attn",
    )(p_l, p_l, p_l, p_l, p_l, p_l, p_l, p_c, p_c, _na_bias_slabs(rpb))
    y_c = None
    if need_ctx_out:
        y_c = pl.pallas_call(
            _na_ctx_kernel,
            grid=(bsz,),
            in_specs=[pl.BlockSpec((1, n_ctx, w), functools.partial(lambda b, c: (b, 0, c), c=cq + n)) for n in range(3)],
            out_specs=pl.BlockSpec((1, n_ctx, w), lambda b: (b, 0, 0)),
            out_shape=jax.ShapeDtypeStruct((bsz, n_ctx, w), F32),
            compiler_params=pltpu.CompilerParams(dimension_semantics=("parallel",)),
            name="na_ctx_attn",
        )(p_c, p_c, p_c)
    return y_l, y_c


def _gelu_tanh(x):
    return 0.5 * x * (1.0 + jnp.tanh(math.sqrt(2.0 / math.pi) * (x + 0.044715 * (x * x * x))))


def _silu(x):
    return x * _sigmoid(x)


def _softplus(x):
    return jnp.maximum(x, 0.0) + jnp.log1p(jnp.exp(-jnp.abs(x)))


def _rope_2d_tables(seq):
    t = jnp.arange(seq, dtype=jnp.int32)
    pos = jnp.stack([t // GRID_W, t % GRID_W], axis=-1).astype(F32)
    nf = ML_HD // 4
    inv_freq = ROPE_BASE ** (-jnp.arange(nf, dtype=F32) / nf)
    ang = jnp.broadcast_to(pos[:, :, None, None] * inv_freq, (seq, 2, 2, nf)).reshape(seq, ML_HD)
    return jnp.cos(ang), jnp.sin(ang)


def _chunk_specs(tt, nchunk, n8, reverse, nb=1):
    def pos(j):
        return nchunk - 1 - j if reverse else j

    def chunk(width, col):
        return pl.BlockSpec((nb, tt, width), lambda b, j: (b, pos(j), col))

    def halo(width, col, delta):
        if delta < 0:
            return pl.BlockSpec((nb, 8, width), lambda b, j: (b, jnp.maximum(pos(j) * (tt // 8) - 1, 0), col))
        return pl.BlockSpec((nb, 8, width), lambda b, j: (b, jnp.minimum((pos(j) + 1) * (tt // 8), n8 - 1), col))

    return chunk, halo


def _const_spec(shape):
    return pl.BlockSpec(shape, lambda b, j: (0,) * len(shape))


def _batch_spec(shape, nb=1):
    return pl.BlockSpec((nb,) + shape, lambda b, j: (b,) + (0,) * len(shape))


def _roll_in_blocks(x, shift):
    rows, w = x.shape
    return pltpu.roll(x.reshape(rows // 8, 8, w), shift, 1).reshape(rows, w)


def _conv4(x, prev8, next8, has_prev, has_next, cw, cb):
    tt = x.shape[0]
    row = lax.broadcasted_iota(jnp.int32, x.shape, 0)
    p6 = prev8[6:7, :] * has_prev
    p7 = prev8[7:8, :] * has_prev
    n0 = next8[0:1, :] * has_next
    xm1 = jnp.where(row == 0, p7, pltpu.roll(x, 1, 0))
    xm2 = jnp.where(row == 0, p6, jnp.where(row == 1, p7, pltpu.roll(x, 2, 0)))
    xp1 = jnp.where(row == tt - 1, n0, pltpu.roll(x, tt - 1, 0))
    return cw[0:1, :] * xm2 + cw[1:2, :] * xm1 + cw[2:3, :] * x + cw[3:4, :] * xp1 + cb


def _cumsum_rows(x, reverse):
    n = x.shape[0]
    row = lax.broadcasted_iota(jnp.int32, x.shape, 0)
    s = 1
    while s < n:
        if reverse:
            x = x + jnp.where(row < n - s, pltpu.roll(x, n - s, 0), 0.0)
        else:
            x = x + jnp.where(row >= s, pltpu.roll(x, s, 0), 0.0)
        s *= 2
    return x


def _causal_mask(n, reverse):
    ii = lax.broadcasted_iota(jnp.int32, (n, n), 0)
    jj = lax.broadcasted_iota(jnp.int32, (n, n), 1)
    return (jj >= ii) if reverse else (jj <= ii)


MASK_NEG = -1e30


LRU_TT = 256


def _lru_kernel(*refs, reverse, finalize, nchunk):
    if finalize:
        (xp_ref, x_ref, xn_ref, g_ref, ho_ref, h0_ref, cw_ref, cb_ref, w_ref, bias_ref, sp_ref,
         o_ref, hl_ref, carry_ref) = refs
    else:
        (xp_ref, x_ref, xn_ref, h0_ref, cw_ref, cb_ref, w_ref, bias_ref, sp_ref, o_ref, hl_ref, carry_ref) = refs
    j = pl.program_id(1)
    c = (nchunk - 1 - j) if reverse else j

    @pl.when(j == 0)
    def _():
        carry_ref[...] = h0_ref[0]

    tt = x_ref.shape[1]
    w = x_ref.shape[2]
    xc = _conv4(x_ref[0], xp_ref[0], xn_ref[0], (c > 0).astype(F32), (c < nchunk - 1).astype(F32),
                cw_ref[...], cb_ref[...])
    g = jnp.dot(xc.astype(BF16), w_ref[...], preferred_element_type=F32)
    r = _sigmoid(g[:, :w] + bias_ref[0:1, :])
    ig = _sigmoid(g[:, w:] + bias_ref[1:2, :])
    log_a = -LRU_C * r * sp_ref[...]
    a = jnp.exp(log_a)
    u = jnp.sqrt(1.0 - jnp.exp(2.0 * log_a)) * (ig * xc)
    sub = lax.broadcasted_iota(jnp.int32, (tt, w), 0) % 8
    s = 1
    while s < 8:
        if reverse:
            keep = sub < 8 - s
            a_s = jnp.where(keep, _roll_in_blocks(a, 8 - s), 1.0)
            u_s = jnp.where(keep, _roll_in_blocks(u, 8 - s), 0.0)
        else:
            keep = sub >= s
            a_s = jnp.where(keep, _roll_in_blocks(a, s), 1.0)
            u_s = jnp.where(keep, _roll_in_blocks(u, s), 0.0)
        u = a * u_s + u
        a = a * a_s
        s *= 2
    new_carry = carry_ref[...]
    nblk = tt // 8
    hs = [None] * nblk
    for blk in (reversed(range(nblk)) if reverse else range(nblk)):
        hb = u[blk * 8:(blk + 1) * 8, :] + a[blk * 8:(blk + 1) * 8, :] * new_carry
        new_carry = hb[0:1, :] if reverse else hb[7:8, :]
        hs[blk] = hb
    h = jnp.concatenate(hs, axis=0)
    carry_ref[...] = new_carry
    hl_ref[0] = new_carry
    if finalize:
        o_ref[0] = (ho_ref[0] + h) * _gelu_tanh(g_ref[0])
    else:
        o_ref[0] = h


def _lru_pass(p, h0, other, cw, cb, wcat, bias, sp, reverse):
    bsz, seq, _ = p.shape
    w = BRANCH_W
    tt = min(LRU_TT, seq)
    nchunk = seq // tt
    chunk, halo = _chunk_specs(tt, nchunk, seq // 8, reverse)
    cx = COL_LRU_X // w
    finalize = other is not None
    in_specs = [halo(w, cx, -1), chunk(w, cx), halo(w, cx, 1)]
    args = [p, p, p]
    if finalize:
        in_specs += [chunk(w, COL_LRU_G // w), chunk(w, 0)]
        args += [p, other]
    in_specs += [_batch_spec((1, w)), _const_spec((CONV_W, w)), _const_spec((1, w)),
                 _const_spec((w, 2 * w)), _const_spec((2, w)), _const_spec((1, w))]
    args += [h0, cw, cb, wcat, bias, sp]
    return pl.pallas_call(
        functools.partial(_lru_kernel, reverse=reverse, finalize=finalize, nchunk=nchunk),
        grid=(bsz, nchunk),
        in_specs=in_specs,
        out_specs=[chunk(w, 0), _batch_spec((1, w))],
        out_shape=[jax.ShapeDtypeStruct((bsz, seq, w), F32), jax.ShapeDtypeStruct((bsz, 1, w), F32)],
        scratch_shapes=[pltpu.VMEM((1, w), F32)],
        compiler_params=pltpu.CompilerParams(dimension_semantics=("parallel", "arbitrary")),
        name="lru_scan",
    )(*args)


def _block_diag(wg):
    g, n, _ = wg.shape
    eye = jnp.eye(g, dtype=wg.dtype)
    return (wg[:, :, None, :] * eye[:, None, :, None]).reshape(g * n, g * n)


def lru_branch(p_l, p_c, conv_w, conv_b, wa, ba, wx, bx, lam, need_ctx_out):
    bsz = p_l.shape[0]
    w = BRANCH_W
    cb = conv_b.reshape(1, w)
    sp = jax.nn.softplus(-lam)
    zeros = jnp.zeros((bsz, 1, w), F32)
    h_c = h_l = None
    for d, reverse in ((0, False), (1, True)):
        wcat = jnp.concatenate([_block_diag(wa[d]), _block_diag(wx[d])], axis=1).astype(BF16)
        bias = jnp.stack([ba[d], bx[d]])
        other_c = h_c if (d == 1 and need_ctx_out) else None
        h_c, st = _lru_pass(p_c, zeros, other_c, conv_w, cb, wcat, bias, sp[d:d + 1], reverse)
        h_l, _ = _lru_pass(p_l, st, h_l if d == 1 else None, conv_w, cb, wcat, bias, sp[d:d + 1], reverse)
    return h_l, (h_c if need_ctx_out else None)


def _ssd_kernel(*refs, reverse, finalize, nchunk, d):
    (xp_ref, x_ref, xn_ref, bp_ref, bc_ref, bn_ref, dt_ref) = refs[:7]
    k = 7
    if finalize:
        z_ref, yp_ref = refs[k:k + 2]
        k += 2
    s0_ref, cwx_ref, cbx_ref, cwb_ref, cbb_ref, dtb_ref, a_ref = refs[k:k + 7]
    k += 7
    if finalize:
        dsk_ref, ng_ref = refs[k:k + 2]
        k += 2
    o_ref, so_ref, s_ref = refs[k:k + 3]
    j = pl.program_id(1)
    c = (nchunk - 1 - j) if reverse else j

    @pl.when(j == 0)
    def _():
        s_ref[...] = s0_ref[...]

    nb = x_ref.shape[0]
    q = x_ref.shape[1]
    has_prev = (c > 0).astype(F32)
    has_next = (c < nchunk - 1).astype(F32)
    mask = _causal_mask(q, reverse)
    first = lax.broadcasted_iota(jnp.int32, (q, 2 * SSD_HD), 1) < SSD_HD
    npair = SSD_HEADS // 2
    pair_group = [(2 * n) // (SSD_HEADS // SSD_GROUPS) for n in range(npair)]
    pairs = [slice(n * 2 * SSD_HD, (n + 1) * 2 * SSD_HD) for n in range(npair)]

    def per_lane(t, n):
        c0 = d * SSD_HEADS + 2 * n
        return jnp.where(first[:t.shape[0]], t[:, c0:c0 + 1], t[:, c0 + 1:c0 + 2])

    xs, xb, dt_t, cum, cum_t, w_end, ecum, etot, b_t, cgs, cbs = [], [], [], [], [], [], [], [], [], [], []
    for bi in range(nb):
        xs.append(_silu(_conv4(x_ref[bi], xp_ref[bi], xn_ref[bi], has_prev, has_next, cwx_ref[...], cbx_ref[...])))
        bc = _silu(_conv4(bc_ref[bi], bp_ref[bi], bn_ref[bi], has_prev, has_next, cwb_ref[...], cbb_ref[...]))
        dt = _softplus(dt_ref[bi] + dtb_ref[...])
        cum.append(_cumsum_rows(dt * a_ref[...], reverse))
        tot = cum[bi][0:1, :] if reverse else cum[bi][q - 1:q, :]
        w_end.append(jnp.exp(tot - cum[bi]) * dt)
        ecum.append(jnp.exp(cum[bi]))
        etot.append(jnp.exp(tot))
        dt_t.append(dt.T)
        cum_t.append(cum[bi].T)
        b_t.append(bc[:, :SSD_GN].T)
        xb.append(xs[bi].astype(BF16))
        cgs.append([bc[:, SSD_GN + g * SSD_STATE:SSD_GN + (g + 1) * SSD_STATE].astype(BF16)
                    for g in range(SSD_GROUPS)])
        cbs.append([lax.dot_general(cgs[bi][g], bc[:, g * SSD_STATE:(g + 1) * SSD_STATE].astype(BF16), _NT,
                                    preferred_element_type=F32) for g in range(SSD_GROUPS)])
    intra = []
    for bi in range(nb):
        for h in range(SSD_HEADS):
            col = d * SSD_HEADS + h
            diff = cum[bi][:, col:col + 1] - cum_t[bi][col:col + 1, :]
            m = (cbs[bi][pair_group[h // 2]] * jnp.exp(jnp.where(mask, diff, MASK_NEG))
                 * dt_t[bi][col:col + 1, :])
            intra.append(jnp.dot(m.astype(BF16), xb[bi][:, pairs[h // 2]], preferred_element_type=F32))
    items = [(bi, n) for bi in range(nb) for n in range(npair)]
    s_old = [s_ref[bi, n] for bi, n in items]
    inter = [jnp.dot(cgs[bi][pair_group[n]], s_old[t].astype(BF16), preferred_element_type=F32)
             for t, (bi, n) in enumerate(items)]
    outs = []
    for t, (bi, n) in enumerate(items):
        g = pair_group[n]
        h0 = bi * SSD_HEADS + 2 * n
        outs.append(jnp.where(first, intra[h0], intra[h0 + 1]) + per_lane(ecum[bi], n) * inter[t])
        xw = (xs[bi][:, pairs[n]] * per_lane(w_end[bi], n)).astype(BF16)
        s_ref[bi, n] = per_lane(etot[bi], n) * s_old[t] + jnp.dot(
            b_t[bi][g * SSD_STATE:(g + 1) * SSD_STATE, :].astype(BF16), xw, preferred_element_type=F32)
    so_ref[...] = s_ref[...]
    for bi in range(nb):
        y = jnp.concatenate(outs[bi * npair:(bi + 1) * npair], axis=-1)
        if finalize:
            yt = (xs[bi] * dsk_ref[...] + yp_ref[bi] + y) * _silu(z_ref[bi])
            o_ref[bi] = yt * lax.rsqrt(jnp.mean(yt * yt, axis=-1, keepdims=True) + LN_EPS) * ng_ref[...]
        else:
            o_ref[bi] = y


def _ssd_pass(p, s0, other, params, reverse, d):
    bsz, seq, _ = p.shape
    w = BRANCH_W
    q = SSD_CHUNK
    nchunk = seq // q
    chunk, halo = _chunk_specs(q, nchunk, seq // 8, reverse, nb=bsz)
    cx, cb2, cdt = COL_SSD_X // w, COL_SSD_B // (2 * SSD_GN), COL_SSD_DT // 128
    finalize = other is not None
    cwx, cbx, cwb, cbb, dtb, arow, dsk, ng = params
    in_specs = [halo(w, cx, -1), chunk(w, cx), halo(w, cx, 1),
                halo(2 * SSD_GN, cb2, -1), chunk(2 * SSD_GN, cb2), halo(2 * SSD_GN, cb2, 1), chunk(128, cdt)]
    args = [p] * 7
    if finalize:
        in_specs += [chunk(w, COL_SSD_Z // w), chunk(w, 0)]
        args += [p, other]
    st_shape = (SSD_HEADS // 2, SSD_STATE, 2 * SSD_HD)
    in_specs += [_batch_spec(st_shape, bsz), _const_spec((CONV_W, w)), _const_spec((1, w)),
                 _const_spec((CONV_W, 2 * SSD_GN)), _const_spec((1, 2 * SSD_GN)), _const_spec((1, 128)),
                 _const_spec((1, 128))]
    args += [s0, cwx, cbx, cwb, cbb, dtb, arow]
    if finalize:
        in_specs += [_const_spec((1, w)), _const_spec((1, w))]
        args += [dsk, ng]
    return pl.pallas_call(
        functools.partial(_ssd_kernel, reverse=reverse, finalize=finalize, nchunk=nchunk, d=d),
        grid=(1, nchunk),
        in_specs=in_specs,
        out_specs=[chunk(w, 0), _batch_spec(st_shape, bsz)],
        out_shape=[jax.ShapeDtypeStruct((bsz, seq, w), F32), jax.ShapeDtypeStruct((bsz,) + st_shape, F32)],
        scratch_shapes=[pltpu.VMEM((bsz,) + st_shape, F32)],
        compiler_params=pltpu.CompilerParams(dimension_semantics=("parallel", "arbitrary")),
        name="ssd_scan",
    )(*args)


def _lane_row(vals, start):
    return jnp.zeros((128,), F32).at[start:start + vals.shape[0]].set(vals.astype(F32)).reshape(1, 128)


def ssd_branch(p_l, p_c, conv_w, conv_b, dt_bias, a_log, d_skip, norm_g, need_ctx_out):
    bsz = p_l.shape[0]
    w = BRANCH_W
    params = (conv_w[:, :w], conv_b[:w].reshape(1, w), conv_w[:, w:], conv_b[w:].reshape(1, 2 * SSD_GN),
              _lane_row(dt_bias.reshape(-1), 0), _lane_row(-jnp.exp(a_log.astype(F32)).reshape(-1), 0),
              jnp.repeat(d_skip, SSD_HD).reshape(1, w), norm_g.reshape(1, w))
    zeros = jnp.zeros((bsz, SSD_HEADS // 2, SSD_STATE, 2 * SSD_HD), F32)
    y_c = y_l = None
    for d, reverse in ((0, False), (1, True)):
        other_c = y_c if (d == 1 and need_ctx_out) else None
        y_c, st = _ssd_pass(p_c, zeros, other_c, params, reverse, d)
        y_l, _ = _ssd_pass(p_l, st, y_l if d == 1 else None, params, reverse, d)
    return y_l, (y_c if need_ctx_out else None)


def _rope_rotate(x):
    wl = x.shape[-1]
    half = ML_HD // 4
    lane = lax.broadcasted_iota(jnp.int32, x.shape, 1)
    return jnp.where(lane % (2 * half) < half, -pltpu.roll(x, wl - half, 1), pltpu.roll(x, half, 1))


def _log_sigmoid(x):
    return jnp.minimum(x, 0.0) - jnp.log1p(jnp.exp(-jnp.abs(x)))


def _mlstm_kernel(*refs, reverse, finalize, rope, nchunk, d):
    (qp_ref, q_ref, qn_ref, kp_ref, k_ref, kn_ref, v_ref, g_ref) = refs[:8]
    n = 8
    if rope:
        cos_ref, sin_ref = refs[n:n + 2]
        n += 2
    if finalize:
        og_ref, hp_ref = refs[n:n + 2]
        n += 2
    c0_ref, n0_ref, m0_ref, cwq_ref, cbq_ref, cwk_ref, cbk_ref, ib_ref, fb_ref = refs[n:n + 9]
    n += 9
    o_ref, co_ref, no_ref, mo_ref, c_ref, n_ref, m_ref = refs[n:n + 7]
    j = pl.program_id(1)
    c = (nchunk - 1 - j) if reverse else j

    @pl.when(j == 0)
    def _():
        c_ref[...] = c0_ref[...]
        n_ref[...] = n0_ref[...]
        m_ref[...] = m0_ref[...]

    nb = q_ref.shape[0]
    qn = q_ref.shape[1]
    has_prev = (c > 0).astype(F32)
    has_next = (c < nchunk - 1).astype(F32)
    mask = _causal_mask(qn, reverse)
    heads = [slice(h * ML_HD, (h + 1) * ML_HD) for h in range(ML_HEADS)]
    items = [(bi, h) for bi in range(nb) for h in range(ML_HEADS)]
    q, k, v, b, li, tot, b_t, li_t = [], [], [], [], [], [], [], []
    for bi in range(nb):
        qi = _silu(_conv4(q_ref[bi], qp_ref[bi], qn_ref[bi], has_prev, has_next, cwq_ref[...], cbq_ref[...]))
        ki = _silu(_conv4(k_ref[bi], kp_ref[bi], kn_ref[bi], has_prev, has_next, cwk_ref[...], cbk_ref[...]))
        if rope:
            qi = qi * cos_ref[...] + _rope_rotate(qi) * sin_ref[...]
            ki = ki * cos_ref[...] + _rope_rotate(ki) * sin_ref[...]
        q.append(qi * (ML_HD ** -0.5))
        k.append(ki)
        v.append(v_ref[bi])
        gb = g_ref[bi]
        li.append(gb + ib_ref[...])
        b.append(_cumsum_rows(_log_sigmoid(gb + fb_ref[...]), reverse))
        tot.append(b[bi][0:1, :] if reverse else b[bi][qn - 1:qn, :])
        b_t.append(b[bi].T)
        li_t.append(li[bi].T)
    qb = [t.astype(BF16) for t in q]
    kb = [t.astype(BF16) for t in k]
    vb = [t.astype(BF16) for t in v]
    qk = [lax.dot_general(qb[bi][:, heads[h]], kb[bi][:, heads[h]], _NT, preferred_element_type=F32)
          for bi, h in items]
    c_old = [c_ref[bi, h] for bi, h in items]
    n_old = [n_ref[bi, h:h + 1, :] for bi, h in items]
    qc = [lax.dot_general(qb[bi][:, heads[h]], c_old[t].astype(BF16), _NT, preferred_element_type=F32)
          for t, (bi, h) in enumerate(items)]
    gate = []
    for bi, h in items:
        ci = 4 * ML_HEADS + d * 2 * ML_HEADS + h
        cf = ci + ML_HEADS
        b_c = b[bi][:, cf:cf + 1]
        b_end = tot[bi][:, cf:cf + 1]
        m_st = m_ref[bi, h:h + 1, 0:1]
        end_log = b_end - b_c + li[bi][:, ci:ci + 1]
        m_new = jnp.maximum(b_end + m_st, jnp.max(end_log, axis=0, keepdims=True))
        dlog = jnp.where(mask, b_c - b_t[bi][cf:cf + 1, :] + li_t[bi][ci:ci + 1, :], MASK_NEG)
        m_inter = b_c + m_st
        m_i = jnp.maximum(jnp.max(dlog, axis=1, keepdims=True), m_inter)
        gate.append((jnp.exp(end_log - m_new), jnp.exp(b_end + m_st - m_new), m_new,
                     jnp.exp(dlog - m_i), jnp.exp(m_inter - m_i), jnp.exp(-m_i)))
    s_all = [qk[t] * gate[t][3] for t in range(len(items))]
    sv = [jnp.dot(s_all[t].astype(BF16), vb[bi][:, heads[h]], preferred_element_type=F32)
          for t, (bi, h) in enumerate(items)]
    upd = [jnp.dot((v[bi][:, heads[h]] * gate[t][0]).T.astype(BF16), kb[bi][:, heads[h]],
                   preferred_element_type=F32) for t, (bi, h) in enumerate(items)]
    outs = []
    for t, (bi, h) in enumerate(items):
        hs = heads[h]
        w, carry_scale, m_new, _, w_in, floor = gate[t]
        num = sv[t] + w_in * qc[t]
        den = (jnp.sum(s_all[t], axis=1, keepdims=True)
               + w_in * jnp.sum(q[bi][:, hs] * n_old[t], axis=1, keepdims=True))
        outs.append(num / jnp.maximum(jnp.abs(den), floor))
        c_ref[bi, h] = carry_scale * c_old[t] + upd[t]
        n_ref[bi, h:h + 1, :] = carry_scale * n_old[t] + jnp.sum(k[bi][:, hs] * w, axis=0, keepdims=True)
        m_ref[bi, h:h + 1, :] = jnp.broadcast_to(m_new, (1, ML_HD))
    co_ref[...] = c_ref[...]
    no_ref[...] = n_ref[...]
    mo_ref[...] = m_ref[...]
    for bi in range(nb):
        hout = jnp.concatenate(outs[bi * ML_HEADS:(bi + 1) * ML_HEADS], axis=-1)
        if finalize:
            o_ref[bi] = _sigmoid(og_ref[bi]) * (hp_ref[bi] + hout)
        else:
            o_ref[bi] = hout


def _mlstm_pass(p, state, other, tables, params, reverse, d):
    bsz, seq, _ = p.shape
    w = BRANCH_W
    qn = ML_CHUNK
    nchunk = seq // qn
    chunk, halo = _chunk_specs(qn, nchunk, seq // 8, reverse, nb=bsz)
    cq = COL_ML // w
    finalize = other is not None
    rope = tables is not None
    in_specs = [halo(w, cq, -1), chunk(w, cq), halo(w, cq, 1), halo(w, cq + 1, -1), chunk(w, cq + 1),
                halo(w, cq + 1, 1), chunk(w, cq + 2), chunk(128, COL_ML_G // 128)]
    args = [p] * 8
    if rope:
        tab = pl.BlockSpec((qn, w), (lambda b, j: (nchunk - 1 - j, 0)) if reverse else (lambda b, j: (j, 0)))
        in_specs += [tab, tab]
        args += list(tables)
    if finalize:
        in_specs += [chunk(w, cq + 3), chunk(w, 0)]
        args += [p, other]
    st_shapes = [(ML_HEADS, ML_HD, ML_HD), (ML_HEADS, ML_HD), (ML_HEADS, ML_HD)]
    in_specs += [_batch_spec(s, bsz) for s in st_shapes]
    in_specs += [_const_spec((CONV_W, w)), _const_spec((1, w)), _const_spec((CONV_W, w)), _const_spec((1, w)),
                 _const_spec((1, 128)), _const_spec((1, 128))]
    args += list(state) + list(params)
    res = pl.pallas_call(
        functools.partial(_mlstm_kernel, reverse=reverse, finalize=finalize, rope=rope, nchunk=nchunk, d=d),
        grid=(1, nchunk),
        in_specs=in_specs,
        out_specs=[chunk(w, 0)] + [_batch_spec(s, bsz) for s in st_shapes],
        out_shape=[jax.ShapeDtypeStruct((bsz, seq, w), F32)]
                  + [jax.ShapeDtypeStruct((bsz,) + s, F32) for s in st_shapes],
        scratch_shapes=[pltpu.VMEM((bsz,) + s, F32) for s in st_shapes],
        compiler_params=pltpu.CompilerParams(dimension_semantics=("parallel", "arbitrary")),
        name="mlstm_scan",
    )(*args)
    return res[0], tuple(res[1:])


def mlstm_branch(p_l, p_c, conv_w, conv_b, i_bias, f_bias, need_ctx_out):
    bsz, seq, _ = p_l.shape
    w = BRANCH_W
    cos, sin = _rope_2d_tables(seq)
    tables = (jnp.tile(cos, (1, ML_HEADS)), jnp.tile(sin, (1, ML_HEADS)))
    zero_h = jnp.zeros_like(i_bias)
    ib = _lane_row(jnp.concatenate([i_bias, zero_h], axis=1).reshape(-1), 4 * ML_HEADS)
    fb = _lane_row(jnp.concatenate([zero_h, f_bias], axis=1).reshape(-1), 4 * ML_HEADS)
    params = (conv_w[:, :w], conv_b[:w].reshape(1, w), conv_w[:, w:], conv_b[w:].reshape(1, w), ib, fb)
    state0 = (jnp.zeros((bsz, ML_HEADS, ML_HD, ML_HD), F32), jnp.zeros((bsz, ML_HEADS, ML_HD), F32),
              jnp.zeros((bsz, ML_HEADS, ML_HD), F32))
    h_c = h_l = None
    for d, reverse in ((0, False), (1, True)):
        other_c = h_c if (d == 1 and need_ctx_out) else None
        h_c, st = _mlstm_pass(p_c, state0, other_c, None, params, reverse, d)
        h_l, _ = _mlstm_pass(p_l, st, h_l if d == 1 else None, tables, params, reverse, d)
    return h_l, (h_c if need_ctx_out else None)


MOE_TM = 512
ROUTE_LANES = 128


def _moe_router_kernel(x_ref, sc_ref, sh_ref, wh_ref, wl_ref, br_ref, hb_ref, rt_ref):
    h = x_ref[0] * (1.0 + sc_ref[0]) + sh_ref[0]
    h_hi = h.astype(BF16)
    hb_ref[0] = h_hi
    h_lo = (h - h_hi.astype(F32)).astype(BF16)
    logits = (jnp.dot(h_hi, wh_ref[...], preferred_element_type=F32)
              + jnp.dot(h_hi, wl_ref[...], preferred_element_type=F32)
              + jnp.dot(h_lo, wh_ref[...], preferred_element_type=F32)) + br_ref[...]
    lane = lax.broadcasted_iota(jnp.int32, logits.shape, 1)
    m1 = jnp.max(logits, axis=-1, keepdims=True)
    i1 = jnp.min(jnp.where(logits == m1, lane, ROUTE_LANES), axis=-1, keepdims=True)
    rest = jnp.where(lane == i1, MASK_NEG, logits)
    m2 = jnp.max(rest, axis=-1, keepdims=True)
    i2 = jnp.min(jnp.where(rest == m2, lane, ROUTE_LANES), axis=-1, keepdims=True)
    e2 = jnp.exp(m2 - m1)
    p1 = 1.0 / (1.0 + e2)
    p2 = e2 * p1
    rt_ref[0] = jnp.where(lane == 0, i1.astype(F32), jnp.where(lane == 1, i2.astype(F32),
                          jnp.where(lane == 2, p1, jnp.where(lane == 3, p2, 0.0))))


def moe_router(x, sc, sh, w_router, b_router, tm=512):
    bsz, seq, d = x.shape
    n_e = w_router.shape[1]
    tm = min(tm, seq)
    wr = jnp.zeros((d, ROUTE_LANES), F32).at[:, :n_e].set(w_router)
    wr_hi = wr.astype(BF16)
    wr_lo = (wr - wr_hi.astype(F32)).astype(BF16)
    br = jnp.full((1, ROUTE_LANES), MASK_NEG, F32).at[0, :n_e].set(b_router)
    return pl.pallas_call(
        _moe_router_kernel,
        grid=(bsz, seq // tm),
        in_specs=[pl.BlockSpec((1, tm, d), lambda b, i: (b, i, 0)),
                  pl.BlockSpec((1, 1, d), lambda b, i: (b, 0, 0)),
                  pl.BlockSpec((1, 1, d), lambda b, i: (b, 0, 0)),
                  pl.BlockSpec((d, ROUTE_LANES), lambda b, i: (0, 0)),
                  pl.BlockSpec((d, ROUTE_LANES), lambda b, i: (0, 0)),
                  pl.BlockSpec((1, ROUTE_LANES), lambda b, i: (0, 0))],
        out_specs=[pl.BlockSpec((1, tm, d), lambda b, i: (b, i, 0)),
                   pl.BlockSpec((1, tm, ROUTE_LANES), lambda b, i: (b, i, 0))],
        out_shape=[jax.ShapeDtypeStruct((bsz, seq, d), BF16), jax.ShapeDtypeStruct((bsz, seq, ROUTE_LANES), F32)],
        compiler_params=pltpu.CompilerParams(dimension_semantics=("parallel", "parallel")),
        name="moe_router",
    )(x, sc, sh, wr_hi, wr_lo, br)


def _route_tables(idx, n_e, tmg):
    n_tok = idx.shape[0]
    e_flat = idx.reshape(-1)
    onehot = (e_flat[:, None] == jnp.arange(n_e, dtype=jnp.int32)[None, :]).astype(jnp.int32)
    csum = jnp.cumsum(onehot, axis=0)
    rank = jnp.take_along_axis(csum - onehot, e_flat[:, None], axis=1)[:, 0]
    padded = ((csum[-1] + tmg - 1) // tmg) * tmg
    ends = jnp.cumsum(padded)
    pos = (ends - padded)[e_flat] + rank
    n_rows = TOP_K * n_tok + n_e * tmg
    ntiles = n_rows // tmg
    tile_start = jnp.arange(ntiles, dtype=jnp.int32) * tmg
    tile_expert = jnp.minimum(jnp.sum(tile_start[:, None] >= ends[None, :], axis=1), n_e - 1)
    src = (jnp.arange(n_rows, dtype=jnp.int32) % n_tok).at[pos].set(
        jnp.arange(TOP_K * n_tok, dtype=jnp.int32) // TOP_K)
    meta = jnp.concatenate([tile_expert, ends[-1:] // tmg]).astype(jnp.int32)
    return pos.reshape(n_tok, TOP_K), src, meta


def _expert_changed(meta_ref, i):
    return (i == 0) | (meta_ref[i] != meta_ref[jnp.maximum(i - 1, 0)])


def _moe_up_kernel(meta_ref, x_ref, wg_ref, wu_ref, a_ref, wgb_ref, wub_ref, *, ntiles):
    i = pl.program_id(1)

    @pl.when(_expert_changed(meta_ref, i))
    def _():
        wgb_ref[...] = wg_ref[0].astype(BF16)
        wub_ref[...] = wu_ref[0].astype(BF16)

    @pl.when(i < meta_ref[ntiles])
    def _():
        x = x_ref[...]
        g = jnp.dot(x, wgb_ref[...], preferred_element_type=F32)
        u = jnp.dot(x, wub_ref[...], preferred_element_type=F32)
        a_ref[...] = (_silu(g) * u).astype(BF16)

    @pl.when(i >= meta_ref[ntiles])
    def _():
        a_ref[...] = jnp.zeros_like(a_ref)


def _moe_down_kernel(meta_ref, a_ref, wd_ref, y_ref, wdb_ref, *, ntiles):
    i = pl.program_id(0)

    @pl.when(_expert_changed(meta_ref, i))
    def _():
        wdb_ref[...] = wd_ref[0].astype(BF16)

    @pl.when(i < meta_ref[ntiles])
    def _():
        y_ref[...] = jnp.dot(a_ref[...], wdb_ref[...], preferred_element_type=F32).astype(y_ref.dtype)

    @pl.when(i >= meta_ref[ntiles])
    def _():
        y_ref[...] = jnp.zeros_like(y_ref)


def moe_experts(xs, meta, wg, wu, wd, tmg, tf):
    n_rows, d = xs.shape
    n_e, _, f = wg.shape
    ntiles = n_rows // tmg
    nf = f // tf
    once = pl.Buffered(1)
    a = pl.pallas_call(
        functools.partial(_moe_up_kernel, ntiles=ntiles),
        grid_spec=pltpu.PrefetchScalarGridSpec(
            num_scalar_prefetch=1,
            grid=(nf, ntiles),
            in_specs=[pl.BlockSpec((tmg, d), lambda j, i, m: (i, 0)),
                      pl.BlockSpec((1, d, tf), lambda j, i, m: (m[i], 0, j), pipeline_mode=once),
                      pl.BlockSpec((1, d, tf), lambda j, i, m: (m[i], 0, j), pipeline_mode=once)],
            out_specs=pl.BlockSpec((tmg, tf), lambda j, i, m: (i, j)),
            scratch_shapes=[pltpu.VMEM((d, tf), BF16), pltpu.VMEM((d, tf), BF16)]),
        out_shape=jax.ShapeDtypeStruct((n_rows, f), BF16),
        compiler_params=pltpu.CompilerParams(
            dimension_semantics=("arbitrary", "arbitrary"), vmem_limit_bytes=V7X_VMEM_LIMIT),
        name="moe_up",
    )(meta, xs, wg, wu)
    return pl.pallas_call(
        functools.partial(_moe_down_kernel, ntiles=ntiles),
        grid_spec=pltpu.PrefetchScalarGridSpec(
            num_scalar_prefetch=1,
            grid=(ntiles,),
            in_specs=[pl.BlockSpec((tmg, f), lambda i, m: (i, 0)),
                      pl.BlockSpec((1, f, d), lambda i, m: (m[i], 0, 0), pipeline_mode=once)],
            out_specs=pl.BlockSpec((tmg, d), lambda i, m: (i, 0)),
            scratch_shapes=[pltpu.VMEM((f, d), BF16)]),
        out_shape=jax.ShapeDtypeStruct((n_rows, d), BF16),
        compiler_params=pltpu.CompilerParams(
            dimension_semantics=("arbitrary",), vmem_limit_bytes=V7X_VMEM_LIMIT),
        name="moe_down",
    )(meta, a, wd)


def _combine_ln_kernel(x_ref, gt_ref, rt_ref, ya_ref, yb_ref, g_ref, b_ref, o_ref, *, alpha):
    rt = rt_ref[0]
    y = rt[:, 2:3] * ya_ref[0].astype(F32) + rt[:, 3:4] * yb_ref[0].astype(F32)
    o_ref[0] = _layer_norm_rows(alpha * x_ref[0] + gt_ref[0] * y, g_ref[...], b_ref[...])


def combine_residual_ln(x, gate, route, ya, yb, ln_g, ln_b, alpha, tm=512):
    bsz, seq, d = x.shape
    tm = min(tm, seq)
    row = pl.BlockSpec((1, tm, d), lambda b, i: (b, i, 0))
    return pl.pallas_call(
        functools.partial(_combine_ln_kernel, alpha=alpha),
        grid=(bsz, seq // tm),
        in_specs=[row, pl.BlockSpec((1, 1, d), lambda b, i: (b, 0, 0)),
                  pl.BlockSpec((1, tm, ROUTE_LANES), lambda b, i: (b, i, 0)), row, row,
                  pl.BlockSpec((1, d), lambda b, i: (0, 0)), pl.BlockSpec((1, d), lambda b, i: (0, 0))],
        out_specs=row,
        out_shape=jax.ShapeDtypeStruct((bsz, seq, d), F32),
        compiler_params=pltpu.CompilerParams(
            dimension_semantics=("parallel", "parallel"), vmem_limit_bytes=V7X_VMEM_LIMIT),
        name="moe_combine_ln",
    )(x, gate, route, ya, yb, ln_g.reshape(1, d), ln_b.reshape(1, d))


def moe_residual_ln(x, sc, sh, gate, w_router, b_router, wg, wu, wd, ln_g, ln_b, alpha):
    bsz, seq, d = x.shape
    n_e, _, f = wg.shape
    n_tok = bsz * seq
    tmg = min(MOE_TM, TOP_K * n_tok)
    hb, route = moe_router(x, sc, sh, w_router, b_router)
    idx = route[..., :TOP_K].astype(jnp.int32).reshape(n_tok, TOP_K)
    pos, src, meta = _route_tables(idx, n_e, tmg)
    xs = hb.reshape(n_tok, d).at[src].get(mode="promise_in_bounds")
    ys = moe_experts(xs, meta, wg, wu, wd, tmg, f // 2)
    ya = ys.at[pos[:, 0]].get(mode="promise_in_bounds").reshape(bsz, seq, d)
    yb = ys.at[pos[:, 1]].get(mode="promise_in_bounds").reshape(bsz, seq, d)
    return combine_residual_ln(x, gate, route, ya, yb, ln_g, ln_b, alpha)


def _mixers(p_l, p_c, need_ctx_out, lru_conv_w, lru_conv_b, lru_wa, lru_ba, lru_wx, lru_bx, lru_lambda,
            na_rpb, ssd_conv_w, ssd_conv_b, ssd_dt_bias, ssd_a_log, ssd_d, ssd_norm_g, ml_conv_w, ml_conv_b,
            ml_i_bias, ml_f_bias):
    ya = lru_branch(p_l, p_c, lru_conv_w, lru_conv_b, lru_wa, lru_ba, lru_wx, lru_bx, lru_lambda, need_ctx_out)
    yb = na_branch(p_l, p_c, na_rpb, need_ctx_out)
    yc = ssd_branch(p_l, p_c, ssd_conv_w, ssd_conv_b, ssd_dt_bias, ssd_a_log, ssd_d, ssd_norm_g, need_ctx_out)
    yd = mlstm_branch(p_l, p_c, ml_conv_w, ml_conv_b, ml_i_bias, ml_f_bias, need_ctx_out)
    return (ya[0], yb[0], yc[0], yd[0]), (ya[1], yb[1], yc[1], yd[1])


def kernel(x, c, ctx, c_ctx, w_ada, b_ada, w_in, lru_conv_w, lru_conv_b, lru_wa, lru_ba, lru_wx, lru_bx, lru_lambda, na_rpb, ssd_conv_w, ssd_conv_b, ssd_dt_bias, ssd_a_log, ssd_d, ssd_norm_g, ml_conv_w, ml_conv_b, ml_i_bias, ml_f_bias, w_branch, w_out, ln_g, ln_b, ffn_w_gate, ffn_w_up, ffn_w_down, moe_w_router, moe_b_router, moe_w_gate, moe_w_up, moe_w_down):
    depth = w_in.shape[0]
    bsz, seq, d = x.shape
    alpha = (2.0 * depth) ** 0.25
    cvecs = jnp.zeros((8, d), F32).at[:bsz].set(c).at[bsz].set(c_ctx)
    mods = ada_modulation(cvecs, w_ada, b_ada)
    xl, xc = x, ctx
    for l in range(depth):
        need_ctx_out = l < depth - 1
        mod = mods[l]
        mod_l = jnp.split(mod[:bsz, None, :], 6, axis=-1)
        mod_c = jnp.split(jnp.broadcast_to(mod[bsz:bsz + 1, None, :], (bsz, 1, 6 * d)), 6, axis=-1)
        w_mix, w_gates = _pack_w_in(w_in[l])
        p_l, hb_l = mod_matmul(xl, mod_l[1], mod_l[0], w_mix)
        p_c, hb_c = mod_matmul(xc, mod_c[1], mod_c[0], w_mix)
        br_l, br_c = _mixers(p_l, p_c, need_ctx_out, lru_conv_w[l], lru_conv_b[l], lru_wa[l], lru_ba[l],
                             lru_wx[l], lru_bx[l], lru_lambda[l], na_rpb[l], ssd_conv_w[l], ssd_conv_b[l],
                             ssd_dt_bias[l], ssd_a_log[l], ssd_d[l], ssd_norm_g[l], ml_conv_w[l], ml_conv_b[l],
                             ml_i_bias[l], ml_f_bias[l])
        wb = w_branch[l].astype(BF16)
        wo = w_out[l].astype(BF16)
        xl = proj_residual_ln(merge_branches(hb_l, w_gates, br_l, wb), wo, xl, mod_l[2], ln_g[l, 0], ln_b[l, 0],
                              alpha)
        if need_ctx_out:
            xc = proj_residual_ln(merge_branches(hb_c, w_gates, br_c, wb), wo, xc, mod_c[2], ln_g[l, 0],
                                  ln_b[l, 0], alpha)
        j = l // 2
        if l % 2 == 0:
            wg = ffn_w_gate[j].astype(BF16)
            wu = ffn_w_up[j].astype(BF16)
            wd = ffn_w_down[j].astype(BF16)

            def ffn(h, m, wg=wg, wu=wu, wd=wd):
                return ffn_residual_ln(h, m[4], m[3], m[5], wg, wu, wd, ln_g[l, 1], ln_b[l, 1], alpha)
        else:
            def ffn(h, m, j=j):
                return moe_residual_ln(h, m[4], m[3], m[5], moe_w_router[j], moe_b_router[j], moe_w_gate[j],
                                       moe_w_up[j], moe_w_down[j], ln_g[l, 1], ln_b[l, 1], alpha)

        xl = ffn(xl, mod_l)
        if need_ctx_out:
            xc = ffn(xc, mod_c)
    return xl
```
